```python
import math
import jax, jax.numpy as jnp
from jax import lax
import numpy as np

D_MODEL = 2048
BATCH = 8
SEQ = 2048
DEPTH = 2

CHUNK = 64
N_META = 16
Q_BLOCK = 128
NORM_EPS = 1e-6
NEG_INF = -1e30

POOL_W = D_MODEL // 4
POOL_WINDOWS = (2, 4, 8, 16)
N_POOL_GROUPS = len(POOL_WINDOWS)
POOL_GROUP_W = POOL_W // N_POOL_GROUPS

DIFF_HEAD_DIM = 128
DIFF_W = D_MODEL // 2
DIFF_HEADS = DIFF_W // (2 * DIFF_HEAD_DIM)
DIFF_SUB = 2 * DIFF_HEADS
DIFF_SUBLN_EPS = 1e-5

REL_BUCKETS = 32
REL_MAX_DIST = 128

RWKV_W = D_MODEL // 4
RWKV_HEAD_DIM = 64
RWKV_HEADS = RWKV_W // RWKV_HEAD_DIM
RWKV_DECAY_LORA = max(32, int(round(RWKV_W ** 0.5 * 1.8 / 32)) * 32)
RWKV_AAA_LORA = max(32, int(round(RWKV_W ** 0.5 * 1.8 / 32)) * 32)
RWKV_MV_LORA = max(32, int(round(RWKV_W ** 0.5 * 1.3 / 32)) * 32)
RWKV_GATE_LORA = max(32, int(round(0.6 * RWKV_W ** 0.8 / 32)) * 32)
RWKV_SHIFT_W = 3 * RWKV_W + RWKV_DECAY_LORA + RWKV_AAA_LORA + RWKV_GATE_LORA
RWKV_LNX_EPS = 64e-5
RWKV_SPLITS = (RWKV_W, 2 * RWKV_W, 3 * RWKV_W, 3 * RWKV_W + RWKV_DECAY_LORA,
               3 * RWKV_W + RWKV_DECAY_LORA + RWKV_AAA_LORA)

N_BRANCH = 3
N_IN = POOL_W + 3 * DIFF_W + RWKV_SHIFT_W + N_BRANCH * D_MODEL
IN_SPLITS = (POOL_W, POOL_W + DIFF_W, POOL_W + 2 * DIFF_W, POOL_W + 3 * DIFF_W,
             POOL_W + 3 * DIFF_W + RWKV_SHIFT_W)

D_FF = 11 * D_MODEL // 4
FFN_CONV = 3

kernel_name = 'chunk_causal_hybrid_pool_diffattn_rwkv7'


def _rms(x, g, eps=NORM_EPS):
    xf = x.astype(jnp.float32)
    y = xf * lax.rsqrt(jnp.mean(xf * xf, axis=-1, keepdims=True) + eps)
    return (y * g.astype(jnp.float32)).astype(x.dtype)


def _chunk_ids(pos):
    return np.where(pos < N_META, 0, 1 + (pos - N_META) // CHUNK)


def _rel_bucket(rel):
    half = REL_BUCKETS // 2
    max_exact = half // 2
    n = np.abs(rel)
    large = max_exact + (np.log(np.maximum(n, 1) / max_exact) / math.log(REL_MAX_DIST / max_exact)
                         * (half - max_exact)).astype(np.int32)
    large = np.minimum(large, half - 1)
    return ((rel > 0) * half + np.where(n < max_exact, n, large)).astype(np.int32)


def _token_shift(z, mu):
    prev = jnp.pad(z, ((0, 0), (1, 0), (0, 0)))[:, :-1]
    return z + (prev - z) * mu


def _pool_mixer(z, w_grp, scale):
    Bn, L, _ = z.shape
    zg = z.reshape(Bn, L, N_POOL_GROUPS, POOL_GROUP_W).astype(jnp.float32)
    csum = jnp.cumsum(zg, axis=1)
    outs = []
    for gi, win in enumerate(POOL_WINDOWS):
        c = csum[:, :, gi]
        lag = jnp.pad(c, ((0, 0), (win, 0), (0, 0)))[:, :L]
        cnt = jnp.minimum(jnp.arange(1, L + 1), win).astype(jnp.float32)
        outs.append((c - lag) / cnt[None, :, None] - zg[:, :, gi])
    pooled = jnp.stack(outs, axis=2).astype(z.dtype)
    y = jnp.einsum('blgc,gcd->blgd', pooled, w_grp)
    return y.reshape(Bn, L, POOL_W) * scale


def _diff_attention(q, k, v, lam, rel_bias):
    Bn, L = q.shape[0], q.shape[1]
    pos = np.arange(L)
    cid = _chunk_ids(pos)
    scale = DIFF_HEAD_DIM ** -0.5
    outs = []
    for q0 in range(0, L, Q_BLOCK):
        q1 = min(q0 + Q_BLOCK, L)
        k_end = min(L, N_META + int(cid[q1 - 1]) * CHUNK)
        bucket = _rel_bucket(pos[None, :k_end] - pos[q0:q1, None])
        mask = cid[None, :k_end] <= cid[q0:q1, None]
        bias = jnp.transpose(rel_bias[bucket].astype(jnp.float32), (2, 0, 1))
        s = jnp.einsum('bqsd,bksd->bsqk', q[:, q0:q1], k[:, :k_end]).astype(jnp.float32) * scale + bias
        s = jnp.where(mask, s, NEG_INF)
        p = jax.nn.softmax(s, axis=-1).reshape(Bn, DIFF_HEADS, 2, q1 - q0, k_end)
        pd = p[:, :, 0] - lam * p[:, :, 1]
        outs.append(jnp.einsum('bhqk,bkhd->bqhd', pd.astype(v.dtype), v[:, :k_end]))
    return jnp.concatenate(outs, axis=1)


def _rwkv7_scan(r, decay, k, v, a, b):
    Bn, L, H, N = r.shape

    def step(S, inp):
        r_t, w_t, k_t, v_t, a_t, b_t = inp
        sa = jnp.einsum('bhij,bhj->bhi', S, a_t)
        S = S * w_t[:, :, None, :] + sa[..., None] * b_t[:, :, None, :] + v_t[..., None] * k_t[:, :, None, :]
        return S, jnp.einsum('bhij,bhj->bhi', S, r_t)

    xs = tuple(jnp.swapaxes(t, 0, 1) for t in (r, decay, k, v, a, b))
    S0 = jnp.zeros((Bn, H, N, N), jnp.float32)
    _, ys = lax.scan(step, S0, xs)
    return jnp.swapaxes(ys, 0, 1)


def _rwkv7_mixer(z, v_first, mu, w0, w2, a0, a2, g2, k_k, k_a, r_k, lnx_w, lnx_b, v_res):
    Bn, L, _ = z.shape
    f32 = jnp.float32
    z = _token_shift(z, mu)
    r, kx, vx, wd, ad, gd = jnp.split(z, RWKV_SPLITS, axis=-1)
    w = -jax.nn.softplus(-(w0 + jnp.tanh(wd) @ w2).astype(f32)) - 0.5
    decay = jnp.exp(-jnp.exp(w))
    if v_res is None:
        v_first = vx
    else:
        v0, v1, v2 = v_res
        vx = vx + (v_first - vx) * jax.nn.sigmoid(v0 + (vx @ v1) @ v2)
    a = jax.nn.sigmoid((a0 + ad @ a2).astype(f32))
    g = jax.nn.sigmoid(gd) @ g2
    hs = (Bn, L, RWKV_HEADS, RWKV_HEAD_DIM)
    kk = (kx * k_k).astype(f32).reshape(hs)
    kk = kk * lax.rsqrt(jnp.maximum(jnp.sum(kk * kk, axis=-1, keepdims=True), 1e-24))
    kmod = (kx.astype(f32) * (1.0 + (a - 1.0) * k_a.astype(f32))).reshape(hs)
    a_h = a.reshape(hs)
    rh = r.astype(f32).reshape(hs)
    vh = vx.astype(f32).reshape(hs)
    y = _rwkv7_scan(rh, decay.reshape(hs), kmod, vh, -kk, kk * a_h)
    mean = jnp.mean(y, axis=-1, keepdims=True)
    var = jnp.mean(jnp.square(y - mean), axis=-1, keepdims=True)
    y = ((y - mean) * lax.rsqrt(var + RWKV_LNX_EPS)).reshape(Bn, L, RWKV_W)
    y = y * lnx_w.astype(f32) + lnx_b.astype(f32)
    bonus = jnp.sum(rh * kmod * r_k.astype(f32), axis=-1, keepdims=True) * vh
    y = (y + bonus.reshape(Bn, L, RWKV_W)).astype(z.dtype) * g
    return y, v_first


def _conv_ffn(h, w_up, conv_w, conv_b, w_down):
    u = h @ w_up
    L = u.shape[1]
    up = jnp.pad(u, ((0, 0), (FFN_CONV - 1, 0), (0, 0)))
    c = conv_b + up[:, 0:L] * conv_w[0]
    for j in range(1, FFN_CONV):
        c = c + up[:, j:j + L] * conv_w[j]
    val, gate = jnp.split(c, 2, axis=-1)
    return (jax.nn.gelu(gate, approximate=True) * val) @ w_down


def setup_inputs(seed: int = 0) -> dict:
    key = jax.random.key(seed)
    ks = iter(jax.random.split(key, 48))

    def nrm(shape, scale):
        return scale * jax.random.normal(next(ks), shape, jnp.float32)

    def gain(shape):
        return 1.0 + nrm(shape, 0.05)

    VR = DEPTH - 1
    conv_w = jnp.concatenate([nrm((DEPTH, FFN_CONV - 1, 2 * D_FF), 0.3),
                              1.0 + nrm((DEPTH, 1, 2 * D_FF), 0.1)], axis=1)
    return {
        'x': nrm((BATCH, SEQ, D_MODEL), 1.0),
        'meta': nrm((N_META, D_MODEL), 1.0),
        'rel_bias': nrm((REL_BUCKETS, DIFF_SUB), 0.5),
        'norm_mix_pre': gain((DEPTH, D_MODEL)),
        'norm_mix_post': gain((DEPTH, D_MODEL)),
        'norm_ffn_pre': gain((DEPTH, D_MODEL)),
        'norm_ffn_post': gain((DEPTH, D_MODEL)),
        'w_in': nrm((DEPTH, D_MODEL, N_IN), D_MODEL ** -0.5),
        'pool_w': nrm((DEPTH, N_POOL_GROUPS, POOL_GROUP_W, POOL_GROUP_W), POOL_GROUP_W ** -0.5),
        'pool_scale': 1.0 + nrm((DEPTH, POOL_W), 0.1),
        'diff_lq1': nrm((DEPTH, DIFF_HEAD_DIM), 0.1),
        'diff_lk1': nrm((DEPTH, DIFF_HEAD_DIM), 0.1),
        'diff_lq2': nrm((DEPTH, DIFF_HEAD_DIM), 0.1),
        'diff_lk2': nrm((DEPTH, DIFF_HEAD_DIM), 0.1),
        'diff_subln': gain((DEPTH, 2 * DIFF_HEAD_DIM)),
        'rwkv_mu': jax.random.uniform(next(ks), (DEPTH, RWKV_SHIFT_W), jnp.float32, 0.0, 1.0),
        'rwkv_w0': jax.random.uniform(next(ks), (DEPTH, RWKV_W), jnp.float32, -6.0, -1.0),
        'rwkv_w2': nrm((DEPTH, RWKV_DECAY_LORA, RWKV_W), 0.1),
        'rwkv_a0': nrm((DEPTH, RWKV_W), 0.1),
        'rwkv_a2': nrm((DEPTH, RWKV_AAA_LORA, RWKV_W), 0.5 * RWKV_AAA_LORA ** -0.5),
        'rwkv_g2': nrm((DEPTH, RWKV_GATE_LORA, RWKV_W), RWKV_GATE_LORA ** -0.5),
        'rwkv_kk': 0.85 + nrm((DEPTH, RWKV_W), 0.05),
        'rwkv_ka': gain((DEPTH, RWKV_W)),
        'rwkv_rk': nrm((DEPTH, RWKV_HEADS, RWKV_HEAD_DIM), 0.1),
        'rwkv_lnx_w': gain((DEPTH, RWKV_W)),
        'rwkv_lnx_b': nrm((DEPTH, RWKV_W), 0.02),
        'rwkv_v0': 0.5 + nrm((VR, RWKV_W), 0.1),
        'rwkv_v1': nrm((VR, RWKV_W, RWKV_MV_LORA), RWKV_W ** -0.5),
        'rwkv_v2': nrm((VR, RWKV_MV_LORA, RWKV_W), 0.5 * RWKV_MV_LORA ** -0.5),
        'p_a': nrm((DEPTH, POOL_W, D_MODEL), POOL_W ** -0.5),
        'p_b': nrm((DEPTH, DIFF_W, D_MODEL), DIFF_W ** -0.5),
        'p_c': nrm((DEPTH, RWKV_W, D_MODEL), RWKV_W ** -0.5),
        'w_o': nrm((DEPTH, D_MODEL, D_MODEL), D_MODEL ** -0.5),
        'ffn_up': nrm((DEPTH, D_MODEL, 2 * D_FF), D_MODEL ** -0.5),
        'ffn_conv_w': conv_w,
        'ffn_conv_b': nrm((DEPTH, 2 * D_FF), 0.02),
        'ffn_down': nrm((DEPTH, D_FF, D_MODEL), D_FF ** -0.5),
    }


def reference(x, meta, rel_bias, norm_mix_pre, norm_mix_post, norm_ffn_pre, norm_ffn_post,
              w_in, pool_w, pool_scale, diff_lq1, diff_lk1, diff_lq2, diff_lk2, diff_subln,
              rwkv_mu, rwkv_w0, rwkv_w2, rwkv_a0, rwkv_a2, rwkv_g2, rwkv_kk, rwkv_ka, rwkv_rk,
              rwkv_lnx_w, rwkv_lnx_b, rwkv_v0, rwkv_v1, rwkv_v2, p_a, p_b, p_c, w_o,
              ffn_up, ffn_conv_w, ffn_conv_b, ffn_down):
    f32 = jnp.float32
    Bn = x.shape[0]
    h = jnp.concatenate([jnp.broadcast_to(meta.astype(x.dtype)[None], (Bn, N_META, D_MODEL)), x], axis=1)
    L = h.shape[1]
    v_first = None
    for l in range(DEPTH):
        lam_init = 0.8 - 0.6 * math.exp(-0.3 * l)
        u = _rms(h, norm_mix_pre[l])
        z_pool, zq, zk, zv, z_rwkv, z_gate = jnp.split(u @ w_in[l], IN_SPLITS, axis=-1)
        y_pool = _pool_mixer(z_pool, pool_w[l], pool_scale[l])
        lam = (jnp.exp(jnp.sum(diff_lq1[l].astype(f32) * diff_lk1[l].astype(f32)))
               - jnp.exp(jnp.sum(diff_lq2[l].astype(f32) * diff_lk2[l].astype(f32))) + lam_init)
        o = _diff_attention(zq.reshape(Bn, L, DIFF_SUB, DIFF_HEAD_DIM),
                            zk.reshape(Bn, L, DIFF_SUB, DIFF_HEAD_DIM),
                            zv.reshape(Bn, L, DIFF_HEADS, 2 * DIFF_HEAD_DIM), lam, rel_bias)
        y_diff = (_rms(o, diff_subln[l], DIFF_SUBLN_EPS) * (1.0 - lam_init)).reshape(Bn, L, DIFF_W)
        v_res = None if l == 0 else (rwkv_v0[l - 1], rwkv_v1[l - 1], rwkv_v2[l - 1])
        y_rwkv, v_first = _rwkv7_mixer(z_rwkv, v_first, rwkv_mu[l], rwkv_w0[l], rwkv_w2[l],
                                       rwkv_a0[l], rwkv_a2[l], rwkv_g2[l], rwkv_kk[l], rwkv_ka[l],
                                       rwkv_rk[l], rwkv_lnx_w[l], rwkv_lnx_b[l], v_res)
        gate = jax.nn.sigmoid(z_gate.reshape(Bn, L, N_BRANCH, D_MODEL))
        merged = (gate[:, :, 0] * (y_pool @ p_a[l]) + gate[:, :, 1] * (y_diff @ p_b[l])
                  + gate[:, :, 2] * (y_rwkv @ p_c[l]))
        h = h + _rms(merged @ w_o[l], norm_mix_post[l])
        f = _conv_ffn(_rms(h, norm_ffn_pre[l]), ffn_up[l], ffn_conv_w[l], ffn_conv_b[l], ffn_down[l])
        h = h + _rms(f, norm_ffn_post[l])
    return h[:, N_META:]
```

```python
import functools
import math

import numpy as np
import jax
import jax.numpy as jnp
from jax import lax
from jax.experimental import pallas as pl
from jax.experimental.pallas import tpu as pltpu

F32 = jnp.float32
BF16 = jnp.bfloat16

CHUNK = 64
NORM_EPS = 1e-6
NEG_INF = -1e30
POOL_WINDOWS = (2, 4, 8, 16)
DIFF_SUBLN_EPS = 1e-5
REL_MAX_DIST = 128
RWKV_LNX_EPS = 64e-5
FFN_CONV = 3

LANES = 128
SUBLANES = 8
ATTN_TILE = 128
VMEM_LIMIT = 56 * 1024 * 1024


def _cparams(sem):
    return pltpu.CompilerParams(dimension_semantics=sem, vmem_limit_bytes=VMEM_LIMIT)


def _round_up(x, m):
    return (x + m - 1) // m * m


def _pick_tile(total, target, mult):
    best = None
    for t in range(mult, min(total, target) + 1, mult):
        if total % t == 0:
            best = t
    assert best is not None, (total, target, mult)
    return best


def _shift_down(x, d):
    row = lax.broadcasted_iota(jnp.int32, x.shape, 0)
    return jnp.where(row >= d, pltpu.roll(x, d, 0), 0.0)


def _sigmoid(x):
    return 1.0 / (1.0 + jnp.exp(-x))


def _dot(a, b):
    return jnp.dot(a.astype(BF16), b.astype(BF16), preferred_element_type=F32)


def _dot_nt(a, b):
    return lax.dot_general(a.astype(BF16), b.astype(BF16), (((1,), (1,)), ((), ())),
                           preferred_element_type=F32)


def _split3(x):
    hi = x.astype(BF16)
    r1 = x - hi.astype(F32)
    mid = r1.astype(BF16)
    lo = (r1 - mid.astype(F32)).astype(BF16)
    return hi, mid, lo


def _dot_exact_rhs(x, ones_bf16):
    hi, mid, lo = _split3(x)
    acc = jnp.dot(hi, ones_bf16, preferred_element_type=F32)
    acc = acc + jnp.dot(mid, ones_bf16, preferred_element_type=F32)
    return acc + jnp.dot(lo, ones_bf16, preferred_element_type=F32)


def _dot_hp(a, b):
    ah, am, al = _split3(a)
    bh, bm, bl = _split3(b)
    d = lambda x, y: jnp.dot(x, y, preferred_element_type=F32)
    small = d(ah, bl) + d(am, bm) + d(al, bh)
    return d(ah, bh) + (d(ah, bm) + d(am, bh)) + small


def _dot_hp_nt(a, b):
    ah, am, al = _split3(a)
    bh, bm, bl = _split3(b)
    d = lambda x, y: lax.dot_general(x, y, (((1,), (1,)), ((), ())), preferred_element_type=F32)
    small = d(ah, bl) + d(am, bm) + d(al, bh)
    return d(ah, bh) + (d(ah, bm) + d(am, bh)) + small


def _rms_kernel(x_ref, g_ref, o_ref):
    x = x_ref[...]
    y = x * lax.rsqrt(jnp.mean(x * x, axis=-1, keepdims=True) + NORM_EPS)
    o_ref[...] = (y * g_ref[...]).astype(o_ref.dtype)


def _rms_call(x, g, tm):
    M, D = x.shape
    return pl.pallas_call(
        _rms_kernel,
        grid=(M // tm,),
        in_specs=[pl.BlockSpec((tm, D), lambda i: (i, 0)),
                  pl.BlockSpec((1, D), lambda i: (0, 0))],
        out_specs=pl.BlockSpec((tm, D), lambda i: (i, 0)),
        out_shape=jax.ShapeDtypeStruct((M, D), BF16),
        compiler_params=_cparams(("parallel",)),
        name="rms_norm",
    )(x, g.reshape(1, D))


def _mm_kernel(a_ref, b_ref, o_ref):
    o_ref[...] = jnp.dot(a_ref[...], b_ref[...], preferred_element_type=F32).astype(o_ref.dtype)


def _mm_call(a, b, out_dtype, tm, tn, name):
    M, K = a.shape
    N = b.shape[1]
    return pl.pallas_call(
        _mm_kernel,
        grid=(M // tm, N // tn),
        in_specs=[pl.BlockSpec((tm, K), lambda i, j: (i, 0)),
                  pl.BlockSpec((K, tn), lambda i, j: (0, j))],
        out_specs=pl.BlockSpec((tm, tn), lambda i, j: (i, j)),
        out_shape=jax.ShapeDtypeStruct((M, N), out_dtype),
        compiler_params=_cparams(("parallel", "arbitrary")),
        name=name,
    )(a, b)


def _mm_res_kernel(a_ref, w_ref, h_ref, gp_ref, gn_ref, oh_ref, ou_ref, acc_ref):
    k = pl.program_id(1)

    @pl.when(k == 0)
    def _():
        acc_ref[...] = jnp.zeros_like(acc_ref)

    acc_ref[...] += jnp.dot(a_ref[...], w_ref[...], preferred_element_type=F32)

    @pl.when(k == pl.num_programs(1) - 1)
    def _():
        f = acc_ref[...]
        y = f * lax.rsqrt(jnp.mean(f * f, axis=-1, keepdims=True) + NORM_EPS) * gp_ref[...]
        hn = h_ref[...] + y
        oh_ref[...] = hn
        u = hn * lax.rsqrt(jnp.mean(hn * hn, axis=-1, keepdims=True) + NORM_EPS) * gn_ref[...]
        ou_ref[...] = u.astype(ou_ref.dtype)


def _mm_res_call(a, w, h, g_post, g_next, tm, tk, name):
    M, K = a.shape
    D = w.shape[1]
    return pl.pallas_call(
        _mm_res_kernel,
        grid=(M // tm, K // tk),
        in_specs=[pl.BlockSpec((tm, tk), lambda i, k: (i, k)),
                  pl.BlockSpec((tk, D), lambda i, k: (k, 0)),
                  pl.BlockSpec((tm, D), lambda i, k: (i, 0)),
                  pl.BlockSpec((1, D), lambda i, k: (0, 0)),
                  pl.BlockSpec((1, D), lambda i, k: (0, 0))],
        out_specs=[pl.BlockSpec((tm, D), lambda i, k: (i, 0)),
                   pl.BlockSpec((tm, D), lambda i, k: (i, 0))],
        out_shape=[jax.ShapeDtypeStruct((M, D), F32),
                   jax.ShapeDtypeStruct((M, D), BF16)],
        scratch_shapes=[pltpu.VMEM((tm, D), F32)],
        compiler_params=_cparams(("parallel", "arbitrary")),
        name=name,
    )(a, w, h, g_post.reshape(1, D), g_next.reshape(1, D))


def _pool_kernel(z_ref, w_ref, s_ref, o_ref, *, gw):
    L = z_ref.shape[1]
    row = lax.broadcasted_iota(jnp.int32, (L, gw), 0)
    for gi, win in enumerate(POOL_WINDOWS):
        sl = slice(gi * gw, (gi + 1) * gw)
        zg = z_ref[0, :, sl].astype(F32)
        s = zg
        d = 1
        while d < win:
            s = s + _shift_down(s, d)
            d *= 2
        cnt = jnp.minimum(row + 1, win).astype(F32)
        pooled = s / cnt - zg
        y = _dot(pooled, w_ref[gi]) * s_ref[:, sl]
        o_ref[0, :, sl] = y.astype(o_ref.dtype)


def _pool_call(z, w, scale, B, Lp):
    PW = z.shape[-1]
    G = len(POOL_WINDOWS)
    gw = PW // G
    z3 = z.reshape(B, Lp, PW)
    out = pl.pallas_call(
        functools.partial(_pool_kernel, gw=gw),
        grid=(B,),
        in_specs=[pl.BlockSpec((1, Lp, PW), lambda b: (b, 0, 0)),
                  pl.BlockSpec((G, gw, gw), lambda b: (0, 0, 0)),
                  pl.BlockSpec((1, PW), lambda b: (0, 0))],
        out_specs=pl.BlockSpec((1, Lp, PW), lambda b: (b, 0, 0)),
        out_shape=jax.ShapeDtypeStruct((B, Lp, PW), BF16),
        compiler_params=_cparams(("parallel",)),
        name="pool_mixer",
    )(z3, w.astype(BF16), scale.reshape(1, PW))
    return out.reshape(B * Lp, PW)


def _rel_bucket(rel, n_buckets):
    half = n_buckets // 2
    max_exact = half // 2
    n = np.abs(rel)
    large = max_exact + (np.log(np.maximum(n, 1) / max_exact) / math.log(REL_MAX_DIST / max_exact)
                         * (half - max_exact)).astype(np.int32)
    large = np.minimum(large, half - 1)
    return ((rel > 0) * half + np.where(n < max_exact, n, large)).astype(np.int32)


def _bias_table_kernel(rb_ref, bucket_ref, pen_ref, o_ref, *, n_buckets):
    s = pl.program_id(0)
    bucket = bucket_ref[...]
    acc = pen_ref[...]
    for b in range(n_buckets):
        acc = acc + jnp.where(bucket == b, rb_ref[b, s], 0.0)
    o_ref[0] = acc


def _bias_table_call(rel_bias, n_meta):
    n_buckets, n_sub = rel_bias.shape
    T = ATTN_TILE
    assert T % CHUNK == 0 and n_meta < CHUNK
    r = np.arange(T)[:, None]
    m = np.arange(3 * T)[None, :] - T
    bucket = _rel_bucket(m - r, n_buckets)
    visible = np.floor_divide(m - n_meta, CHUNK) <= np.floor_divide(r - n_meta, CHUNK)
    pen = np.where(visible, 0.0, NEG_INF).astype(np.float32)
    return pl.pallas_call(
        functools.partial(_bias_table_kernel, n_buckets=n_buckets),
        grid=(n_sub,),
        in_specs=[pl.BlockSpec(memory_space=pltpu.SMEM),
                  pl.BlockSpec((T, 3 * T), lambda s: (0, 0)),
                  pl.BlockSpec((T, 3 * T), lambda s: (0, 0))],
        out_specs=pl.BlockSpec((1, T, 3 * T), lambda s: (s, 0, 0)),
        out_shape=jax.ShapeDtypeStruct((n_sub, T, 3 * T), F32),
        compiler_params=_cparams(("arbitrary",)),
        name="bias_table",
    )(rel_bias, jnp.asarray(bucket), jnp.asarray(pen))


def _attn_kernel(rb_ref, lq1_ref, lk1_ref, lq2_ref, lk2_ref, sub_ref, q_ref, k_ref, v_ref, w_ref,
                 o_ref, m_sc, l_sc, acc_sc, *, hd, scale, lam_init, far_bucket, nt):
    T = ATTN_TILE
    h = pl.program_id(1)
    i = pl.program_id(2)
    m_sc[...] = jnp.full_like(m_sc, NEG_INF)
    l_sc[...] = jnp.zeros_like(l_sc)
    acc_sc[...] = jnp.zeros_like(acc_sc)
    q = q_ref[0]

    def update(s, sc, vt):
        m_prev = m_sc[s]
        m_new = jnp.maximum(m_prev, jnp.max(sc, axis=-1, keepdims=True))
        alpha = jnp.exp(m_prev - m_new)
        p = jnp.exp(sc - m_new)
        l_sc[s] = alpha * l_sc[s] + jnp.sum(p, axis=-1, keepdims=True)
        acc_sc[s] = alpha * acc_sc[s] + jnp.dot(p.astype(BF16), vt, preferred_element_type=F32)
        m_sc[s] = m_new

    def far_body(j, carry):
        start = pl.multiple_of(j * T, T)
        kt = k_ref[0, pl.ds(start, T), :]
        vt = v_ref[0, pl.ds(start, T), :]
        for s in range(2):
            sc = _dot_nt(q[:, s * hd:(s + 1) * hd], kt[:, s * hd:(s + 1) * hd]) * scale
            update(s, sc + rb_ref[far_bucket, 2 * h + s], vt)
        return carry

    lax.fori_loop(0, jnp.maximum(i - 1, 0), far_body, 0)

    for dj in (-1, 0, 1):
        j = i + dj
        jc = jnp.clip(j, 0, nt - 1)
        pen = jnp.where(j == jc, 0.0, NEG_INF)
        start = pl.multiple_of(jc * T, T)
        kt = k_ref[0, pl.ds(start, T), :]
        vt = v_ref[0, pl.ds(start, T), :]
        for s in range(2):
            sc = _dot_nt(q[:, s * hd:(s + 1) * hd], kt[:, s * hd:(s + 1) * hd]) * scale
            update(s, sc + w_ref[s, :, (dj + 1) * T:(dj + 2) * T] + pen, vt)

    lam = (jnp.exp(jnp.sum(lq1_ref[...] * lk1_ref[...], axis=-1, keepdims=True))
           - jnp.exp(jnp.sum(lq2_ref[...] * lk2_ref[...], axis=-1, keepdims=True)) + lam_init)
    o = acc_sc[0] / l_sc[0] - lam * (acc_sc[1] / l_sc[1])
    y = o * lax.rsqrt(jnp.mean(o * o, axis=-1, keepdims=True) + DIFF_SUBLN_EPS)
    o_ref[0] = (y * sub_ref[...] * (1.0 - lam_init)).astype(o_ref.dtype)


def _attn_call(zqkv, table, rel_bias, lq1, lk1, lq2, lk2, subln, B, Lp, lam_init):
    hd = lq1.shape[0]
    DW = zqkv.shape[-1] // 3
    n_heads = DW // (2 * hd)
    T = ATTN_TILE
    nt = Lp // T
    qkv3 = zqkv.reshape(B, Lp, 3 * DW)
    hw = 2 * hd
    n_buckets = rel_bias.shape[0]
    vec = lambda a: a.reshape(1, -1)
    kern = functools.partial(_attn_kernel, hd=hd, scale=hd ** -0.5, lam_init=lam_init,
                             far_bucket=n_buckets // 2 - 1, nt=nt)
    out = pl.pallas_call(
        kern,
        grid=(B, n_heads, nt),
        in_specs=[pl.BlockSpec(memory_space=pltpu.SMEM),
                  pl.BlockSpec((1, hd), lambda b, h, i: (0, 0)),
                  pl.BlockSpec((1, hd), lambda b, h, i: (0, 0)),
                  pl.BlockSpec((1, hd), lambda b, h, i: (0, 0)),
                  pl.BlockSpec((1, hd), lambda b, h, i: (0, 0)),
                  pl.BlockSpec((1, hw), lambda b, h, i: (0, 0)),
                  pl.BlockSpec((1, T, hw), lambda b, h, i: (b, i, h)),
                  pl.BlockSpec((1, Lp, hw), lambda b, h, i: (b, 0, n_heads + h)),
                  pl.BlockSpec((1, Lp, hw), lambda b, h, i: (b, 0, 2 * n_heads + h)),
                  pl.BlockSpec((2, T, 3 * T), lambda b, h, i: (h, 0, 0))],
        out_specs=pl.BlockSpec((1, T, hw), lambda b, h, i: (b, i, h)),
        out_shape=jax.ShapeDtypeStruct((B, Lp, DW), BF16),
        scratch_shapes=[pltpu.VMEM((2, T, 1), F32), pltpu.VMEM((2, T, 1), F32),
                        pltpu.VMEM((2, T, hw), F32)],
        compiler_params=_cparams(("parallel", "parallel", "arbitrary")),
        name="diff_attention",
    )(rel_bias, vec(lq1), vec(lk1), vec(lq2), vec(lk2), vec(subln), qkv3, qkv3, qkv3, table)
    return out.reshape(B * Lp, DW)


def _rwkv_prep_kernel(*refs, rw, has_vres):
    if has_vres:
        (z_ref, zp_ref, mu_ref, w0_ref, w2_ref, a0_ref, a2_ref, g2_ref, kk_ref, ka_ref, ones_ref,
         vf_ref, v0_ref, v1_ref, v2_ref,
         r_o, lw_o, k_o, v_o, a_o, b_o, g_o) = refs
    else:
        (z_ref, zp_ref, mu_ref, w0_ref, w2_ref, a0_ref, a2_ref, g2_ref, kk_ref, ka_ref, ones_ref,
         r_o, lw_o, k_o, v_o, a_o, b_o, g_o) = refs
    i = pl.program_id(1)
    z = z_ref[0]
    tl = z.shape[0]
    last_prev = zp_ref[0, SUBLANES - 1:SUBLANES, :]
    last_prev = jnp.where(i > 0, last_prev, 0.0)
    row = lax.broadcasted_iota(jnp.int32, z.shape, 0)
    prev = jnp.where(row >= 1, pltpu.roll(z, 1, 0), last_prev)
    zs = z + (prev - z) * mu_ref[...]
    r = zs[:, 0:rw]
    kx = zs[:, rw:2 * rw]
    vx = zs[:, 2 * rw:3 * rw]
    wd = zs[:, 3 * rw:3 * rw + LANES]
    ad = zs[:, 3 * rw + LANES:3 * rw + 2 * LANES]
    gd = zs[:, 3 * rw + 2 * LANES:3 * rw + 3 * LANES]
    wl = w0_ref[...] + _dot(jnp.tanh(wd), w2_ref[...])
    sp = jnp.maximum(-wl, 0.0) + jnp.log(1.0 + jnp.exp(-jnp.abs(wl)))
    lw = -jnp.exp(-sp - 0.5)
    if has_vres:
        gate_v = _sigmoid(v0_ref[...] + _dot(_dot(vx, v1_ref[...]), v2_ref[...]))
        vx = vx + (vf_ref[0] - vx) * gate_v
    a = _sigmoid(a0_ref[...] + _dot(ad, a2_ref[...]))
    g = _dot(_sigmoid(gd), g2_ref[...])
    kk = kx * kk_ref[...]
    ss = _dot_exact_rhs(kk * kk, ones_ref[...])
    kk = kk * lax.rsqrt(jnp.maximum(ss, 1e-24))
    kmod = kx * (1.0 + (a - 1.0) * ka_ref[...])
    r_o[0] = r
    lw_o[0] = lw
    k_o[0] = kmod
    v_o[0] = vx
    a_o[0] = -kk
    b_o[0] = kk * a
    g_o[0] = g


def _pad_rows(w, n):
    return jnp.pad(w, ((0, n - w.shape[0]), (0, 0)))


def _rwkv_prep_call(zr, B, Lp, rw, mu_p, w0, w2, a0, a2, g2, k_k, k_a, ones_bd, v_first, v_res):
    ZW = zr.shape[-1]
    z3 = zr.reshape(B, Lp, ZW)
    tl = _pick_tile(Lp, 384, SUBLANES)
    nb = tl // SUBLANES
    has_vres = v_res is not None
    vec = lambda a: a.reshape(1, -1)
    full = lambda a: pl.BlockSpec(a.shape, lambda b, i: (0,) * a.ndim)
    w2p = _pad_rows(w2, LANES).astype(BF16)
    a2p = _pad_rows(a2, LANES).astype(BF16)
    g2p = _pad_rows(g2, LANES).astype(BF16)
    args = [z3, z3, vec(mu_p), vec(w0), w2p, vec(a0), a2p, g2p, vec(k_k), vec(k_a), ones_bd]
    in_specs = [pl.BlockSpec((1, tl, ZW), lambda b, i: (b, i, 0)),
                pl.BlockSpec((1, SUBLANES, ZW), lambda b, i: (b, jnp.maximum(i * nb - 1, 0), 0))]
    in_specs += [full(a) for a in args[2:]]
    if has_vres:
        v0, v1, v2 = v_res
        v1p = jnp.pad(v1, ((0, 0), (0, LANES - v1.shape[1]))).astype(BF16)
        v2p = _pad_rows(v2, LANES).astype(BF16)
        extra = [v_first, vec(v0), v1p, v2p]
        args += extra
        in_specs += [pl.BlockSpec((1, tl, rw), lambda b, i: (b, i, 0))] + [full(a) for a in extra[1:]]
    o_spec = pl.BlockSpec((1, tl, rw), lambda b, i: (b, i, 0))
    o_shape = jax.ShapeDtypeStruct((B, Lp, rw), F32)
    return pl.pallas_call(
        functools.partial(_rwkv_prep_kernel, rw=rw, has_vres=has_vres),
        grid=(B, Lp // tl),
        in_specs=in_specs,
        out_specs=[o_spec] * 7,
        out_shape=[o_shape] * 7,
        compiler_params=_cparams(("parallel", "arbitrary")),
        name="rwkv_prep",
    )(*args)


def _scan_kernel(r_ref, lw_ref, k_ref, v_ref, a_ref, b_ref, g_ref, rk_ref, lnw_ref, lnb_ref, ones_ref,
                 o_ref, h_sc, *, hd):
    C = r_ref.shape[1]
    W = r_ref.shape[2]
    c = pl.program_id(1)

    @pl.when(c == 0)
    def _():
        h_sc[...] = jnp.zeros_like(h_sc)

    r = r_ref[0]
    lw = lw_ref[0]
    k = k_ref[0]
    v = v_ref[0]
    a = a_ref[0]
    b = b_ref[0]

    cum = lw
    d = 1
    while d < C:
        cum = cum + _shift_down(cum, d)
        d *= 2
    tot = cum[C - 1:C, :]
    e_end = jnp.exp(tot - cum)
    e_neg = jnp.exp(-cum)
    at = a * jnp.exp(cum - lw)
    rt = r * jnp.exp(cum)
    bt = b * e_neg
    kt = k * e_neg
    bb = b * e_end
    kb = k * e_end
    p_c = jnp.exp(tot)

    lane = lax.broadcasted_iota(jnp.int32, (C, LANES), 1)
    first = lane < hd
    rr = lax.broadcasted_iota(jnp.int32, (2 * C, 2 * C), 0)
    cc = lax.broadcasted_iota(jnp.int32, (2 * C, 2 * C), 1)
    tr = jnp.where(rr >= C, rr - C, rr)
    tc = jnp.where(cc >= C, cc - C, cc)
    strict = tr > tc
    lower = tr >= tc
    eye = rr == cc

    def stack(x):
        return jnp.concatenate([jnp.where(first, x, 0.0), jnp.where(first, 0.0, x)], axis=0)

    ys = []
    for p in range(W // LANES):
        sl = slice(p * LANES, (p + 1) * LANES)
        s_a, s_r, s_b, s_k = stack(at[:, sl]), stack(rt[:, sl]), stack(bt[:, sl]), stack(kt[:, sl])
        s_bb, s_kb, s_v = stack(bb[:, sl]), stack(kb[:, sl]), stack(v[:, sl])
        n = jnp.where(strict, _dot_hp_nt(s_a, s_b), 0.0)
        a_ak = jnp.where(strict, _dot_hp_nt(s_a, s_k), 0.0)
        m_rb = jnp.where(lower, _dot_hp_nt(s_r, s_b), 0.0)
        m_rk = jnp.where(lower, _dot_hp_nt(s_r, s_k), 0.0)
        t_inv = jnp.where(eye, 1.0, 0.0) + n
        pw = n
        step = 2
        while step < C:
            pw = _dot_hp(pw, pw)
            t_inv = t_inv + _dot_hp(t_inv, pw)
            step *= 2
        a_hat = _dot_hp(t_inv, s_a)
        u0 = _dot_hp(t_inv, _dot_hp(a_ak, s_v))
        r_hat = s_r + _dot_hp(m_rb, a_hat)
        y0 = _dot_hp(m_rb, u0) + _dot_hp(m_rk, s_v)
        bb_t = s_bb.T
        g_mat = jnp.where(eye, p_c[:, sl], 0.0) + _dot_hp(bb_t, a_hat)
        h_add = _dot_hp(bb_t, u0) + _dot_hp(s_kb.T, s_v)
        h0 = h_sc[p]
        y = _dot_hp(r_hat, h0) + y0
        h_sc[p] = _dot_hp(g_mat, h0) + h_add
        ys.append(y[0:C] + y[C:2 * C])
    y = jnp.concatenate(ys, axis=1)

    ones = ones_ref[...]
    inv_hd = 1.0 / hd
    mean = _dot_exact_rhs(y, ones) * inv_hd
    yc = y - mean
    var = _dot_exact_rhs(yc * yc, ones) * inv_hd
    yn = yc * lax.rsqrt(var + RWKV_LNX_EPS) * lnw_ref[...] + lnb_ref[...]
    bonus = _dot_exact_rhs(r * k * rk_ref[...], ones) * v
    o_ref[0] = ((yn + bonus) * g_ref[0]).astype(o_ref.dtype)


def _scan_call(streams, r_k, lnx_w, lnx_b, ones_bd, B, Lp, hd):
    r, lw, k, v, a, b, g = streams
    rw = r.shape[-1]
    C = CHUNK
    assert 2 * hd == LANES and rw % LANES == 0 and Lp % C == 0
    vec = lambda x: x.reshape(1, -1)
    blk = pl.BlockSpec((1, C, rw), lambda bi, c: (bi, c, 0))
    par = pl.BlockSpec((1, rw), lambda bi, c: (0, 0))
    out = pl.pallas_call(
        functools.partial(_scan_kernel, hd=hd),
        grid=(B, Lp // C),
        in_specs=[blk] * 7 + [par] * 3 + [pl.BlockSpec((rw, rw), lambda bi, c: (0, 0))],
        out_specs=blk,
        out_shape=jax.ShapeDtypeStruct((B, Lp, rw), BF16),
        scratch_shapes=[pltpu.VMEM((rw // LANES, LANES, LANES), F32)],
        compiler_params=_cparams(("parallel", "arbitrary")),
        name="rwkv_scan",
    )(r, lw, k, v, a, b, g, vec(r_k), vec(lnx_w), vec(lnx_b), ones_bd)
    return out.reshape(B * Lp, rw)


def _merge_kernel(yp_ref, yd_ref, yr_ref, g0_ref, g1_ref, g2_ref, pa_ref, pb_ref, pc_ref, o_ref):
    dot = lambda x, w: jnp.dot(x[...], w[...], preferred_element_type=F32)
    m = _sigmoid(g0_ref[...].astype(F32)) * dot(yp_ref, pa_ref)
    m = m + _sigmoid(g1_ref[...].astype(F32)) * dot(yd_ref, pb_ref)
    m = m + _sigmoid(g2_ref[...].astype(F32)) * dot(yr_ref, pc_ref)
    o_ref[...] = m.astype(o_ref.dtype)


def _merge_call(yp, yd, yr, zg, pa, pb, pc, tm):
    M = yp.shape[0]
    D = pa.shape[1]
    rows = lambda a: pl.BlockSpec((tm, a.shape[1]), lambda i: (i, 0))
    full = lambda a: pl.BlockSpec(a.shape, lambda i: (0, 0))
    gate = lambda n: pl.BlockSpec((tm, D), lambda i: (i, n))
    return pl.pallas_call(
        _merge_kernel,
        grid=(M // tm,),
        in_specs=[rows(yp), rows(yd), rows(yr), gate(0), gate(1), gate(2), full(pa), full(pb), full(pc)],
        out_specs=pl.BlockSpec((tm, D), lambda i: (i, 0)),
        out_shape=jax.ShapeDtypeStruct((M, D), BF16),
        compiler_params=_cparams(("parallel",)),
        name="gated_merge",
    )(yp, yd, yr, zg, zg, zg, pa, pb, pc)


def _ffn_up_kernel(u_ref, wv_ref, wg_ref, cwv_ref, cwg_ref, cbv_ref, cbg_ref, o_ref):
    u = u_ref[...]

    def conv(w_ref, cw_ref, cb_ref):
        x = jnp.dot(u, w_ref[...], preferred_element_type=F32)
        c = cb_ref[...] + x * cw_ref[FFN_CONV - 1:FFN_CONV, :]
        for j in range(FFN_CONV - 1):
            c = c + _shift_down(x, FFN_CONV - 1 - j) * cw_ref[j:j + 1, :]
        return c

    val = conv(wv_ref, cwv_ref, cbv_ref)
    gate = conv(wg_ref, cwg_ref, cbg_ref)
    gelu = 0.5 * gate * (1.0 + jnp.tanh(math.sqrt(2.0 / math.pi) * (gate + 0.044715 * gate * gate * gate)))
    o_ref[...] = (gelu * val).astype(o_ref.dtype)


def _ffn_up_call(u, w_up, conv_w, conv_b, B, Lp, tn):
    M, D = u.shape
    FF = w_up.shape[1] // 2
    nt = FF // tn
    return pl.pallas_call(
        _ffn_up_kernel,
        grid=(B, nt),
        in_specs=[pl.BlockSpec((Lp, D), lambda b, j: (b, 0)),
                  pl.BlockSpec((D, tn), lambda b, j: (0, j)),
                  pl.BlockSpec((D, tn), lambda b, j: (0, nt + j)),
                  pl.BlockSpec((FFN_CONV, tn), lambda b, j: (0, j)),
                  pl.BlockSpec((FFN_CONV, tn), lambda b, j: (0, nt + j)),
                  pl.BlockSpec((1, tn), lambda b, j: (0, j)),
                  pl.BlockSpec((1, tn), lambda b, j: (0, nt + j))],
        out_specs=pl.BlockSpec((Lp, tn), lambda b, j: (b, j)),
        out_shape=jax.ShapeDtypeStruct((M, FF), BF16),
        compiler_params=_cparams(("parallel", "arbitrary")),
        name="ffn_up_conv_geglu",
    )(u, w_up, w_up, conv_w, conv_w, conv_b.reshape(1, -1), conv_b.reshape(1, -1))


def kernel(x, meta, rel_bias, norm_mix_pre, norm_mix_post, norm_ffn_pre, norm_ffn_post, w_in, pool_w, pool_scale, diff_lq1, diff_lk1, diff_lq2, diff_lk2, diff_subln, rwkv_mu, rwkv_w0, rwkv_w2, rwkv_a0, rwkv_a2, rwkv_g2, rwkv_kk, rwkv_ka, rwkv_rk, rwkv_lnx_w, rwkv_lnx_b, rwkv_v0, rwkv_v1, rwkv_v2, p_a, p_b, p_c, w_o, ffn_up, ffn_conv_w, ffn_conv_b, ffn_down):
    B, S, D = x.shape
    n_meta = meta.shape[0]
    depth = w_in.shape[0]
    L = S + n_meta
    Lp = _round_up(L, ATTN_TILE)
    M = B * Lp

    PW = pool_scale.shape[1]
    DW = p_b.shape[1]
    RW = rwkv_w0.shape[1]
    hd_r = rwkv_rk.shape[2]
    dl, al, gl = rwkv_w2.shape[1], rwkv_a2.shape[1], rwkv_g2.shape[1]
    off_q = PW
    off_r = PW + 3 * DW
    off_g = off_r + 3 * RW + dl + al + gl
    assert off_g + 3 * D == w_in.shape[2]
    assert max(dl, al, gl, rwkv_v1.shape[2]) <= LANES

    h = jnp.concatenate([jnp.broadcast_to(meta.astype(x.dtype)[None], (B, n_meta, D)), x,
                         jnp.zeros((B, Lp - L, D), x.dtype)], axis=1).reshape(M, D)

    hid = np.arange(RW) // hd_r
    ones_bd = jnp.asarray((hid[:, None] == hid[None, :]).astype(np.float32), dtype=BF16)

    def pad_lora(a, axis):
        segs = [lax.slice_in_dim(a, 0, 3 * RW, axis=axis)]
        o = 3 * RW
        for n in (dl, al, gl):
            seg = lax.slice_in_dim(a, o, o + n, axis=axis)
            pad = [(0, 0)] * a.ndim
            pad[axis] = (0, LANES - n)
            segs.append(jnp.pad(seg, pad))
            o += n
        return jnp.concatenate(segs, axis=axis)

    table = _bias_table_call(rel_bias, n_meta)

    tm_big = _pick_tile(M, 1088, 16)
    tm_res = _pick_tile(M, 544, 16)
    u = _rms_call(h, norm_mix_pre[0], tm_res)
    v_first = None
    for l in range(depth):
        lam_init = 0.8 - 0.6 * math.exp(-0.3 * l)
        wl = w_in[l]
        w_gate = wl[:, off_g:].astype(BF16)
        w_qkv = wl[:, off_q:off_r].astype(BF16)
        w_pool = wl[:, :PW].astype(BF16)
        w_rwkv = pad_lora(wl[:, off_r:off_g], 1).astype(BF16)
        mu_p = pad_lora(rwkv_mu[l], 0)

        zg = _mm_call(u, w_gate, BF16, tm_big, _pick_tile(3 * D, 768, LANES), "in_proj_gate")
        zqkv = _mm_call(u, w_qkv, BF16, tm_big, _pick_tile(3 * DW, 768, LANES), "in_proj_qkv")
        zp = _mm_call(u, w_pool, BF16, tm_big, _pick_tile(PW, 768, LANES), "in_proj_pool")
        zr = _mm_call(u, w_rwkv, F32, tm_big, _pick_tile(w_rwkv.shape[1], 768, LANES), "in_proj_rwkv")

        yp = _pool_call(zp, pool_w[l], pool_scale[l], B, Lp)
        yd = _attn_call(zqkv, table, rel_bias, diff_lq1[l], diff_lk1[l], diff_lq2[l], diff_lk2[l],
                        diff_subln[l], B, Lp, lam_init)
        v_res = None if l == 0 else (rwkv_v0[l - 1], rwkv_v1[l - 1], rwkv_v2[l - 1])
        streams = _rwkv_prep_call(zr, B, Lp, RW, mu_p, rwkv_w0[l], rwkv_w2[l], rwkv_a0[l], rwkv_a2[l],
                                  rwkv_g2[l], rwkv_kk[l], rwkv_ka[l], ones_bd, v_first, v_res)
        if l == 0:
            v_first = streams[3]
        yr = _scan_call(streams, rwkv_rk[l], rwkv_lnx_w[l], rwkv_lnx_b[l], ones_bd, B, Lp, hd_r)

        merged = _merge_call(yp, yd, yr, zg, p_a[l].astype(BF16), p_b[l].astype(BF16),
                             p_c[l].astype(BF16), tm_res)
        h, u2 = _mm_res_call(merged, w_o[l].astype(BF16), h, norm_mix_post[l], norm_ffn_pre[l],
                             tm_res, _pick_tile(D, 512, LANES), "out_proj_res")
        FF = ffn_down.shape[1]
        act = _ffn_up_call(u2, ffn_up[l].astype(BF16), ffn_conv_w[l], ffn_conv_b[l], B, Lp,
                           _pick_tile(FF, 256, LANES))
        g_next = norm_mix_pre[l + 1] if l + 1 < depth else norm_mix_pre[l]
        h, u = _mm_res_call(act, ffn_down[l].astype(BF16), h, norm_ffn_post[l], g_next,
                            tm_res, _pick_tile(FF, 512, LANES), "ffn_down_res")
    return h.reshape(B, Lp, D)[:, n_meta:n_meta + S]
```

```python
import functools
import math

import numpy as np
import jax
import jax.numpy as jnp
from jax import lax
from jax.experimental import pallas as pl
from jax.experimental.pallas import tpu as pltpu

F32 = jnp.float32
BF16 = jnp.bfloat16

CHUNK = 64
NORM_EPS = 1e-6
NEG_INF = -1e30
POOL_WINDOWS = (2, 4, 8, 16)
DIFF_SUBLN_EPS = 1e-5
REL_MAX_DIST = 128
RWKV_LNX_EPS = 64e-5
FFN_CONV = 3

LANES = 128
SUBLANES = 8
ATTN_TILE = 128
ATTN_Q_TILE = 512
VMEM_LIMIT = 56 * 1024 * 1024


def _cparams(sem):
    return pltpu.CompilerParams(dimension_semantics=sem, vmem_limit_bytes=VMEM_LIMIT)


def _round_up(x, m):
    return (x + m - 1) // m * m


def _pick_tile(total, target, mult):
    best = None
    for t in range(mult, min(total, target) + 1, mult):
        if total % t == 0:
            best = t
    assert best is not None, (total, target, mult)
    return best


def _shift_down(x, d):
    row = lax.broadcasted_iota(jnp.int32, x.shape, 0)
    return jnp.where(row >= d, pltpu.roll(x, d, 0), 0.0)


def _sigmoid(x):
    return 1.0 / (1.0 + jnp.exp(-x))


def _dot(a, b):
    return jnp.dot(a.astype(BF16), b.astype(BF16), preferred_element_type=F32)


def _dot_nt(a, b):
    return lax.dot_general(a.astype(BF16), b.astype(BF16), (((1,), (1,)), ((), ())),
                           preferred_element_type=F32)


def _split3(x):
    hi = x.astype(BF16)
    r1 = x - hi.astype(F32)
    mid = r1.astype(BF16)
    lo = (r1 - mid.astype(F32)).astype(BF16)
    return hi, mid, lo


def _dot_exact_rhs(x, ones_bf16):
    hi, mid, lo = _split3(x)
    acc = jnp.dot(hi, ones_bf16, preferred_element_type=F32)
    acc = acc + jnp.dot(mid, ones_bf16, preferred_element_type=F32)
    return acc + jnp.dot(lo, ones_bf16, preferred_element_type=F32)


def _rms_kernel(x_ref, g_ref, o_ref):
    x = x_ref[...]
    y = x * lax.rsqrt(jnp.mean(x * x, axis=-1, keepdims=True) + NORM_EPS)
    o_ref[...] = (y * g_ref[...]).astype(o_ref.dtype)


def _rms_call(x, g, tm):
    M, D = x.shape
    return pl.pallas_call(
        _rms_kernel,
        grid=(M // tm,),
        in_specs=[pl.BlockSpec((tm, D), lambda i: (i, 0)),
                  pl.BlockSpec((1, D), lambda i: (0, 0))],
        out_specs=pl.BlockSpec((tm, D), lambda i: (i, 0)),
        out_shape=jax.ShapeDtypeStruct((M, D), BF16),
        compiler_params=_cparams(("parallel",)),
        name="rms_norm",
    )(x, g.reshape(1, D))


def _mm_kernel(a_ref, b_ref, o_ref):
    o_ref[...] = jnp.dot(a_ref[...], b_ref[...], preferred_element_type=F32).astype(o_ref.dtype)


def _mm_call(a, b, out_dtype, tm, tn, name):
    M, K = a.shape
    N = b.shape[1]
    return pl.pallas_call(
        _mm_kernel,
        grid=(M // tm, N // tn),
        in_specs=[pl.BlockSpec((tm, K), lambda i, j: (i, 0)),
                  pl.BlockSpec((K, tn), lambda i, j: (0, j))],
        out_specs=pl.BlockSpec((tm, tn), lambda i, j: (i, j)),
        out_shape=jax.ShapeDtypeStruct((M, N), out_dtype),
        compiler_params=_cparams(("parallel", "arbitrary")),
        name=name,
    )(a, b)


def _mm_res_kernel(a_ref, w_ref, h_ref, gp_ref, gn_ref, oh_ref, ou_ref, acc_ref):
    k = pl.program_id(1)

    @pl.when(k == 0)
    def _():
        acc_ref[...] = jnp.zeros_like(acc_ref)

    acc_ref[...] += jnp.dot(a_ref[...], w_ref[...], preferred_element_type=F32)

    @pl.when(k == pl.num_programs(1) - 1)
    def _():
        f = acc_ref[...]
        y = f * lax.rsqrt(jnp.mean(f * f, axis=-1, keepdims=True) + NORM_EPS) * gp_ref[...]
        hn = h_ref[...] + y
        oh_ref[...] = hn
        u = hn * lax.rsqrt(jnp.mean(hn * hn, axis=-1, keepdims=True) + NORM_EPS) * gn_ref[...]
        ou_ref[...] = u.astype(ou_ref.dtype)


def _mm_res_call(a, w, h, g_post, g_next, tm, tk, name):
    M, K = a.shape
    D = w.shape[1]
    return pl.pallas_call(
        _mm_res_kernel,
        grid=(M // tm, K // tk),
        in_specs=[pl.BlockSpec((tm, tk), lambda i, k: (i, k)),
                  pl.BlockSpec((tk, D), lambda i, k: (k, 0)),
                  pl.BlockSpec((tm, D), lambda i, k: (i, 0)),
                  pl.BlockSpec((1, D), lambda i, k: (0, 0)),
                  pl.BlockSpec((1, D), lambda i, k: (0, 0))],
        out_specs=[pl.BlockSpec((tm, D), lambda i, k: (i, 0)),
                   pl.BlockSpec((tm, D), lambda i, k: (i, 0))],
        out_shape=[jax.ShapeDtypeStruct((M, D), F32),
                   jax.ShapeDtypeStruct((M, D), BF16)],
        scratch_shapes=[pltpu.VMEM((tm, D), F32)],
        compiler_params=_cparams(("parallel", "arbitrary")),
        name=name,
    )(a, w, h, g_post.reshape(1, D), g_next.reshape(1, D))


def _pool_kernel(z_ref, w_ref, s_ref, o_ref, *, gw):
    L = z_ref.shape[1]
    row = lax.broadcasted_iota(jnp.int32, (L, gw), 0)
    for gi, win in enumerate(POOL_WINDOWS):
        sl = slice(gi * gw, (gi + 1) * gw)
        zg = z_ref[0, :, sl].astype(F32)
        s = zg
        d = 1
        while d < win:
            s = s + _shift_down(s, d)
            d *= 2
        cnt = jnp.minimum(row + 1, win).astype(F32)
        pooled = s / cnt - zg
        y = _dot(pooled, w_ref[gi]) * s_ref[:, sl]
        o_ref[0, :, sl] = y.astype(o_ref.dtype)


def _pool_call(z, w, scale, B, Lp):
    PW = z.shape[-1]
    G = len(POOL_WINDOWS)
    gw = PW // G
    z3 = z.reshape(B, Lp, PW)
    out = pl.pallas_call(
        functools.partial(_pool_kernel, gw=gw),
        grid=(B,),
        in_specs=[pl.BlockSpec((1, Lp, PW), lambda b: (b, 0, 0)),
                  pl.BlockSpec((G, gw, gw), lambda b: (0, 0, 0)),
                  pl.BlockSpec((1, PW), lambda b: (0, 0))],
        out_specs=pl.BlockSpec((1, Lp, PW), lambda b: (b, 0, 0)),
        out_shape=jax.ShapeDtypeStruct((B, Lp, PW), BF16),
        compiler_params=_cparams(("parallel",)),
        name="pool_mixer",
    )(z3, w.astype(BF16), scale.reshape(1, PW))
    return out.reshape(B * Lp, PW)


def _rel_bucket(rel, n_buckets):
    half = n_buckets // 2
    max_exact = half // 2
    n = np.abs(rel)
    large = max_exact + (np.log(np.maximum(n, 1) / max_exact) / math.log(REL_MAX_DIST / max_exact)
                         * (half - max_exact)).astype(np.int32)
    large = np.minimum(large, half - 1)
    return ((rel > 0) * half + np.where(n < max_exact, n, large)).astype(np.int32)


def _bias_table_kernel(rb_ref, bucket_ref, pen_ref, o_ref, *, n_buckets):
    s = pl.program_id(0)
    bucket = bucket_ref[...]
    acc = pen_ref[...]
    for b in range(n_buckets):
        acc = acc + jnp.where(bucket == b, rb_ref[b, s], 0.0)
    o_ref[0] = acc


def _bias_table_call(rel_bias, n_meta, tq):
    n_buckets, n_sub = rel_bias.shape
    KT = ATTN_TILE
    assert tq % KT == 0 and KT % CHUNK == 0 and n_meta <= KT
    n_near = tq // KT + 2
    n_slabs = n_near + 2
    far = n_buckets // 2 - 1
    r = np.arange(tq)[:, None]
    c = np.arange(KT)[None, :]
    assert np.all(_rel_bucket(-(KT + 1 + np.arange(4 * KT)), n_buckets) == far)
    bucket = np.full((n_slabs, tq, KT), far, np.int32)
    pen = np.zeros((n_slabs, tq, KT), np.float32)
    pen[-1] = NEG_INF
    for idx in range(n_near):
        key = (idx - 1) * KT + c
        bucket[1 + idx] = _rel_bucket(key - r, n_buckets)
        visible = np.floor_divide(key - n_meta, CHUNK) <= np.floor_divide(r - n_meta, CHUNK)
        pen[1 + idx] = np.where(visible, 0.0, NEG_INF)
    blk = pl.BlockSpec((n_slabs, tq, KT), lambda s: (0, 0, 0))
    return pl.pallas_call(
        functools.partial(_bias_table_kernel, n_buckets=n_buckets),
        grid=(n_sub,),
        in_specs=[pl.BlockSpec(memory_space=pltpu.SMEM), blk, blk],
        out_specs=pl.BlockSpec((1, n_slabs, tq, KT), lambda s: (s, 0, 0, 0)),
        out_shape=jax.ShapeDtypeStruct((n_sub, n_slabs, tq, KT), F32),
        compiler_params=_cparams(("arbitrary",)),
        name="bias_table_%d" % tq,
    )(rel_bias, jnp.asarray(bucket), jnp.asarray(pen))


def _attn_kernel(*refs, hd, scale, lam_init, nt, tile0, aliased):
    if aliased:
        refs = refs[1:]
    (lq1_ref, lk1_ref, lq2_ref, lk2_ref, sub_ref, q_ref, k_ref, v_ref, w_ref,
     o_ref, s_sc, mx_sc, l_sc, acc_sc) = refs
    KT = ATTN_TILE
    tq = q_ref.shape[1]
    ratio = tq // KT
    n_slabs = w_ref.shape[1]
    i = pl.program_id(2)
    jt0 = tile0 + i * ratio
    n_need = jnp.minimum(jt0 + ratio + 1, nt)
    n_pairs = (n_need + 1) // 2
    q = q_ref[0]

    def slab(j):
        return jnp.where(j < nt, jnp.clip(j - jt0 + 2, 0, n_slabs - 1), n_slabs - 1)

    def two_tiles(ref, jj):
        j0 = 2 * jj
        j1 = jnp.minimum(j0 + 1, nt - 1)
        t0 = ref[0, pl.ds(pl.multiple_of(j0 * KT, KT), KT), :]
        t1 = ref[0, pl.ds(pl.multiple_of(j1 * KT, KT), KT), :]
        return jnp.concatenate([t0, t1], axis=0)

    mx_sc[...] = jnp.full_like(mx_sc, NEG_INF)

    def pass1(jj, carry):
        kt = two_tiles(k_ref, jj)
        i0 = slab(2 * jj)
        i1 = slab(2 * jj + 1)
        for s in range(2):
            sc = _dot_nt(q[:, s * hd:(s + 1) * hd], kt[:, s * hd:(s + 1) * hd]) * scale
            sc = sc + jnp.concatenate([w_ref[s, i0], w_ref[s, i1]], axis=1)
            s_sc[s, jj] = sc
            mx_sc[s] = jnp.maximum(mx_sc[s], jnp.maximum(sc[:, :KT], sc[:, KT:]))
        return carry

    lax.fori_loop(0, n_pairs, pass1, 0)
    m = [jnp.max(mx_sc[s], axis=-1, keepdims=True) for s in range(2)]
    l_sc[...] = jnp.zeros_like(l_sc)
    acc_sc[...] = jnp.zeros_like(acc_sc)

    def pass2(jj, carry):
        vt = two_tiles(v_ref, jj)
        for s in range(2):
            p = jnp.exp(s_sc[s, jj] - m[s])
            l_sc[s] += p[:, :KT] + p[:, KT:]
            acc_sc[s] += jnp.dot(p.astype(BF16), vt, preferred_element_type=F32)
        return carry

    lax.fori_loop(0, n_pairs, pass2, 0)
    l = [jnp.sum(l_sc[s], axis=-1, keepdims=True) for s in range(2)]
    lam = (jnp.exp(jnp.sum(lq1_ref[...] * lk1_ref[...], axis=-1, keepdims=True))
           - jnp.exp(jnp.sum(lq2_ref[...] * lk2_ref[...], axis=-1, keepdims=True)) + lam_init)
    o = acc_sc[0] / l[0] - lam * (acc_sc[1] / l[1])
    y = o * lax.rsqrt(jnp.mean(o * o, axis=-1, keepdims=True) + DIFF_SUBLN_EPS)
    o_ref[0] = (y * sub_ref[...] * (1.0 - lam_init)).astype(o_ref.dtype)


def _attn_segment(prev, qkv3, table, params, lam_init, tq, row0, n_tiles):
    B, Lp, W3 = qkv3.shape
    DW = W3 // 3
    hd = params[0].shape[1]
    hw = 2 * hd
    n_heads = DW // hw
    KT = ATTN_TILE
    nt = Lp // KT
    blk0 = row0 // tq
    aliased = prev is not None
    kern = functools.partial(_attn_kernel, hd=hd, scale=hd ** -0.5, lam_init=lam_init, nt=nt,
                             tile0=row0 // KT, aliased=aliased)
    par = lambda n: pl.BlockSpec((1, n), lambda b, h, i: (0, 0))
    in_specs = [par(hd)] * 4 + [
        par(hw),
        pl.BlockSpec((1, tq, hw), lambda b, h, i: (b, blk0 + i, h)),
        pl.BlockSpec((1, Lp, hw), lambda b, h, i: (b, 0, n_heads + h)),
        pl.BlockSpec((1, Lp, hw), lambda b, h, i: (b, 0, 2 * n_heads + h)),
        pl.BlockSpec((2,) + table.shape[1:], lambda b, h, i: (h, 0, 0, 0))]
    args = list(params) + [qkv3, qkv3, qkv3, table]
    if aliased:
        in_specs = [pl.BlockSpec(memory_space=pl.ANY)] + in_specs
        args = [prev] + args
    max_pairs = (nt + 1) // 2
    return pl.pallas_call(
        kern,
        grid=(B, n_heads, n_tiles),
        in_specs=in_specs,
        out_specs=pl.BlockSpec((1, tq, hw), lambda b, h, i: (b, blk0 + i, h)),
        out_shape=jax.ShapeDtypeStruct((B, Lp, DW), BF16),
        scratch_shapes=[pltpu.VMEM((2, max_pairs, tq, 2 * KT), F32), pltpu.VMEM((2, tq, KT), F32),
                        pltpu.VMEM((2, tq, KT), F32), pltpu.VMEM((2, tq, hw), F32)],
        input_output_aliases={0: 0} if aliased else {},
        compiler_params=_cparams(("parallel", "parallel", "arbitrary")),
        name="diff_attention_%d" % tq,
    )(*args)


def _attn_segments(Lp):
    big = ATTN_Q_TILE
    n_big = Lp // big
    segs = []
    if n_big:
        segs.append((big, 0, n_big))
    rest = Lp - n_big * big
    if rest:
        segs.append((ATTN_TILE, n_big * big, rest // ATTN_TILE))
    return segs


def _attn_call(zqkv, tables, lq1, lk1, lq2, lk2, subln, B, Lp, lam_init):
    vec = lambda a: a.reshape(1, -1)
    params = [vec(lq1), vec(lk1), vec(lq2), vec(lk2), vec(subln)]
    qkv3 = zqkv.reshape(B, Lp, zqkv.shape[-1])
    out = None
    for tq, row0, n_tiles in _attn_segments(Lp):
        out = _attn_segment(out, qkv3, tables[tq], params, lam_init, tq, row0, n_tiles)
    return out.reshape(B * Lp, -1)


def _rwkv_prep_kernel(*refs, rw, has_vres):
    if has_vres:
        (z_ref, zp_ref, mu_ref, w0_ref, w2_ref, a0_ref, a2_ref, g2_ref, kk_ref, ka_ref, ones_ref,
         vf_ref, v0_ref, v1_ref, v2_ref,
         r_o, lw_o, k_o, v_o, a_o, b_o, g_o) = refs
    else:
        (z_ref, zp_ref, mu_ref, w0_ref, w2_ref, a0_ref, a2_ref, g2_ref, kk_ref, ka_ref, ones_ref,
         r_o, lw_o, k_o, v_o, a_o, b_o, g_o) = refs
    i = pl.program_id(1)
    z = z_ref[0]
    last_prev = zp_ref[0, SUBLANES - 1:SUBLANES, :]
    last_prev = jnp.where(i > 0, last_prev, 0.0)
    row = lax.broadcasted_iota(jnp.int32, z.shape, 0)
    prev = jnp.where(row >= 1, pltpu.roll(z, 1, 0), last_prev)
    zs = z + (prev - z) * mu_ref[...]
    r = zs[:, 0:rw]
    kx = zs[:, rw:2 * rw]
    vx = zs[:, 2 * rw:3 * rw]
    wd = zs[:, 3 * rw:3 * rw + LANES]
    ad = zs[:, 3 * rw + LANES:3 * rw + 2 * LANES]
    gd = zs[:, 3 * rw + 2 * LANES:3 * rw + 3 * LANES]
    wl = w0_ref[...] + _dot(jnp.tanh(wd), w2_ref[...])
    sp = jnp.maximum(-wl, 0.0) + jnp.log(1.0 + jnp.exp(-jnp.abs(wl)))
    lw = -jnp.exp(-sp - 0.5)
    if has_vres:
        gate_v = _sigmoid(v0_ref[...] + _dot(_dot(vx, v1_ref[...]), v2_ref[...]))
        vx = vx + (vf_ref[0] - vx) * gate_v
    a = _sigmoid(a0_ref[...] + _dot(ad, a2_ref[...]))
    g = _dot(_sigmoid(gd), g2_ref[...])
    kk = kx * kk_ref[...]
    ss = _dot_exact_rhs(kk * kk, ones_ref[...])
    kk = kk * lax.rsqrt(jnp.maximum(ss, 1e-24))
    kmod = kx * (1.0 + (a - 1.0) * ka_ref[...])
    r_o[0] = r
    lw_o[0] = lw
    k_o[0] = kmod
    v_o[0] = vx
    a_o[0] = -kk
    b_o[0] = kk * a
    g_o[0] = g


def _pad_rows(w, n):
    return jnp.pad(w, ((0, n - w.shape[0]), (0, 0)))


def _rwkv_prep_call(zr, B, Lp, rw, mu_p, w0, w2, a0, a2, g2, k_k, k_a, ones_bd, v_first, v_res):
    ZW = zr.shape[-1]
    z3 = zr.reshape(B, Lp, ZW)
    tl = _pick_tile(Lp, 384, SUBLANES)
    nb = tl // SUBLANES
    has_vres = v_res is not None
    vec = lambda a: a.reshape(1, -1)
    full = lambda a: pl.BlockSpec(a.shape, lambda b, i: (0,) * a.ndim)
    w2p = _pad_rows(w2, LANES).astype(BF16)
    a2p = _pad_rows(a2, LANES).astype(BF16)
    g2p = _pad_rows(g2, LANES).astype(BF16)
    args = [z3, z3, vec(mu_p), vec(w0), w2p, vec(a0), a2p, g2p, vec(k_k), vec(k_a), ones_bd]
    in_specs = [pl.BlockSpec((1, tl, ZW), lambda b, i: (b, i, 0)),
                pl.BlockSpec((1, SUBLANES, ZW), lambda b, i: (b, jnp.maximum(i * nb - 1, 0), 0))]
    in_specs += [full(a) for a in args[2:]]
    if has_vres:
        v0, v1, v2 = v_res
        v1p = jnp.pad(v1, ((0, 0), (0, LANES - v1.shape[1]))).astype(BF16)
        v2p = _pad_rows(v2, LANES).astype(BF16)
        extra = [v_first, vec(v0), v1p, v2p]
        args += extra
        in_specs += [pl.BlockSpec((1, tl, rw), lambda b, i: (b, i, 0))] + [full(a) for a in extra[1:]]
    o_spec = pl.BlockSpec((1, tl, rw), lambda b, i: (b, i, 0))
    o_shape = jax.ShapeDtypeStruct((B, Lp, rw), F32)
    return pl.pallas_call(
        functools.partial(_rwkv_prep_kernel, rw=rw, has_vres=has_vres),
        grid=(B, Lp // tl),
        in_specs=in_specs,
        out_specs=[o_spec] * 7,
        out_shape=[o_shape] * 7,
        compiler_params=_cparams(("parallel", "arbitrary")),
        name="rwkv_prep",
    )(*args)


def _scan_kernel(r_ref, lw_ref, k_ref, v_ref, a_ref, b_ref, g_ref, rk_ref, lnw_ref, lnb_ref, ones_ref,
                 o_ref, h_sc, *, hd):
    C = r_ref.shape[1]
    C2 = 2 * C
    W = r_ref.shape[2]
    c = pl.program_id(1)

    @pl.when(c == 0)
    def _():
        h_sc[...] = jnp.zeros_like(h_sc)

    r = r_ref[0]
    lw = lw_ref[0]
    k = k_ref[0]
    v = v_ref[0]
    a = a_ref[0]
    b = b_ref[0]

    cum = lw
    d = 1
    while d < C:
        cum = cum + _shift_down(cum, d)
        d *= 2
    tot = cum[C - 1:C, :]
    e_end = jnp.exp(tot - cum)
    e_neg = jnp.exp(-cum)
    at = a * jnp.exp(cum - lw)
    rt = r * jnp.exp(cum)
    bt = b * e_neg
    kt = k * e_neg
    bb = b * e_end
    kb = k * e_end
    p_c = jnp.exp(tot)

    lane = lax.broadcasted_iota(jnp.int32, (C, LANES), 1)
    first = lane < hd
    rr = lax.broadcasted_iota(jnp.int32, (C2, C2), 0)
    cc = lax.broadcasted_iota(jnp.int32, (C2, C2), 1)
    tr = jnp.where(rr >= C, rr - C, rr)
    tc = jnp.where(cc >= C, cc - C, cc)
    strict = tr > tc
    lower = tr >= tc
    eye = rr == cc

    def stack(x):
        return jnp.concatenate([jnp.where(first, x, 0.0), jnp.where(first, 0.0, x)], axis=0)

    ys = []
    for p in range(W // LANES):
        sl = slice(p * LANES, (p + 1) * LANES)
        s_a, s_r, s_b, s_k = stack(at[:, sl]), stack(rt[:, sl]), stack(bt[:, sl]), stack(kt[:, sl])
        s_bb, s_kb, s_v = stack(bb[:, sl]), stack(kb[:, sl]), stack(v[:, sl])
        sc = _dot_nt(jnp.concatenate([s_a, s_r], axis=0), jnp.concatenate([s_b, s_k], axis=0))
        n = jnp.where(strict, sc[:C2, :C2], 0.0)
        a_ak = jnp.where(strict, sc[:C2, C2:], 0.0)
        m_rb = jnp.where(lower, sc[C2:, :C2], 0.0)
        m_rk = jnp.where(lower, sc[C2:, C2:], 0.0)
        t_inv = jnp.where(eye, 1.0, 0.0) + n
        pw = n
        step = 2
        while step < C:
            pw = _dot(pw, pw)
            t_inv = t_inv + _dot(t_inv, pw)
            step *= 2
        au = _dot(t_inv, jnp.concatenate([s_a, _dot(a_ak, s_v)], axis=1))
        mb = _dot(m_rb, au)
        r_hat = s_r + mb[:, :LANES]
        y0 = mb[:, LANES:] + _dot(m_rk, s_v)
        bu = _dot(s_bb.T, au)
        g_mat = jnp.where(eye, p_c[:, sl], 0.0) + bu[:, :LANES]
        h_add = bu[:, LANES:] + _dot(s_kb.T, s_v)
        yh = _dot(jnp.concatenate([r_hat, g_mat], axis=0), h_sc[p])
        y = yh[:C2] + y0
        h_sc[p] = yh[C2:] + h_add
        ys.append(y[0:C] + y[C:C2])
    y = jnp.concatenate(ys, axis=1)

    ones = ones_ref[...]
    inv_hd = 1.0 / hd
    mean = _dot_exact_rhs(y, ones) * inv_hd
    yc = y - mean
    var = _dot_exact_rhs(yc * yc, ones) * inv_hd
    yn = yc * lax.rsqrt(var + RWKV_LNX_EPS) * lnw_ref[...] + lnb_ref[...]
    bonus = _dot_exact_rhs(r * k * rk_ref[...], ones) * v
    o_ref[0] = ((yn + bonus) * g_ref[0]).astype(o_ref.dtype)


def _scan_call(streams, r_k, lnx_w, lnx_b, ones_bd, B, Lp, hd):
    r, lw, k, v, a, b, g = streams
    rw = r.shape[-1]
    C = CHUNK
    assert 2 * hd == LANES and rw % LANES == 0 and Lp % C == 0
    vec = lambda x: x.reshape(1, -1)
    blk = pl.BlockSpec((1, C, rw), lambda bi, c: (bi, c, 0))
    par = pl.BlockSpec((1, rw), lambda bi, c: (0, 0))
    out = pl.pallas_call(
        functools.partial(_scan_kernel, hd=hd),
        grid=(B, Lp // C),
        in_specs=[blk] * 7 + [par] * 3 + [pl.BlockSpec((rw, rw), lambda bi, c: (0, 0))],
        out_specs=blk,
        out_shape=jax.ShapeDtypeStruct((B, Lp, rw), BF16),
        scratch_shapes=[pltpu.VMEM((rw // LANES, LANES, LANES), F32)],
        compiler_params=_cparams(("parallel", "arbitrary")),
        name="rwkv_scan",
    )(r, lw, k, v, a, b, g, vec(r_k), vec(lnx_w), vec(lnx_b), ones_bd)
    return out.reshape(B * Lp, rw)


def _merge_kernel(yp_ref, yd_ref, yr_ref, g0_ref, g1_ref, g2_ref, pa_ref, pb_ref, pc_ref, o_ref):
    dot = lambda x, w: jnp.dot(x[...], w[...], preferred_element_type=F32)
    m = _sigmoid(g0_ref[...].astype(F32)) * dot(yp_ref, pa_ref)
    m = m + _sigmoid(g1_ref[...].astype(F32)) * dot(yd_ref, pb_ref)
    m = m + _sigmoid(g2_ref[...].astype(F32)) * dot(yr_ref, pc_ref)
    o_ref[...] = m.astype(o_ref.dtype)


def _merge_call(yp, yd, yr, zg, pa, pb, pc, tm):
    M = yp.shape[0]
    D = pa.shape[1]
    rows = lambda a: pl.BlockSpec((tm, a.shape[1]), lambda i: (i, 0))
    full = lambda a: pl.BlockSpec(a.shape, lambda i: (0, 0))
    gate = lambda n: pl.BlockSpec((tm, D), lambda i: (i, n))
    return pl.pallas_call(
        _merge_kernel,
        grid=(M // tm,),
        in_specs=[rows(yp), rows(yd), rows(yr), gate(0), gate(1), gate(2), full(pa), full(pb), full(pc)],
        out_specs=pl.BlockSpec((tm, D), lambda i: (i, 0)),
        out_shape=jax.ShapeDtypeStruct((M, D), BF16),
        compiler_params=_cparams(("parallel",)),
        name="gated_merge",
    )(yp, yd, yr, zg, zg, zg, pa, pb, pc)


def _ffn_up_kernel(u_ref, wv_ref, wg_ref, cwv_ref, cwg_ref, cbv_ref, cbg_ref, o_ref):
    u = u_ref[...]

    def conv(w_ref, cw_ref, cb_ref):
        x = jnp.dot(u, w_ref[...], preferred_element_type=F32)
        c = cb_ref[...] + x * cw_ref[FFN_CONV - 1:FFN_CONV, :]
        for j in range(FFN_CONV - 1):
            c = c + _shift_down(x, FFN_CONV - 1 - j) * cw_ref[j:j + 1, :]
        return c

    val = conv(wv_ref, cwv_ref, cbv_ref)
    gate = conv(wg_ref, cwg_ref, cbg_ref)
    gelu = 0.5 * gate * (1.0 + jnp.tanh(math.sqrt(2.0 / math.pi) * (gate + 0.044715 * gate * gate * gate)))
    o_ref[...] = (gelu * val).astype(o_ref.dtype)


def _ffn_up_call(u, w_up, conv_w, conv_b, B, Lp, tn):
    M, D = u.shape
    FF = w_up.shape[1] // 2
    nt = FF // tn
    return pl.pallas_call(
        _ffn_up_kernel,
        grid=(B, nt),
        in_specs=[pl.BlockSpec((Lp, D), lambda b, j: (b, 0)),
                  pl.BlockSpec((D, tn), lambda b, j: (0, j)),
                  pl.BlockSpec((D, tn), lambda b, j: (0, nt + j)),
                  pl.BlockSpec((FFN_CONV, tn), lambda b, j: (0, j)),
                  pl.BlockSpec((FFN_CONV, tn), lambda b, j: (0, nt + j)),
                  pl.BlockSpec((1, tn), lambda b, j: (0, j)),
                  pl.BlockSpec((1, tn), lambda b, j: (0, nt + j))],
        out_specs=pl.BlockSpec((Lp, tn), lambda b, j: (b, j)),
        out_shape=jax.ShapeDtypeStruct((M, FF), BF16),
        compiler_params=_cparams(("parallel", "arbitrary")),
        name="ffn_up_conv_geglu",
    )(u, w_up, w_up, conv_w, conv_w, conv_b.reshape(1, -1), conv_b.reshape(1, -1))


def kernel(x, meta, rel_bias, norm_mix_pre, norm_mix_post, norm_ffn_pre, norm_ffn_post, w_in, pool_w, pool_scale, diff_lq1, diff_lk1, diff_lq2, diff_lk2, diff_subln, rwkv_mu, rwkv_w0, rwkv_w2, rwkv_a0, rwkv_a2, rwkv_g2, rwkv_kk, rwkv_ka, rwkv_rk, rwkv_lnx_w, rwkv_lnx_b, rwkv_v0, rwkv_v1, rwkv_v2, p_a, p_b, p_c, w_o, ffn_up, ffn_conv_w, ffn_conv_b, ffn_down):
    B, S, D = x.shape
    n_meta = meta.shape[0]
    depth = w_in.shape[0]
    L = S + n_meta
    Lp = _round_up(L, ATTN_TILE)
    M = B * Lp

    PW = pool_scale.shape[1]
    DW = p_b.shape[1]
    RW = rwkv_w0.shape[1]
    hd_r = rwkv_rk.shape[2]
    dl, al, gl = rwkv_w2.shape[1], rwkv_a2.shape[1], rwkv_g2.shape[1]
    off_q = PW
    off_r = PW + 3 * DW
    off_g = off_r + 3 * RW + dl + al + gl
    assert off_g + 3 * D == w_in.shape[2]
    assert max(dl, al, gl, rwkv_v1.shape[2]) <= LANES

    h = jnp.concatenate([jnp.broadcast_to(meta.astype(x.dtype)[None], (B, n_meta, D)), x,
                         jnp.zeros((B, Lp - L, D), x.dtype)], axis=1).reshape(M, D)

    hid = np.arange(RW) // hd_r
    ones_bd = jnp.asarray((hid[:, None] == hid[None, :]).astype(np.float32), dtype=BF16)

    def pad_lora(a, axis):
        segs = [lax.slice_in_dim(a, 0, 3 * RW, axis=axis)]
        o = 3 * RW
        for n in (dl, al, gl):
            seg = lax.slice_in_dim(a, o, o + n, axis=axis)
            pad = [(0, 0)] * a.ndim
            pad[axis] = (0, LANES - n)
            segs.append(jnp.pad(seg, pad))
            o += n
        return jnp.concatenate(segs, axis=axis)

    tables = {tq: _bias_table_call(rel_bias, n_meta, tq) for tq, _, _ in _attn_segments(Lp)}

    tm_big = _pick_tile(M, 1088, 16)
    tm_res = _pick_tile(M, 544, 16)
    u = _rms_call(h, norm_mix_pre[0], tm_res)
    v_first = None
    for l in range(depth):
        lam_init = 0.8 - 0.6 * math.exp(-0.3 * l)
        wl = w_in[l]
        w_gate = wl[:, off_g:].astype(BF16)
        w_qkv = wl[:, off_q:off_r].astype(BF16)
        w_pool = wl[:, :PW].astype(BF16)
        w_rwkv = pad_lora(wl[:, off_r:off_g], 1).astype(BF16)
        mu_p = pad_lora(rwkv_mu[l], 0)

        zg = _mm_call(u, w_gate, BF16, tm_big, _pick_tile(3 * D, 768, LANES), "in_proj_gate")
        zqkv = _mm_call(u, w_qkv, BF16, tm_big, _pick_tile(3 * DW, 768, LANES), "in_proj_qkv")
        zp = _mm_call(u, w_pool, BF16, tm_big, _pick_tile(PW, 768, LANES), "in_proj_pool")
        zr = _mm_call(u, w_rwkv, F32, tm_big, _pick_tile(w_rwkv.shape[1], 768, LANES), "in_proj_rwkv")

        yp = _pool_call(zp, pool_w[l], pool_scale[l], B, Lp)
        yd = _attn_call(zqkv, tables, diff_lq1[l], diff_lk1[l], diff_lq2[l], diff_lk2[l],
                        diff_subln[l], B, Lp, lam_init)
        v_res = None if l == 0 else (rwkv_v0[l - 1], rwkv_v1[l - 1], rwkv_v2[l - 1])
        streams = _rwkv_prep_call(zr, B, Lp, RW, mu_p, rwkv_w0[l], rwkv_w2[l], rwkv_a0[l], rwkv_a2[l],
                                  rwkv_g2[l], rwkv_kk[l], rwkv_ka[l], ones_bd, v_first, v_res)
        if l == 0:
            v_first = streams[3]
        yr = _scan_call(streams, rwkv_rk[l], rwkv_lnx_w[l], rwkv_lnx_b[l], ones_bd, B, Lp, hd_r)

        merged = _merge_call(yp, yd, yr, zg, p_a[l].astype(BF16), p_b[l].astype(BF16),
                             p_c[l].astype(BF16), tm_res)
        h, u2 = _mm_res_call(merged, w_o[l].astype(BF16), h, norm_mix_post[l], norm_ffn_pre[l],
                             tm_res, _pick_tile(D, 512, LANES), "out_proj_res")
        FF = ffn_down.shape[1]
        act = _ffn_up_call(u2, ffn_up[l].astype(BF16), ffn_conv_w[l], ffn_conv_b[l], B, Lp,
                           _pick_tile(FF, 256, LANES))
        g_next = norm_mix_pre[l + 1] if l + 1 < depth else norm_mix_pre[l]
        h, u = _mm_res_call(act, ffn_down[l].astype(BF16), h, norm_ffn_post[l], g_next,
                            tm_res, _pick_tile(FF, 512, LANES), "ffn_down_res")
    return h.reshape(B, Lp, D)[:, n_meta:n_meta + S]
```

```python
import functools
import math

import numpy as np
import jax
import jax.numpy as jnp
from jax import lax
from jax.experimental import pallas as pl
from jax.experimental.pallas import tpu as pltpu

F32 = jnp.float32
BF16 = jnp.bfloat16

CHUNK = 64
NORM_EPS = 1e-6
NEG_INF = -1e30
POOL_WINDOWS = (2, 4, 8, 16)
DIFF_SUBLN_EPS = 1e-5
REL_MAX_DIST = 128
RWKV_LNX_EPS = 64e-5
FFN_CONV = 3

LANES = 128
SUBLANES = 8
ATTN_TILE = 128
ATTN_Q_TILE = 512
VMEM_LIMIT = 56 * 1024 * 1024


def _cparams(sem):
    return pltpu.CompilerParams(dimension_semantics=sem, vmem_limit_bytes=VMEM_LIMIT)


def _round_up(x, m):
    return (x + m - 1) // m * m


def _pick_tile(total, target, mult):
    best = None
    for t in range(mult, min(total, target) + 1, mult):
        if total % t == 0:
            best = t
    assert best is not None, (total, target, mult)
    return best


def _shift_down(x, d):
    row = lax.broadcasted_iota(jnp.int32, x.shape, 0)
    return jnp.where(row >= d, pltpu.roll(x, d, 0), 0.0)


def _sigmoid(x):
    return 1.0 / (1.0 + jnp.exp(-x))


def _dot(a, b):
    return jnp.dot(a.astype(BF16), b.astype(BF16), preferred_element_type=F32)


def _dot_nt(a, b):
    return lax.dot_general(a.astype(BF16), b.astype(BF16), (((1,), (1,)), ((), ())),
                           preferred_element_type=F32)


def _split3(x):
    hi = x.astype(BF16)
    r1 = x - hi.astype(F32)
    mid = r1.astype(BF16)
    lo = (r1 - mid.astype(F32)).astype(BF16)
    return hi, mid, lo


def _dot_exact_rhs(x, ones_bf16):
    hi, mid, lo = _split3(x)
    acc = jnp.dot(hi, ones_bf16, preferred_element_type=F32)
    acc = acc + jnp.dot(mid, ones_bf16, preferred_element_type=F32)
    return acc + jnp.dot(lo, ones_bf16, preferred_element_type=F32)


def _rms_kernel(x_ref, g_ref, o_ref):
    x = x_ref[...]
    y = x * lax.rsqrt(jnp.mean(x * x, axis=-1, keepdims=True) + NORM_EPS)
    o_ref[...] = (y * g_ref[...]).astype(o_ref.dtype)


def _rms_call(x, g, tm):
    M, D = x.shape
    return pl.pallas_call(
        _rms_kernel,
        grid=(M // tm,),
        in_specs=[pl.BlockSpec((tm, D), lambda i: (i, 0)),
                  pl.BlockSpec((1, D), lambda i: (0, 0))],
        out_specs=pl.BlockSpec((tm, D), lambda i: (i, 0)),
        out_shape=jax.ShapeDtypeStruct((M, D), BF16),
        compiler_params=_cparams(("parallel",)),
        name="rms_norm",
    )(x, g.reshape(1, D))


def _mm_kernel(a_ref, b_ref, o_ref):
    o_ref[...] = jnp.dot(a_ref[...], b_ref[...], preferred_element_type=F32).astype(o_ref.dtype)


def _mm_call(a, b, out_dtype, tm, tn, name):
    M, K = a.shape
    N = b.shape[1]
    return pl.pallas_call(
        _mm_kernel,
        grid=(M // tm, N // tn),
        in_specs=[pl.BlockSpec((tm, K), lambda i, j: (i, 0)),
                  pl.BlockSpec((K, tn), lambda i, j: (0, j))],
        out_specs=pl.BlockSpec((tm, tn), lambda i, j: (i, j)),
        out_shape=jax.ShapeDtypeStruct((M, N), out_dtype),
        compiler_params=_cparams(("parallel", "arbitrary")),
        name=name,
    )(a, b)


def _mm_res_kernel(a_ref, w_ref, h_ref, gp_ref, gn_ref, oh_ref, ou_ref, acc_ref):
    k = pl.program_id(1)

    @pl.when(k == 0)
    def _():
        acc_ref[...] = jnp.zeros_like(acc_ref)

    acc_ref[...] += jnp.dot(a_ref[...], w_ref[...], preferred_element_type=F32)

    @pl.when(k == pl.num_programs(1) - 1)
    def _():
        f = acc_ref[...]
        y = f * lax.rsqrt(jnp.mean(f * f, axis=-1, keepdims=True) + NORM_EPS) * gp_ref[...]
        hn = h_ref[...] + y
        oh_ref[...] = hn
        u = hn * lax.rsqrt(jnp.mean(hn * hn, axis=-1, keepdims=True) + NORM_EPS) * gn_ref[...]
        ou_ref[...] = u.astype(ou_ref.dtype)


def _mm_res_call(a, w, h, g_post, g_next, tm, tk, name):
    M, K = a.shape
    D = w.shape[1]
    return pl.pallas_call(
        _mm_res_kernel,
        grid=(M // tm, K // tk),
        in_specs=[pl.BlockSpec((tm, tk), lambda i, k: (i, k)),
                  pl.BlockSpec((tk, D), lambda i, k: (k, 0)),
                  pl.BlockSpec((tm, D), lambda i, k: (i, 0)),
                  pl.BlockSpec((1, D), lambda i, k: (0, 0)),
                  pl.BlockSpec((1, D), lambda i, k: (0, 0))],
        out_specs=[pl.BlockSpec((tm, D), lambda i, k: (i, 0)),
                   pl.BlockSpec((tm, D), lambda i, k: (i, 0))],
        out_shape=[jax.ShapeDtypeStruct((M, D), F32),
                   jax.ShapeDtypeStruct((M, D), BF16)],
        scratch_shapes=[pltpu.VMEM((tm, D), F32)],
        compiler_params=_cparams(("parallel", "arbitrary")),
        name=name,
    )(a, w, h, g_post.reshape(1, D), g_next.reshape(1, D))


def _pool_kernel(z_ref, w_ref, s_ref, o_ref, *, gw):
    L = z_ref.shape[1]
    row = lax.broadcasted_iota(jnp.int32, (L, gw), 0)
    for gi, win in enumerate(POOL_WINDOWS):
        sl = slice(gi * gw, (gi + 1) * gw)
        zg = z_ref[0, :, sl].astype(F32)
        s = zg
        d = 1
        while d < win:
            s = s + _shift_down(s, d)
            d *= 2
        cnt = jnp.minimum(row + 1, win).astype(F32)
        pooled = s / cnt - zg
        y = _dot(pooled, w_ref[gi]) * s_ref[:, sl]
        o_ref[0, :, sl] = y.astype(o_ref.dtype)


def _pool_call(z, w, scale, B, Lp):
    PW = z.shape[-1]
    G = len(POOL_WINDOWS)
    gw = PW // G
    z3 = z.reshape(B, Lp, PW)
    out = pl.pallas_call(
        functools.partial(_pool_kernel, gw=gw),
        grid=(B,),
        in_specs=[pl.BlockSpec((1, Lp, PW), lambda b: (b, 0, 0)),
                  pl.BlockSpec((G, gw, gw), lambda b: (0, 0, 0)),
                  pl.BlockSpec((1, PW), lambda b: (0, 0))],
        out_specs=pl.BlockSpec((1, Lp, PW), lambda b: (b, 0, 0)),
        out_shape=jax.ShapeDtypeStruct((B, Lp, PW), BF16),
        compiler_params=_cparams(("parallel",)),
        name="pool_mixer",
    )(z3, w.astype(BF16), scale.reshape(1, PW))
    return out.reshape(B * Lp, PW)


def _rel_bucket(rel, n_buckets):
    half = n_buckets // 2
    max_exact = half // 2
    n = np.abs(rel)
    large = max_exact + (np.log(np.maximum(n, 1) / max_exact) / math.log(REL_MAX_DIST / max_exact)
                         * (half - max_exact)).astype(np.int32)
    large = np.minimum(large, half - 1)
    return ((rel > 0) * half + np.where(n < max_exact, n, large)).astype(np.int32)


def _bias_table_kernel(rb_ref, bucket_ref, pen_ref, o_ref, *, n_buckets):
    s = pl.program_id(0)
    bucket = bucket_ref[...]
    acc = pen_ref[...]
    for b in range(n_buckets):
        acc = acc + jnp.where(bucket == b, rb_ref[b, s], 0.0)
    o_ref[0] = acc


def _bias_table_call(rel_bias, n_meta, tq):
    n_buckets, n_sub = rel_bias.shape
    KT = ATTN_TILE
    assert tq % KT == 0 and KT % CHUNK == 0 and n_meta <= KT
    n_near = tq // KT + 2
    n_slabs = n_near + 2
    far = n_buckets // 2 - 1
    r = np.arange(tq)[:, None]
    c = np.arange(KT)[None, :]
    assert np.all(_rel_bucket(-(KT + 1 + np.arange(4 * KT)), n_buckets) == far)
    bucket = np.full((n_slabs, tq, KT), far, np.int32)
    pen = np.zeros((n_slabs, tq, KT), np.float32)
    pen[-1] = NEG_INF
    for idx in range(n_near):
        key = (idx - 1) * KT + c
        bucket[1 + idx] = _rel_bucket(key - r, n_buckets)
        visible = np.floor_divide(key - n_meta, CHUNK) <= np.floor_divide(r - n_meta, CHUNK)
        pen[1 + idx] = np.where(visible, 0.0, NEG_INF)
    blk = pl.BlockSpec((n_slabs, tq, KT), lambda s: (0, 0, 0))
    return pl.pallas_call(
        functools.partial(_bias_table_kernel, n_buckets=n_buckets),
        grid=(n_sub,),
        in_specs=[pl.BlockSpec(memory_space=pltpu.SMEM), blk, blk],
        out_specs=pl.BlockSpec((1, n_slabs, tq, KT), lambda s: (s, 0, 0, 0)),
        out_shape=jax.ShapeDtypeStruct((n_sub, n_slabs, tq, KT), F32),
        compiler_params=_cparams(("arbitrary",)),
        name="bias_table_%d" % tq,
    )(rel_bias, jnp.asarray(bucket), jnp.asarray(pen))


def _attn_kernel(*refs, hd, scale, lam_init, nt, tile0, aliased):
    if aliased:
        refs = refs[1:]
    (lq1_ref, lk1_ref, lq2_ref, lk2_ref, sub_ref, q_ref, k_ref, v_ref, w_ref,
     o_ref, s_sc, mx_sc, l_sc, acc_sc) = refs
    KT = ATTN_TILE
    tq = q_ref.shape[1]
    ratio = tq // KT
    n_slabs = w_ref.shape[1]
    i = pl.program_id(2)
    jt0 = tile0 + i * ratio
    n_need = jnp.minimum(jt0 + ratio + 1, nt)
    n_pairs = (n_need + 1) // 2
    q = q_ref[0]

    def slab(j):
        return jnp.where(j < nt, jnp.clip(j - jt0 + 2, 0, n_slabs - 1), n_slabs - 1)

    def two_tiles(ref, jj):
        j0 = 2 * jj
        j1 = jnp.minimum(j0 + 1, nt - 1)
        t0 = ref[0, pl.ds(pl.multiple_of(j0 * KT, KT), KT), :]
        t1 = ref[0, pl.ds(pl.multiple_of(j1 * KT, KT), KT), :]
        return jnp.concatenate([t0, t1], axis=0)

    mx_sc[...] = jnp.full_like(mx_sc, NEG_INF)

    def pass1(jj, carry):
        kt = two_tiles(k_ref, jj)
        i0 = slab(2 * jj)
        i1 = slab(2 * jj + 1)
        for s in range(2):
            sc = _dot_nt(q[:, s * hd:(s + 1) * hd], kt[:, s * hd:(s + 1) * hd]) * scale
            sc = sc + jnp.concatenate([w_ref[s, i0], w_ref[s, i1]], axis=1)
            s_sc[s, jj] = sc
            mx_sc[s] = jnp.maximum(mx_sc[s], jnp.maximum(sc[:, :KT], sc[:, KT:]))
        return carry

    lax.fori_loop(0, n_pairs, pass1, 0)
    m = [jnp.max(mx_sc[s], axis=-1, keepdims=True) for s in range(2)]
    l_sc[...] = jnp.zeros_like(l_sc)
    acc_sc[...] = jnp.zeros_like(acc_sc)

    def pass2(jj, carry):
        vt = two_tiles(v_ref, jj)
        for s in range(2):
            p = jnp.exp(s_sc[s, jj] - m[s])
            l_sc[s] += p[:, :KT] + p[:, KT:]
            acc_sc[s] += jnp.dot(p.astype(BF16), vt, preferred_element_type=F32)
        return carry

    lax.fori_loop(0, n_pairs, pass2, 0)
    l = [jnp.sum(l_sc[s], axis=-1, keepdims=True) for s in range(2)]
    lam = (jnp.exp(jnp.sum(lq1_ref[...] * lk1_ref[...], axis=-1, keepdims=True))
           - jnp.exp(jnp.sum(lq2_ref[...] * lk2_ref[...], axis=-1, keepdims=True)) + lam_init)
    o = acc_sc[0] / l[0] - lam * (acc_sc[1] / l[1])
    y = o * lax.rsqrt(jnp.mean(o * o, axis=-1, keepdims=True) + DIFF_SUBLN_EPS)
    o_ref[0] = (y * sub_ref[...] * (1.0 - lam_init)).astype(o_ref.dtype)


def _attn_segment(prev, qkv3, table, params, lam_init, tq, row0, n_tiles):
    B, Lp, W3 = qkv3.shape
    DW = W3 // 3
    hd = params[0].shape[1]
    hw = 2 * hd
    n_heads = DW // hw
    KT = ATTN_TILE
    nt = Lp // KT
    blk0 = row0 // tq
    aliased = prev is not None
    kern = functools.partial(_attn_kernel, hd=hd, scale=hd ** -0.5, lam_init=lam_init, nt=nt,
                             tile0=row0 // KT, aliased=aliased)
    par = lambda n: pl.BlockSpec((1, n), lambda b, h, i: (0, 0))
    in_specs = [par(hd)] * 4 + [
        par(hw),
        pl.BlockSpec((1, tq, hw), lambda b, h, i: (b, blk0 + i, h)),
        pl.BlockSpec((1, Lp, hw), lambda b, h, i: (b, 0, n_heads + h)),
        pl.BlockSpec((1, Lp, hw), lambda b, h, i: (b, 0, 2 * n_heads + h)),
        pl.BlockSpec((2,) + table.shape[1:], lambda b, h, i: (h, 0, 0, 0))]
    args = list(params) + [qkv3, qkv3, qkv3, table]
    if aliased:
        in_specs = [pl.BlockSpec(memory_space=pl.ANY)] + in_specs
        args = [prev] + args
    max_pairs = (nt + 1) // 2
    return pl.pallas_call(
        kern,
        grid=(B, n_heads, n_tiles),
        in_specs=in_specs,
        out_specs=pl.BlockSpec((1, tq, hw), lambda b, h, i: (b, blk0 + i, h)),
        out_shape=jax.ShapeDtypeStruct((B, Lp, DW), BF16),
        scratch_shapes=[pltpu.VMEM((2, max_pairs, tq, 2 * KT), F32), pltpu.VMEM((2, tq, KT), F32),
                        pltpu.VMEM((2, tq, KT), F32), pltpu.VMEM((2, tq, hw), F32)],
        input_output_aliases={0: 0} if aliased else {},
        compiler_params=_cparams(("parallel", "parallel", "arbitrary")),
        name="diff_attention_%d" % tq,
    )(*args)


def _attn_segments(Lp):
    big = ATTN_Q_TILE
    n_big = Lp // big
    segs = []
    if n_big:
        segs.append((big, 0, n_big))
    rest = Lp - n_big * big
    if rest:
        segs.append((ATTN_TILE, n_big * big, rest // ATTN_TILE))
    return segs


def _attn_call(zqkv, tables, lq1, lk1, lq2, lk2, subln, B, Lp, lam_init):
    vec = lambda a: a.reshape(1, -1)
    params = [vec(lq1), vec(lk1), vec(lq2), vec(lk2), vec(subln)]
    qkv3 = zqkv.reshape(B, Lp, zqkv.shape[-1])
    out = None
    for tq, row0, n_tiles in _attn_segments(Lp):
        out = _attn_segment(out, qkv3, tables[tq], params, lam_init, tq, row0, n_tiles)
    return out.reshape(B * Lp, -1)


def _rwkv_prep_kernel(*refs, rw, has_vres):
    if has_vres:
        (z_ref, zp_ref, mu_ref, w0_ref, w2_ref, a0_ref, a2_ref, g2_ref, kk_ref, ka_ref, ones_ref,
         vf_ref, v0_ref, v1_ref, v2_ref,
         r_o, lw_o, k_o, v_o, a_o, b_o, g_o) = refs
    else:
        (z_ref, zp_ref, mu_ref, w0_ref, w2_ref, a0_ref, a2_ref, g2_ref, kk_ref, ka_ref, ones_ref,
         r_o, lw_o, k_o, v_o, a_o, b_o, g_o) = refs
    i = pl.program_id(1)
    z = z_ref[0]
    last_prev = zp_ref[0, SUBLANES - 1:SUBLANES, :]
    last_prev = jnp.where(i > 0, last_prev, 0.0)
    row = lax.broadcasted_iota(jnp.int32, z.shape, 0)
    prev = jnp.where(row >= 1, pltpu.roll(z, 1, 0), last_prev)
    zs = z + (prev - z) * mu_ref[...]
    r = zs[:, 0:rw]
    kx = zs[:, rw:2 * rw]
    vx = zs[:, 2 * rw:3 * rw]
    wd = zs[:, 3 * rw:3 * rw + LANES]
    ad = zs[:, 3 * rw + LANES:3 * rw + 2 * LANES]
    gd = zs[:, 3 * rw + 2 * LANES:3 * rw + 3 * LANES]
    wl = w0_ref[...] + _dot(jnp.tanh(wd), w2_ref[...])
    sp = jnp.maximum(-wl, 0.0) + jnp.log(1.0 + jnp.exp(-jnp.abs(wl)))
    lw = -jnp.exp(-sp - 0.5)
    if has_vres:
        gate_v = _sigmoid(v0_ref[...] + _dot(_dot(vx, v1_ref[...]), v2_ref[...]))
        vx = vx + (vf_ref[0] - vx) * gate_v
    a = _sigmoid(a0_ref[...] + _dot(ad, a2_ref[...]))
    g = _dot(_sigmoid(gd), g2_ref[...])
    kk = kx * kk_ref[...]
    ss = _dot_exact_rhs(kk * kk, ones_ref[...])
    kk = kk * lax.rsqrt(jnp.maximum(ss, 1e-24))
    kmod = kx * (1.0 + (a - 1.0) * ka_ref[...])
    r_o[0] = r
    lw_o[0] = lw
    k_o[0] = kmod
    v_o[0] = vx
    a_o[0] = -kk
    b_o[0] = kk * a
    g_o[0] = g


def _pad_rows(w, n):
    return jnp.pad(w, ((0, n - w.shape[0]), (0, 0)))


def _rwkv_prep_call(zr, B, Lp, rw, mu_p, w0, w2, a0, a2, g2, k_k, k_a, ones_bd, v_first, v_res):
    ZW = zr.shape[-1]
    z3 = zr.reshape(B, Lp, ZW)
    tl = _pick_tile(Lp, 384, SUBLANES)
    nb = tl // SUBLANES
    has_vres = v_res is not None
    vec = lambda a: a.reshape(1, -1)
    full = lambda a: pl.BlockSpec(a.shape, lambda b, i: (0,) * a.ndim)
    w2p = _pad_rows(w2, LANES).astype(BF16)
    a2p = _pad_rows(a2, LANES).astype(BF16)
    g2p = _pad_rows(g2, LANES).astype(BF16)
    args = [z3, z3, vec(mu_p), vec(w0), w2p, vec(a0), a2p, g2p, vec(k_k), vec(k_a), ones_bd]
    in_specs = [pl.BlockSpec((1, tl, ZW), lambda b, i: (b, i, 0)),
                pl.BlockSpec((1, SUBLANES, ZW), lambda b, i: (b, jnp.maximum(i * nb - 1, 0), 0))]
    in_specs += [full(a) for a in args[2:]]
    if has_vres:
        v0, v1, v2 = v_res
        v1p = jnp.pad(v1, ((0, 0), (0, LANES - v1.shape[1]))).astype(BF16)
        v2p = _pad_rows(v2, LANES).astype(BF16)
        extra = [v_first, vec(v0), v1p, v2p]
        args += extra
        in_specs += [pl.BlockSpec((1, tl, rw), lambda b, i: (b, i, 0))] + [full(a) for a in extra[1:]]
    o_spec = pl.BlockSpec((1, tl, rw), lambda b, i: (b, i, 0))
    o_shape = jax.ShapeDtypeStruct((B, Lp, rw), F32)
    return pl.pallas_call(
        functools.partial(_rwkv_prep_kernel, rw=rw, has_vres=has_vres),
        grid=(B, Lp // tl),
        in_specs=in_specs,
        out_specs=[o_spec] * 7,
        out_shape=[o_shape] * 7,
        compiler_params=_cparams(("parallel", "arbitrary")),
        name="rwkv_prep",
    )(*args)


def _scan_kernel(r_ref, lw_ref, k_ref, v_ref, a_ref, b_ref, g_ref, rk_ref, lnw_ref, lnb_ref, ones_ref,
                 o_ref, h_sc, *, hd):
    C = r_ref.shape[1]
    C2 = 2 * C
    W = r_ref.shape[2]
    c = pl.program_id(1)

    @pl.when(c == 0)
    def _():
        h_sc[...] = jnp.zeros_like(h_sc)

    r = r_ref[0]
    lw = lw_ref[0]
    k = k_ref[0]
    v = v_ref[0]
    a = a_ref[0]
    b = b_ref[0]

    cum = lw
    d = 1
    while d < C:
        cum = cum + _shift_down(cum, d)
        d *= 2
    tot = cum[C - 1:C, :]
    e_end = jnp.exp(tot - cum)
    e_neg = jnp.exp(-cum)
    at = a * jnp.exp(cum - lw)
    rt = r * jnp.exp(cum)
    bt = b * e_neg
    kt = k * e_neg
    bb = b * e_end
    kb = k * e_end
    p_c = jnp.exp(tot)

    lane = lax.broadcasted_iota(jnp.int32, (C, LANES), 1)
    first = lane < hd
    rr = lax.broadcasted_iota(jnp.int32, (C2, C2), 0)
    cc = lax.broadcasted_iota(jnp.int32, (C2, C2), 1)
    tr = jnp.where(rr >= C, rr - C, rr)
    tc = jnp.where(cc >= C, cc - C, cc)
    strict = tr > tc
    lower = tr >= tc
    eye = rr == cc

    def stack(x):
        return jnp.concatenate([jnp.where(first, x, 0.0), jnp.where(first, 0.0, x)], axis=0)

    pairs = range(W // LANES)
    stacked = lambda x: [stack(x[:, p * LANES:(p + 1) * LANES]) for p in pairs]
    s_a, s_r, s_b, s_k = stacked(at), stacked(rt), stacked(bt), stacked(kt)
    s_bb, s_kb, s_v = stacked(bb), stacked(kb), stacked(v)
    sc = [_dot_nt(jnp.concatenate([s_a[p], s_r[p]], axis=0), jnp.concatenate([s_b[p], s_k[p]], axis=0))
          for p in pairs]
    n = [jnp.where(strict, sc[p][:C2, :C2], 0.0) for p in pairs]
    a_ak = [jnp.where(strict, sc[p][:C2, C2:], 0.0) for p in pairs]
    m_rb = [jnp.where(lower, sc[p][C2:, :C2], 0.0) for p in pairs]
    m_rk = [jnp.where(lower, sc[p][C2:, C2:], 0.0) for p in pairs]
    xv = [_dot(jnp.concatenate([a_ak[p], m_rk[p], s_kb[p].T], axis=0), s_v[p]) for p in pairs]
    t_inv = [jnp.where(eye, 1.0, 0.0) + n[p] for p in pairs]
    pw = [_dot(n[p], n[p]) for p in pairs]
    step = 4
    while step < C:
        tp = [_dot(jnp.concatenate([t_inv[p], pw[p]], axis=0), pw[p]) for p in pairs]
        t_inv = [t_inv[p] + tp[p][:C2] for p in pairs]
        pw = [tp[p][C2:] for p in pairs]
        step *= 2
    t_inv = [t_inv[p] + _dot(t_inv[p], pw[p]) for p in pairs]
    au = [_dot(t_inv[p], jnp.concatenate([s_a[p], xv[p][:C2]], axis=1)) for p in pairs]
    mbu = [_dot(jnp.concatenate([m_rb[p], s_bb[p].T], axis=0), au[p]) for p in pairs]
    ys = []
    for p in pairs:
        r_hat = s_r[p] + mbu[p][:C2, :LANES]
        y0 = mbu[p][:C2, LANES:] + xv[p][C2:2 * C2]
        g_mat = jnp.where(eye, p_c[:, p * LANES:(p + 1) * LANES], 0.0) + mbu[p][C2:, :LANES]
        h_add = mbu[p][C2:, LANES:] + xv[p][2 * C2:]
        yh = _dot(jnp.concatenate([r_hat, g_mat], axis=0), h_sc[p])
        y = yh[:C2] + y0
        h_sc[p] = yh[C2:] + h_add
        ys.append(y[0:C] + y[C:C2])
    y = jnp.concatenate(ys, axis=1)

    ones = ones_ref[...]
    inv_hd = 1.0 / hd
    mean = _dot_exact_rhs(y, ones) * inv_hd
    yc = y - mean
    var = _dot_exact_rhs(yc * yc, ones) * inv_hd
    yn = yc * lax.rsqrt(var + RWKV_LNX_EPS) * lnw_ref[...] + lnb_ref[...]
    bonus = _dot_exact_rhs(r * k * rk_ref[...], ones) * v
    o_ref[0] = ((yn + bonus) * g_ref[0]).astype(o_ref.dtype)


def _scan_call(streams, r_k, lnx_w, lnx_b, ones_bd, B, Lp, hd):
    r, lw, k, v, a, b, g = streams
    rw = r.shape[-1]
    C = CHUNK
    assert 2 * hd == LANES and rw % LANES == 0 and Lp % C == 0
    vec = lambda x: x.reshape(1, -1)
    blk = pl.BlockSpec((1, C, rw), lambda bi, c: (bi, c, 0))
    par = pl.BlockSpec((1, rw), lambda bi, c: (0, 0))
    out = pl.pallas_call(
        functools.partial(_scan_kernel, hd=hd),
        grid=(B, Lp // C),
        in_specs=[blk] * 7 + [par] * 3 + [pl.BlockSpec((rw, rw), lambda bi, c: (0, 0))],
        out_specs=blk,
        out_shape=jax.ShapeDtypeStruct((B, Lp, rw), BF16),
        scratch_shapes=[pltpu.VMEM((rw // LANES, LANES, LANES), F32)],
        compiler_params=_cparams(("parallel", "arbitrary")),
        name="rwkv_scan",
    )(r, lw, k, v, a, b, g, vec(r_k), vec(lnx_w), vec(lnx_b), ones_bd)
    return out.reshape(B * Lp, rw)


def _merge_kernel(yp_ref, yd_ref, yr_ref, g0_ref, g1_ref, g2_ref, pa_ref, pb_ref, pc_ref, o_ref):
    dot = lambda x, w: jnp.dot(x[...], w[...], preferred_element_type=F32)
    m = _sigmoid(g0_ref[...].astype(F32)) * dot(yp_ref, pa_ref)
    m = m + _sigmoid(g1_ref[...].astype(F32)) * dot(yd_ref, pb_ref)
    m = m + _sigmoid(g2_ref[...].astype(F32)) * dot(yr_ref, pc_ref)
    o_ref[...] = m.astype(o_ref.dtype)


def _merge_call(yp, yd, yr, zg, pa, pb, pc, tm):
    M = yp.shape[0]
    D = pa.shape[1]
    rows = lambda a: pl.BlockSpec((tm, a.shape[1]), lambda i: (i, 0))
    full = lambda a: pl.BlockSpec(a.shape, lambda i: (0, 0))
    gate = lambda n: pl.BlockSpec((tm, D), lambda i: (i, n))
    return pl.pallas_call(
        _merge_kernel,
        grid=(M // tm,),
        in_specs=[rows(yp), rows(yd), rows(yr), gate(0), gate(1), gate(2), full(pa), full(pb), full(pc)],
        out_specs=pl.BlockSpec((tm, D), lambda i: (i, 0)),
        out_shape=jax.ShapeDtypeStruct((M, D), BF16),
        compiler_params=_cparams(("parallel",)),
        name="gated_merge",
    )(yp, yd, yr, zg, zg, zg, pa, pb, pc)


def _ffn_up_kernel(u_ref, wv_ref, wg_ref, cwv_ref, cwg_ref, cbv_ref, cbg_ref, o_ref, *, n_sub):
    L = u_ref.shape[0]
    mb = L // n_sub
    halo = 2 * SUBLANES
    k_in = math.sqrt(2.0 / math.pi)

    def conv(x, cw_ref, cb_ref):
        c = cb_ref[...] + x * cw_ref[FFN_CONV - 1:FFN_CONV, :]
        for j in range(FFN_CONV - 1):
            c = c + _shift_down(x, FFN_CONV - 1 - j) * cw_ref[j:j + 1, :]
        return c

    def tail(sb, xv, xg):
        val = conv(xv, cwv_ref, cbv_ref)
        gate = conv(xg, cwg_ref, cbg_ref)
        inner = gate * (2.0 * k_in + (2.0 * k_in * 0.044715) * (gate * gate))
        act = gate * val / (1.0 + jnp.exp(-inner))
        if sb:
            act = act[halo:]
        o_ref[sb * mb:(sb + 1) * mb, :] = act.astype(o_ref.dtype)

    pending = None
    for sb in range(n_sub):
        lo = sb * mb - (halo if sb else 0)
        u = u_ref[lo:(sb + 1) * mb, :]
        xv = jnp.dot(u, wv_ref[...], preferred_element_type=F32)
        xg = jnp.dot(u, wg_ref[...], preferred_element_type=F32)
        if pending is not None:
            tail(*pending)
        pending = (sb, xv, xg)
    tail(*pending)


def _ffn_up_call(u, w_up, conv_w, conv_b, B, Lp, tn):
    M, D = u.shape
    FF = w_up.shape[1] // 2
    nt = FF // tn
    n_sub = 4 if Lp % (4 * 2 * SUBLANES) == 0 else 1
    return pl.pallas_call(
        functools.partial(_ffn_up_kernel, n_sub=n_sub),
        grid=(B, nt),
        in_specs=[pl.BlockSpec((Lp, D), lambda b, j: (b, 0)),
                  pl.BlockSpec((D, tn), lambda b, j: (0, j)),
                  pl.BlockSpec((D, tn), lambda b, j: (0, nt + j)),
                  pl.BlockSpec((FFN_CONV, tn), lambda b, j: (0, j)),
                  pl.BlockSpec((FFN_CONV, tn), lambda b, j: (0, nt + j)),
                  pl.BlockSpec((1, tn), lambda b, j: (0, j)),
                  pl.BlockSpec((1, tn), lambda b, j: (0, nt + j))],
        out_specs=pl.BlockSpec((Lp, tn), lambda b, j: (b, j)),
        out_shape=jax.ShapeDtypeStruct((M, FF), BF16),
        compiler_params=_cparams(("parallel", "arbitrary")),
        name="ffn_up_conv_geglu",
    )(u, w_up, w_up, conv_w, conv_w, conv_b.reshape(1, -1), conv_b.reshape(1, -1))


def kernel(x, meta, rel_bias, norm_mix_pre, norm_mix_post, norm_ffn_pre, norm_ffn_post, w_in, pool_w, pool_scale, diff_lq1, diff_lk1, diff_lq2, diff_lk2, diff_subln, rwkv_mu, rwkv_w0, rwkv_w2, rwkv_a0, rwkv_a2, rwkv_g2, rwkv_kk, rwkv_ka, rwkv_rk, rwkv_lnx_w, rwkv_lnx_b, rwkv_v0, rwkv_v1, rwkv_v2, p_a, p_b, p_c, w_o, ffn_up, ffn_conv_w, ffn_conv_b, ffn_down):
    B, S, D = x.shape
    n_meta = meta.shape[0]
    depth = w_in.shape[0]
    L = S + n_meta
    Lp = _round_up(L, ATTN_TILE)
    M = B * Lp

    PW = pool_scale.shape[1]
    DW = p_b.shape[1]
    RW = rwkv_w0.shape[1]
    hd_r = rwkv_rk.shape[2]
    dl, al, gl = rwkv_w2.shape[1], rwkv_a2.shape[1], rwkv_g2.shape[1]
    off_q = PW
    off_r = PW + 3 * DW
    off_g = off_r + 3 * RW + dl + al + gl
    assert off_g + 3 * D == w_in.shape[2]
    assert max(dl, al, gl, rwkv_v1.shape[2]) <= LANES

    h = jnp.concatenate([jnp.broadcast_to(meta.astype(x.dtype)[None], (B, n_meta, D)), x,
                         jnp.zeros((B, Lp - L, D), x.dtype)], axis=1).reshape(M, D)

    hid = np.arange(RW) // hd_r
    ones_bd = jnp.asarray((hid[:, None] == hid[None, :]).astype(np.float32), dtype=BF16)

    def pad_lora(a, axis):
        segs = [lax.slice_in_dim(a, 0, 3 * RW, axis=axis)]
        o = 3 * RW
        for n in (dl, al, gl):
            seg = lax.slice_in_dim(a, o, o + n, axis=axis)
            pad = [(0, 0)] * a.ndim
            pad[axis] = (0, LANES - n)
            segs.append(jnp.pad(seg, pad))
            o += n
        return jnp.concatenate(segs, axis=axis)

    tables = {tq: _bias_table_call(rel_bias, n_meta, tq) for tq, _, _ in _attn_segments(Lp)}

    tm_big = _pick_tile(M, 1088, 16)
    tm_res = _pick_tile(M, 544, 16)
    u = _rms_call(h, norm_mix_pre[0], tm_res)
    v_first = None
    for l in range(depth):
        lam_init = 0.8 - 0.6 * math.exp(-0.3 * l)
        wl = w_in[l]
        w_gate = wl[:, off_g:].astype(BF16)
        w_qkv = wl[:, off_q:off_r].astype(BF16)
        w_pool = wl[:, :PW].astype(BF16)
        w_rwkv = pad_lora(wl[:, off_r:off_g], 1).astype(BF16)
        mu_p = pad_lora(rwkv_mu[l], 0)

        zg = _mm_call(u, w_gate, BF16, tm_big, _pick_tile(3 * D, 768, LANES), "in_proj_gate")
        zqkv = _mm_call(u, w_qkv, BF16, tm_big, _pick_tile(3 * DW, 768, LANES), "in_proj_qkv")
        zp = _mm_call(u, w_pool, BF16, tm_big, _pick_tile(PW, 768, LANES), "in_proj_pool")
        zr = _mm_call(u, w_rwkv, F32, tm_big, _pick_tile(w_rwkv.shape[1], 768, LANES), "in_proj_rwkv")

        yp = _pool_call(zp, pool_w[l], pool_scale[l], B, Lp)
        yd = _attn_call(zqkv, tables, diff_lq1[l], diff_lk1[l], diff_lq2[l], diff_lk2[l],
                        diff_subln[l], B, Lp, lam_init)
        v_res = None if l == 0 else (rwkv_v0[l - 1], rwkv_v1[l - 1], rwkv_v2[l - 1])
        streams = _rwkv_prep_call(zr, B, Lp, RW, mu_p, rwkv_w0[l], rwkv_w2[l], rwkv_a0[l], rwkv_a2[l],
                                  rwkv_g2[l], rwkv_kk[l], rwkv_ka[l], ones_bd, v_first, v_res)
        if l == 0:
            v_first = streams[3]
        yr = _scan_call(streams, rwkv_rk[l], rwkv_lnx_w[l], rwkv_lnx_b[l], ones_bd, B, Lp, hd_r)

        merged = _merge_call(yp, yd, yr, zg, p_a[l].astype(BF16), p_b[l].astype(BF16),
                             p_c[l].astype(BF16), tm_res)
        h, u2 = _mm_res_call(merged, w_o[l].astype(BF16), h, norm_mix_post[l], norm_ffn_pre[l],
                             tm_res, _pick_tile(D, 512, LANES), "out_proj_res")
        FF = ffn_down.shape[1]
        act = _ffn_up_call(u2, ffn_up[l].astype(BF16), ffn_conv_w[l], ffn_conv_b[l], B, Lp,
                           _pick_tile(FF, 256, LANES))
        g_next = norm_mix_pre[l + 1] if l + 1 < depth else norm_mix_pre[l]
        h, u = _mm_res_call(act, ffn_down[l].astype(BF16), h, norm_ffn_post[l], g_next,
                            tm_res, _pick_tile(FF, 512, LANES), "ffn_down_res")
    return h.reshape(B, Lp, D)[:, n_meta:n_meta + S]
```

```python
import functools
import math

import numpy as np
import jax
import jax.numpy as jnp
from jax import lax
from jax.experimental import pallas as pl
from jax.experimental.pallas import tpu as pltpu

F32 = jnp.float32
BF16 = jnp.bfloat16

CHUNK = 64
NORM_EPS = 1e-6
NEG_INF = -1e30
POOL_WINDOWS = (2, 4, 8, 16)
DIFF_SUBLN_EPS = 1e-5
REL_MAX_DIST = 128
RWKV_LNX_EPS = 64e-5
FFN_CONV = 3

LANES = 128
SUBLANES = 8
ATTN_TILE = 128
ATTN_Q_TILE = 512
SCAN_BATCH = 4
VMEM_LIMIT = 56 * 1024 * 1024


def _cparams(sem):
    return pltpu.CompilerParams(dimension_semantics=sem, vmem_limit_bytes=VMEM_LIMIT)


def _round_up(x, m):
    return (x + m - 1) // m * m


def _pick_tile(total, target, mult):
    best = None
    for t in range(mult, min(total, target) + 1, mult):
        if total % t == 0:
            best = t
    assert best is not None, (total, target, mult)
    return best


def _shift_down(x, d):
    row = lax.broadcasted_iota(jnp.int32, x.shape, 0)
    return jnp.where(row >= d, pltpu.roll(x, d, 0), 0.0)


def _sigmoid(x):
    return 1.0 / (1.0 + jnp.exp(-x))


def _dot(a, b):
    return jnp.dot(a.astype(BF16), b.astype(BF16), preferred_element_type=F32)


def _dot_nt(a, b):
    return lax.dot_general(a.astype(BF16), b.astype(BF16), (((1,), (1,)), ((), ())),
                           preferred_element_type=F32)


def _split3(x):
    hi = x.astype(BF16)
    r1 = x - hi.astype(F32)
    mid = r1.astype(BF16)
    lo = (r1 - mid.astype(F32)).astype(BF16)
    return hi, mid, lo


def _dot_exact_rhs(x, ones_bf16):
    hi, mid, lo = _split3(x)
    acc = jnp.dot(hi, ones_bf16, preferred_element_type=F32)
    acc = acc + jnp.dot(mid, ones_bf16, preferred_element_type=F32)
    return acc + jnp.dot(lo, ones_bf16, preferred_element_type=F32)


def _rms_kernel(x_ref, g_ref, o_ref):
    x = x_ref[...]
    y = x * lax.rsqrt(jnp.mean(x * x, axis=-1, keepdims=True) + NORM_EPS)
    o_ref[...] = (y * g_ref[...]).astype(o_ref.dtype)


def _rms_call(x, g, tm):
    M, D = x.shape
    return pl.pallas_call(
        _rms_kernel,
        grid=(M // tm,),
        in_specs=[pl.BlockSpec((tm, D), lambda i: (i, 0)),
                  pl.BlockSpec((1, D), lambda i: (0, 0))],
        out_specs=pl.BlockSpec((tm, D), lambda i: (i, 0)),
        out_shape=jax.ShapeDtypeStruct((M, D), BF16),
        compiler_params=_cparams(("parallel",)),
        name="rms_norm",
    )(x, g.reshape(1, D))


def _mm_kernel(a_ref, b_ref, o_ref):
    o_ref[...] = jnp.dot(a_ref[...], b_ref[...], preferred_element_type=F32).astype(o_ref.dtype)


def _mm_call(a, b, out_dtype, tm, tn, name):
    M, K = a.shape
    N = b.shape[1]
    return pl.pallas_call(
        _mm_kernel,
        grid=(M // tm, N // tn),
        in_specs=[pl.BlockSpec((tm, K), lambda i, j: (i, 0)),
                  pl.BlockSpec((K, tn), lambda i, j: (0, j))],
        out_specs=pl.BlockSpec((tm, tn), lambda i, j: (i, j)),
        out_shape=jax.ShapeDtypeStruct((M, N), out_dtype),
        compiler_params=_cparams(("parallel", "arbitrary")),
        name=name,
    )(a, b)


def _mm_res_kernel(a_ref, w_ref, h_ref, gp_ref, gn_ref, oh_ref, ou_ref):
    f = jnp.dot(a_ref[...], w_ref[...], preferred_element_type=F32)
    y = f * lax.rsqrt(jnp.mean(f * f, axis=-1, keepdims=True) + NORM_EPS) * gp_ref[...]
    hn = h_ref[...] + y
    oh_ref[...] = hn
    u = hn * lax.rsqrt(jnp.mean(hn * hn, axis=-1, keepdims=True) + NORM_EPS) * gn_ref[...]
    ou_ref[...] = u.astype(ou_ref.dtype)


def _mm_res_tile(M, K, D):
    budget = VMEM_LIMIT - 8 * 1024 * 1024
    best = None
    for tm in range(16, M + 1, 16):
        if M % tm:
            continue
        need = (K * D * 2
                + 2 * tm * K * 2
                + 2 * 2 * tm * D * 4
                + 2 * tm * D * 2
                + 2 * tm * D * 4)
        if need <= budget:
            best = tm
    assert best is not None
    return best


def _mm_res_call(a, w, h, g_post, g_next, name):
    M, K = a.shape
    D = w.shape[1]
    tm = _mm_res_tile(M, K, D)
    return pl.pallas_call(
        _mm_res_kernel,
        grid=(M // tm,),
        in_specs=[pl.BlockSpec((tm, K), lambda i: (i, 0)),
                  pl.BlockSpec((K, D), lambda i: (0, 0), pipeline_mode=pl.Buffered(1)),
                  pl.BlockSpec((tm, D), lambda i: (i, 0)),
                  pl.BlockSpec((1, D), lambda i: (0, 0)),
                  pl.BlockSpec((1, D), lambda i: (0, 0))],
        out_specs=[pl.BlockSpec((tm, D), lambda i: (i, 0)),
                   pl.BlockSpec((tm, D), lambda i: (i, 0))],
        out_shape=[jax.ShapeDtypeStruct((M, D), F32),
                   jax.ShapeDtypeStruct((M, D), BF16)],
        compiler_params=_cparams(("parallel",)),
        name=name,
    )(a, w, h, g_post.reshape(1, D), g_next.reshape(1, D))


def _pool_kernel(z_ref, w_ref, s_ref, o_ref, *, gw):
    L = z_ref.shape[1]
    row = lax.broadcasted_iota(jnp.int32, (L, gw), 0)
    for gi, win in enumerate(POOL_WINDOWS):
        sl = slice(gi * gw, (gi + 1) * gw)
        zg = z_ref[0, :, sl].astype(F32)
        s = zg
        d = 1
        while d < win:
            s = s + _shift_down(s, d)
            d *= 2
        cnt = jnp.minimum(row + 1, win).astype(F32)
        pooled = s / cnt - zg
        y = _dot(pooled, w_ref[gi]) * s_ref[:, sl]
        o_ref[0, :, sl] = y.astype(o_ref.dtype)


def _pool_call(z, w, scale, B, Lp):
    PW = z.shape[-1]
    G = len(POOL_WINDOWS)
    gw = PW // G
    z3 = z.reshape(B, Lp, PW)
    out = pl.pallas_call(
        functools.partial(_pool_kernel, gw=gw),
        grid=(B,),
        in_specs=[pl.BlockSpec((1, Lp, PW), lambda b: (b, 0, 0)),
                  pl.BlockSpec((G, gw, gw), lambda b: (0, 0, 0)),
                  pl.BlockSpec((1, PW), lambda b: (0, 0))],
        out_specs=pl.BlockSpec((1, Lp, PW), lambda b: (b, 0, 0)),
        out_shape=jax.ShapeDtypeStruct((B, Lp, PW), BF16),
        compiler_params=_cparams(("parallel",)),
        name="pool_mixer",
    )(z3, w.astype(BF16), scale.reshape(1, PW))
    return out.reshape(B * Lp, PW)


def _rel_bucket(rel, n_buckets):
    half = n_buckets // 2
    max_exact = half // 2
    n = np.abs(rel)
    large = max_exact + (np.log(np.maximum(n, 1) / max_exact) / math.log(REL_MAX_DIST / max_exact)
                         * (half - max_exact)).astype(np.int32)
    large = np.minimum(large, half - 1)
    return ((rel > 0) * half + np.where(n < max_exact, n, large)).astype(np.int32)


def _bias_table_kernel(rb_ref, bucket_ref, pen_ref, o_ref, *, n_buckets):
    s = pl.program_id(0)
    bucket = bucket_ref[...]
    acc = pen_ref[...]
    for b in range(n_buckets):
        acc = acc + jnp.where(bucket == b, rb_ref[b, s], 0.0)
    o_ref[0] = acc


def _bias_table_call(rel_bias, n_meta, tq):
    n_buckets, n_sub = rel_bias.shape
    KT = ATTN_TILE
    assert tq % KT == 0 and KT % CHUNK == 0 and n_meta <= KT
    n_near = tq // KT + 2
    n_slabs = n_near + 2
    far = n_buckets // 2 - 1
    r = np.arange(tq)[:, None]
    c = np.arange(KT)[None, :]
    assert np.all(_rel_bucket(-(KT + 1 + np.arange(4 * KT)), n_buckets) == far)
    bucket = np.full((n_slabs, tq, KT), far, np.int32)
    pen = np.zeros((n_slabs, tq, KT), np.float32)
    pen[-1] = NEG_INF
    for idx in range(n_near):
        key = (idx - 1) * KT + c
        bucket[1 + idx] = _rel_bucket(key - r, n_buckets)
        visible = np.floor_divide(key - n_meta, CHUNK) <= np.floor_divide(r - n_meta, CHUNK)
        pen[1 + idx] = np.where(visible, 0.0, NEG_INF)
    blk = pl.BlockSpec((n_slabs, tq, KT), lambda s: (0, 0, 0))
    return pl.pallas_call(
        functools.partial(_bias_table_kernel, n_buckets=n_buckets),
        grid=(n_sub,),
        in_specs=[pl.BlockSpec(memory_space=pltpu.SMEM), blk, blk],
        out_specs=pl.BlockSpec((1, n_slabs, tq, KT), lambda s: (s, 0, 0, 0)),
        out_shape=jax.ShapeDtypeStruct((n_sub, n_slabs, tq, KT), F32),
        compiler_params=_cparams(("arbitrary",)),
        name="bias_table_%d" % tq,
    )(rel_bias, jnp.asarray(bucket), jnp.asarray(pen))


def _attn_kernel(*refs, hd, scale, lam_init, nt, tile0, aliased):
    if aliased:
        refs = refs[1:]
    (lq1_ref, lk1_ref, lq2_ref, lk2_ref, sub_ref, q_ref, k_ref, v_ref, w_ref,
     o_ref, s_sc, mx_sc, l_sc, acc_sc) = refs
    KT = ATTN_TILE
    tq = q_ref.shape[1]
    ratio = tq // KT
    n_slabs = w_ref.shape[1]
    i = pl.program_id(2)
    jt0 = tile0 + i * ratio
    n_need = jnp.minimum(jt0 + ratio + 1, nt)
    n_pairs = (n_need + 1) // 2
    q = q_ref[0]

    def slab(j):
        return jnp.where(j < nt, jnp.clip(j - jt0 + 2, 0, n_slabs - 1), n_slabs - 1)

    def two_tiles(ref, jj):
        j0 = 2 * jj
        j1 = jnp.minimum(j0 + 1, nt - 1)
        t0 = ref[0, pl.ds(pl.multiple_of(j0 * KT, KT), KT), :]
        t1 = ref[0, pl.ds(pl.multiple_of(j1 * KT, KT), KT), :]
        return jnp.concatenate([t0, t1], axis=0)

    mx_sc[...] = jnp.full_like(mx_sc, NEG_INF)

    def pass1(jj, carry):
        kt = two_tiles(k_ref, jj)
        i0 = slab(2 * jj)
        i1 = slab(2 * jj + 1)
        for s in range(2):
            sc = _dot_nt(q[:, s * hd:(s + 1) * hd], kt[:, s * hd:(s + 1) * hd]) * scale
            sc = sc + jnp.concatenate([w_ref[s, i0], w_ref[s, i1]], axis=1)
            s_sc[s, jj] = sc
            mx_sc[s] = jnp.maximum(mx_sc[s], jnp.maximum(sc[:, :KT], sc[:, KT:]))
        return carry

    lax.fori_loop(0, n_pairs, pass1, 0)
    m = [jnp.max(mx_sc[s], axis=-1, keepdims=True) for s in range(2)]
    l_sc[...] = jnp.zeros_like(l_sc)
    acc_sc[...] = jnp.zeros_like(acc_sc)

    def pass2(jj, carry):
        vt = two_tiles(v_ref, jj)
        for s in range(2):
            p = jnp.exp(s_sc[s, jj] - m[s])
            l_sc[s] += p[:, :KT] + p[:, KT:]
            acc_sc[s] += jnp.dot(p.astype(BF16), vt, preferred_element_type=F32)
        return carry

    lax.fori_loop(0, n_pairs, pass2, 0)
    l = [jnp.sum(l_sc[s], axis=-1, keepdims=True) for s in range(2)]
    lam = (jnp.exp(jnp.sum(lq1_ref[...] * lk1_ref[...], axis=-1, keepdims=True))
           - jnp.exp(jnp.sum(lq2_ref[...] * lk2_ref[...], axis=-1, keepdims=True)) + lam_init)
    o = acc_sc[0] / l[0] - lam * (acc_sc[1] / l[1])
    y = o * lax.rsqrt(jnp.mean(o * o, axis=-1, keepdims=True) + DIFF_SUBLN_EPS)
    o_ref[0] = (y * sub_ref[...] * (1.0 - lam_init)).astype(o_ref.dtype)


def _attn_segment(prev, qkv3, table, params, lam_init, tq, row0, n_tiles):
    B, Lp, W3 = qkv3.shape
    DW = W3 // 3
    hd = params[0].shape[1]
    hw = 2 * hd
    n_heads = DW // hw
    KT = ATTN_TILE
    nt = Lp // KT
    blk0 = row0 // tq
    aliased = prev is not None
    kern = functools.partial(_attn_kernel, hd=hd, scale=hd ** -0.5, lam_init=lam_init, nt=nt,
                             tile0=row0 // KT, aliased=aliased)
    par = lambda n: pl.BlockSpec((1, n), lambda b, h, i: (0, 0))
    in_specs = [par(hd)] * 4 + [
        par(hw),
        pl.BlockSpec((1, tq, hw), lambda b, h, i: (b, blk0 + i, h)),
        pl.BlockSpec((1, Lp, hw), lambda b, h, i: (b, 0, n_heads + h)),
        pl.BlockSpec((1, Lp, hw), lambda b, h, i: (b, 0, 2 * n_heads + h)),
        pl.BlockSpec((2,) + table.shape[1:], lambda b, h, i: (h, 0, 0, 0))]
    args = list(params) + [qkv3, qkv3, qkv3, table]
    if aliased:
        in_specs = [pl.BlockSpec(memory_space=pl.ANY)] + in_specs
        args = [prev] + args
    max_pairs = (nt + 1) // 2
    return pl.pallas_call(
        kern,
        grid=(B, n_heads, n_tiles),
        in_specs=in_specs,
        out_specs=pl.BlockSpec((1, tq, hw), lambda b, h, i: (b, blk0 + i, h)),
        out_shape=jax.ShapeDtypeStruct((B, Lp, DW), BF16),
        scratch_shapes=[pltpu.VMEM((2, max_pairs, tq, 2 * KT), F32), pltpu.VMEM((2, tq, KT), F32),
                        pltpu.VMEM((2, tq, KT), F32), pltpu.VMEM((2, tq, hw), F32)],
        input_output_aliases={0: 0} if aliased else {},
        compiler_params=_cparams(("parallel", "parallel", "arbitrary")),
        name="diff_attention_%d" % tq,
    )(*args)


def _attn_segments(Lp):
    big = ATTN_Q_TILE
    n_big = Lp // big
    segs = []
    if n_big:
        segs.append((big, 0, n_big))
    rest = Lp - n_big * big
    if rest:
        segs.append((ATTN_TILE, n_big * big, rest // ATTN_TILE))
    return segs


def _attn_call(zqkv, tables, lq1, lk1, lq2, lk2, subln, B, Lp, lam_init):
    vec = lambda a: a.reshape(1, -1)
    params = [vec(lq1), vec(lk1), vec(lq2), vec(lk2), vec(subln)]
    qkv3 = zqkv.reshape(B, Lp, zqkv.shape[-1])
    out = None
    for tq, row0, n_tiles in _attn_segments(Lp):
        out = _attn_segment(out, qkv3, tables[tq], params, lam_init, tq, row0, n_tiles)
    return out.reshape(B * Lp, -1)


def _rwkv_prep_kernel(*refs, rw, has_vres):
    if has_vres:
        (z_ref, zp_ref, mu_ref, w0_ref, w2_ref, a0_ref, a2_ref, g2_ref, kk_ref, ka_ref, ones_ref,
         vf_ref, v0_ref, v1_ref, v2_ref,
         r_o, lw_o, k_o, v_o, a_o, b_o, g_o) = refs
    else:
        (z_ref, zp_ref, mu_ref, w0_ref, w2_ref, a0_ref, a2_ref, g2_ref, kk_ref, ka_ref, ones_ref,
         r_o, lw_o, k_o, v_o, a_o, b_o, g_o) = refs
    i = pl.program_id(1)
    z = z_ref[0]
    last_prev = zp_ref[0, SUBLANES - 1:SUBLANES, :]
    last_prev = jnp.where(i > 0, last_prev, 0.0)
    row = lax.broadcasted_iota(jnp.int32, z.shape, 0)
    prev = jnp.where(row >= 1, pltpu.roll(z, 1, 0), last_prev)
    zs = z + (prev - z) * mu_ref[...]
    r = zs[:, 0:rw]
    kx = zs[:, rw:2 * rw]
    vx = zs[:, 2 * rw:3 * rw]
    wd = zs[:, 3 * rw:3 * rw + LANES]
    ad = zs[:, 3 * rw + LANES:3 * rw + 2 * LANES]
    gd = zs[:, 3 * rw + 2 * LANES:3 * rw + 3 * LANES]
    wl = w0_ref[...] + _dot(jnp.tanh(wd), w2_ref[...])
    sp = jnp.maximum(-wl, 0.0) + jnp.log(1.0 + jnp.exp(-jnp.abs(wl)))
    lw = -jnp.exp(-sp - 0.5)
    if has_vres:
        gate_v = _sigmoid(v0_ref[...] + _dot(_dot(vx, v1_ref[...]), v2_ref[...]))
        vx = vx + (vf_ref[0] - vx) * gate_v
    a = _sigmoid(a0_ref[...] + _dot(ad, a2_ref[...]))
    g = _dot(_sigmoid(gd), g2_ref[...])
    kk = kx * kk_ref[...]
    ss = _dot_exact_rhs(kk * kk, ones_ref[...])
    kk = kk * lax.rsqrt(jnp.maximum(ss, 1e-24))
    kmod = kx * (1.0 + (a - 1.0) * ka_ref[...])
    r_o[0] = r
    lw_o[0] = lw
    k_o[0] = kmod
    v_o[0] = vx
    a_o[0] = -kk
    b_o[0] = kk * a
    g_o[0] = g


def _pad_rows(w, n):
    return jnp.pad(w, ((0, n - w.shape[0]), (0, 0)))


def _rwkv_prep_call(zr, B, Lp, rw, mu_p, w0, w2, a0, a2, g2, k_k, k_a, ones_bd, v_first, v_res):
    ZW = zr.shape[-1]
    z3 = zr.reshape(B, Lp, ZW)
    tl = _pick_tile(Lp, 384, SUBLANES)
    nb = tl // SUBLANES
    has_vres = v_res is not None
    vec = lambda a: a.reshape(1, -1)
    full = lambda a: pl.BlockSpec(a.shape, lambda b, i: (0,) * a.ndim)
    w2p = _pad_rows(w2, LANES).astype(BF16)
    a2p = _pad_rows(a2, LANES).astype(BF16)
    g2p = _pad_rows(g2, LANES).astype(BF16)
    args = [z3, z3, vec(mu_p), vec(w0), w2p, vec(a0), a2p, g2p, vec(k_k), vec(k_a), ones_bd]
    in_specs = [pl.BlockSpec((1, tl, ZW), lambda b, i: (b, i, 0)),
                pl.BlockSpec((1, SUBLANES, ZW), lambda b, i: (b, jnp.maximum(i * nb - 1, 0), 0))]
    in_specs += [full(a) for a in args[2:]]
    if has_vres:
        v0, v1, v2 = v_res
        v1p = jnp.pad(v1, ((0, 0), (0, LANES - v1.shape[1]))).astype(BF16)
        v2p = _pad_rows(v2, LANES).astype(BF16)
        extra = [v_first, vec(v0), v1p, v2p]
        args += extra
        in_specs += [pl.BlockSpec((1, tl, rw), lambda b, i: (b, i, 0))] + [full(a) for a in extra[1:]]
    o_spec = pl.BlockSpec((1, tl, rw), lambda b, i: (b, i, 0))
    o_shape = jax.ShapeDtypeStruct((B, Lp, rw), F32)
    return pl.pallas_call(
        functools.partial(_rwkv_prep_kernel, rw=rw, has_vres=has_vres),
        grid=(B, Lp // tl),
        in_specs=in_specs,
        out_specs=[o_spec] * 7,
        out_shape=[o_shape] * 7,
        compiler_params=_cparams(("parallel", "arbitrary")),
        name="rwkv_prep",
    )(*args)


def _scan_kernel(r_ref, lw_ref, k_ref, v_ref, a_ref, b_ref, g_ref, rk_ref, lnw_ref, lnb_ref, ones_ref,
                 o_ref, h_sc, *, hd):
    nb = r_ref.shape[0]
    C = r_ref.shape[1]
    C2 = 2 * C
    WB = r_ref.shape[2]
    W = nb * WB
    c = pl.program_id(1)

    @pl.when(c == 0)
    def _():
        h_sc[...] = jnp.zeros_like(h_sc)

    side_by_side = lambda ref: jnp.concatenate([ref[i] for i in range(nb)], axis=1)
    r = side_by_side(r_ref)
    lw = side_by_side(lw_ref)
    k = side_by_side(k_ref)
    v = side_by_side(v_ref)
    a = side_by_side(a_ref)
    b = side_by_side(b_ref)

    cum = lw
    d = 1
    while d < C:
        cum = cum + _shift_down(cum, d)
        d *= 2
    tot = cum[C - 1:C, :]
    e_end = jnp.exp(tot - cum)
    e_neg = jnp.exp(-cum)
    at = a * jnp.exp(cum - lw)
    rt = r * jnp.exp(cum)
    bt = b * e_neg
    kt = k * e_neg
    bb = b * e_end
    kb = k * e_end
    p_c = jnp.exp(tot)

    lane = lax.broadcasted_iota(jnp.int32, (C, LANES), 1)
    first = lane < hd
    rr = lax.broadcasted_iota(jnp.int32, (C2, C2), 0)
    cc = lax.broadcasted_iota(jnp.int32, (C2, C2), 1)
    tr = jnp.where(rr >= C, rr - C, rr)
    tc = jnp.where(cc >= C, cc - C, cc)
    strict = tr > tc
    lower = tr >= tc
    eye = rr == cc

    def stack(x):
        return jnp.concatenate([jnp.where(first, x, 0.0), jnp.where(first, 0.0, x)], axis=0)

    pairs = range(W // LANES)
    stacked = lambda x: [stack(x[:, p * LANES:(p + 1) * LANES]) for p in pairs]
    s_a, s_r, s_b, s_k = stacked(at), stacked(rt), stacked(bt), stacked(kt)
    s_bb, s_kb, s_v = stacked(bb), stacked(kb), stacked(v)
    sc = [_dot_nt(jnp.concatenate([s_a[p], s_r[p]], axis=0), jnp.concatenate([s_b[p], s_k[p]], axis=0))
          for p in pairs]
    n = [jnp.where(strict, sc[p][:C2, :C2], 0.0) for p in pairs]
    a_ak = [jnp.where(strict, sc[p][:C2, C2:], 0.0) for p in pairs]
    m_rb = [jnp.where(lower, sc[p][C2:, :C2], 0.0) for p in pairs]
    m_rk = [jnp.where(lower, sc[p][C2:, C2:], 0.0) for p in pairs]
    xv = [_dot(jnp.concatenate([a_ak[p], m_rk[p], s_kb[p].T], axis=0), s_v[p]) for p in pairs]
    t_inv = [jnp.where(eye, 1.0, 0.0) + n[p] for p in pairs]
    pw = [_dot(n[p], n[p]) for p in pairs]
    step = 4
    while step < C:
        tp = [_dot(jnp.concatenate([t_inv[p], pw[p]], axis=0), pw[p]) for p in pairs]
        t_inv = [t_inv[p] + tp[p][:C2] for p in pairs]
        pw = [tp[p][C2:] for p in pairs]
        step *= 2
    t_inv = [t_inv[p] + _dot(t_inv[p], pw[p]) for p in pairs]
    au = [_dot(t_inv[p], jnp.concatenate([s_a[p], xv[p][:C2]], axis=1)) for p in pairs]
    mbu = [_dot(jnp.concatenate([m_rb[p], s_bb[p].T], axis=0), au[p]) for p in pairs]
    ys = []
    for p in pairs:
        r_hat = s_r[p] + mbu[p][:C2, :LANES]
        y0 = mbu[p][:C2, LANES:] + xv[p][C2:2 * C2]
        g_mat = jnp.where(eye, p_c[:, p * LANES:(p + 1) * LANES], 0.0) + mbu[p][C2:, :LANES]
        h_add = mbu[p][C2:, LANES:] + xv[p][2 * C2:]
        yh = _dot(jnp.concatenate([r_hat, g_mat], axis=0), h_sc[p])
        y = yh[:C2] + y0
        h_sc[p] = yh[C2:] + h_add
        ys.append(y[0:C] + y[C:C2])
    ones = ones_ref[...]
    inv_hd = 1.0 / hd
    ppb = WB // LANES
    for i in range(nb):
        y = jnp.concatenate(ys[i * ppb:(i + 1) * ppb], axis=1)
        mean = _dot_exact_rhs(y, ones) * inv_hd
        yc = y - mean
        var = _dot_exact_rhs(yc * yc, ones) * inv_hd
        yn = yc * lax.rsqrt(var + RWKV_LNX_EPS) * lnw_ref[...] + lnb_ref[...]
        bonus = _dot_exact_rhs(r_ref[i] * k_ref[i] * rk_ref[...], ones) * v_ref[i]
        o_ref[i] = ((yn + bonus) * g_ref[i]).astype(o_ref.dtype)


def _scan_call(streams, r_k, lnx_w, lnx_b, ones_bd, B, Lp, hd):
    r, lw, k, v, a, b, g = streams
    rw = r.shape[-1]
    C = CHUNK
    assert 2 * hd == LANES and rw % LANES == 0 and Lp % C == 0
    vec = lambda x: x.reshape(1, -1)
    nb = SCAN_BATCH if B % SCAN_BATCH == 0 else 1
    blk = pl.BlockSpec((nb, C, rw), lambda bi, c: (bi, c, 0))
    par = pl.BlockSpec((1, rw), lambda bi, c: (0, 0))
    out = pl.pallas_call(
        functools.partial(_scan_kernel, hd=hd),
        grid=(B // nb, Lp // C),
        in_specs=[blk] * 7 + [par] * 3 + [pl.BlockSpec((rw, rw), lambda bi, c: (0, 0))],
        out_specs=blk,
        out_shape=jax.ShapeDtypeStruct((B, Lp, rw), BF16),
        scratch_shapes=[pltpu.VMEM((nb * rw // LANES, LANES, LANES), F32)],
        compiler_params=_cparams(("parallel", "arbitrary")),
        name="rwkv_scan",
    )(r, lw, k, v, a, b, g, vec(r_k), vec(lnx_w), vec(lnx_b), ones_bd)
    return out.reshape(B * Lp, rw)


def _merge_kernel(yp_ref, yd_ref, yr_ref, g0_ref, g1_ref, g2_ref, pa_ref, pb_ref, pc_ref, o_ref):
    dot = lambda x, w: jnp.dot(x[...], w[...], preferred_element_type=F32)
    m = _sigmoid(g0_ref[...].astype(F32)) * dot(yp_ref, pa_ref)
    m = m + _sigmoid(g1_ref[...].astype(F32)) * dot(yd_ref, pb_ref)
    m = m + _sigmoid(g2_ref[...].astype(F32)) * dot(yr_ref, pc_ref)
    o_ref[...] = m.astype(o_ref.dtype)


def _merge_call(yp, yd, yr, zg, pa, pb, pc, tm):
    M = yp.shape[0]
    D = pa.shape[1]
    rows = lambda a: pl.BlockSpec((tm, a.shape[1]), lambda i: (i, 0))
    full = lambda a: pl.BlockSpec(a.shape, lambda i: (0, 0))
    gate = lambda n: pl.BlockSpec((tm, D), lambda i: (i, n))
    return pl.pallas_call(
        _merge_kernel,
        grid=(M // tm,),
        in_specs=[rows(yp), rows(yd), rows(yr), gate(0), gate(1), gate(2), full(pa), full(pb), full(pc)],
        out_specs=pl.BlockSpec((tm, D), lambda i: (i, 0)),
        out_shape=jax.ShapeDtypeStruct((M, D), BF16),
        compiler_params=_cparams(("parallel",)),
        name="gated_merge",
    )(yp, yd, yr, zg, zg, zg, pa, pb, pc)


def _ffn_up_kernel(u_ref, wv_ref, wg_ref, cwv_ref, cwg_ref, cbv_ref, cbg_ref, o_ref, *, n_sub):
    L = u_ref.shape[0]
    mb = L // n_sub
    halo = 2 * SUBLANES
    k_in = math.sqrt(2.0 / math.pi)

    def conv(x, cw_ref, cb_ref):
        c = cb_ref[...] + x * cw_ref[FFN_CONV - 1:FFN_CONV, :]
        for j in range(FFN_CONV - 1):
            c = c + _shift_down(x, FFN_CONV - 1 - j) * cw_ref[j:j + 1, :]
        return c

    def tail(sb, xv, xg):
        val = conv(xv, cwv_ref, cbv_ref)
        gate = conv(xg, cwg_ref, cbg_ref)
        inner = gate * (2.0 * k_in + (2.0 * k_in * 0.044715) * (gate * gate))
        act = gate * val / (1.0 + jnp.exp(-inner))
        if sb:
            act = act[halo:]
        o_ref[sb * mb:(sb + 1) * mb, :] = act.astype(o_ref.dtype)

    pending = None
    for sb in range(n_sub):
        lo = sb * mb - (halo if sb else 0)
        u = u_ref[lo:(sb + 1) * mb, :]
        xv = jnp.dot(u, wv_ref[...], preferred_element_type=F32)
        xg = jnp.dot(u, wg_ref[...], preferred_element_type=F32)
        if pending is not None:
            tail(*pending)
        pending = (sb, xv, xg)
    tail(*pending)


def _ffn_up_call(u, w_up, conv_w, conv_b, B, Lp, tn):
    M, D = u.shape
    FF = w_up.shape[1] // 2
    nt = FF // tn
    n_sub = 4 if Lp % (4 * 2 * SUBLANES) == 0 else 1
    return pl.pallas_call(
        functools.partial(_ffn_up_kernel, n_sub=n_sub),
        grid=(B, nt),
        in_specs=[pl.BlockSpec((Lp, D), lambda b, j: (b, 0)),
                  pl.BlockSpec((D, tn), lambda b, j: (0, j)),
                  pl.BlockSpec((D, tn), lambda b, j: (0, nt + j)),
                  pl.BlockSpec((FFN_CONV, tn), lambda b, j: (0, j)),
                  pl.BlockSpec((FFN_CONV, tn), lambda b, j: (0, nt + j)),
                  pl.BlockSpec((1, tn), lambda b, j: (0, j)),
                  pl.BlockSpec((1, tn), lambda b, j: (0, nt + j))],
        out_specs=pl.BlockSpec((Lp, tn), lambda b, j: (b, j)),
        out_shape=jax.ShapeDtypeStruct((M, FF), BF16),
        compiler_params=_cparams(("parallel", "arbitrary")),
        name="ffn_up_conv_geglu",
    )(u, w_up, w_up, conv_w, conv_w, conv_b.reshape(1, -1), conv_b.reshape(1, -1))


def kernel(x, meta, rel_bias, norm_mix_pre, norm_mix_post, norm_ffn_pre, norm_ffn_post, w_in, pool_w, pool_scale, diff_lq1, diff_lk1, diff_lq2, diff_lk2, diff_subln, rwkv_mu, rwkv_w0, rwkv_w2, rwkv_a0, rwkv_a2, rwkv_g2, rwkv_kk, rwkv_ka, rwkv_rk, rwkv_lnx_w, rwkv_lnx_b, rwkv_v0, rwkv_v1, rwkv_v2, p_a, p_b, p_c, w_o, ffn_up, ffn_conv_w, ffn_conv_b, ffn_down):
    B, S, D = x.shape
    n_meta = meta.shape[0]
    depth = w_in.shape[0]
    L = S + n_meta
    Lp = _round_up(L, ATTN_TILE)
    M = B * Lp

    PW = pool_scale.shape[1]
    DW = p_b.shape[1]
    RW = rwkv_w0.shape[1]
    hd_r = rwkv_rk.shape[2]
    dl, al, gl = rwkv_w2.shape[1], rwkv_a2.shape[1], rwkv_g2.shape[1]
    off_q = PW
    off_r = PW + 3 * DW
    off_g = off_r + 3 * RW + dl + al + gl
    assert off_g + 3 * D == w_in.shape[2]
    assert max(dl, al, gl, rwkv_v1.shape[2]) <= LANES

    h = jnp.concatenate([jnp.broadcast_to(meta.astype(x.dtype)[None], (B, n_meta, D)), x,
                         jnp.zeros((B, Lp - L, D), x.dtype)], axis=1).reshape(M, D)

    hid = np.arange(RW) // hd_r
    ones_bd = jnp.asarray((hid[:, None] == hid[None, :]).astype(np.float32), dtype=BF16)

    def pad_lora(a, axis):
        segs = [lax.slice_in_dim(a, 0, 3 * RW, axis=axis)]
        o = 3 * RW
        for n in (dl, al, gl):
            seg = lax.slice_in_dim(a, o, o + n, axis=axis)
            pad = [(0, 0)] * a.ndim
            pad[axis] = (0, LANES - n)
            segs.append(jnp.pad(seg, pad))
            o += n
        return jnp.concatenate(segs, axis=axis)

    tables = {tq: _bias_table_call(rel_bias, n_meta, tq) for tq, _, _ in _attn_segments(Lp)}

    tm_big = _pick_tile(M, 1088, 16)
    tm_res = _pick_tile(M, 544, 16)
    u = _rms_call(h, norm_mix_pre[0], tm_res)
    v_first = None
    for l in range(depth):
        lam_init = 0.8 - 0.6 * math.exp(-0.3 * l)
        wl = w_in[l]
        w_gate = wl[:, off_g:].astype(BF16)
        w_qkv = wl[:, off_q:off_r].astype(BF16)
        w_pool = wl[:, :PW].astype(BF16)
        w_rwkv = pad_lora(wl[:, off_r:off_g], 1).astype(BF16)
        mu_p = pad_lora(rwkv_mu[l], 0)

        zg = _mm_call(u, w_gate, BF16, tm_big, _pick_tile(3 * D, 768, LANES), "in_proj_gate")
        zqkv = _mm_call(u, w_qkv, BF16, tm_big, _pick_tile(3 * DW, 768, LANES), "in_proj_qkv")
        zp = _mm_call(u, w_pool, BF16, tm_big, _pick_tile(PW, 768, LANES), "in_proj_pool")
        zr = _mm_call(u, w_rwkv, F32, tm_big, _pick_tile(w_rwkv.shape[1], 768, LANES), "in_proj_rwkv")

        yp = _pool_call(zp, pool_w[l], pool_scale[l], B, Lp)
        yd = _attn_call(zqkv, tables, diff_lq1[l], diff_lk1[l], diff_lq2[l], diff_lk2[l],
                        diff_subln[l], B, Lp, lam_init)
        v_res = None if l == 0 else (rwkv_v0[l - 1], rwkv_v1[l - 1], rwkv_v2[l - 1])
        streams = _rwkv_prep_call(zr, B, Lp, RW, mu_p, rwkv_w0[l], rwkv_w2[l], rwkv_a0[l], rwkv_a2[l],
                                  rwkv_g2[l], rwkv_kk[l], rwkv_ka[l], ones_bd, v_first, v_res)
        if l == 0:
            v_first = streams[3]
        yr = _scan_call(streams, rwkv_rk[l], rwkv_lnx_w[l], rwkv_lnx_b[l], ones_bd, B, Lp, hd_r)

        merged = _merge_call(yp, yd, yr, zg, p_a[l].astype(BF16), p_b[l].astype(BF16),
                             p_c[l].astype(BF16), tm_res)
        h, u2 = _mm_res_call(merged, w_o[l].astype(BF16), h, norm_mix_post[l], norm_ffn_pre[l],
                             "out_proj_res")
        FF = ffn_down.shape[1]
        act = _ffn_up_call(u2, ffn_up[l].astype(BF16), ffn_conv_w[l], ffn_conv_b[l], B, Lp,
                           _pick_tile(FF, 256, LANES))
        g_next = norm_mix_pre[l + 1] if l + 1 < depth else norm_mix_pre[l]
        h, u = _mm_res_call(act, ffn_down[l].astype(BF16), h, norm_ffn_post[l], g_next,
                            "ffn_down_res")
    return h.reshape(B, Lp, D)[:, n_meta:n_meta + S]
```

```python
import functools
import math

import numpy as np
import jax
import jax.numpy as jnp
from jax import lax
from jax.experimental import pallas as pl
from jax.experimental.pallas import tpu as pltpu

F32 = jnp.float32
BF16 = jnp.bfloat16

CHUNK = 64
NORM_EPS = 1e-6
NEG_INF = -1e30
POOL_WINDOWS = (2, 4, 8, 16)
DIFF_SUBLN_EPS = 1e-5
REL_MAX_DIST = 128
RWKV_LNX_EPS = 64e-5
FFN_CONV = 3

LANES = 128
SUBLANES = 8
ATTN_TILE = 128
ATTN_Q_TILE = 512
SCAN_BATCH = 4
VMEM_LIMIT = 56 * 1024 * 1024


def _cparams(sem):
    return pltpu.CompilerParams(dimension_semantics=sem, vmem_limit_bytes=VMEM_LIMIT)


def _round_up(x, m):
    return (x + m - 1) // m * m


def _pick_tile(total, target, mult):
    best = None
    for t in range(mult, min(total, target) + 1, mult):
        if total % t == 0:
            best = t
    assert best is not None, (total, target, mult)
    return best


def _shift_down(x, d):
    row = lax.broadcasted_iota(jnp.int32, x.shape, 0)
    return jnp.where(row >= d, pltpu.roll(x, d, 0), 0.0)


def _sigmoid(x):
    return 1.0 / (1.0 + jnp.exp(-x))


def _dot(a, b):
    return jnp.dot(a.astype(BF16), b.astype(BF16), preferred_element_type=F32)


def _dot_nt(a, b):
    return lax.dot_general(a.astype(BF16), b.astype(BF16), (((1,), (1,)), ((), ())),
                           preferred_element_type=F32)


def _split3(x):
    hi = x.astype(BF16)
    r1 = x - hi.astype(F32)
    mid = r1.astype(BF16)
    lo = (r1 - mid.astype(F32)).astype(BF16)
    return hi, mid, lo


def _dot_exact_rhs(x, ones_bf16):
    hi, mid, lo = _split3(x)
    acc = jnp.dot(hi, ones_bf16, preferred_element_type=F32)
    acc = acc + jnp.dot(mid, ones_bf16, preferred_element_type=F32)
    return acc + jnp.dot(lo, ones_bf16, preferred_element_type=F32)


def _rms_kernel(x_ref, g_ref, o_ref):
    x = x_ref[...]
    y = x * lax.rsqrt(jnp.mean(x * x, axis=-1, keepdims=True) + NORM_EPS)
    o_ref[...] = (y * g_ref[...]).astype(o_ref.dtype)


def _rms_call(x, g, tm):
    M, D = x.shape
    return pl.pallas_call(
        _rms_kernel,
        grid=(M // tm,),
        in_specs=[pl.BlockSpec((tm, D), lambda i: (i, 0)),
                  pl.BlockSpec((1, D), lambda i: (0, 0))],
        out_specs=pl.BlockSpec((tm, D), lambda i: (i, 0)),
        out_shape=jax.ShapeDtypeStruct((M, D), BF16),
        compiler_params=_cparams(("parallel",)),
        name="rms_norm",
    )(x, g.reshape(1, D))


def _mm_kernel(a_ref, b_ref, o_ref):
    o_ref[...] = jnp.dot(a_ref[...], b_ref[...], preferred_element_type=F32).astype(o_ref.dtype)


def _mm_call(a, b, out_dtype, tm, tn, name):
    M, K = a.shape
    N = b.shape[1]
    return pl.pallas_call(
        _mm_kernel,
        grid=(M // tm, N // tn),
        in_specs=[pl.BlockSpec((tm, K), lambda i, j: (i, 0)),
                  pl.BlockSpec((K, tn), lambda i, j: (0, j))],
        out_specs=pl.BlockSpec((tm, tn), lambda i, j: (i, j)),
        out_shape=jax.ShapeDtypeStruct((M, N), out_dtype),
        compiler_params=_cparams(("parallel", "arbitrary")),
        name=name,
    )(a, b)


def _mm_res_kernel(a_ref, w_ref, h_ref, gp_ref, gn_ref, oh_ref, ou_ref):
    f = jnp.dot(a_ref[...], w_ref[...], preferred_element_type=F32)
    y = f * lax.rsqrt(jnp.mean(f * f, axis=-1, keepdims=True) + NORM_EPS) * gp_ref[...]
    hn = h_ref[...] + y
    oh_ref[...] = hn
    u = hn * lax.rsqrt(jnp.mean(hn * hn, axis=-1, keepdims=True) + NORM_EPS) * gn_ref[...]
    ou_ref[...] = u.astype(ou_ref.dtype)


def _mm_res_tile(M, K, D):
    budget = VMEM_LIMIT - 8 * 1024 * 1024
    best = None
    for tm in range(16, M + 1, 16):
        if M % tm:
            continue
        need = (K * D * 2
                + 2 * tm * K * 2
                + 2 * 2 * tm * D * 4
                + 2 * tm * D * 2
                + 2 * tm * D * 4)
        if need <= budget:
            best = tm
    assert best is not None
    return best


def _mm_res_call(a, w, h, g_post, g_next, name):
    M, K = a.shape
    D = w.shape[1]
    tm = _mm_res_tile(M, K, D)
    return pl.pallas_call(
        _mm_res_kernel,
        grid=(M // tm,),
        in_specs=[pl.BlockSpec((tm, K), lambda i: (i, 0)),
                  pl.BlockSpec((K, D), lambda i: (0, 0), pipeline_mode=pl.Buffered(1)),
                  pl.BlockSpec((tm, D), lambda i: (i, 0)),
                  pl.BlockSpec((1, D), lambda i: (0, 0)),
                  pl.BlockSpec((1, D), lambda i: (0, 0))],
        out_specs=[pl.BlockSpec((tm, D), lambda i: (i, 0)),
                   pl.BlockSpec((tm, D), lambda i: (i, 0))],
        out_shape=[jax.ShapeDtypeStruct((M, D), F32),
                   jax.ShapeDtypeStruct((M, D), BF16)],
        compiler_params=_cparams(("parallel",)),
        name=name,
    )(a, w, h, g_post.reshape(1, D), g_next.reshape(1, D))


def _pool_kernel(z_ref, w_ref, s_ref, o_ref, *, gw):
    L = z_ref.shape[1]
    row = lax.broadcasted_iota(jnp.int32, (L, gw), 0)
    for gi, win in enumerate(POOL_WINDOWS):
        sl = slice(gi * gw, (gi + 1) * gw)
        zg = z_ref[0, :, sl].astype(F32)
        s = zg
        d = 1
        while d < win:
            s = s + _shift_down(s, d)
            d *= 2
        cnt = jnp.minimum(row + 1, win).astype(F32)
        pooled = s / cnt - zg
        y = _dot(pooled, w_ref[gi]) * s_ref[:, sl]
        o_ref[0, :, sl] = y.astype(o_ref.dtype)


def _pool_call(z, w, scale, B, Lp):
    PW = z.shape[-1]
    G = len(POOL_WINDOWS)
    gw = PW // G
    z3 = z.reshape(B, Lp, PW)
    out = pl.pallas_call(
        functools.partial(_pool_kernel, gw=gw),
        grid=(B,),
        in_specs=[pl.BlockSpec((1, Lp, PW), lambda b: (b, 0, 0)),
                  pl.BlockSpec((G, gw, gw), lambda b: (0, 0, 0)),
                  pl.BlockSpec((1, PW), lambda b: (0, 0))],
        out_specs=pl.BlockSpec((1, Lp, PW), lambda b: (b, 0, 0)),
        out_shape=jax.ShapeDtypeStruct((B, Lp, PW), BF16),
        compiler_params=_cparams(("parallel",)),
        name="pool_mixer",
    )(z3, w.astype(BF16), scale.reshape(1, PW))
    return out.reshape(B * Lp, PW)


def _rel_bucket(rel, n_buckets):
    half = n_buckets // 2
    max_exact = half // 2
    n = np.abs(rel)
    large = max_exact + (np.log(np.maximum(n, 1) / max_exact) / math.log(REL_MAX_DIST / max_exact)
                         * (half - max_exact)).astype(np.int32)
    large = np.minimum(large, half - 1)
    return ((rel > 0) * half + np.where(n < max_exact, n, large)).astype(np.int32)


def _bias_table_kernel(rb_ref, bucket_ref, pen_ref, o_ref, *, n_buckets):
    s = pl.program_id(0)
    bucket = bucket_ref[...]
    acc = pen_ref[...]
    for b in range(n_buckets):
        acc = acc + jnp.where(bucket == b, rb_ref[b, s], 0.0)
    o_ref[0] = acc


def _bias_table_call(rel_bias, n_meta, tq):
    n_buckets, n_sub = rel_bias.shape
    KT = ATTN_TILE
    assert tq % KT == 0 and KT % CHUNK == 0 and n_meta <= KT
    n_near = tq // KT + 2
    n_slabs = n_near + 2
    far = n_buckets // 2 - 1
    r = np.arange(tq)[:, None]
    c = np.arange(KT)[None, :]
    assert np.all(_rel_bucket(-(KT + 1 + np.arange(4 * KT)), n_buckets) == far)
    bucket = np.full((n_slabs, tq, KT), far, np.int32)
    pen = np.zeros((n_slabs, tq, KT), np.float32)
    pen[-1] = NEG_INF
    for idx in range(n_near):
        key = (idx - 1) * KT + c
        bucket[1 + idx] = _rel_bucket(key - r, n_buckets)
        visible = np.floor_divide(key - n_meta, CHUNK) <= np.floor_divide(r - n_meta, CHUNK)
        pen[1 + idx] = np.where(visible, 0.0, NEG_INF)
    blk = pl.BlockSpec((n_slabs, tq, KT), lambda s: (0, 0, 0))
    return pl.pallas_call(
        functools.partial(_bias_table_kernel, n_buckets=n_buckets),
        grid=(n_sub,),
        in_specs=[pl.BlockSpec(memory_space=pltpu.SMEM), blk, blk],
        out_specs=pl.BlockSpec((1, n_slabs, tq, KT), lambda s: (s, 0, 0, 0)),
        out_shape=jax.ShapeDtypeStruct((n_sub, n_slabs, tq, KT), F32),
        compiler_params=_cparams(("arbitrary",)),
        name="bias_table_%d" % tq,
    )(rel_bias, jnp.asarray(bucket), jnp.asarray(pen))


def _attn_kernel(*refs, hd, scale, lam_init, nt, tile0, aliased):
    if aliased:
        refs = refs[1:]
    (lq1_ref, lk1_ref, lq2_ref, lk2_ref, sub_ref, q_ref, k_ref, v_ref, w_ref,
     o_ref, s_sc, mx_sc, l_sc, acc_sc) = refs
    KT = ATTN_TILE
    tq = q_ref.shape[1]
    ratio = tq // KT
    n_slabs = w_ref.shape[1]
    i = pl.program_id(2)
    jt0 = tile0 + i * ratio
    n_need = jnp.minimum(jt0 + ratio + 1, nt)
    n_pairs = (n_need + 1) // 2
    q = q_ref[0]

    def slab(j):
        return jnp.where(j < nt, jnp.clip(j - jt0 + 2, 0, n_slabs - 1), n_slabs - 1)

    def two_tiles(ref, jj):
        j0 = 2 * jj
        j1 = jnp.minimum(j0 + 1, nt - 1)
        t0 = ref[0, pl.ds(pl.multiple_of(j0 * KT, KT), KT), :]
        t1 = ref[0, pl.ds(pl.multiple_of(j1 * KT, KT), KT), :]
        return jnp.concatenate([t0, t1], axis=0)

    mx_sc[...] = jnp.full_like(mx_sc, NEG_INF)

    def pass1(jj, carry):
        kt = two_tiles(k_ref, jj)
        i0 = slab(2 * jj)
        i1 = slab(2 * jj + 1)
        for s in range(2):
            sc = _dot_nt(q[:, s * hd:(s + 1) * hd], kt[:, s * hd:(s + 1) * hd]) * scale
            sc = sc + jnp.concatenate([w_ref[s, i0], w_ref[s, i1]], axis=1)
            s_sc[s, jj] = sc
            mx_sc[s] = jnp.maximum(mx_sc[s], jnp.maximum(sc[:, :KT], sc[:, KT:]))
        return carry

    lax.fori_loop(0, n_pairs, pass1, 0)
    m = [jnp.max(mx_sc[s], axis=-1, keepdims=True) for s in range(2)]
    l_sc[...] = jnp.zeros_like(l_sc)
    acc_sc[...] = jnp.zeros_like(acc_sc)

    def pass2(jj, carry):
        vt = two_tiles(v_ref, jj)
        for s in range(2):
            p = jnp.exp(s_sc[s, jj] - m[s])
            l_sc[s] += p[:, :KT] + p[:, KT:]
            acc_sc[s] += jnp.dot(p.astype(BF16), vt, preferred_element_type=F32)
        return carry

    lax.fori_loop(0, n_pairs, pass2, 0)
    l = [jnp.sum(l_sc[s], axis=-1, keepdims=True) for s in range(2)]
    lam = (jnp.exp(jnp.sum(lq1_ref[...] * lk1_ref[...], axis=-1, keepdims=True))
           - jnp.exp(jnp.sum(lq2_ref[...] * lk2_ref[...], axis=-1, keepdims=True)) + lam_init)
    o = acc_sc[0] / l[0] - lam * (acc_sc[1] / l[1])
    y = o * lax.rsqrt(jnp.mean(o * o, axis=-1, keepdims=True) + DIFF_SUBLN_EPS)
    o_ref[0] = (y * sub_ref[...] * (1.0 - lam_init)).astype(o_ref.dtype)


def _attn_segment(prev, qkv3, table, params, lam_init, tq, row0, n_tiles):
    B, Lp, W3 = qkv3.shape
    DW = W3 // 3
    hd = params[0].shape[1]
    hw = 2 * hd
    n_heads = DW // hw
    KT = ATTN_TILE
    nt = Lp // KT
    blk0 = row0 // tq
    aliased = prev is not None
    kern = functools.partial(_attn_kernel, hd=hd, scale=hd ** -0.5, lam_init=lam_init, nt=nt,
                             tile0=row0 // KT, aliased=aliased)
    par = lambda n: pl.BlockSpec((1, n), lambda b, h, i: (0, 0))
    in_specs = [par(hd)] * 4 + [
        par(hw),
        pl.BlockSpec((1, tq, hw), lambda b, h, i: (b, blk0 + i, h)),
        pl.BlockSpec((1, Lp, hw), lambda b, h, i: (b, 0, n_heads + h)),
        pl.BlockSpec((1, Lp, hw), lambda b, h, i: (b, 0, 2 * n_heads + h)),
        pl.BlockSpec((2,) + table.shape[1:], lambda b, h, i: (h, 0, 0, 0))]
    args = list(params) + [qkv3, qkv3, qkv3, table]
    if aliased:
        in_specs = [pl.BlockSpec(memory_space=pl.ANY)] + in_specs
        args = [prev] + args
    max_pairs = (nt + 1) // 2
    return pl.pallas_call(
        kern,
        grid=(B, n_heads, n_tiles),
        in_specs=in_specs,
        out_specs=pl.BlockSpec((1, tq, hw), lambda b, h, i: (b, blk0 + i, h)),
        out_shape=jax.ShapeDtypeStruct((B, Lp, DW), BF16),
        scratch_shapes=[pltpu.VMEM((2, max_pairs, tq, 2 * KT), F32), pltpu.VMEM((2, tq, KT), F32),
                        pltpu.VMEM((2, tq, KT), F32), pltpu.VMEM((2, tq, hw), F32)],
        input_output_aliases={0: 0} if aliased else {},
        compiler_params=_cparams(("parallel", "parallel", "arbitrary")),
        name="diff_attention_%d" % tq,
    )(*args)


def _attn_segments(Lp):
    big = ATTN_Q_TILE
    n_big = Lp // big
    segs = []
    if n_big:
        segs.append((big, 0, n_big))
    rest = Lp - n_big * big
    if rest:
        segs.append((ATTN_TILE, n_big * big, rest // ATTN_TILE))
    return segs


def _attn_call(zqkv, tables, lq1, lk1, lq2, lk2, subln, B, Lp, lam_init):
    vec = lambda a: a.reshape(1, -1)
    params = [vec(lq1), vec(lk1), vec(lq2), vec(lk2), vec(subln)]
    qkv3 = zqkv.reshape(B, Lp, zqkv.shape[-1])
    out = None
    for tq, row0, n_tiles in _attn_segments(Lp):
        out = _attn_segment(out, qkv3, tables[tq], params, lam_init, tq, row0, n_tiles)
    return out.reshape(B * Lp, -1)


def _rwkv_prep_kernel(*refs, rw, has_vres):
    if has_vres:
        (z_ref, zp_ref, mu_ref, w0_ref, w2_ref, a0_ref, a2_ref, g2_ref, kk_ref, ka_ref, ones_ref,
         vf_ref, v0_ref, v1_ref, v2_ref,
         r_o, lw_o, k_o, v_o, a_o, b_o, g_o) = refs
    else:
        (z_ref, zp_ref, mu_ref, w0_ref, w2_ref, a0_ref, a2_ref, g2_ref, kk_ref, ka_ref, ones_ref,
         r_o, lw_o, k_o, v_o, a_o, b_o, g_o) = refs
    i = pl.program_id(1)
    z = z_ref[0]
    last_prev = zp_ref[0, SUBLANES - 1:SUBLANES, :]
    last_prev = jnp.where(i > 0, last_prev, 0.0)
    row = lax.broadcasted_iota(jnp.int32, z.shape, 0)
    prev = jnp.where(row >= 1, pltpu.roll(z, 1, 0), last_prev)
    zs = z + (prev - z) * mu_ref[...]
    r = zs[:, 0:rw]
    kx = zs[:, rw:2 * rw]
    vx = zs[:, 2 * rw:3 * rw]
    wd = zs[:, 3 * rw:3 * rw + LANES]
    ad = zs[:, 3 * rw + LANES:3 * rw + 2 * LANES]
    gd = zs[:, 3 * rw + 2 * LANES:3 * rw + 3 * LANES]
    wl = w0_ref[...] + _dot(jnp.tanh(wd), w2_ref[...])
    sp = jnp.maximum(-wl, 0.0) + jnp.log(1.0 + jnp.exp(-jnp.abs(wl)))
    lw = -jnp.exp(-sp - 0.5)
    if has_vres:
        gate_v = _sigmoid(v0_ref[...] + _dot(_dot(vx, v1_ref[...]), v2_ref[...]))
        vx = vx + (vf_ref[0] - vx) * gate_v
    a = _sigmoid(a0_ref[...] + _dot(ad, a2_ref[...]))
    g = _dot(_sigmoid(gd), g2_ref[...])
    kk = kx * kk_ref[...]
    ss = _dot_exact_rhs(kk * kk, ones_ref[...])
    kk = kk * lax.rsqrt(jnp.maximum(ss, 1e-24))
    kmod = kx * (1.0 + (a - 1.0) * ka_ref[...])
    r_o[0] = r
    lw_o[0] = lw
    k_o[0] = kmod
    v_o[0] = vx
    a_o[0] = -kk
    b_o[0] = kk * a
    g_o[0] = g


def _pad_rows(w, n):
    return jnp.pad(w, ((0, n - w.shape[0]), (0, 0)))


def _rwkv_prep_call(zr, B, Lp, rw, mu_p, w0, w2, a0, a2, g2, k_k, k_a, ones_bd, v_first, v_res):
    ZW = zr.shape[-1]
    z3 = zr.reshape(B, Lp, ZW)
    tl = _pick_tile(Lp, 384, SUBLANES)
    nb = tl // SUBLANES
    has_vres = v_res is not None
    vec = lambda a: a.reshape(1, -1)
    full = lambda a: pl.BlockSpec(a.shape, lambda b, i: (0,) * a.ndim)
    w2p = _pad_rows(w2, LANES).astype(BF16)
    a2p = _pad_rows(a2, LANES).astype(BF16)
    g2p = _pad_rows(g2, LANES).astype(BF16)
    args = [z3, z3, vec(mu_p), vec(w0), w2p, vec(a0), a2p, g2p, vec(k_k), vec(k_a), ones_bd]
    in_specs = [pl.BlockSpec((1, tl, ZW), lambda b, i: (b, i, 0)),
                pl.BlockSpec((1, SUBLANES, ZW), lambda b, i: (b, jnp.maximum(i * nb - 1, 0), 0))]
    in_specs += [full(a) for a in args[2:]]
    if has_vres:
        v0, v1, v2 = v_res
        v1p = jnp.pad(v1, ((0, 0), (0, LANES - v1.shape[1]))).astype(BF16)
        v2p = _pad_rows(v2, LANES).astype(BF16)
        extra = [v_first, vec(v0), v1p, v2p]
        args += extra
        in_specs += [pl.BlockSpec((1, tl, rw), lambda b, i: (b, i, 0))] + [full(a) for a in extra[1:]]
    o_spec = pl.BlockSpec((1, tl, rw), lambda b, i: (b, i, 0))
    o_shape = jax.ShapeDtypeStruct((B, Lp, rw), F32)
    return pl.pallas_call(
        functools.partial(_rwkv_prep_kernel, rw=rw, has_vres=has_vres),
        grid=(B, Lp // tl),
        in_specs=in_specs,
        out_specs=[o_spec] * 7,
        out_shape=[o_shape] * 7,
        compiler_params=_cparams(("parallel", "arbitrary")),
        name="rwkv_prep",
    )(*args)


def _scan_kernel(r_ref, lw_ref, k_ref, v_ref, a_ref, b_ref, g_ref, rk_ref, lnw_ref, lnb_ref, ones_ref,
                 o_ref, h_sc, *, hd):
    nb = r_ref.shape[0]
    C = r_ref.shape[1]
    C2 = 2 * C
    WB = r_ref.shape[2]
    W = nb * WB
    c = pl.program_id(1)

    @pl.when(c == 0)
    def _():
        h_sc[...] = jnp.zeros_like(h_sc)

    side_by_side = lambda ref: jnp.concatenate([ref[i] for i in range(nb)], axis=1)
    r = side_by_side(r_ref)
    lw = side_by_side(lw_ref)
    k = side_by_side(k_ref)
    v = side_by_side(v_ref)
    a = side_by_side(a_ref)
    b = side_by_side(b_ref)

    cum = lw
    d = 1
    while d < C:
        cum = cum + _shift_down(cum, d)
        d *= 2
    tot = cum[C - 1:C, :]
    e_end = jnp.exp(tot - cum)
    e_neg = jnp.exp(-cum)
    at = a * jnp.exp(cum - lw)
    rt = r * jnp.exp(cum)
    bt = b * e_neg
    kt = k * e_neg
    bb = b * e_end
    kb = k * e_end
    p_c = jnp.exp(tot)

    lane = lax.broadcasted_iota(jnp.int32, (C, LANES), 1)
    first = lane < hd
    rr = lax.broadcasted_iota(jnp.int32, (C2, C2), 0)
    cc = lax.broadcasted_iota(jnp.int32, (C2, C2), 1)
    tr = jnp.where(rr >= C, rr - C, rr)
    tc = jnp.where(cc >= C, cc - C, cc)
    strict = tr > tc
    lower = tr >= tc
    eye = rr == cc

    def stack(x):
        return jnp.concatenate([jnp.where(first, x, 0.0), jnp.where(first, 0.0, x)], axis=0)

    pairs = range(W // LANES)
    stacked = lambda x: [stack(x[:, p * LANES:(p + 1) * LANES]) for p in pairs]
    s_a, s_r, s_b, s_k = stacked(at), stacked(rt), stacked(bt), stacked(kt)
    s_bb, s_kb, s_v = stacked(bb), stacked(kb), stacked(v)
    sc = [_dot_nt(jnp.concatenate([s_a[p], s_r[p]], axis=0), jnp.concatenate([s_b[p], s_k[p]], axis=0))
          for p in pairs]
    n = [jnp.where(strict, sc[p][:C2, :C2], 0.0) for p in pairs]
    a_ak = [jnp.where(strict, sc[p][:C2, C2:], 0.0) for p in pairs]
    m_rb = [jnp.where(lower, sc[p][C2:, :C2], 0.0) for p in pairs]
    m_rk = [jnp.where(lower, sc[p][C2:, C2:], 0.0) for p in pairs]
    xv = [_dot(jnp.concatenate([a_ak[p], m_rk[p], s_kb[p].T], axis=0), s_v[p]) for p in pairs]
    t_inv = [jnp.where(eye, 1.0, 0.0) + n[p] for p in pairs]
    pw = [_dot(n[p], n[p]) for p in pairs]
    step = 4
    while step < C:
        tp = [_dot(jnp.concatenate([t_inv[p], pw[p]], axis=0), pw[p]) for p in pairs]
        t_inv = [t_inv[p] + tp[p][:C2] for p in pairs]
        pw = [tp[p][C2:] for p in pairs]
        step *= 2
    t_inv = [t_inv[p] + _dot(t_inv[p], pw[p]) for p in pairs]
    au = [_dot(t_inv[p], jnp.concatenate([s_a[p], xv[p][:C2]], axis=1)) for p in pairs]
    mbu = [_dot(jnp.concatenate([m_rb[p], s_bb[p].T], axis=0), au[p]) for p in pairs]
    ys = []
    for p in pairs:
        r_hat = s_r[p] + mbu[p][:C2, :LANES]
        y0 = mbu[p][:C2, LANES:] + xv[p][C2:2 * C2]
        g_mat = jnp.where(eye, p_c[:, p * LANES:(p + 1) * LANES], 0.0) + mbu[p][C2:, :LANES]
        h_add = mbu[p][C2:, LANES:] + xv[p][2 * C2:]
        yh = _dot(jnp.concatenate([r_hat, g_mat], axis=0), h_sc[p])
        y = yh[:C2] + y0
        h_sc[p] = yh[C2:] + h_add
        ys.append(y[0:C] + y[C:C2])
    ones = ones_ref[...]
    inv_hd = 1.0 / hd
    ppb = WB // LANES
    for i in range(nb):
        y = jnp.concatenate(ys[i * ppb:(i + 1) * ppb], axis=1)
        mean = _dot_exact_rhs(y, ones) * inv_hd
        yc = y - mean
        var = _dot_exact_rhs(yc * yc, ones) * inv_hd
        yn = yc * lax.rsqrt(var + RWKV_LNX_EPS) * lnw_ref[...] + lnb_ref[...]
        bonus = _dot_exact_rhs(r_ref[i] * k_ref[i] * rk_ref[...], ones) * v_ref[i]
        o_ref[i] = ((yn + bonus) * g_ref[i]).astype(o_ref.dtype)


def _scan_call(streams, r_k, lnx_w, lnx_b, ones_bd, B, Lp, hd):
    r, lw, k, v, a, b, g = streams
    rw = r.shape[-1]
    C = CHUNK
    assert 2 * hd == LANES and rw % LANES == 0 and Lp % C == 0
    vec = lambda x: x.reshape(1, -1)
    nb = SCAN_BATCH if B % SCAN_BATCH == 0 else 1
    blk = pl.BlockSpec((nb, C, rw), lambda bi, c: (bi, c, 0))
    par = pl.BlockSpec((1, rw), lambda bi, c: (0, 0))
    out = pl.pallas_call(
        functools.partial(_scan_kernel, hd=hd),
        grid=(B // nb, Lp // C),
        in_specs=[blk] * 7 + [par] * 3 + [pl.BlockSpec((rw, rw), lambda bi, c: (0, 0))],
        out_specs=blk,
        out_shape=jax.ShapeDtypeStruct((B, Lp, rw), BF16),
        scratch_shapes=[pltpu.VMEM((nb * rw // LANES, LANES, LANES), F32)],
        compiler_params=_cparams(("parallel", "arbitrary")),
        name="rwkv_scan",
    )(r, lw, k, v, a, b, g, vec(r_k), vec(lnx_w), vec(lnx_b), ones_bd)
    return out.reshape(B * Lp, rw)


def _merge_kernel(yp_ref, yd_ref, yr_ref, g0_ref, g1_ref, g2_ref, pa_ref, pb_ref, pc_ref, o_ref):
    dot = lambda x, w: jnp.dot(x[...], w[...], preferred_element_type=F32)
    m = _sigmoid(g0_ref[...].astype(F32)) * dot(yp_ref, pa_ref)
    m = m + _sigmoid(g1_ref[...].astype(F32)) * dot(yd_ref, pb_ref)
    m = m + _sigmoid(g2_ref[...].astype(F32)) * dot(yr_ref, pc_ref)
    o_ref[...] = m.astype(o_ref.dtype)


def _merge_call(yp, yd, yr, zg, pa, pb, pc, tm):
    M = yp.shape[0]
    D = pa.shape[1]
    rows = lambda a: pl.BlockSpec((tm, a.shape[1]), lambda i: (i, 0))
    full = lambda a: pl.BlockSpec(a.shape, lambda i: (0, 0))
    gate = lambda n: pl.BlockSpec((tm, D), lambda i: (i, n))
    return pl.pallas_call(
        _merge_kernel,
        grid=(M // tm,),
        in_specs=[rows(yp), rows(yd), rows(yr), gate(0), gate(1), gate(2), full(pa), full(pb), full(pc)],
        out_specs=pl.BlockSpec((tm, D), lambda i: (i, 0)),
        out_shape=jax.ShapeDtypeStruct((M, D), BF16),
        compiler_params=_cparams(("parallel",)),
        name="gated_merge",
    )(yp, yd, yr, zg, zg, zg, pa, pb, pc)


def _ffn_up_kernel(u_ref, wv_ref, wg_ref, cwv_ref, cwg_ref, cbv_ref, cbg_ref, o_ref, *scratch, n_sub):
    xg_scs, xv_scs, a_scs = scratch[:n_sub], scratch[n_sub:2 * n_sub], scratch[2 * n_sub:]
    L = u_ref.shape[0]
    mb = L // n_sub
    st = mb // SUBLANES
    PAD = SUBLANES
    n_slab = a_scs[0].shape[0]
    k_in = math.sqrt(2.0 / math.pi)
    zeros = jnp.zeros((PAD, LANES), F32)
    for s in range(n_slab):
        xv_scs[0][s, 0:PAD, :] = zeros
        xg_scs[0][s, 0:PAD, :] = zeros

    def taps(cw_ref, cb_ref, s):
        sl = slice(s * LANES, (s + 1) * LANES)
        rows = [jnp.broadcast_to(cw_ref[j:j + 1, sl], (SUBLANES, LANES)) for j in range(FFN_CONV)]
        return rows, jnp.broadcast_to(cb_ref[:, sl], (SUBLANES, LANES))

    def tail(sb):
        lo = sb * mb
        xg_sc, xv_sc, a_sc = xg_scs[sb], xv_scs[sb], a_scs[sb]
        for s in range(n_slab):
            wg, bg = taps(cwg_ref, cbg_ref, s)
            wv, bv = taps(cwv_ref, cbv_ref, s)
            ld = lambda ref, j: ref[s, pl.ds(PAD + j, SUBLANES, stride=st), :]
            xg = [ld(xg_sc, j) for j in range(1 - FFN_CONV, 0)]
            xv = [ld(xv_sc, j) for j in range(1 - FFN_CONV, 0)]
            for j in range(st):
                xg.append(ld(xg_sc, j))
                xv.append(ld(xv_sc, j))
                gate = bg
                val = bv
                for t in range(FFN_CONV):
                    gate = gate + xg[j + t] * wg[t]
                    val = val + xv[j + t] * wv[t]
                inner = gate * (2.0 * k_in + (2.0 * k_in * 0.044715) * (gate * gate))
                act = gate / (1.0 + jnp.exp(-inner)) * val
                a_sc[s, pl.ds(j, SUBLANES, stride=st), :] = act
        for s in range(n_slab):
            o_ref[lo:lo + mb, s * LANES:(s + 1) * LANES] = a_sc[s].astype(o_ref.dtype)

    for sb in range(n_sub):
        lo = sb * mb
        u = u_ref[lo:lo + mb, :]
        for x_scs, w_ref in ((xg_scs, wg_ref), (xv_scs, wv_ref)):
            x = jnp.dot(u, w_ref[...], preferred_element_type=F32)
            for s in range(n_slab):
                xs = x[:, s * LANES:(s + 1) * LANES]
                x_scs[sb][s, PAD:PAD + mb, :] = xs
                if sb + 1 < n_sub:
                    x_scs[sb + 1][s, 0:PAD, :] = xs[mb - PAD:mb]
        if sb:
            tail(sb - 1)
    tail(n_sub - 1)


def _ffn_up_call(u, w_up, conv_w, conv_b, B, Lp, tn):
    M, D = u.shape
    FF = w_up.shape[1] // 2
    nt = FF // tn
    n_sub = 4 if Lp % (4 * 2 * SUBLANES) == 0 else 1
    mb = Lp // n_sub
    return pl.pallas_call(
        functools.partial(_ffn_up_kernel, n_sub=n_sub),
        grid=(B, nt),
        in_specs=[pl.BlockSpec((Lp, D), lambda b, j: (b, 0)),
                  pl.BlockSpec((D, tn), lambda b, j: (0, j)),
                  pl.BlockSpec((D, tn), lambda b, j: (0, nt + j)),
                  pl.BlockSpec((FFN_CONV, tn), lambda b, j: (0, j)),
                  pl.BlockSpec((FFN_CONV, tn), lambda b, j: (0, nt + j)),
                  pl.BlockSpec((1, tn), lambda b, j: (0, j)),
                  pl.BlockSpec((1, tn), lambda b, j: (0, nt + j))],
        out_specs=pl.BlockSpec((Lp, tn), lambda b, j: (b, j)),
        out_shape=jax.ShapeDtypeStruct((M, FF), BF16),
        scratch_shapes=([pltpu.VMEM((tn // LANES, SUBLANES + mb, LANES), F32)] * (2 * n_sub)
                        + [pltpu.VMEM((tn // LANES, mb, LANES), F32)] * n_sub),
        compiler_params=_cparams(("parallel", "arbitrary")),
        name="ffn_up_conv_geglu",
    )(u, w_up, w_up, conv_w, conv_w, conv_b.reshape(1, -1), conv_b.reshape(1, -1))


def kernel(x, meta, rel_bias, norm_mix_pre, norm_mix_post, norm_ffn_pre, norm_ffn_post, w_in, pool_w, pool_scale, diff_lq1, diff_lk1, diff_lq2, diff_lk2, diff_subln, rwkv_mu, rwkv_w0, rwkv_w2, rwkv_a0, rwkv_a2, rwkv_g2, rwkv_kk, rwkv_ka, rwkv_rk, rwkv_lnx_w, rwkv_lnx_b, rwkv_v0, rwkv_v1, rwkv_v2, p_a, p_b, p_c, w_o, ffn_up, ffn_conv_w, ffn_conv_b, ffn_down):
    B, S, D = x.shape
    n_meta = meta.shape[0]
    depth = w_in.shape[0]
    L = S + n_meta
    Lp = _round_up(L, ATTN_TILE)
    M = B * Lp

    PW = pool_scale.shape[1]
    DW = p_b.shape[1]
    RW = rwkv_w0.shape[1]
    hd_r = rwkv_rk.shape[2]
    dl, al, gl = rwkv_w2.shape[1], rwkv_a2.shape[1], rwkv_g2.shape[1]
    off_q = PW
    off_r = PW + 3 * DW
    off_g = off_r + 3 * RW + dl + al + gl
    assert off_g + 3 * D == w_in.shape[2]
    assert max(dl, al, gl, rwkv_v1.shape[2]) <= LANES

    h = jnp.concatenate([jnp.broadcast_to(meta.astype(x.dtype)[None], (B, n_meta, D)), x,
                         jnp.zeros((B, Lp - L, D), x.dtype)], axis=1).reshape(M, D)

    hid = np.arange(RW) // hd_r
    ones_bd = jnp.asarray((hid[:, None] == hid[None, :]).astype(np.float32), dtype=BF16)

    def pad_lora(a, axis):
        segs = [lax.slice_in_dim(a, 0, 3 * RW, axis=axis)]
        o = 3 * RW
        for n in (dl, al, gl):
            seg = lax.slice_in_dim(a, o, o + n, axis=axis)
            pad = [(0, 0)] * a.ndim
            pad[axis] = (0, LANES - n)
            segs.append(jnp.pad(seg, pad))
            o += n
        return jnp.concatenate(segs, axis=axis)

    tables = {tq: _bias_table_call(rel_bias, n_meta, tq) for tq, _, _ in _attn_segments(Lp)}

    tm_big = _pick_tile(M, 1088, 16)
    tm_res = _pick_tile(M, 544, 16)
    u = _rms_call(h, norm_mix_pre[0], tm_res)
    v_first = None
    for l in range(depth):
        lam_init = 0.8 - 0.6 * math.exp(-0.3 * l)
        wl = w_in[l]
        w_gate = wl[:, off_g:].astype(BF16)
        w_qkv = wl[:, off_q:off_r].astype(BF16)
        w_pool = wl[:, :PW].astype(BF16)
        w_rwkv = pad_lora(wl[:, off_r:off_g], 1).astype(BF16)
        mu_p = pad_lora(rwkv_mu[l], 0)

        zg = _mm_call(u, w_gate, BF16, tm_big, _pick_tile(3 * D, 768, LANES), "in_proj_gate")
        zqkv = _mm_call(u, w_qkv, BF16, tm_big, _pick_tile(3 * DW, 768, LANES), "in_proj_qkv")
        zp = _mm_call(u, w_pool, BF16, tm_big, _pick_tile(PW, 768, LANES), "in_proj_pool")
        zr = _mm_call(u, w_rwkv, F32, tm_big, _pick_tile(w_rwkv.shape[1], 768, LANES), "in_proj_rwkv")

        yp = _pool_call(zp, pool_w[l], pool_scale[l], B, Lp)
        yd = _attn_call(zqkv, tables, diff_lq1[l], diff_lk1[l], diff_lq2[l], diff_lk2[l],
                        diff_subln[l], B, Lp, lam_init)
        v_res = None if l == 0 else (rwkv_v0[l - 1], rwkv_v1[l - 1], rwkv_v2[l - 1])
        streams = _rwkv_prep_call(zr, B, Lp, RW, mu_p, rwkv_w0[l], rwkv_w2[l], rwkv_a0[l], rwkv_a2[l],
                                  rwkv_g2[l], rwkv_kk[l], rwkv_ka[l], ones_bd, v_first, v_res)
        if l == 0:
            v_first = streams[3]
        yr = _scan_call(streams, rwkv_rk[l], rwkv_lnx_w[l], rwkv_lnx_b[l], ones_bd, B, Lp, hd_r)

        merged = _merge_call(yp, yd, yr, zg, p_a[l].astype(BF16), p_b[l].astype(BF16),
                             p_c[l].astype(BF16), tm_res)
        h, u2 = _mm_res_call(merged, w_o[l].astype(BF16), h, norm_mix_post[l], norm_ffn_pre[l],
                             "out_proj_res")
        FF = ffn_down.shape[1]
        act = _ffn_up_call(u2, ffn_up[l].astype(BF16), ffn_conv_w[l], ffn_conv_b[l], B, Lp,
                           _pick_tile(FF, 512, LANES))
        g_next = norm_mix_pre[l + 1] if l + 1 < depth else norm_mix_pre[l]
        h, u = _mm_res_call(act, ffn_down[l].astype(BF16), h, norm_ffn_post[l], g_next,
                            "ffn_down_res")
    return h.reshape(B, Lp, D)[:, n_meta:n_meta + S]
```

```python
import functools
import math

import numpy as np
import jax
import jax.numpy as jnp
from jax import lax
from jax.experimental import pallas as pl
from jax.experimental.pallas import tpu as pltpu

F32 = jnp.float32
BF16 = jnp.bfloat16

CHUNK = 64
NORM_EPS = 1e-6
NEG_INF = -1e30
POOL_WINDOWS = (2, 4, 8, 16)
DIFF_SUBLN_EPS = 1e-5
REL_MAX_DIST = 128
RWKV_LNX_EPS = 64e-5
FFN_CONV = 3

LANES = 128
SUBLANES = 8
ATTN_TILE = 128
ATTN_Q_TILE = 512
ATTN_UNROLL = 3
SCAN_BATCH = 4
VMEM_LIMIT = 56 * 1024 * 1024


def _cparams(sem):
    return pltpu.CompilerParams(dimension_semantics=sem, vmem_limit_bytes=VMEM_LIMIT)


def _round_up(x, m):
    return (x + m - 1) // m * m


def _pick_tile(total, target, mult):
    best = None
    for t in range(mult, min(total, target) + 1, mult):
        if total % t == 0:
            best = t
    assert best is not None, (total, target, mult)
    return best


def _shift_down(x, d):
    row = lax.broadcasted_iota(jnp.int32, x.shape, 0)
    return jnp.where(row >= d, pltpu.roll(x, d, 0), 0.0)


def _sigmoid(x):
    return 1.0 / (1.0 + jnp.exp(-x))


def _dot(a, b):
    return jnp.dot(a.astype(BF16), b.astype(BF16), preferred_element_type=F32)


def _dot_nt(a, b):
    return lax.dot_general(a.astype(BF16), b.astype(BF16), (((1,), (1,)), ((), ())),
                           preferred_element_type=F32)


def _split3(x):
    hi = x.astype(BF16)
    r1 = x - hi.astype(F32)
    mid = r1.astype(BF16)
    lo = (r1 - mid.astype(F32)).astype(BF16)
    return hi, mid, lo


def _dot_exact_rhs(x, ones_bf16):
    hi, mid, lo = _split3(x)
    acc = jnp.dot(hi, ones_bf16, preferred_element_type=F32)
    acc = acc + jnp.dot(mid, ones_bf16, preferred_element_type=F32)
    return acc + jnp.dot(lo, ones_bf16, preferred_element_type=F32)


def _rms_kernel(x_ref, g_ref, o_ref):
    x = x_ref[...]
    y = x * lax.rsqrt(jnp.mean(x * x, axis=-1, keepdims=True) + NORM_EPS)
    o_ref[...] = (y * g_ref[...]).astype(o_ref.dtype)


def _rms_call(x, g, tm):
    M, D = x.shape
    return pl.pallas_call(
        _rms_kernel,
        grid=(M // tm,),
        in_specs=[pl.BlockSpec((tm, D), lambda i: (i, 0)),
                  pl.BlockSpec((1, D), lambda i: (0, 0))],
        out_specs=pl.BlockSpec((tm, D), lambda i: (i, 0)),
        out_shape=jax.ShapeDtypeStruct((M, D), BF16),
        compiler_params=_cparams(("parallel",)),
        name="rms_norm",
    )(x, g.reshape(1, D))


def _mm_kernel(a_ref, b_ref, o_ref):
    o_ref[...] = jnp.dot(a_ref[...], b_ref[...], preferred_element_type=F32).astype(o_ref.dtype)


def _mm_call(a, b, out_dtype, tm, tn, name):
    M, K = a.shape
    N = b.shape[1]
    return pl.pallas_call(
        _mm_kernel,
        grid=(M // tm, N // tn),
        in_specs=[pl.BlockSpec((tm, K), lambda i, j: (i, 0)),
                  pl.BlockSpec((K, tn), lambda i, j: (0, j))],
        out_specs=pl.BlockSpec((tm, tn), lambda i, j: (i, j)),
        out_shape=jax.ShapeDtypeStruct((M, N), out_dtype),
        compiler_params=_cparams(("parallel", "arbitrary")),
        name=name,
    )(a, b)


def _mm_res_kernel(a_ref, w_ref, h_ref, gp_ref, gn_ref, oh_ref, ou_ref):
    f = jnp.dot(a_ref[...], w_ref[...], preferred_element_type=F32)
    y = f * lax.rsqrt(jnp.mean(f * f, axis=-1, keepdims=True) + NORM_EPS) * gp_ref[...]
    hn = h_ref[...] + y
    oh_ref[...] = hn
    u = hn * lax.rsqrt(jnp.mean(hn * hn, axis=-1, keepdims=True) + NORM_EPS) * gn_ref[...]
    ou_ref[...] = u.astype(ou_ref.dtype)


def _mm_res_tile(M, K, D):
    budget = VMEM_LIMIT - 8 * 1024 * 1024
    best = None
    for tm in range(16, M + 1, 16):
        if M % tm:
            continue
        need = (K * D * 2
                + 2 * tm * K * 2
                + 2 * 2 * tm * D * 4
                + 2 * tm * D * 2
                + 2 * tm * D * 4)
        if need <= budget:
            best = tm
    assert best is not None
    return best


def _mm_res_call(a, w, h, g_post, g_next, name):
    M, K = a.shape
    D = w.shape[1]
    tm = _mm_res_tile(M, K, D)
    return pl.pallas_call(
        _mm_res_kernel,
        grid=(M // tm,),
        in_specs=[pl.BlockSpec((tm, K), lambda i: (i, 0)),
                  pl.BlockSpec((K, D), lambda i: (0, 0), pipeline_mode=pl.Buffered(1)),
                  pl.BlockSpec((tm, D), lambda i: (i, 0)),
                  pl.BlockSpec((1, D), lambda i: (0, 0)),
                  pl.BlockSpec((1, D), lambda i: (0, 0))],
        out_specs=[pl.BlockSpec((tm, D), lambda i: (i, 0)),
                   pl.BlockSpec((tm, D), lambda i: (i, 0))],
        out_shape=[jax.ShapeDtypeStruct((M, D), F32),
                   jax.ShapeDtypeStruct((M, D), BF16)],
        compiler_params=_cparams(("parallel",)),
        name=name,
    )(a, w, h, g_post.reshape(1, D), g_next.reshape(1, D))


def _pool_kernel(z_ref, w_ref, s_ref, o_ref, *, gw):
    L = z_ref.shape[1]
    row = lax.broadcasted_iota(jnp.int32, (L, gw), 0)
    for gi, win in enumerate(POOL_WINDOWS):
        sl = slice(gi * gw, (gi + 1) * gw)
        zg = z_ref[0, :, sl].astype(F32)
        s = zg
        d = 1
        while d < win:
            s = s + _shift_down(s, d)
            d *= 2
        cnt = jnp.minimum(row + 1, win).astype(F32)
        pooled = s / cnt - zg
        y = _dot(pooled, w_ref[gi]) * s_ref[:, sl]
        o_ref[0, :, sl] = y.astype(o_ref.dtype)


def _pool_call(z, w, scale, B, Lp):
    PW = z.shape[-1]
    G = len(POOL_WINDOWS)
    gw = PW // G
    z3 = z.reshape(B, Lp, PW)
    out = pl.pallas_call(
        functools.partial(_pool_kernel, gw=gw),
        grid=(B,),
        in_specs=[pl.BlockSpec((1, Lp, PW), lambda b: (b, 0, 0)),
                  pl.BlockSpec((G, gw, gw), lambda b: (0, 0, 0)),
                  pl.BlockSpec((1, PW), lambda b: (0, 0))],
        out_specs=pl.BlockSpec((1, Lp, PW), lambda b: (b, 0, 0)),
        out_shape=jax.ShapeDtypeStruct((B, Lp, PW), BF16),
        compiler_params=_cparams(("parallel",)),
        name="pool_mixer",
    )(z3, w.astype(BF16), scale.reshape(1, PW))
    return out.reshape(B * Lp, PW)


def _rel_bucket(rel, n_buckets):
    half = n_buckets // 2
    max_exact = half // 2
    n = np.abs(rel)
    large = max_exact + (np.log(np.maximum(n, 1) / max_exact) / math.log(REL_MAX_DIST / max_exact)
                         * (half - max_exact)).astype(np.int32)
    large = np.minimum(large, half - 1)
    return ((rel > 0) * half + np.where(n < max_exact, n, large)).astype(np.int32)


def _bias_table_kernel(rb_ref, bucket_ref, pen_ref, o_ref, *, n_buckets):
    s = pl.program_id(0)
    bucket = bucket_ref[...]
    acc = pen_ref[...]
    for b in range(n_buckets):
        acc = acc + jnp.where(bucket == b, rb_ref[b, s], 0.0)
    o_ref[0] = acc


def _bias_table_call(rel_bias, n_meta, tq):
    n_buckets, n_sub = rel_bias.shape
    KT = ATTN_TILE
    assert tq % KT == 0 and KT % CHUNK == 0 and n_meta <= KT
    n_near = tq // KT + 2
    n_slabs = n_near + 2
    far = n_buckets // 2 - 1
    r = np.arange(tq)[:, None]
    c = np.arange(KT)[None, :]
    assert np.all(_rel_bucket(-(KT + 1 + np.arange(4 * KT)), n_buckets) == far)
    bucket = np.full((n_slabs, tq, KT), far, np.int32)
    pen = np.zeros((n_slabs, tq, KT), np.float32)
    pen[-1] = NEG_INF
    for idx in range(n_near):
        key = (idx - 1) * KT + c
        bucket[1 + idx] = _rel_bucket(key - r, n_buckets)
        visible = np.floor_divide(key - n_meta, CHUNK) <= np.floor_divide(r - n_meta, CHUNK)
        pen[1 + idx] = np.where(visible, 0.0, NEG_INF)
    blk = pl.BlockSpec((n_slabs, tq, KT), lambda s: (0, 0, 0))
    return pl.pallas_call(
        functools.partial(_bias_table_kernel, n_buckets=n_buckets),
        grid=(n_sub,),
        in_specs=[pl.BlockSpec(memory_space=pltpu.SMEM), blk, blk],
        out_specs=pl.BlockSpec((1, n_slabs, tq, KT), lambda s: (s, 0, 0, 0)),
        out_shape=jax.ShapeDtypeStruct((n_sub, n_slabs, tq, KT), F32),
        compiler_params=_cparams(("arbitrary",)),
        name="bias_table_%d" % tq,
    )(rel_bias, jnp.asarray(bucket), jnp.asarray(pen))


def _attn_kernel(*refs, hd, scale, lam_init, nt, tile0):
    (lq1_ref, lk1_ref, lq2_ref, lk2_ref, sub_ref, q_ref, k_ref, v_ref, w_ref,
     o_ref, s_sc, mx_sc, l_sc, acc_sc) = refs
    KT = ATTN_TILE
    tq = q_ref.shape[1]
    ratio = tq // KT
    n_slabs = w_ref.shape[1]
    i = pl.program_id(2)
    jt0 = tile0 + i * ratio
    n_need = jnp.minimum(jt0 + ratio + 1, nt)
    n_pairs = (n_need + 1) // 2
    q = q_ref[0]

    def slab(j):
        return jnp.where(j < nt, jnp.clip(j - jt0 + 2, 0, n_slabs - 1), n_slabs - 1)

    def two_tiles(ref, jj):
        j0 = 2 * jj
        j1 = jnp.minimum(j0 + 1, nt - 1)
        t0 = ref[0, pl.ds(pl.multiple_of(j0 * KT, KT), KT), :]
        t1 = ref[0, pl.ds(pl.multiple_of(j1 * KT, KT), KT), :]
        return jnp.concatenate([t0, t1], axis=0)

    mx_sc[...] = jnp.full_like(mx_sc, NEG_INF)

    def unrolled(step):
        def group(t, carry):
            for r in range(ATTN_UNROLL):
                step(ATTN_UNROLL * t + r)
            return carry

        def single(jj, carry):
            step(jj)
            return carry
        n_groups = n_pairs // ATTN_UNROLL
        lax.fori_loop(0, n_groups, group, 0)
        lax.fori_loop(n_groups * ATTN_UNROLL, n_pairs, single, 0)

    def pass1(jj):
        kt = two_tiles(k_ref, jj)
        i0 = slab(2 * jj)
        i1 = slab(2 * jj + 1)
        for s in range(2):
            sc = _dot_nt(q[:, s * hd:(s + 1) * hd], kt[:, s * hd:(s + 1) * hd]) * scale
            sc = sc + jnp.concatenate([w_ref[s, i0], w_ref[s, i1]], axis=1)
            s_sc[s, jj] = sc
            mx_sc[s] = jnp.maximum(mx_sc[s], jnp.maximum(sc[:, :KT], sc[:, KT:]))

    unrolled(pass1)
    m = [jnp.max(mx_sc[s], axis=-1, keepdims=True) for s in range(2)]
    l_sc[...] = jnp.zeros_like(l_sc)
    acc_sc[...] = jnp.zeros_like(acc_sc)

    def pass2(jj):
        vt = two_tiles(v_ref, jj)
        for s in range(2):
            p = jnp.exp(s_sc[s, jj] - m[s])
            l_sc[s] += p[:, :KT] + p[:, KT:]
            acc_sc[s] += jnp.dot(p.astype(BF16), vt, preferred_element_type=F32)

    unrolled(pass2)
    l = [jnp.sum(l_sc[s], axis=-1, keepdims=True) for s in range(2)]
    lam = (jnp.exp(jnp.sum(lq1_ref[...] * lk1_ref[...], axis=-1, keepdims=True))
           - jnp.exp(jnp.sum(lq2_ref[...] * lk2_ref[...], axis=-1, keepdims=True)) + lam_init)
    o = acc_sc[0] / l[0] - lam * (acc_sc[1] / l[1])
    y = o * lax.rsqrt(jnp.mean(o * o, axis=-1, keepdims=True) + DIFF_SUBLN_EPS)
    o_ref[0] = (y * sub_ref[...] * (1.0 - lam_init)).astype(o_ref.dtype)


def _attn_segment(qkv3, table, params, lam_init, tq, row0, n_tiles):
    B, Lp, W3 = qkv3.shape
    DW = W3 // 3
    hd = params[0].shape[1]
    hw = 2 * hd
    n_heads = DW // hw
    KT = ATTN_TILE
    nt = Lp // KT
    blk0 = row0 // tq
    kern = functools.partial(_attn_kernel, hd=hd, scale=hd ** -0.5, lam_init=lam_init, nt=nt,
                             tile0=row0 // KT)
    par = lambda n: pl.BlockSpec((1, n), lambda b, h, i: (0, 0))
    in_specs = [par(hd)] * 4 + [
        par(hw),
        pl.BlockSpec((1, tq, hw), lambda b, h, i: (b, blk0 + i, h)),
        pl.BlockSpec((1, Lp, hw), lambda b, h, i: (b, 0, n_heads + h)),
        pl.BlockSpec((1, Lp, hw), lambda b, h, i: (b, 0, 2 * n_heads + h)),
        pl.BlockSpec((2,) + table.shape[1:], lambda b, h, i: (h, 0, 0, 0))]
    args = list(params) + [qkv3, qkv3, qkv3, table]
    max_pairs = (nt + 1) // 2
    return pl.pallas_call(
        kern,
        grid=(B, n_heads, n_tiles),
        in_specs=in_specs,
        out_specs=pl.BlockSpec((1, tq, hw), lambda b, h, i: (b, i, h)),
        out_shape=jax.ShapeDtypeStruct((B, n_tiles * tq, DW), BF16),
        scratch_shapes=[pltpu.VMEM((2, max_pairs, tq, 2 * KT), F32), pltpu.VMEM((2, tq, KT), F32),
                        pltpu.VMEM((2, tq, KT), F32), pltpu.VMEM((2, tq, hw), F32)],
        compiler_params=_cparams(("parallel", "parallel", "arbitrary")),
        name="diff_attention_%d" % tq,
    )(*args)


def _attn_segments(Lp):
    big = ATTN_Q_TILE
    n_big = Lp // big
    segs = []
    if n_big:
        segs.append((big, 0, n_big))
    rest = Lp - n_big * big
    if rest:
        segs.append((ATTN_TILE, n_big * big, rest // ATTN_TILE))
    return segs


def _attn_call(zqkv, tables, lq1, lk1, lq2, lk2, subln, B, Lp, lam_init):
    vec = lambda a: a.reshape(1, -1)
    params = [vec(lq1), vec(lk1), vec(lq2), vec(lk2), vec(subln)]
    qkv3 = zqkv.reshape(B, Lp, zqkv.shape[-1])
    outs = [_attn_segment(qkv3, tables[tq], params, lam_init, tq, row0, n_tiles)
            for tq, row0, n_tiles in _attn_segments(Lp)]
    out = outs[0] if len(outs) == 1 else jnp.concatenate(outs, axis=1)
    return out.reshape(B * Lp, -1)


def _rwkv_prep_kernel(*refs, rw, has_vres):
    if has_vres:
        (z_ref, zp_ref, mu_ref, w0_ref, w2_ref, a0_ref, a2_ref, g2_ref, kk_ref, ka_ref, ones_ref,
         vf_ref, v0_ref, v1_ref, v2_ref,
         r_o, lw_o, k_o, v_o, a_o, b_o, g_o) = refs
    else:
        (z_ref, zp_ref, mu_ref, w0_ref, w2_ref, a0_ref, a2_ref, g2_ref, kk_ref, ka_ref, ones_ref,
         r_o, lw_o, k_o, v_o, a_o, b_o, g_o) = refs
    i = pl.program_id(1)
    z = z_ref[0]
    last_prev = zp_ref[0, SUBLANES - 1:SUBLANES, :]
    last_prev = jnp.where(i > 0, last_prev, 0.0)
    row = lax.broadcasted_iota(jnp.int32, z.shape, 0)
    prev = jnp.where(row >= 1, pltpu.roll(z, 1, 0), last_prev)
    zs = z + (prev - z) * mu_ref[...]
    r = zs[:, 0:rw]
    kx = zs[:, rw:2 * rw]
    vx = zs[:, 2 * rw:3 * rw]
    wd = zs[:, 3 * rw:3 * rw + LANES]
    ad = zs[:, 3 * rw + LANES:3 * rw + 2 * LANES]
    gd = zs[:, 3 * rw + 2 * LANES:3 * rw + 3 * LANES]
    wl = w0_ref[...] + _dot(jnp.tanh(wd), w2_ref[...])
    sp = jnp.maximum(-wl, 0.0) + jnp.log(1.0 + jnp.exp(-jnp.abs(wl)))
    lw = -jnp.exp(-sp - 0.5)
    if has_vres:
        gate_v = _sigmoid(v0_ref[...] + _dot(_dot(vx, v1_ref[...]), v2_ref[...]))
        vx = vx + (vf_ref[0] - vx) * gate_v
    a = _sigmoid(a0_ref[...] + _dot(ad, a2_ref[...]))
    g = _dot(_sigmoid(gd), g2_ref[...])
    kk = kx * kk_ref[...]
    ss = _dot_exact_rhs(kk * kk, ones_ref[...])
    kk = kk * lax.rsqrt(jnp.maximum(ss, 1e-24))
    kmod = kx * (1.0 + (a - 1.0) * ka_ref[...])
    r_o[0] = r
    lw_o[0] = lw
    k_o[0] = kmod
    v_o[0] = vx
    a_o[0] = -kk
    b_o[0] = kk * a
    g_o[0] = g


def _pad_rows(w, n):
    return jnp.pad(w, ((0, n - w.shape[0]), (0, 0)))


def _rwkv_prep_call(zr, B, Lp, rw, mu_p, w0, w2, a0, a2, g2, k_k, k_a, ones_bd, v_first, v_res):
    ZW = zr.shape[-1]
    z3 = zr.reshape(B, Lp, ZW)
    tl = _pick_tile(Lp, 384, SUBLANES)
    nb = tl // SUBLANES
    has_vres = v_res is not None
    vec = lambda a: a.reshape(1, -1)
    full = lambda a: pl.BlockSpec(a.shape, lambda b, i: (0,) * a.ndim)
    w2p = _pad_rows(w2, LANES).astype(BF16)
    a2p = _pad_rows(a2, LANES).astype(BF16)
    g2p = _pad_rows(g2, LANES).astype(BF16)
    args = [z3, z3, vec(mu_p), vec(w0), w2p, vec(a0), a2p, g2p, vec(k_k), vec(k_a), ones_bd]
    in_specs = [pl.BlockSpec((1, tl, ZW), lambda b, i: (b, i, 0)),
                pl.BlockSpec((1, SUBLANES, ZW), lambda b, i: (b, jnp.maximum(i * nb - 1, 0), 0))]
    in_specs += [full(a) for a in args[2:]]
    if has_vres:
        v0, v1, v2 = v_res
        v1p = jnp.pad(v1, ((0, 0), (0, LANES - v1.shape[1]))).astype(BF16)
        v2p = _pad_rows(v2, LANES).astype(BF16)
        extra = [v_first, vec(v0), v1p, v2p]
        args += extra
        in_specs += [pl.BlockSpec((1, tl, rw), lambda b, i: (b, i, 0))] + [full(a) for a in extra[1:]]
    o_spec = pl.BlockSpec((1, tl, rw), lambda b, i: (b, i, 0))
    o_shape = jax.ShapeDtypeStruct((B, Lp, rw), F32)
    return pl.pallas_call(
        functools.partial(_rwkv_prep_kernel, rw=rw, has_vres=has_vres),
        grid=(B, Lp // tl),
        in_specs=in_specs,
        out_specs=[o_spec] * 7,
        out_shape=[o_shape] * 7,
        compiler_params=_cparams(("parallel", "arbitrary")),
        name="rwkv_prep",
    )(*args)


def _scan_kernel(r_ref, lw_ref, k_ref, v_ref, a_ref, b_ref, g_ref, rk_ref, lnw_ref, lnb_ref, ones_ref,
                 o_ref, h_sc, *, hd):
    nb = r_ref.shape[0]
    C = r_ref.shape[1]
    C2 = 2 * C
    WB = r_ref.shape[2]
    W = nb * WB
    c = pl.program_id(1)

    @pl.when(c == 0)
    def _():
        h_sc[...] = jnp.zeros_like(h_sc)

    side_by_side = lambda ref: jnp.concatenate([ref[i] for i in range(nb)], axis=1)
    r = side_by_side(r_ref)
    lw = side_by_side(lw_ref)
    k = side_by_side(k_ref)
    v = side_by_side(v_ref)
    a = side_by_side(a_ref)
    b = side_by_side(b_ref)

    cum = lw
    d = 1
    while d < C:
        cum = cum + _shift_down(cum, d)
        d *= 2
    tot = cum[C - 1:C, :]
    e_end = jnp.exp(tot - cum)
    e_neg = jnp.exp(-cum)
    at = a * jnp.exp(cum - lw)
    rt = r * jnp.exp(cum)
    bt = b * e_neg
    kt = k * e_neg
    bb = b * e_end
    kb = k * e_end
    p_c = jnp.exp(tot)

    lane = lax.broadcasted_iota(jnp.int32, (C, LANES), 1)
    first = lane < hd
    rr = lax.broadcasted_iota(jnp.int32, (C2, C2), 0)
    cc = lax.broadcasted_iota(jnp.int32, (C2, C2), 1)
    tr = jnp.where(rr >= C, rr - C, rr)
    tc = jnp.where(cc >= C, cc - C, cc)
    strict = tr > tc
    lower = tr >= tc
    eye = rr == cc

    def stack(x):
        return jnp.concatenate([jnp.where(first, x, 0.0), jnp.where(first, 0.0, x)], axis=0)

    pairs = range(W // LANES)
    stacked = lambda x: [stack(x[:, p * LANES:(p + 1) * LANES]) for p in pairs]
    s_a, s_r, s_b, s_k = stacked(at), stacked(rt), stacked(bt), stacked(kt)
    s_bb, s_kb, s_v = stacked(bb), stacked(kb), stacked(v)
    sc = [_dot_nt(jnp.concatenate([s_a[p], s_r[p]], axis=0), jnp.concatenate([s_b[p], s_k[p]], axis=0))
          for p in pairs]
    n = [jnp.where(strict, sc[p][:C2, :C2], 0.0) for p in pairs]
    a_ak = [jnp.where(strict, sc[p][:C2, C2:], 0.0) for p in pairs]
    m_rb = [jnp.where(lower, sc[p][C2:, :C2], 0.0) for p in pairs]
    m_rk = [jnp.where(lower, sc[p][C2:, C2:], 0.0) for p in pairs]
    xv = [_dot(jnp.concatenate([a_ak[p], m_rk[p], s_kb[p].T], axis=0), s_v[p]) for p in pairs]
    t_inv = [jnp.where(eye, 1.0, 0.0) + n[p] for p in pairs]
    pw = [_dot(n[p], n[p]) for p in pairs]
    step = 4
    while step < C:
        tp = [_dot(jnp.concatenate([t_inv[p], pw[p]], axis=0), pw[p]) for p in pairs]
        t_inv = [t_inv[p] + tp[p][:C2] for p in pairs]
        pw = [tp[p][C2:] for p in pairs]
        step *= 2
    t_inv = [t_inv[p] + _dot(t_inv[p], pw[p]) for p in pairs]
    au = [_dot(t_inv[p], jnp.concatenate([s_a[p], xv[p][:C2]], axis=1)) for p in pairs]
    mbu = [_dot(jnp.concatenate([m_rb[p], s_bb[p].T], axis=0), au[p]) for p in pairs]
    ys = []
    for p in pairs:
        r_hat = s_r[p] + mbu[p][:C2, :LANES]
        y0 = mbu[p][:C2, LANES:] + xv[p][C2:2 * C2]
        g_mat = jnp.where(eye, p_c[:, p * LANES:(p + 1) * LANES], 0.0) + mbu[p][C2:, :LANES]
        h_add = mbu[p][C2:, LANES:] + xv[p][2 * C2:]
        yh = _dot(jnp.concatenate([r_hat, g_mat], axis=0), h_sc[p])
        y = yh[:C2] + y0
        h_sc[p] = yh[C2:] + h_add
        ys.append(y[0:C] + y[C:C2])
    ones = ones_ref[...]
    inv_hd = 1.0 / hd
    ppb = WB // LANES
    for i in range(nb):
        y = jnp.concatenate(ys[i * ppb:(i + 1) * ppb], axis=1)
        mean = _dot_exact_rhs(y, ones) * inv_hd
        yc = y - mean
        var = _dot_exact_rhs(yc * yc, ones) * inv_hd
        yn = yc * lax.rsqrt(var + RWKV_LNX_EPS) * lnw_ref[...] + lnb_ref[...]
        bonus = _dot_exact_rhs(r_ref[i] * k_ref[i] * rk_ref[...], ones) * v_ref[i]
        o_ref[i] = ((yn + bonus) * g_ref[i]).astype(o_ref.dtype)


def _scan_call(streams, r_k, lnx_w, lnx_b, ones_bd, B, Lp, hd):
    r, lw, k, v, a, b, g = streams
    rw = r.shape[-1]
    C = CHUNK
    assert 2 * hd == LANES and rw % LANES == 0 and Lp % C == 0
    vec = lambda x: x.reshape(1, -1)
    nb = SCAN_BATCH if B % SCAN_BATCH == 0 else 1
    blk = pl.BlockSpec((nb, C, rw), lambda bi, c: (bi, c, 0))
    par = pl.BlockSpec((1, rw), lambda bi, c: (0, 0))
    out = pl.pallas_call(
        functools.partial(_scan_kernel, hd=hd),
        grid=(B // nb, Lp // C),
        in_specs=[blk] * 7 + [par] * 3 + [pl.BlockSpec((rw, rw), lambda bi, c: (0, 0))],
        out_specs=blk,
        out_shape=jax.ShapeDtypeStruct((B, Lp, rw), BF16),
        scratch_shapes=[pltpu.VMEM((nb * rw // LANES, LANES, LANES), F32)],
        compiler_params=_cparams(("parallel", "arbitrary")),
        name="rwkv_scan",
    )(r, lw, k, v, a, b, g, vec(r_k), vec(lnx_w), vec(lnx_b), ones_bd)
    return out.reshape(B * Lp, rw)


def _merge_kernel(yp_ref, yd_ref, yr_ref, g0_ref, g1_ref, g2_ref, pa_ref, pb_ref, pc_ref, o_ref):
    dot = lambda x, w: jnp.dot(x[...], w[...], preferred_element_type=F32)
    m = _sigmoid(g0_ref[...].astype(F32)) * dot(yp_ref, pa_ref)
    m = m + _sigmoid(g1_ref[...].astype(F32)) * dot(yd_ref, pb_ref)
    m = m + _sigmoid(g2_ref[...].astype(F32)) * dot(yr_ref, pc_ref)
    o_ref[...] = m.astype(o_ref.dtype)


def _merge_call(yp, yd, yr, zg, pa, pb, pc, tm):
    M = yp.shape[0]
    D = pa.shape[1]
    rows = lambda a: pl.BlockSpec((tm, a.shape[1]), lambda i: (i, 0))
    full = lambda a: pl.BlockSpec(a.shape, lambda i: (0, 0))
    gate = lambda n: pl.BlockSpec((tm, D), lambda i: (i, n))
    return pl.pallas_call(
        _merge_kernel,
        grid=(M // tm,),
        in_specs=[rows(yp), rows(yd), rows(yr), gate(0), gate(1), gate(2), full(pa), full(pb), full(pc)],
        out_specs=pl.BlockSpec((tm, D), lambda i: (i, 0)),
        out_shape=jax.ShapeDtypeStruct((M, D), BF16),
        compiler_params=_cparams(("parallel",)),
        name="gated_merge",
    )(yp, yd, yr, zg, zg, zg, pa, pb, pc)


def _ffn_up_kernel(u_ref, wv_ref, wg_ref, cwv_ref, cwg_ref, cbv_ref, cbg_ref, o_ref, *scratch, n_sub):
    xg_scs, xv_scs, a_scs = scratch[:n_sub], scratch[n_sub:2 * n_sub], scratch[2 * n_sub:]
    L = u_ref.shape[0]
    mb = L // n_sub
    st = mb // SUBLANES
    PAD = SUBLANES
    n_slab = a_scs[0].shape[0]
    k_in = math.sqrt(2.0 / math.pi)
    zeros = jnp.zeros((PAD, LANES), F32)
    for s in range(n_slab):
        xv_scs[0][s, 0:PAD, :] = zeros
        xg_scs[0][s, 0:PAD, :] = zeros

    def taps(cw_ref, cb_ref, s):
        sl = slice(s * LANES, (s + 1) * LANES)
        rows = [jnp.broadcast_to(cw_ref[j:j + 1, sl], (SUBLANES, LANES)) for j in range(FFN_CONV)]
        return rows, jnp.broadcast_to(cb_ref[:, sl], (SUBLANES, LANES))

    def tail(sb):
        lo = sb * mb
        xg_sc, xv_sc, a_sc = xg_scs[sb], xv_scs[sb], a_scs[sb]
        for s in range(n_slab):
            wg, bg = taps(cwg_ref, cbg_ref, s)
            wv, bv = taps(cwv_ref, cbv_ref, s)
            ld = lambda ref, j: ref[s, pl.ds(PAD + j, SUBLANES, stride=st), :]
            xg = [ld(xg_sc, j) for j in range(1 - FFN_CONV, 0)]
            xv = [ld(xv_sc, j) for j in range(1 - FFN_CONV, 0)]
            for j in range(st):
                xg.append(ld(xg_sc, j))
                xv.append(ld(xv_sc, j))
                gate = bg
                val = bv
                for t in range(FFN_CONV):
                    gate = gate + xg[j + t] * wg[t]
                    val = val + xv[j + t] * wv[t]
                inner = gate * (2.0 * k_in + (2.0 * k_in * 0.044715) * (gate * gate))
                act = gate / (1.0 + jnp.exp(-inner)) * val
                a_sc[s, pl.ds(j, SUBLANES, stride=st), :] = act
        for s in range(n_slab):
            o_ref[lo:lo + mb, s * LANES:(s + 1) * LANES] = a_sc[s].astype(o_ref.dtype)

    for sb in range(n_sub):
        lo = sb * mb
        u = u_ref[lo:lo + mb, :]
        for x_scs, w_ref in ((xg_scs, wg_ref), (xv_scs, wv_ref)):
            x = jnp.dot(u, w_ref[...], preferred_element_type=F32)
            for s in range(n_slab):
                xs = x[:, s * LANES:(s + 1) * LANES]
                x_scs[sb][s, PAD:PAD + mb, :] = xs
                if sb + 1 < n_sub:
                    x_scs[sb + 1][s, 0:PAD, :] = xs[mb - PAD:mb]
        if sb:
            tail(sb - 1)
    tail(n_sub - 1)


def _ffn_up_call(u, w_up, conv_w, conv_b, B, Lp, tn):
    M, D = u.shape
    FF = w_up.shape[1] // 2
    nt = FF // tn
    n_sub = 4 if Lp % (4 * 2 * SUBLANES) == 0 else 1
    mb = Lp // n_sub
    return pl.pallas_call(
        functools.partial(_ffn_up_kernel, n_sub=n_sub),
        grid=(B, nt),
        in_specs=[pl.BlockSpec((Lp, D), lambda b, j: (b, 0)),
                  pl.BlockSpec((D, tn), lambda b, j: (0, j)),
                  pl.BlockSpec((D, tn), lambda b, j: (0, nt + j)),
                  pl.BlockSpec((FFN_CONV, tn), lambda b, j: (0, j)),
                  pl.BlockSpec((FFN_CONV, tn), lambda b, j: (0, nt + j)),
                  pl.BlockSpec((1, tn), lambda b, j: (0, j)),
                  pl.BlockSpec((1, tn), lambda b, j: (0, nt + j))],
        out_specs=pl.BlockSpec((Lp, tn), lambda b, j: (b, j)),
        out_shape=jax.ShapeDtypeStruct((M, FF), BF16),
        scratch_shapes=([pltpu.VMEM((tn // LANES, SUBLANES + mb, LANES), F32)] * (2 * n_sub)
                        + [pltpu.VMEM((tn // LANES, mb, LANES), F32)] * n_sub),
        compiler_params=_cparams(("parallel", "arbitrary")),
        name="ffn_up_conv_geglu",
    )(u, w_up, w_up, conv_w, conv_w, conv_b.reshape(1, -1), conv_b.reshape(1, -1))


def kernel(x, meta, rel_bias, norm_mix_pre, norm_mix_post, norm_ffn_pre, norm_ffn_post, w_in, pool_w, pool_scale, diff_lq1, diff_lk1, diff_lq2, diff_lk2, diff_subln, rwkv_mu, rwkv_w0, rwkv_w2, rwkv_a0, rwkv_a2, rwkv_g2, rwkv_kk, rwkv_ka, rwkv_rk, rwkv_lnx_w, rwkv_lnx_b, rwkv_v0, rwkv_v1, rwkv_v2, p_a, p_b, p_c, w_o, ffn_up, ffn_conv_w, ffn_conv_b, ffn_down):
    B, S, D = x.shape
    n_meta = meta.shape[0]
    depth = w_in.shape[0]
    L = S + n_meta
    Lp = _round_up(L, ATTN_TILE)
    M = B * Lp

    PW = pool_scale.shape[1]
    DW = p_b.shape[1]
    RW = rwkv_w0.shape[1]
    hd_r = rwkv_rk.shape[2]
    dl, al, gl = rwkv_w2.shape[1], rwkv_a2.shape[1], rwkv_g2.shape[1]
    off_q = PW
    off_r = PW + 3 * DW
    off_g = off_r + 3 * RW + dl + al + gl
    assert off_g + 3 * D == w_in.shape[2]
    assert max(dl, al, gl, rwkv_v1.shape[2]) <= LANES

    h = jnp.concatenate([jnp.broadcast_to(meta.astype(x.dtype)[None], (B, n_meta, D)), x,
                         jnp.zeros((B, Lp - L, D), x.dtype)], axis=1).reshape(M, D)

    hid = np.arange(RW) // hd_r
    ones_bd = jnp.asarray((hid[:, None] == hid[None, :]).astype(np.float32), dtype=BF16)

    def pad_lora(a, axis):
        segs = [lax.slice_in_dim(a, 0, 3 * RW, axis=axis)]
        o = 3 * RW
        for n in (dl, al, gl):
            seg = lax.slice_in_dim(a, o, o + n, axis=axis)
            pad = [(0, 0)] * a.ndim
            pad[axis] = (0, LANES - n)
            segs.append(jnp.pad(seg, pad))
            o += n
        return jnp.concatenate(segs, axis=axis)

    tables = {tq: _bias_table_call(rel_bias, n_meta, tq) for tq, _, _ in _attn_segments(Lp)}

    tm_big = _pick_tile(M, 1088, 16)
    tm_res = _pick_tile(M, 544, 16)
    u = _rms_call(h, norm_mix_pre[0], tm_res)
    v_first = None
    for l in range(depth):
        lam_init = 0.8 - 0.6 * math.exp(-0.3 * l)
        wl = w_in[l]
        w_gate = wl[:, off_g:].astype(BF16)
        w_qkv = wl[:, off_q:off_r].astype(BF16)
        w_pool = wl[:, :PW].astype(BF16)
        w_rwkv = pad_lora(wl[:, off_r:off_g], 1).astype(BF16)
        mu_p = pad_lora(rwkv_mu[l], 0)

        zg = _mm_call(u, w_gate, BF16, tm_big, _pick_tile(3 * D, 768, LANES), "in_proj_gate")
        zqkv = _mm_call(u, w_qkv, BF16, tm_big, _pick_tile(3 * DW, 768, LANES), "in_proj_qkv")
        zp = _mm_call(u, w_pool, BF16, tm_big, _pick_tile(PW, 768, LANES), "in_proj_pool")
        zr = _mm_call(u, w_rwkv, F32, tm_big, _pick_tile(w_rwkv.shape[1], 768, LANES), "in_proj_rwkv")

        yp = _pool_call(zp, pool_w[l], pool_scale[l], B, Lp)
        yd = _attn_call(zqkv, tables, diff_lq1[l], diff_lk1[l], diff_lq2[l], diff_lk2[l],
                        diff_subln[l], B, Lp, lam_init)
        v_res = None if l == 0 else (rwkv_v0[l - 1], rwkv_v1[l - 1], rwkv_v2[l - 1])
        streams = _rwkv_prep_call(zr, B, Lp, RW, mu_p, rwkv_w0[l], rwkv_w2[l], rwkv_a0[l], rwkv_a2[l],
                                  rwkv_g2[l], rwkv_kk[l], rwkv_ka[l], ones_bd, v_first, v_res)
        if l == 0:
            v_first = streams[3]
        yr = _scan_call(streams, rwkv_rk[l], rwkv_lnx_w[l], rwkv_lnx_b[l], ones_bd, B, Lp, hd_r)

        merged = _merge_call(yp, yd, yr, zg, p_a[l].astype(BF16), p_b[l].astype(BF16),
                             p_c[l].astype(BF16), tm_res)
        h, u2 = _mm_res_call(merged, w_o[l].astype(BF16), h, norm_mix_post[l], norm_ffn_pre[l],
                             "out_proj_res")
        FF = ffn_down.shape[1]
        act = _ffn_up_call(u2, ffn_up[l].astype(BF16), ffn_conv_w[l], ffn_conv_b[l], B, Lp,
                           _pick_tile(FF, 512, LANES))
        g_next = norm_mix_pre[l + 1] if l + 1 < depth else norm_mix_pre[l]
        h, u = _mm_res_call(act, ffn_down[l].astype(BF16), h, norm_ffn_post[l], g_next,
                            "ffn_down_res")
    return h.reshape(B, Lp, D)[:, n_meta:n_meta + S]
```

```python
import functools
import math

import numpy as np
import jax
import jax.numpy as jnp
from jax import lax
from jax.experimental import pallas as pl
from jax.experimental.pallas import tpu as pltpu

F32 = jnp.float32
BF16 = jnp.bfloat16

CHUNK = 64
NORM_EPS = 1e-6
NEG_INF = -1e30
POOL_WINDOWS = (2, 4, 8, 16)
DIFF_SUBLN_EPS = 1e-5
REL_MAX_DIST = 128
RWKV_LNX_EPS = 64e-5
FFN_CONV = 3

LANES = 128
SUBLANES = 8
ATTN_TILE = 128
ATTN_Q_TILE = 512
ATTN_UNROLL = 3
SCAN_BATCH = 4
VMEM_LIMIT = 56 * 1024 * 1024


def _cparams(sem):
    return pltpu.CompilerParams(dimension_semantics=sem, vmem_limit_bytes=VMEM_LIMIT)


def _round_up(x, m):
    return (x + m - 1) // m * m


def _pick_tile(total, target, mult):
    best = None
    for t in range(mult, min(total, target) + 1, mult):
        if total % t == 0:
            best = t
    assert best is not None, (total, target, mult)
    return best


def _shift_down(x, d):
    row = lax.broadcasted_iota(jnp.int32, x.shape, 0)
    return jnp.where(row >= d, pltpu.roll(x, d, 0), 0.0)


def _sigmoid(x):
    return 1.0 / (1.0 + jnp.exp(-x))


def _dot(a, b):
    return jnp.dot(a.astype(BF16), b.astype(BF16), preferred_element_type=F32)


def _dot_nt(a, b):
    return lax.dot_general(a.astype(BF16), b.astype(BF16), (((1,), (1,)), ((), ())),
                           preferred_element_type=F32)


def _split3(x):
    hi = x.astype(BF16)
    r1 = x - hi.astype(F32)
    mid = r1.astype(BF16)
    lo = (r1 - mid.astype(F32)).astype(BF16)
    return hi, mid, lo


def _dot_exact_rhs(x, ones_bf16):
    hi, mid, lo = _split3(x)
    acc = jnp.dot(hi, ones_bf16, preferred_element_type=F32)
    acc = acc + jnp.dot(mid, ones_bf16, preferred_element_type=F32)
    return acc + jnp.dot(lo, ones_bf16, preferred_element_type=F32)


def _rms_kernel(x_ref, g_ref, o_ref):
    x = x_ref[...]
    y = x * lax.rsqrt(jnp.mean(x * x, axis=-1, keepdims=True) + NORM_EPS)
    o_ref[...] = (y * g_ref[...]).astype(o_ref.dtype)


def _rms_call(x, g, tm):
    M, D = x.shape
    return pl.pallas_call(
        _rms_kernel,
        grid=(M // tm,),
        in_specs=[pl.BlockSpec((tm, D), lambda i: (i, 0)),
                  pl.BlockSpec((1, D), lambda i: (0, 0))],
        out_specs=pl.BlockSpec((tm, D), lambda i: (i, 0)),
        out_shape=jax.ShapeDtypeStruct((M, D), BF16),
        compiler_params=_cparams(("parallel",)),
        name="rms_norm",
    )(x, g.reshape(1, D))


def _mm_kernel(a_ref, b_ref, o_ref):
    o_ref[...] = jnp.dot(a_ref[...], b_ref[...], preferred_element_type=F32).astype(o_ref.dtype)


def _mm_call(a, b, out_dtype, tm, tn, name):
    M, K = a.shape
    N = b.shape[1]
    return pl.pallas_call(
        _mm_kernel,
        grid=(M // tm, N // tn),
        in_specs=[pl.BlockSpec((tm, K), lambda i, j: (i, 0)),
                  pl.BlockSpec((K, tn), lambda i, j: (0, j))],
        out_specs=pl.BlockSpec((tm, tn), lambda i, j: (i, j)),
        out_shape=jax.ShapeDtypeStruct((M, N), out_dtype),
        compiler_params=_cparams(("parallel", "arbitrary")),
        name=name,
    )(a, b)


def _mm_res_kernel(a_ref, w_ref, h_ref, gp_ref, gn_ref, oh_ref, ou_ref):
    f = jnp.dot(a_ref[...], w_ref[...], preferred_element_type=F32)
    y = f * lax.rsqrt(jnp.mean(f * f, axis=-1, keepdims=True) + NORM_EPS) * gp_ref[...]
    hn = h_ref[...] + y
    oh_ref[...] = hn
    u = hn * lax.rsqrt(jnp.mean(hn * hn, axis=-1, keepdims=True) + NORM_EPS) * gn_ref[...]
    ou_ref[...] = u.astype(ou_ref.dtype)


def _mm_res_tile(M, K, D):
    budget = VMEM_LIMIT - 8 * 1024 * 1024
    best = None
    for tm in range(16, M + 1, 16):
        if M % tm:
            continue
        need = (K * D * 2
                + 2 * tm * K * 2
                + 2 * 2 * tm * D * 4
                + 2 * tm * D * 2
                + 2 * tm * D * 4)
        if need <= budget:
            best = tm
    assert best is not None
    return best


def _mm_res_call(a, w, h, g_post, g_next, name):
    M, K = a.shape
    D = w.shape[1]
    tm = _mm_res_tile(M, K, D)
    return pl.pallas_call(
        _mm_res_kernel,
        grid=(M // tm,),
        in_specs=[pl.BlockSpec((tm, K), lambda i: (i, 0)),
                  pl.BlockSpec((K, D), lambda i: (0, 0), pipeline_mode=pl.Buffered(1)),
                  pl.BlockSpec((tm, D), lambda i: (i, 0)),
                  pl.BlockSpec((1, D), lambda i: (0, 0)),
                  pl.BlockSpec((1, D), lambda i: (0, 0))],
        out_specs=[pl.BlockSpec((tm, D), lambda i: (i, 0)),
                   pl.BlockSpec((tm, D), lambda i: (i, 0))],
        out_shape=[jax.ShapeDtypeStruct((M, D), F32),
                   jax.ShapeDtypeStruct((M, D), BF16)],
        compiler_params=_cparams(("parallel",)),
        name=name,
    )(a, w, h, g_post.reshape(1, D), g_next.reshape(1, D))


def _pool_kernel(z_ref, w_ref, s_ref, o_ref, *, gw):
    L = z_ref.shape[1]
    row = lax.broadcasted_iota(jnp.int32, (L, gw), 0)
    for gi, win in enumerate(POOL_WINDOWS):
        sl = slice(gi * gw, (gi + 1) * gw)
        zg = z_ref[0, :, sl].astype(F32)
        s = zg
        d = 1
        while d < win:
            s = s + _shift_down(s, d)
            d *= 2
        cnt = jnp.minimum(row + 1, win).astype(F32)
        pooled = s / cnt - zg
        y = _dot(pooled, w_ref[gi]) * s_ref[:, sl]
        o_ref[0, :, sl] = y.astype(o_ref.dtype)


def _pool_call(z, w, scale, B, Lp):
    PW = z.shape[-1]
    G = len(POOL_WINDOWS)
    gw = PW // G
    z3 = z.reshape(B, Lp, PW)
    out = pl.pallas_call(
        functools.partial(_pool_kernel, gw=gw),
        grid=(B,),
        in_specs=[pl.BlockSpec((1, Lp, PW), lambda b: (b, 0, 0)),
                  pl.BlockSpec((G, gw, gw), lambda b: (0, 0, 0)),
                  pl.BlockSpec((1, PW), lambda b: (0, 0))],
        out_specs=pl.BlockSpec((1, Lp, PW), lambda b: (b, 0, 0)),
        out_shape=jax.ShapeDtypeStruct((B, Lp, PW), BF16),
        compiler_params=_cparams(("parallel",)),
        name="pool_mixer",
    )(z3, w.astype(BF16), scale.reshape(1, PW))
    return out.reshape(B * Lp, PW)


def _rel_bucket(rel, n_buckets):
    half = n_buckets // 2
    max_exact = half // 2
    n = np.abs(rel)
    large = max_exact + (np.log(np.maximum(n, 1) / max_exact) / math.log(REL_MAX_DIST / max_exact)
                         * (half - max_exact)).astype(np.int32)
    large = np.minimum(large, half - 1)
    return ((rel > 0) * half + np.where(n < max_exact, n, large)).astype(np.int32)


def _bias_table_kernel(rb_ref, bucket_ref, pen_ref, o_ref, *, n_buckets):
    s = pl.program_id(0)
    bucket = bucket_ref[...]
    acc = pen_ref[...]
    for b in range(n_buckets):
        acc = acc + jnp.where(bucket == b, rb_ref[b, s], 0.0)
    o_ref[0] = acc


def _bias_table_call(rel_bias, n_meta, tq):
    n_buckets, n_sub = rel_bias.shape
    KT = ATTN_TILE
    assert tq % KT == 0 and KT % CHUNK == 0 and n_meta <= KT
    n_near = tq // KT + 2
    n_slabs = n_near + 2
    far = n_buckets // 2 - 1
    r = np.arange(tq)[:, None]
    c = np.arange(KT)[None, :]
    assert np.all(_rel_bucket(-(KT + 1 + np.arange(4 * KT)), n_buckets) == far)
    bucket = np.full((n_slabs, tq, KT), far, np.int32)
    pen = np.zeros((n_slabs, tq, KT), np.float32)
    pen[-1] = NEG_INF
    for idx in range(n_near):
        key = (idx - 1) * KT + c
        bucket[1 + idx] = _rel_bucket(key - r, n_buckets)
        visible = np.floor_divide(key - n_meta, CHUNK) <= np.floor_divide(r - n_meta, CHUNK)
        pen[1 + idx] = np.where(visible, 0.0, NEG_INF)
    blk = pl.BlockSpec((n_slabs, tq, KT), lambda s: (0, 0, 0))
    return pl.pallas_call(
        functools.partial(_bias_table_kernel, n_buckets=n_buckets),
        grid=(n_sub,),
        in_specs=[pl.BlockSpec(memory_space=pltpu.SMEM), blk, blk],
        out_specs=pl.BlockSpec((1, n_slabs, tq, KT), lambda s: (s, 0, 0, 0)),
        out_shape=jax.ShapeDtypeStruct((n_sub, n_slabs, tq, KT), F32),
        compiler_params=_cparams(("arbitrary",)),
        name="bias_table_%d" % tq,
    )(rel_bias, jnp.asarray(bucket), jnp.asarray(pen))


def _attn_kernel(*refs, hd, scale, lam_init, nt, tile0, live):
    (lq1_ref, lk1_ref, lq2_ref, lk2_ref, sub_ref, q_ref, k_ref, v_ref, w_ref,
     o_ref, s_sc, mx_sc, l_sc, acc_sc, k_sc, v_sc) = refs
    KT = ATTN_TILE
    tq = q_ref.shape[1]
    ratio = -(-tq // KT)
    Lk = k_ref.shape[1]

    @pl.when(pl.program_id(2) == 0)
    def _():
        k_sc[0:Lk, :] = k_ref[0]
        v_sc[0:Lk, :] = v_ref[0]
        if k_sc.shape[0] > Lk:
            k_sc[Lk:, :] = jnp.zeros((k_sc.shape[0] - Lk, k_sc.shape[1]), k_sc.dtype)
            v_sc[Lk:, :] = jnp.zeros((v_sc.shape[0] - Lk, v_sc.shape[1]), v_sc.dtype)
    n_slabs = w_ref.shape[1]
    i = pl.program_id(2)
    jt0 = tile0 + i * ratio
    n_need = jnp.minimum(jt0 + ratio + 1, nt)
    n_pairs = (n_need + 1) // 2
    q = q_ref[0, 0:live, :]

    def slab(j):
        return jnp.where(j < nt, jnp.clip(j - jt0 + 2, 0, n_slabs - 1), n_slabs - 1)

    def two_tiles(ref, jj):
        j0 = 2 * jj
        j1 = jnp.minimum(j0 + 1, nt - 1)
        t0 = ref[pl.ds(pl.multiple_of(j0 * KT, KT), KT), :]
        t1 = ref[pl.ds(pl.multiple_of(j1 * KT, KT), KT), :]
        return jnp.concatenate([t0, t1], axis=0)

    mx_sc[...] = jnp.full_like(mx_sc, NEG_INF)

    def unrolled(step):
        def group(t, carry):
            for r in range(ATTN_UNROLL):
                step(ATTN_UNROLL * t + r)
            return carry

        def single(jj, carry):
            step(jj)
            return carry
        n_groups = n_pairs // ATTN_UNROLL
        lax.fori_loop(0, n_groups, group, 0)
        lax.fori_loop(n_groups * ATTN_UNROLL, n_pairs, single, 0)

    def pass1(jj):
        kt = two_tiles(k_sc, jj)
        i0 = slab(2 * jj)
        i1 = slab(2 * jj + 1)
        for s in range(2):
            sc = _dot_nt(q[:, s * hd:(s + 1) * hd], kt[:, s * hd:(s + 1) * hd]) * scale
            sc = sc + jnp.concatenate([w_ref[s, i0, 0:live, :], w_ref[s, i1, 0:live, :]], axis=1)
            s_sc[s, jj] = sc
            mx_sc[s] = jnp.maximum(mx_sc[s], jnp.maximum(sc[:, :KT], sc[:, KT:]))

    unrolled(pass1)
    m = [jnp.max(mx_sc[s], axis=-1, keepdims=True) for s in range(2)]
    l_sc[...] = jnp.zeros_like(l_sc)
    acc_sc[...] = jnp.zeros_like(acc_sc)

    def pass2(jj):
        vt = two_tiles(v_sc, jj)
        for s in range(2):
            p = jnp.exp(s_sc[s, jj] - m[s])
            l_sc[s] += p[:, :KT] + p[:, KT:]
            acc_sc[s] += jnp.dot(p.astype(BF16), vt, preferred_element_type=F32)

    unrolled(pass2)
    l = [jnp.sum(l_sc[s], axis=-1, keepdims=True) for s in range(2)]
    lam = (jnp.exp(jnp.sum(lq1_ref[...] * lk1_ref[...], axis=-1, keepdims=True))
           - jnp.exp(jnp.sum(lq2_ref[...] * lk2_ref[...], axis=-1, keepdims=True)) + lam_init)
    o = acc_sc[0] / l[0] - lam * (acc_sc[1] / l[1])
    y = o * lax.rsqrt(jnp.mean(o * o, axis=-1, keepdims=True) + DIFF_SUBLN_EPS)
    o_ref[0, 0:live, :] = (y * sub_ref[...] * (1.0 - lam_init)).astype(o_ref.dtype)
    if live < tq:
        o_ref[0, live:tq, :] = jnp.zeros((tq - live, o_ref.shape[2]), o_ref.dtype)


def _attn_segment(qkv3, table, params, lam_init, tq, row0, n_tiles, live):
    B, Lp, W3 = qkv3.shape
    DW = W3 // 3
    hd = params[0].shape[1]
    hw = 2 * hd
    n_heads = DW // hw
    KT = ATTN_TILE
    nt = -(-Lp // KT)
    blk0 = row0 // tq
    kern = functools.partial(_attn_kernel, hd=hd, scale=hd ** -0.5, lam_init=lam_init, nt=nt,
                             tile0=row0 // KT, live=live)
    par = lambda n: pl.BlockSpec((1, n), lambda b, h, i: (0, 0))
    in_specs = [par(hd)] * 4 + [
        par(hw),
        pl.BlockSpec((1, tq, hw), lambda b, h, i: (b, blk0 + i, h)),
        pl.BlockSpec((1, Lp, hw), lambda b, h, i: (b, 0, n_heads + h)),
        pl.BlockSpec((1, Lp, hw), lambda b, h, i: (b, 0, 2 * n_heads + h)),
        pl.BlockSpec((2, table.shape[1], tq, KT), lambda b, h, i: (h, 0, 0, 0))]
    args = list(params) + [qkv3, qkv3, qkv3, table]
    max_pairs = (nt + 1) // 2
    return pl.pallas_call(
        kern,
        grid=(B, n_heads, n_tiles),
        in_specs=in_specs,
        out_specs=pl.BlockSpec((1, tq, hw), lambda b, h, i: (b, i, h)),
        out_shape=jax.ShapeDtypeStruct((B, n_tiles * tq, DW), BF16),
        scratch_shapes=[pltpu.VMEM((2, max_pairs, live, 2 * KT), F32), pltpu.VMEM((2, live, KT), F32),
                        pltpu.VMEM((2, live, KT), F32), pltpu.VMEM((2, live, hw), F32),
                        pltpu.VMEM((nt * KT, hw), BF16), pltpu.VMEM((nt * KT, hw), BF16)],
        compiler_params=_cparams(("parallel", "parallel", "arbitrary")),
        name="diff_attention_%d" % tq,
    )(*args)


def _attn_segments(Lp):
    big = ATTN_Q_TILE
    n_big = Lp // big
    segs = []
    if n_big:
        segs.append((big, 0, n_big))
    row0 = n_big * big
    n_mid = (Lp - row0) // ATTN_TILE
    if n_mid:
        segs.append((ATTN_TILE, row0, n_mid))
        row0 += n_mid * ATTN_TILE
    if Lp > row0:
        segs.append((Lp - row0, row0, 1))
    return segs


def _attn_call(zqkv, tables, lq1, lk1, lq2, lk2, subln, B, L, Lp, lam_init):
    vec = lambda a: a.reshape(1, -1)
    params = [vec(lq1), vec(lk1), vec(lq2), vec(lk2), vec(subln)]
    qkv3 = zqkv.reshape(B, Lp, zqkv.shape[-1])
    outs = []
    for tq, row0, n_tiles in _attn_segments(Lp):
        live = min(tq, _round_up(L - row0, 2 * SUBLANES)) if n_tiles == 1 else tq
        outs.append(_attn_segment(qkv3, tables[tq], params, lam_init, tq, row0, n_tiles, live))
    out = outs[0] if len(outs) == 1 else jnp.concatenate(outs, axis=1)
    return out.reshape(B * Lp, -1)


def _rwkv_prep_kernel(*refs, rw, has_vres):
    if has_vres:
        (z_ref, zp_ref, mu_ref, w0_ref, w2_ref, a0_ref, a2_ref, g2_ref, kk_ref, ka_ref, ones_ref,
         vf_ref, v0_ref, v1_ref, v2_ref,
         r_o, lw_o, k_o, v_o, a_o, b_o, g_o) = refs
    else:
        (z_ref, zp_ref, mu_ref, w0_ref, w2_ref, a0_ref, a2_ref, g2_ref, kk_ref, ka_ref, ones_ref,
         r_o, lw_o, k_o, v_o, a_o, b_o, g_o) = refs
    i = pl.program_id(1)
    z = z_ref[0]
    last_prev = zp_ref[0, SUBLANES - 1:SUBLANES, :]
    last_prev = jnp.where(i > 0, last_prev, 0.0)
    row = lax.broadcasted_iota(jnp.int32, z.shape, 0)
    prev = jnp.where(row >= 1, pltpu.roll(z, 1, 0), last_prev)
    zs = z + (prev - z) * mu_ref[...]
    r = zs[:, 0:rw]
    kx = zs[:, rw:2 * rw]
    vx = zs[:, 2 * rw:3 * rw]
    wd = zs[:, 3 * rw:3 * rw + LANES]
    ad = zs[:, 3 * rw + LANES:3 * rw + 2 * LANES]
    gd = zs[:, 3 * rw + 2 * LANES:3 * rw + 3 * LANES]
    wl = w0_ref[...] + _dot(jnp.tanh(wd), w2_ref[...])
    sp = jnp.maximum(-wl, 0.0) + jnp.log(1.0 + jnp.exp(-jnp.abs(wl)))
    lw = -jnp.exp(-sp - 0.5)
    if has_vres:
        gate_v = _sigmoid(v0_ref[...] + _dot(_dot(vx, v1_ref[...]), v2_ref[...]))
        vx = vx + (vf_ref[0] - vx) * gate_v
    a = _sigmoid(a0_ref[...] + _dot(ad, a2_ref[...]))
    g = _dot(_sigmoid(gd), g2_ref[...])
    kk = kx * kk_ref[...]
    ss = _dot_exact_rhs(kk * kk, ones_ref[...])
    kk = kk * lax.rsqrt(jnp.maximum(ss, 1e-24))
    kmod = kx * (1.0 + (a - 1.0) * ka_ref[...])
    r_o[0] = r
    lw_o[0] = lw
    k_o[0] = kmod
    v_o[0] = vx
    a_o[0] = -kk
    b_o[0] = kk * a
    g_o[0] = g


def _pad_rows(w, n):
    return jnp.pad(w, ((0, n - w.shape[0]), (0, 0)))


def _rwkv_prep_call(zr, B, Lp, rw, mu_p, w0, w2, a0, a2, g2, k_k, k_a, ones_bd, v_first, v_res):
    ZW = zr.shape[-1]
    z3 = zr.reshape(B, Lp, ZW)
    tl = _pick_tile(Lp, 384, SUBLANES)
    nb = tl // SUBLANES
    has_vres = v_res is not None
    vec = lambda a: a.reshape(1, -1)
    full = lambda a: pl.BlockSpec(a.shape, lambda b, i: (0,) * a.ndim)
    w2p = _pad_rows(w2, LANES).astype(BF16)
    a2p = _pad_rows(a2, LANES).astype(BF16)
    g2p = _pad_rows(g2, LANES).astype(BF16)
    args = [z3, z3, vec(mu_p), vec(w0), w2p, vec(a0), a2p, g2p, vec(k_k), vec(k_a), ones_bd]
    in_specs = [pl.BlockSpec((1, tl, ZW), lambda b, i: (b, i, 0)),
                pl.BlockSpec((1, SUBLANES, ZW), lambda b, i: (b, jnp.maximum(i * nb - 1, 0), 0))]
    in_specs += [full(a) for a in args[2:]]
    if has_vres:
        v0, v1, v2 = v_res
        v1p = jnp.pad(v1, ((0, 0), (0, LANES - v1.shape[1]))).astype(BF16)
        v2p = _pad_rows(v2, LANES).astype(BF16)
        extra = [v_first, vec(v0), v1p, v2p]
        args += extra
        in_specs += [pl.BlockSpec((1, tl, rw), lambda b, i: (b, i, 0))] + [full(a) for a in extra[1:]]
    o_spec = pl.BlockSpec((1, tl, rw), lambda b, i: (b, i, 0))
    o_shape = jax.ShapeDtypeStruct((B, Lp, rw), F32)
    return pl.pallas_call(
        functools.partial(_rwkv_prep_kernel, rw=rw, has_vres=has_vres),
        grid=(B, Lp // tl),
        in_specs=in_specs,
        out_specs=[o_spec] * 7,
        out_shape=[o_shape] * 7,
        compiler_params=_cparams(("parallel", "arbitrary")),
        name="rwkv_prep",
    )(*args)


def _scan_kernel(r_ref, lw_ref, k_ref, v_ref, a_ref, b_ref, g_ref, rk_ref, lnw_ref, lnb_ref, ones_ref,
                 o_ref, h_sc, *, hd):
    nb = r_ref.shape[0]
    C = r_ref.shape[1]
    C2 = 2 * C
    WB = r_ref.shape[2]
    W = nb * WB
    c = pl.program_id(1)

    @pl.when(c == 0)
    def _():
        h_sc[...] = jnp.zeros_like(h_sc)

    side_by_side = lambda ref: jnp.concatenate([ref[i] for i in range(nb)], axis=1)
    r = side_by_side(r_ref)
    lw = side_by_side(lw_ref)
    k = side_by_side(k_ref)
    v = side_by_side(v_ref)
    a = side_by_side(a_ref)
    b = side_by_side(b_ref)

    cum = lw
    d = 1
    while d < C:
        cum = cum + _shift_down(cum, d)
        d *= 2
    tot = cum[C - 1:C, :]
    e_end = jnp.exp(tot - cum)
    e_neg = jnp.exp(-cum)
    at = a * jnp.exp(cum - lw)
    rt = r * jnp.exp(cum)
    bt = b * e_neg
    kt = k * e_neg
    bb = b * e_end
    kb = k * e_end
    p_c = jnp.exp(tot)

    lane = lax.broadcasted_iota(jnp.int32, (C, LANES), 1)
    first = lane < hd
    rr = lax.broadcasted_iota(jnp.int32, (C2, C2), 0)
    cc = lax.broadcasted_iota(jnp.int32, (C2, C2), 1)
    tr = jnp.where(rr >= C, rr - C, rr)
    tc = jnp.where(cc >= C, cc - C, cc)
    strict = tr > tc
    lower = tr >= tc
    eye = rr == cc

    def stack(x):
        return jnp.concatenate([jnp.where(first, x, 0.0), jnp.where(first, 0.0, x)], axis=0)

    pairs = range(W // LANES)
    stacked = lambda x: [stack(x[:, p * LANES:(p + 1) * LANES]) for p in pairs]
    s_a, s_r, s_b, s_k = stacked(at), stacked(rt), stacked(bt), stacked(kt)
    s_bb, s_kb, s_v = stacked(bb), stacked(kb), stacked(v)
    sc = [_dot_nt(jnp.concatenate([s_a[p], s_r[p]], axis=0), jnp.concatenate([s_b[p], s_k[p]], axis=0))
          for p in pairs]
    n = [jnp.where(strict, sc[p][:C2, :C2], 0.0) for p in pairs]
    a_ak = [jnp.where(strict, sc[p][:C2, C2:], 0.0) for p in pairs]
    m_rb = [jnp.where(lower, sc[p][C2:, :C2], 0.0) for p in pairs]
    m_rk = [jnp.where(lower, sc[p][C2:, C2:], 0.0) for p in pairs]
    xv = [_dot(jnp.concatenate([a_ak[p], m_rk[p], s_kb[p].T], axis=0), s_v[p]) for p in pairs]
    t_inv = [jnp.where(eye, 1.0, 0.0) + n[p] for p in pairs]
    pw = [_dot(n[p], n[p]) for p in pairs]
    step = 4
    while step < C:
        tp = [_dot(jnp.concatenate([t_inv[p], pw[p]], axis=0), pw[p]) for p in pairs]
        t_inv = [t_inv[p] + tp[p][:C2] for p in pairs]
        pw = [tp[p][C2:] for p in pairs]
        step *= 2
    t_inv = [t_inv[p] + _dot(t_inv[p], pw[p]) for p in pairs]
    au = [_dot(t_inv[p], jnp.concatenate([s_a[p], xv[p][:C2]], axis=1)) for p in pairs]
    mbu = [_dot(jnp.concatenate([m_rb[p], s_bb[p].T], axis=0), au[p]) for p in pairs]
    ys = []
    for p in pairs:
        r_hat = s_r[p] + mbu[p][:C2, :LANES]
        y0 = mbu[p][:C2, LANES:] + xv[p][C2:2 * C2]
        g_mat = jnp.where(eye, p_c[:, p * LANES:(p + 1) * LANES], 0.0) + mbu[p][C2:, :LANES]
        h_add = mbu[p][C2:, LANES:] + xv[p][2 * C2:]
        yh = _dot(jnp.concatenate([r_hat, g_mat], axis=0), h_sc[p])
        y = yh[:C2] + y0
        h_sc[p] = yh[C2:] + h_add
        ys.append(y[0:C] + y[C:C2])
    ones = ones_ref[...]
    inv_hd = 1.0 / hd
    ppb = WB // LANES
    for i in range(nb):
        y = jnp.concatenate(ys[i * ppb:(i + 1) * ppb], axis=1)
        mean = _dot_exact_rhs(y, ones) * inv_hd
        yc = y - mean
        var = _dot_exact_rhs(yc * yc, ones) * inv_hd
        yn = yc * lax.rsqrt(var + RWKV_LNX_EPS) * lnw_ref[...] + lnb_ref[...]
        bonus = _dot_exact_rhs(r_ref[i] * k_ref[i] * rk_ref[...], ones) * v_ref[i]
        o_ref[i] = ((yn + bonus) * g_ref[i]).astype(o_ref.dtype)


def _scan_call(streams, r_k, lnx_w, lnx_b, ones_bd, B, Lp, hd):
    r, lw, k, v, a, b, g = streams
    rw = r.shape[-1]
    C = CHUNK
    assert 2 * hd == LANES and rw % LANES == 0 and Lp % C == 0
    vec = lambda x: x.reshape(1, -1)
    nb = SCAN_BATCH if B % SCAN_BATCH == 0 else 1
    blk = pl.BlockSpec((nb, C, rw), lambda bi, c: (bi, c, 0))
    par = pl.BlockSpec((1, rw), lambda bi, c: (0, 0))
    out = pl.pallas_call(
        functools.partial(_scan_kernel, hd=hd),
        grid=(B // nb, Lp // C),
        in_specs=[blk] * 7 + [par] * 3 + [pl.BlockSpec((rw, rw), lambda bi, c: (0, 0))],
        out_specs=blk,
        out_shape=jax.ShapeDtypeStruct((B, Lp, rw), BF16),
        scratch_shapes=[pltpu.VMEM((nb * rw // LANES, LANES, LANES), F32)],
        compiler_params=_cparams(("parallel", "arbitrary")),
        name="rwkv_scan",
    )(r, lw, k, v, a, b, g, vec(r_k), vec(lnx_w), vec(lnx_b), ones_bd)
    return out.reshape(B * Lp, rw)


def _merge_kernel(yp_ref, yd_ref, yr_ref, g0_ref, g1_ref, g2_ref, pa_ref, pb_ref, pc_ref, o_ref):
    dot = lambda x, w: jnp.dot(x[...], w[...], preferred_element_type=F32)
    m = _sigmoid(g0_ref[...].astype(F32)) * dot(yp_ref, pa_ref)
    m = m + _sigmoid(g1_ref[...].astype(F32)) * dot(yd_ref, pb_ref)
    m = m + _sigmoid(g2_ref[...].astype(F32)) * dot(yr_ref, pc_ref)
    o_ref[...] = m.astype(o_ref.dtype)


def _merge_call(yp, yd, yr, zg, pa, pb, pc, tm):
    M = yp.shape[0]
    D = pa.shape[1]
    rows = lambda a: pl.BlockSpec((tm, a.shape[1]), lambda i: (i, 0))
    full = lambda a: pl.BlockSpec(a.shape, lambda i: (0, 0))
    gate = lambda n: pl.BlockSpec((tm, D), lambda i: (i, n))
    return pl.pallas_call(
        _merge_kernel,
        grid=(M // tm,),
        in_specs=[rows(yp), rows(yd), rows(yr), gate(0), gate(1), gate(2), full(pa), full(pb), full(pc)],
        out_specs=pl.BlockSpec((tm, D), lambda i: (i, 0)),
        out_shape=jax.ShapeDtypeStruct((M, D), BF16),
        compiler_params=_cparams(("parallel",)),
        name="gated_merge",
    )(yp, yd, yr, zg, zg, zg, pa, pb, pc)


def _ffn_up_kernel(u_ref, wv_ref, wg_ref, cwv_ref, cwg_ref, cbv_ref, cbg_ref, o_ref, *scratch, n_sub):
    xg_scs, xv_scs, a_scs = scratch[:n_sub], scratch[n_sub:2 * n_sub], scratch[2 * n_sub:]
    L = u_ref.shape[0]
    mb = L // n_sub
    st = mb // SUBLANES
    PAD = SUBLANES
    n_slab = a_scs[0].shape[0]
    k_in = math.sqrt(2.0 / math.pi)
    zeros = jnp.zeros((PAD, LANES), F32)
    for s in range(n_slab):
        xv_scs[0][s, 0:PAD, :] = zeros
        xg_scs[0][s, 0:PAD, :] = zeros

    def taps(cw_ref, cb_ref, s):
        sl = slice(s * LANES, (s + 1) * LANES)
        rows = [jnp.broadcast_to(cw_ref[j:j + 1, sl], (SUBLANES, LANES)) for j in range(FFN_CONV)]
        return rows, jnp.broadcast_to(cb_ref[:, sl], (SUBLANES, LANES))

    def tail(sb):
        lo = sb * mb
        xg_sc, xv_sc, a_sc = xg_scs[sb], xv_scs[sb], a_scs[sb]
        for s in range(n_slab):
            wg, bg = taps(cwg_ref, cbg_ref, s)
            wv, bv = taps(cwv_ref, cbv_ref, s)
            ld = lambda ref, j: ref[s, pl.ds(PAD + j, SUBLANES, stride=st), :]
            xg = [ld(xg_sc, j) for j in range(1 - FFN_CONV, 0)]
            xv = [ld(xv_sc, j) for j in range(1 - FFN_CONV, 0)]
            for j in range(st):
                xg.append(ld(xg_sc, j))
                xv.append(ld(xv_sc, j))
                gate = bg
                val = bv
                for t in range(FFN_CONV):
                    gate = gate + xg[j + t] * wg[t]
                    val = val + xv[j + t] * wv[t]
                inner = gate * (2.0 * k_in + (2.0 * k_in * 0.044715) * (gate * gate))
                act = gate / (1.0 + jnp.exp(-inner)) * val
                a_sc[s, pl.ds(j, SUBLANES, stride=st), :] = act
        for s in range(n_slab):
            o_ref[lo:lo + mb, s * LANES:(s + 1) * LANES] = a_sc[s].astype(o_ref.dtype)

    for sb in range(n_sub):
        lo = sb * mb
        u = u_ref[lo:lo + mb, :]
        for x_scs, w_ref in ((xg_scs, wg_ref), (xv_scs, wv_ref)):
            x = jnp.dot(u, w_ref[...], preferred_element_type=F32)
            for s in range(n_slab):
                xs = x[:, s * LANES:(s + 1) * LANES]
                x_scs[sb][s, PAD:PAD + mb, :] = xs
                if sb + 1 < n_sub:
                    x_scs[sb + 1][s, 0:PAD, :] = xs[mb - PAD:mb]
        if sb:
            tail(sb - 1)
    tail(n_sub - 1)


def _ffn_up_call(u, w_up, conv_w, conv_b, B, Lp, tn):
    M, D = u.shape
    FF = w_up.shape[1] // 2
    nt = FF // tn
    n_sub = 4 if Lp % (4 * 2 * SUBLANES) == 0 else 1
    mb = Lp // n_sub
    return pl.pallas_call(
        functools.partial(_ffn_up_kernel, n_sub=n_sub),
        grid=(B, nt),
        in_specs=[pl.BlockSpec((Lp, D), lambda b, j: (b, 0)),
                  pl.BlockSpec((D, tn), lambda b, j: (0, j)),
                  pl.BlockSpec((D, tn), lambda b, j: (0, nt + j)),
                  pl.BlockSpec((FFN_CONV, tn), lambda b, j: (0, j)),
                  pl.BlockSpec((FFN_CONV, tn), lambda b, j: (0, nt + j)),
                  pl.BlockSpec((1, tn), lambda b, j: (0, j)),
                  pl.BlockSpec((1, tn), lambda b, j: (0, nt + j))],
        out_specs=pl.BlockSpec((Lp, tn), lambda b, j: (b, j)),
        out_shape=jax.ShapeDtypeStruct((M, FF), BF16),
        scratch_shapes=([pltpu.VMEM((tn // LANES, SUBLANES + mb, LANES), F32)] * (2 * n_sub)
                        + [pltpu.VMEM((tn // LANES, mb, LANES), F32)] * n_sub),
        compiler_params=_cparams(("parallel", "arbitrary")),
        name="ffn_up_conv_geglu",
    )(u, w_up, w_up, conv_w, conv_w, conv_b.reshape(1, -1), conv_b.reshape(1, -1))


def kernel(x, meta, rel_bias, norm_mix_pre, norm_mix_post, norm_ffn_pre, norm_ffn_post, w_in, pool_w, pool_scale, diff_lq1, diff_lk1, diff_lq2, diff_lk2, diff_subln, rwkv_mu, rwkv_w0, rwkv_w2, rwkv_a0, rwkv_a2, rwkv_g2, rwkv_kk, rwkv_ka, rwkv_rk, rwkv_lnx_w, rwkv_lnx_b, rwkv_v0, rwkv_v1, rwkv_v2, p_a, p_b, p_c, w_o, ffn_up, ffn_conv_w, ffn_conv_b, ffn_down):
    B, S, D = x.shape
    n_meta = meta.shape[0]
    depth = w_in.shape[0]
    L = S + n_meta
    Lp = _round_up(L, CHUNK)
    M = B * Lp

    PW = pool_scale.shape[1]
    DW = p_b.shape[1]
    RW = rwkv_w0.shape[1]
    hd_r = rwkv_rk.shape[2]
    dl, al, gl = rwkv_w2.shape[1], rwkv_a2.shape[1], rwkv_g2.shape[1]
    off_q = PW
    off_r = PW + 3 * DW
    off_g = off_r + 3 * RW + dl + al + gl
    assert off_g + 3 * D == w_in.shape[2]
    assert Lp == L or S % CHUNK == 0
    assert max(dl, al, gl, rwkv_v1.shape[2]) <= LANES

    h = jnp.concatenate([jnp.broadcast_to(meta.astype(x.dtype)[None], (B, n_meta, D)), x,
                         jnp.zeros((B, Lp - L, D), x.dtype)], axis=1).reshape(M, D)

    hid = np.arange(RW) // hd_r
    ones_bd = jnp.asarray((hid[:, None] == hid[None, :]).astype(np.float32), dtype=BF16)

    def pad_lora(a, axis):
        segs = [lax.slice_in_dim(a, 0, 3 * RW, axis=axis)]
        o = 3 * RW
        for n in (dl, al, gl):
            seg = lax.slice_in_dim(a, o, o + n, axis=axis)
            pad = [(0, 0)] * a.ndim
            pad[axis] = (0, LANES - n)
            segs.append(jnp.pad(seg, pad))
            o += n
        return jnp.concatenate(segs, axis=axis)

    tables = {}
    for tq, _, _ in _attn_segments(Lp):
        tt = _round_up(tq, ATTN_TILE)
        if tt not in tables:
            tables[tt] = _bias_table_call(rel_bias, n_meta, tt)
        tables[tq] = tables[tt]

    tm_big = _pick_tile(M, 1088, 16)
    tm_res = _pick_tile(M, 544, 16)
    u = _rms_call(h, norm_mix_pre[0], tm_res)
    v_first = None
    for l in range(depth):
        lam_init = 0.8 - 0.6 * math.exp(-0.3 * l)
        wl = w_in[l]
        w_gate = wl[:, off_g:].astype(BF16)
        w_qkv = wl[:, off_q:off_r].astype(BF16)
        w_pool = wl[:, :PW].astype(BF16)
        w_rwkv = pad_lora(wl[:, off_r:off_g], 1).astype(BF16)
        mu_p = pad_lora(rwkv_mu[l], 0)

        zg = _mm_call(u, w_gate, BF16, tm_big, _pick_tile(3 * D, 768, LANES), "in_proj_gate")
        zqkv = _mm_call(u, w_qkv, BF16, tm_big, _pick_tile(3 * DW, 768, LANES), "in_proj_qkv")
        zp = _mm_call(u, w_pool, BF16, tm_big, _pick_tile(PW, 768, LANES), "in_proj_pool")
        zr = _mm_call(u, w_rwkv, F32, tm_big, _pick_tile(w_rwkv.shape[1], 768, LANES), "in_proj_rwkv")

        yp = _pool_call(zp, pool_w[l], pool_scale[l], B, Lp)
        yd = _attn_call(zqkv, tables, diff_lq1[l], diff_lk1[l], diff_lq2[l], diff_lk2[l],
                        diff_subln[l], B, L, Lp, lam_init)
        v_res = None if l == 0 else (rwkv_v0[l - 1], rwkv_v1[l - 1], rwkv_v2[l - 1])
        streams = _rwkv_prep_call(zr, B, Lp, RW, mu_p, rwkv_w0[l], rwkv_w2[l], rwkv_a0[l], rwkv_a2[l],
                                  rwkv_g2[l], rwkv_kk[l], rwkv_ka[l], ones_bd, v_first, v_res)
        if l == 0:
            v_first = streams[3]
        yr = _scan_call(streams, rwkv_rk[l], rwkv_lnx_w[l], rwkv_lnx_b[l], ones_bd, B, Lp, hd_r)

        merged = _merge_call(yp, yd, yr, zg, p_a[l].astype(BF16), p_b[l].astype(BF16),
                             p_c[l].astype(BF16), tm_res)
        h, u2 = _mm_res_call(merged, w_o[l].astype(BF16), h, norm_mix_post[l], norm_ffn_pre[l],
                             "out_proj_res")
        FF = ffn_down.shape[1]
        act = _ffn_up_call(u2, ffn_up[l].astype(BF16), ffn_conv_w[l], ffn_conv_b[l], B, Lp,
                           _pick_tile(FF, 512, LANES))
        g_next = norm_mix_pre[l + 1] if l + 1 < depth else norm_mix_pre[l]
        h, u = _mm_res_call(act, ffn_down[l].astype(BF16), h, norm_ffn_post[l], g_next,
                            "ffn_down_res")
    return h.reshape(B, Lp, D)[:, n_meta:n_meta + S]
```

```python
import functools
import math

import numpy as np
import jax
import jax.numpy as jnp
from jax import lax
from jax.experimental import pallas as pl
from jax.experimental.pallas import tpu as pltpu

F32 = jnp.float32
BF16 = jnp.bfloat16

CHUNK = 64
NORM_EPS = 1e-6
NEG_INF = -1e30
POOL_WINDOWS = (2, 4, 8, 16)
DIFF_SUBLN_EPS = 1e-5
REL_MAX_DIST = 128
RWKV_LNX_EPS = 64e-5
FFN_CONV = 3

LANES = 128
SUBLANES = 8
ATTN_TILE = 128
ATTN_Q_TILE = 512
ATTN_UNROLL = 3
SCAN_BATCH = 4
VMEM_LIMIT = 56 * 1024 * 1024


def _cparams(sem):
    return pltpu.CompilerParams(dimension_semantics=sem, vmem_limit_bytes=VMEM_LIMIT)


def _round_up(x, m):
    return (x + m - 1) // m * m


def _pick_tile(total, target, mult):
    best = None
    for t in range(mult, min(total, target) + 1, mult):
        if total % t == 0:
            best = t
    assert best is not None, (total, target, mult)
    return best


def _shift_down(x, d):
    row = lax.broadcasted_iota(jnp.int32, x.shape, 0)
    return jnp.where(row >= d, pltpu.roll(x, d, 0), 0.0)


def _sigmoid(x):
    return 1.0 / (1.0 + jnp.exp(-x))


def _dot(a, b):
    return jnp.dot(a.astype(BF16), b.astype(BF16), preferred_element_type=F32)


def _dot_nt(a, b):
    return lax.dot_general(a.astype(BF16), b.astype(BF16), (((1,), (1,)), ((), ())),
                           preferred_element_type=F32)


def _split3(x):
    hi = x.astype(BF16)
    r1 = x - hi.astype(F32)
    mid = r1.astype(BF16)
    lo = (r1 - mid.astype(F32)).astype(BF16)
    return hi, mid, lo


def _dot_exact_rhs(x, ones_bf16):
    hi, mid, lo = _split3(x)
    acc = jnp.dot(hi, ones_bf16, preferred_element_type=F32)
    acc = acc + jnp.dot(mid, ones_bf16, preferred_element_type=F32)
    return acc + jnp.dot(lo, ones_bf16, preferred_element_type=F32)


def _rms_kernel(x_ref, g_ref, o_ref):
    x = x_ref[...]
    y = x * lax.rsqrt(jnp.mean(x * x, axis=-1, keepdims=True) + NORM_EPS)
    o_ref[...] = (y * g_ref[...]).astype(o_ref.dtype)


def _rms_call(x, g, tm):
    M, D = x.shape
    return pl.pallas_call(
        _rms_kernel,
        grid=(M // tm,),
        in_specs=[pl.BlockSpec((tm, D), lambda i: (i, 0)),
                  pl.BlockSpec((1, D), lambda i: (0, 0))],
        out_specs=pl.BlockSpec((tm, D), lambda i: (i, 0)),
        out_shape=jax.ShapeDtypeStruct((M, D), BF16),
        compiler_params=_cparams(("parallel",)),
        name="rms_norm",
    )(x, g.reshape(1, D))


def _mm_kernel(a_ref, b_ref, o_ref):
    o_ref[...] = jnp.dot(a_ref[...], b_ref[...], preferred_element_type=F32).astype(o_ref.dtype)


def _mm_call(a, b, out_dtype, tm, tn, name):
    M, K = a.shape
    N = b.shape[1]
    return pl.pallas_call(
        _mm_kernel,
        grid=(M // tm, N // tn),
        in_specs=[pl.BlockSpec((tm, K), lambda i, j: (i, 0)),
                  pl.BlockSpec((K, tn), lambda i, j: (0, j))],
        out_specs=pl.BlockSpec((tm, tn), lambda i, j: (i, j)),
        out_shape=jax.ShapeDtypeStruct((M, N), out_dtype),
        compiler_params=_cparams(("parallel", "arbitrary")),
        name=name,
    )(a, b)


def _mm_res_kernel(a_ref, w_ref, h_ref, gp_ref, gn_ref, oh_ref, ou_ref):
    f = jnp.dot(a_ref[...], w_ref[...], preferred_element_type=F32)
    y = f * lax.rsqrt(jnp.mean(f * f, axis=-1, keepdims=True) + NORM_EPS) * gp_ref[...]
    hn = h_ref[...] + y
    oh_ref[...] = hn
    u = hn * lax.rsqrt(jnp.mean(hn * hn, axis=-1, keepdims=True) + NORM_EPS) * gn_ref[...]
    ou_ref[...] = u.astype(ou_ref.dtype)


def _mm_res_tile(M, K, D):
    budget = VMEM_LIMIT - 8 * 1024 * 1024
    best = None
    for tm in range(16, M + 1, 16):
        if M % tm:
            continue
        need = (K * D * 2
                + 2 * tm * K * 2
                + 2 * 2 * tm * D * 4
                + 2 * tm * D * 2
                + 2 * tm * D * 4)
        if need <= budget:
            best = tm
    assert best is not None
    return best


def _mm_res_call(a, w, h, g_post, g_next, name):
    M, K = a.shape
    D = w.shape[1]
    tm = _mm_res_tile(M, K, D)
    return pl.pallas_call(
        _mm_res_kernel,
        grid=(M // tm,),
        in_specs=[pl.BlockSpec((tm, K), lambda i: (i, 0)),
                  pl.BlockSpec((K, D), lambda i: (0, 0), pipeline_mode=pl.Buffered(1)),
                  pl.BlockSpec((tm, D), lambda i: (i, 0)),
                  pl.BlockSpec((1, D), lambda i: (0, 0)),
                  pl.BlockSpec((1, D), lambda i: (0, 0))],
        out_specs=[pl.BlockSpec((tm, D), lambda i: (i, 0)),
                   pl.BlockSpec((tm, D), lambda i: (i, 0))],
        out_shape=[jax.ShapeDtypeStruct((M, D), F32),
                   jax.ShapeDtypeStruct((M, D), BF16)],
        compiler_params=_cparams(("parallel",)),
        name=name,
    )(a, w, h, g_post.reshape(1, D), g_next.reshape(1, D))


def _pool_kernel(z_ref, w_ref, s_ref, o_ref, *, gw):
    L = z_ref.shape[1]
    row = lax.broadcasted_iota(jnp.int32, (L, gw), 0)
    for gi, win in enumerate(POOL_WINDOWS):
        sl = slice(gi * gw, (gi + 1) * gw)
        zg = z_ref[0, :, sl].astype(F32)
        s = zg
        d = 1
        while d < win:
            s = s + _shift_down(s, d)
            d *= 2
        cnt = jnp.minimum(row + 1, win).astype(F32)
        pooled = s / cnt - zg
        y = _dot(pooled, w_ref[gi]) * s_ref[:, sl]
        o_ref[0, :, sl] = y.astype(o_ref.dtype)


def _pool_call(z, w, scale, B, Lp):
    PW = z.shape[-1]
    G = len(POOL_WINDOWS)
    gw = PW // G
    z3 = z.reshape(B, Lp, PW)
    out = pl.pallas_call(
        functools.partial(_pool_kernel, gw=gw),
        grid=(B,),
        in_specs=[pl.BlockSpec((1, Lp, PW), lambda b: (b, 0, 0)),
                  pl.BlockSpec((G, gw, gw), lambda b: (0, 0, 0)),
                  pl.BlockSpec((1, PW), lambda b: (0, 0))],
        out_specs=pl.BlockSpec((1, Lp, PW), lambda b: (b, 0, 0)),
        out_shape=jax.ShapeDtypeStruct((B, Lp, PW), BF16),
        compiler_params=_cparams(("parallel",)),
        name="pool_mixer",
    )(z3, w.astype(BF16), scale.reshape(1, PW))
    return out.reshape(B * Lp, PW)


def _rel_bucket(rel, n_buckets):
    half = n_buckets // 2
    max_exact = half // 2
    n = np.abs(rel)
    large = max_exact + (np.log(np.maximum(n, 1) / max_exact) / math.log(REL_MAX_DIST / max_exact)
                         * (half - max_exact)).astype(np.int32)
    large = np.minimum(large, half - 1)
    return ((rel > 0) * half + np.where(n < max_exact, n, large)).astype(np.int32)


def _bias_table_kernel(rb_ref, bucket_ref, pen_ref, o_ref, *, n_buckets):
    s = pl.program_id(0)
    bucket = bucket_ref[...]
    acc = pen_ref[...]
    for b in range(n_buckets):
        acc = acc + jnp.where(bucket == b, rb_ref[b, s], 0.0)
    o_ref[0] = acc


def _bias_table_call(rel_bias, n_meta, tq):
    n_buckets, n_sub = rel_bias.shape
    KT = ATTN_TILE
    assert tq % KT == 0 and KT % CHUNK == 0 and n_meta <= KT
    n_near = tq // KT + 2
    n_slabs = n_near + 2
    far = n_buckets // 2 - 1
    r = np.arange(tq)[:, None]
    c = np.arange(KT)[None, :]
    assert np.all(_rel_bucket(-(KT + 1 + np.arange(4 * KT)), n_buckets) == far)
    bucket = np.full((n_slabs, tq, KT), far, np.int32)
    pen = np.zeros((n_slabs, tq, KT), np.float32)
    pen[-1] = NEG_INF
    for idx in range(n_near):
        key = (idx - 1) * KT + c
        bucket[1 + idx] = _rel_bucket(key - r, n_buckets)
        visible = np.floor_divide(key - n_meta, CHUNK) <= np.floor_divide(r - n_meta, CHUNK)
        pen[1 + idx] = np.where(visible, 0.0, NEG_INF)
    blk = pl.BlockSpec((n_slabs, tq, KT), lambda s: (0, 0, 0))
    return pl.pallas_call(
        functools.partial(_bias_table_kernel, n_buckets=n_buckets),
        grid=(n_sub,),
        in_specs=[pl.BlockSpec(memory_space=pltpu.SMEM), blk, blk],
        out_specs=pl.BlockSpec((1, n_slabs, tq, KT), lambda s: (s, 0, 0, 0)),
        out_shape=jax.ShapeDtypeStruct((n_sub, n_slabs, tq, KT), F32),
        compiler_params=_cparams(("arbitrary",)),
        name="bias_table_%d" % tq,
    )(rel_bias, jnp.asarray(bucket), jnp.asarray(pen))


def _attn_kernel(*refs, hd, lam_init, nt, tile0, live):
    (lq1_ref, lk1_ref, lq2_ref, lk2_ref, sub_ref, q_ref, k_ref, v_ref, w_ref,
     o_ref, s_sc, mx_sc, l_sc, acc_sc, k_sc, v_sc) = refs
    KT = ATTN_TILE
    tq = q_ref.shape[1]
    ratio = -(-tq // KT)
    Lk = k_ref.shape[1]

    @pl.when(pl.program_id(2) == 0)
    def _():
        k_sc[0:Lk, :] = k_ref[0]
        v_sc[0:Lk, :] = v_ref[0]
        if k_sc.shape[0] > Lk:
            k_sc[Lk:, :] = jnp.zeros((k_sc.shape[0] - Lk, k_sc.shape[1]), k_sc.dtype)
            v_sc[Lk:, :] = jnp.zeros((v_sc.shape[0] - Lk, v_sc.shape[1]), v_sc.dtype)
    n_slabs = w_ref.shape[1]
    i = pl.program_id(2)
    jt0 = tile0 + i * ratio
    n_need = jnp.minimum(jt0 + ratio + 1, nt)
    n_pairs = (n_need + 1) // 2
    q = q_ref[0, 0:live, :]

    def slab(j):
        return jnp.where(j < nt, jnp.clip(j - jt0 + 2, 0, n_slabs - 1), n_slabs - 1)

    def two_tiles(ref, jj):
        j0 = 2 * jj
        j1 = jnp.minimum(j0 + 1, nt - 1)
        t0 = ref[pl.ds(pl.multiple_of(j0 * KT, KT), KT), :]
        t1 = ref[pl.ds(pl.multiple_of(j1 * KT, KT), KT), :]
        return jnp.concatenate([t0, t1], axis=0)

    mx_sc[...] = jnp.full_like(mx_sc, NEG_INF)

    def unrolled(step):
        def group(t, carry):
            step([ATTN_UNROLL * t + r for r in range(ATTN_UNROLL)])
            return carry

        def single(jj, carry):
            step([jj])
            return carry
        n_groups = n_pairs // ATTN_UNROLL
        lax.fori_loop(0, n_groups, group, 0)
        lax.fori_loop(n_groups * ATTN_UNROLL, n_pairs, single, 0)

    def pass1(jjs):
        kts = [two_tiles(k_sc, jj) for jj in jjs]
        for s in range(2):
            mx = None
            for jj, kt in zip(jjs, kts):
                sc = _dot_nt(q[:, s * hd:(s + 1) * hd], kt[:, s * hd:(s + 1) * hd])
                sc = sc + jnp.concatenate([w_ref[s, slab(2 * jj), 0:live, :],
                                           w_ref[s, slab(2 * jj + 1), 0:live, :]], axis=1)
                s_sc[s, jj] = sc
                half = jnp.maximum(sc[:, :KT], sc[:, KT:])
                mx = half if mx is None else jnp.maximum(mx, half)
            mx_sc[s] = jnp.maximum(mx_sc[s], mx)

    unrolled(pass1)
    m = [jnp.max(mx_sc[s], axis=-1, keepdims=True) for s in range(2)]
    l_sc[...] = jnp.zeros_like(l_sc)
    acc_sc[...] = jnp.zeros_like(acc_sc)

    def pass2(jjs):
        for jj in jjs:
            vt = two_tiles(v_sc, jj)
            for s in range(2):
                p = jnp.exp(s_sc[s, jj] - m[s])
                l_sc[s] += p[:, :KT] + p[:, KT:]
                acc_sc[s] += jnp.dot(p.astype(BF16), vt, preferred_element_type=F32)

    unrolled(pass2)
    l = [jnp.sum(l_sc[s], axis=-1, keepdims=True) for s in range(2)]
    lam = (jnp.exp(jnp.sum(lq1_ref[...] * lk1_ref[...], axis=-1, keepdims=True))
           - jnp.exp(jnp.sum(lq2_ref[...] * lk2_ref[...], axis=-1, keepdims=True)) + lam_init)
    o = acc_sc[0] / l[0] - lam * (acc_sc[1] / l[1])
    y = o * lax.rsqrt(jnp.mean(o * o, axis=-1, keepdims=True) + DIFF_SUBLN_EPS)
    o_ref[0, 0:live, :] = (y * sub_ref[...] * (1.0 - lam_init)).astype(o_ref.dtype)
    if live < tq:
        o_ref[0, live:tq, :] = jnp.zeros((tq - live, o_ref.shape[2]), o_ref.dtype)


def _attn_segment(qkv3, table, params, lam_init, tq, row0, n_tiles, live):
    B, Lp, W3 = qkv3.shape
    DW = W3 // 3
    hd = params[0].shape[1]
    hw = 2 * hd
    n_heads = DW // hw
    KT = ATTN_TILE
    nt = -(-Lp // KT)
    blk0 = row0 // tq
    kern = functools.partial(_attn_kernel, hd=hd, lam_init=lam_init, nt=nt,
                             tile0=row0 // KT, live=live)
    par = lambda n: pl.BlockSpec((1, n), lambda b, h, i: (0, 0))
    in_specs = [par(hd)] * 4 + [
        par(hw),
        pl.BlockSpec((1, tq, hw), lambda b, h, i: (b, blk0 + i, h)),
        pl.BlockSpec((1, Lp, hw), lambda b, h, i: (b, 0, n_heads + h)),
        pl.BlockSpec((1, Lp, hw), lambda b, h, i: (b, 0, 2 * n_heads + h)),
        pl.BlockSpec((2, table.shape[1], tq, KT), lambda b, h, i: (h, 0, 0, 0))]
    args = list(params) + [qkv3, qkv3, qkv3, table]
    max_pairs = (nt + 1) // 2
    return pl.pallas_call(
        kern,
        grid=(B, n_heads, n_tiles),
        in_specs=in_specs,
        out_specs=pl.BlockSpec((1, tq, hw), lambda b, h, i: (b, i, h)),
        out_shape=jax.ShapeDtypeStruct((B, n_tiles * tq, DW), BF16),
        scratch_shapes=[pltpu.VMEM((2, max_pairs, live, 2 * KT), F32), pltpu.VMEM((2, live, KT), F32),
                        pltpu.VMEM((2, live, KT), F32), pltpu.VMEM((2, live, hw), F32),
                        pltpu.VMEM((nt * KT, hw), BF16), pltpu.VMEM((nt * KT, hw), BF16)],
        compiler_params=_cparams(("parallel", "parallel", "arbitrary")),
        name="diff_attention_%d" % tq,
    )(*args)


def _attn_segments(Lp):
    big = ATTN_Q_TILE
    n_big = Lp // big
    segs = []
    if n_big:
        segs.append((big, 0, n_big))
    row0 = n_big * big
    n_mid = (Lp - row0) // ATTN_TILE
    if n_mid:
        segs.append((ATTN_TILE, row0, n_mid))
        row0 += n_mid * ATTN_TILE
    if Lp > row0:
        segs.append((Lp - row0, row0, 1))
    return segs


def _attn_call(zqkv, tables, lq1, lk1, lq2, lk2, subln, B, L, Lp, lam_init):
    vec = lambda a: a.reshape(1, -1)
    params = [vec(lq1), vec(lk1), vec(lq2), vec(lk2), vec(subln)]
    qkv3 = zqkv.reshape(B, Lp, zqkv.shape[-1])
    outs = []
    for tq, row0, n_tiles in _attn_segments(Lp):
        live = min(tq, _round_up(L - row0, 2 * SUBLANES)) if n_tiles == 1 else tq
        outs.append(_attn_segment(qkv3, tables[tq], params, lam_init, tq, row0, n_tiles, live))
    out = outs[0] if len(outs) == 1 else jnp.concatenate(outs, axis=1)
    return out.reshape(B * Lp, -1)


def _rwkv_prep_kernel(*refs, rw, has_vres):
    if has_vres:
        (z_ref, zp_ref, mu_ref, w0_ref, w2_ref, a0_ref, a2_ref, g2_ref, kk_ref, ka_ref, ones_ref,
         vf_ref, v0_ref, v1_ref, v2_ref,
         r_o, lw_o, k_o, v_o, a_o, b_o, g_o) = refs
    else:
        (z_ref, zp_ref, mu_ref, w0_ref, w2_ref, a0_ref, a2_ref, g2_ref, kk_ref, ka_ref, ones_ref,
         r_o, lw_o, k_o, v_o, a_o, b_o, g_o) = refs
    i = pl.program_id(1)
    z = z_ref[0]
    last_prev = zp_ref[0, SUBLANES - 1:SUBLANES, :]
    last_prev = jnp.where(i > 0, last_prev, 0.0)
    row = lax.broadcasted_iota(jnp.int32, z.shape, 0)
    prev = jnp.where(row >= 1, pltpu.roll(z, 1, 0), last_prev)
    zs = z + (prev - z) * mu_ref[...]
    r = zs[:, 0:rw]
    kx = zs[:, rw:2 * rw]
    vx = zs[:, 2 * rw:3 * rw]
    wd = zs[:, 3 * rw:3 * rw + LANES]
    ad = zs[:, 3 * rw + LANES:3 * rw + 2 * LANES]
    gd = zs[:, 3 * rw + 2 * LANES:3 * rw + 3 * LANES]
    wl = w0_ref[...] + _dot(jnp.tanh(wd), w2_ref[...])
    sp = jnp.maximum(-wl, 0.0) + jnp.log(1.0 + jnp.exp(-jnp.abs(wl)))
    lw = -jnp.exp(-sp - 0.5)
    if has_vres:
        gate_v = _sigmoid(v0_ref[...] + _dot(_dot(vx, v1_ref[...]), v2_ref[...]))
        vx = vx + (vf_ref[0] - vx) * gate_v
    a = _sigmoid(a0_ref[...] + _dot(ad, a2_ref[...]))
    g = _dot(_sigmoid(gd), g2_ref[...])
    kk = kx * kk_ref[...]
    ss = _dot_exact_rhs(kk * kk, ones_ref[...])
    kk = kk * lax.rsqrt(jnp.maximum(ss, 1e-24))
    kmod = kx * (1.0 + (a - 1.0) * ka_ref[...])
    r_o[0] = r
    lw_o[0] = lw
    k_o[0] = kmod
    v_o[0] = vx
    a_o[0] = -kk
    b_o[0] = kk * a
    g_o[0] = g


def _pad_rows(w, n):
    return jnp.pad(w, ((0, n - w.shape[0]), (0, 0)))


def _rwkv_prep_call(zr, B, Lp, rw, mu_p, w0, w2, a0, a2, g2, k_k, k_a, ones_bd, v_first, v_res):
    ZW = zr.shape[-1]
    z3 = zr.reshape(B, Lp, ZW)
    tl = _pick_tile(Lp, 384, SUBLANES)
    nb = tl // SUBLANES
    has_vres = v_res is not None
    vec = lambda a: a.reshape(1, -1)
    full = lambda a: pl.BlockSpec(a.shape, lambda b, i: (0,) * a.ndim)
    w2p = _pad_rows(w2, LANES).astype(BF16)
    a2p = _pad_rows(a2, LANES).astype(BF16)
    g2p = _pad_rows(g2, LANES).astype(BF16)
    args = [z3, z3, vec(mu_p), vec(w0), w2p, vec(a0), a2p, g2p, vec(k_k), vec(k_a), ones_bd]
    in_specs = [pl.BlockSpec((1, tl, ZW), lambda b, i: (b, i, 0)),
                pl.BlockSpec((1, SUBLANES, ZW), lambda b, i: (b, jnp.maximum(i * nb - 1, 0), 0))]
    in_specs += [full(a) for a in args[2:]]
    if has_vres:
        v0, v1, v2 = v_res
        v1p = jnp.pad(v1, ((0, 0), (0, LANES - v1.shape[1]))).astype(BF16)
        v2p = _pad_rows(v2, LANES).astype(BF16)
        extra = [v_first, vec(v0), v1p, v2p]
        args += extra
        in_specs += [pl.BlockSpec((1, tl, rw), lambda b, i: (b, i, 0))] + [full(a) for a in extra[1:]]
    o_spec = pl.BlockSpec((1, tl, rw), lambda b, i: (b, i, 0))
    o_shape = jax.ShapeDtypeStruct((B, Lp, rw), F32)
    return pl.pallas_call(
        functools.partial(_rwkv_prep_kernel, rw=rw, has_vres=has_vres),
        grid=(B, Lp // tl),
        in_specs=in_specs,
        out_specs=[o_spec] * 7,
        out_shape=[o_shape] * 7,
        compiler_params=_cparams(("parallel", "arbitrary")),
        name="rwkv_prep",
    )(*args)


def _scan_kernel(r_ref, lw_ref, k_ref, v_ref, a_ref, b_ref, g_ref, rk_ref, lnw_ref, lnb_ref, ones_ref,
                 o_ref, h_sc, *, hd):
    nb = r_ref.shape[0]
    C = r_ref.shape[1]
    C2 = 2 * C
    WB = r_ref.shape[2]
    W = nb * WB
    c = pl.program_id(1)

    @pl.when(c == 0)
    def _():
        h_sc[...] = jnp.zeros_like(h_sc)

    side_by_side = lambda ref: jnp.concatenate([ref[i] for i in range(nb)], axis=1)
    r = side_by_side(r_ref)
    lw = side_by_side(lw_ref)
    k = side_by_side(k_ref)
    v = side_by_side(v_ref)
    a = side_by_side(a_ref)
    b = side_by_side(b_ref)

    cum = lw
    d = 1
    while d < C:
        cum = cum + _shift_down(cum, d)
        d *= 2
    tot = cum[C - 1:C, :]
    e_end = jnp.exp(tot - cum)
    e_neg = jnp.exp(-cum)
    at = a * jnp.exp(cum - lw)
    rt = r * jnp.exp(cum)
    bt = b * e_neg
    kt = k * e_neg
    bb = b * e_end
    kb = k * e_end
    p_c = jnp.exp(tot)

    lane = lax.broadcasted_iota(jnp.int32, (C, LANES), 1)
    first = lane < hd
    rr = lax.broadcasted_iota(jnp.int32, (C2, C2), 0)
    cc = lax.broadcasted_iota(jnp.int32, (C2, C2), 1)
    tr = jnp.where(rr >= C, rr - C, rr)
    tc = jnp.where(cc >= C, cc - C, cc)
    strict = tr > tc
    lower = tr >= tc
    eye = rr == cc

    def stack(x):
        return jnp.concatenate([jnp.where(first, x, 0.0), jnp.where(first, 0.0, x)], axis=0)

    pairs = range(W // LANES)
    stacked = lambda x: [stack(x[:, p * LANES:(p + 1) * LANES]) for p in pairs]
    s_a, s_r, s_b, s_k = stacked(at), stacked(rt), stacked(bt), stacked(kt)
    s_bb, s_kb, s_v = stacked(bb), stacked(kb), stacked(v)
    sc = [_dot_nt(jnp.concatenate([s_a[p], s_r[p]], axis=0), jnp.concatenate([s_b[p], s_k[p]], axis=0))
          for p in pairs]
    n = [jnp.where(strict, sc[p][:C2, :C2], 0.0) for p in pairs]
    a_ak = [jnp.where(strict, sc[p][:C2, C2:], 0.0) for p in pairs]
    m_rb = [jnp.where(lower, sc[p][C2:, :C2], 0.0) for p in pairs]
    m_rk = [jnp.where(lower, sc[p][C2:, C2:], 0.0) for p in pairs]
    xv = [_dot(jnp.concatenate([a_ak[p], m_rk[p], s_kb[p].T], axis=0), s_v[p]) for p in pairs]
    t_inv = [jnp.where(eye, 1.0, 0.0) + n[p] for p in pairs]
    pw = [_dot(n[p], n[p]) for p in pairs]
    step = 4
    while step < C:
        tp = [_dot(jnp.concatenate([t_inv[p], pw[p]], axis=0), pw[p]) for p in pairs]
        t_inv = [t_inv[p] + tp[p][:C2] for p in pairs]
        pw = [tp[p][C2:] for p in pairs]
        step *= 2
    t_inv = [t_inv[p] + _dot(t_inv[p], pw[p]) for p in pairs]
    au = [_dot(t_inv[p], jnp.concatenate([s_a[p], xv[p][:C2]], axis=1)) for p in pairs]
    mbu = [_dot(jnp.concatenate([m_rb[p], s_bb[p].T], axis=0), au[p]) for p in pairs]
    ys = []
    for p in pairs:
        r_hat = s_r[p] + mbu[p][:C2, :LANES]
        y0 = mbu[p][:C2, LANES:] + xv[p][C2:2 * C2]
        g_mat = jnp.where(eye, p_c[:, p * LANES:(p + 1) * LANES], 0.0) + mbu[p][C2:, :LANES]
        h_add = mbu[p][C2:, LANES:] + xv[p][2 * C2:]
        yh = _dot(jnp.concatenate([r_hat, g_mat], axis=0), h_sc[p])
        y = yh[:C2] + y0
        h_sc[p] = yh[C2:] + h_add
        ys.append(y[0:C] + y[C:C2])
    ones = ones_ref[...]
    inv_hd = 1.0 / hd
    ppb = WB // LANES
    for i in range(nb):
        y = jnp.concatenate(ys[i * ppb:(i + 1) * ppb], axis=1)
        mean = _dot_exact_rhs(y, ones) * inv_hd
        yc = y - mean
        var = _dot_exact_rhs(yc * yc, ones) * inv_hd
        yn = yc * lax.rsqrt(var + RWKV_LNX_EPS) * lnw_ref[...] + lnb_ref[...]
        bonus = _dot_exact_rhs(r_ref[i] * k_ref[i] * rk_ref[...], ones) * v_ref[i]
        o_ref[i] = ((yn + bonus) * g_ref[i]).astype(o_ref.dtype)


def _scan_call(streams, r_k, lnx_w, lnx_b, ones_bd, B, Lp, hd):
    r, lw, k, v, a, b, g = streams
    rw = r.shape[-1]
    C = CHUNK
    assert 2 * hd == LANES and rw % LANES == 0 and Lp % C == 0
    vec = lambda x: x.reshape(1, -1)
    nb = SCAN_BATCH if B % SCAN_BATCH == 0 else 1
    blk = pl.BlockSpec((nb, C, rw), lambda bi, c: (bi, c, 0))
    par = pl.BlockSpec((1, rw), lambda bi, c: (0, 0))
    out = pl.pallas_call(
        functools.partial(_scan_kernel, hd=hd),
        grid=(B // nb, Lp // C),
        in_specs=[blk] * 7 + [par] * 3 + [pl.BlockSpec((rw, rw), lambda bi, c: (0, 0))],
        out_specs=blk,
        out_shape=jax.ShapeDtypeStruct((B, Lp, rw), BF16),
        scratch_shapes=[pltpu.VMEM((nb * rw // LANES, LANES, LANES), F32)],
        compiler_params=_cparams(("parallel", "arbitrary")),
        name="rwkv_scan",
    )(r, lw, k, v, a, b, g, vec(r_k), vec(lnx_w), vec(lnx_b), ones_bd)
    return out.reshape(B * Lp, rw)


def _merge_kernel(yp_ref, yd_ref, yr_ref, g0_ref, g1_ref, g2_ref, pa_ref, pb_ref, pc_ref, o_ref):
    dot = lambda x, w: jnp.dot(x[...], w[...], preferred_element_type=F32)
    m = _sigmoid(g0_ref[...].astype(F32)) * dot(yp_ref, pa_ref)
    m = m + _sigmoid(g1_ref[...].astype(F32)) * dot(yd_ref, pb_ref)
    m = m + _sigmoid(g2_ref[...].astype(F32)) * dot(yr_ref, pc_ref)
    o_ref[...] = m.astype(o_ref.dtype)


def _merge_call(yp, yd, yr, zg, pa, pb, pc, tm):
    M = yp.shape[0]
    D = pa.shape[1]
    rows = lambda a: pl.BlockSpec((tm, a.shape[1]), lambda i: (i, 0))
    full = lambda a: pl.BlockSpec(a.shape, lambda i: (0, 0))
    gate = lambda n: pl.BlockSpec((tm, D), lambda i: (i, n))
    return pl.pallas_call(
        _merge_kernel,
        grid=(M // tm,),
        in_specs=[rows(yp), rows(yd), rows(yr), gate(0), gate(1), gate(2), full(pa), full(pb), full(pc)],
        out_specs=pl.BlockSpec((tm, D), lambda i: (i, 0)),
        out_shape=jax.ShapeDtypeStruct((M, D), BF16),
        compiler_params=_cparams(("parallel",)),
        name="gated_merge",
    )(yp, yd, yr, zg, zg, zg, pa, pb, pc)


def _ffn_up_kernel(u_ref, wv_ref, wg_ref, cwv_ref, cwg_ref, cbv_ref, cbg_ref, o_ref, *scratch, n_sub):
    xg_scs, xv_scs, a_scs = scratch[:n_sub], scratch[n_sub:2 * n_sub], scratch[2 * n_sub:]
    L = u_ref.shape[0]
    mb = L // n_sub
    st = mb // SUBLANES
    PAD = SUBLANES
    n_slab = a_scs[0].shape[0]
    k_in = math.sqrt(2.0 / math.pi)
    zeros = jnp.zeros((PAD, LANES), F32)
    for s in range(n_slab):
        xv_scs[0][s, 0:PAD, :] = zeros
        xg_scs[0][s, 0:PAD, :] = zeros

    def taps(cw_ref, cb_ref, s):
        sl = slice(s * LANES, (s + 1) * LANES)
        rows = [jnp.broadcast_to(cw_ref[j:j + 1, sl], (SUBLANES, LANES)) for j in range(FFN_CONV)]
        return rows, jnp.broadcast_to(cb_ref[:, sl], (SUBLANES, LANES))

    def tail(sb):
        lo = sb * mb
        xg_sc, xv_sc, a_sc = xg_scs[sb], xv_scs[sb], a_scs[sb]
        for s in range(n_slab):
            wg, bg = taps(cwg_ref, cbg_ref, s)
            wv, bv = taps(cwv_ref, cbv_ref, s)
            ld = lambda ref, j: ref[s, pl.ds(PAD + j, SUBLANES, stride=st), :]
            xg = [ld(xg_sc, j) for j in range(1 - FFN_CONV, 0)]
            xv = [ld(xv_sc, j) for j in range(1 - FFN_CONV, 0)]
            for j in range(st):
                xg.append(ld(xg_sc, j))
                xv.append(ld(xv_sc, j))
                gate = bg
                val = bv
                for t in range(FFN_CONV):
                    gate = gate + xg[j + t] * wg[t]
                    val = val + xv[j + t] * wv[t]
                inner = gate * (2.0 * k_in + (2.0 * k_in * 0.044715) * (gate * gate))
                act = gate / (1.0 + jnp.exp(-inner)) * val
                a_sc[s, pl.ds(j, SUBLANES, stride=st), :] = act
        for s in range(n_slab):
            o_ref[lo:lo + mb, s * LANES:(s + 1) * LANES] = a_sc[s].astype(o_ref.dtype)

    for sb in range(n_sub):
        lo = sb * mb
        u = u_ref[lo:lo + mb, :]
        for x_scs, w_ref in ((xg_scs, wg_ref), (xv_scs, wv_ref)):
            x = jnp.dot(u, w_ref[...], preferred_element_type=F32)
            for s in range(n_slab):
                xs = x[:, s * LANES:(s + 1) * LANES]
                x_scs[sb][s, PAD:PAD + mb, :] = xs
                if sb + 1 < n_sub:
                    x_scs[sb + 1][s, 0:PAD, :] = xs[mb - PAD:mb]
        if sb:
            tail(sb - 1)
    tail(n_sub - 1)


def _ffn_up_call(u, w_up, conv_w, conv_b, B, Lp, tn):
    M, D = u.shape
    FF = w_up.shape[1] // 2
    nt = FF // tn
    n_sub = 4 if Lp % (4 * 2 * SUBLANES) == 0 else 1
    mb = Lp // n_sub
    return pl.pallas_call(
        functools.partial(_ffn_up_kernel, n_sub=n_sub),
        grid=(B, nt),
        in_specs=[pl.BlockSpec((Lp, D), lambda b, j: (b, 0)),
                  pl.BlockSpec((D, tn), lambda b, j: (0, j)),
                  pl.BlockSpec((D, tn), lambda b, j: (0, nt + j)),
                  pl.BlockSpec((FFN_CONV, tn), lambda b, j: (0, j)),
                  pl.BlockSpec((FFN_CONV, tn), lambda b, j: (0, nt + j)),
                  pl.BlockSpec((1, tn), lambda b, j: (0, j)),
                  pl.BlockSpec((1, tn), lambda b, j: (0, nt + j))],
        out_specs=pl.BlockSpec((Lp, tn), lambda b, j: (b, j)),
        out_shape=jax.ShapeDtypeStruct((M, FF), BF16),
        scratch_shapes=([pltpu.VMEM((tn // LANES, SUBLANES + mb, LANES), F32)] * (2 * n_sub)
                        + [pltpu.VMEM((tn // LANES, mb, LANES), F32)] * n_sub),
        compiler_params=_cparams(("parallel", "arbitrary")),
        name="ffn_up_conv_geglu",
    )(u, w_up, w_up, conv_w, conv_w, conv_b.reshape(1, -1), conv_b.reshape(1, -1))


def kernel(x, meta, rel_bias, norm_mix_pre, norm_mix_post, norm_ffn_pre, norm_ffn_post, w_in, pool_w, pool_scale, diff_lq1, diff_lk1, diff_lq2, diff_lk2, diff_subln, rwkv_mu, rwkv_w0, rwkv_w2, rwkv_a0, rwkv_a2, rwkv_g2, rwkv_kk, rwkv_ka, rwkv_rk, rwkv_lnx_w, rwkv_lnx_b, rwkv_v0, rwkv_v1, rwkv_v2, p_a, p_b, p_c, w_o, ffn_up, ffn_conv_w, ffn_conv_b, ffn_down):
    B, S, D = x.shape
    n_meta = meta.shape[0]
    depth = w_in.shape[0]
    L = S + n_meta
    Lp = _round_up(L, CHUNK)
    M = B * Lp

    PW = pool_scale.shape[1]
    DW = p_b.shape[1]
    RW = rwkv_w0.shape[1]
    hd_r = rwkv_rk.shape[2]
    dl, al, gl = rwkv_w2.shape[1], rwkv_a2.shape[1], rwkv_g2.shape[1]
    off_q = PW
    off_r = PW + 3 * DW
    off_g = off_r + 3 * RW + dl + al + gl
    assert off_g + 3 * D == w_in.shape[2]
    assert Lp == L or S % CHUNK == 0
    assert max(dl, al, gl, rwkv_v1.shape[2]) <= LANES

    h = jnp.concatenate([jnp.broadcast_to(meta.astype(x.dtype)[None], (B, n_meta, D)), x,
                         jnp.zeros((B, Lp - L, D), x.dtype)], axis=1).reshape(M, D)

    hid = np.arange(RW) // hd_r
    ones_bd = jnp.asarray((hid[:, None] == hid[None, :]).astype(np.float32), dtype=BF16)

    def pad_lora(a, axis):
        segs = [lax.slice_in_dim(a, 0, 3 * RW, axis=axis)]
        o = 3 * RW
        for n in (dl, al, gl):
            seg = lax.slice_in_dim(a, o, o + n, axis=axis)
            pad = [(0, 0)] * a.ndim
            pad[axis] = (0, LANES - n)
            segs.append(jnp.pad(seg, pad))
            o += n
        return jnp.concatenate(segs, axis=axis)

    tables = {}
    for tq, _, _ in _attn_segments(Lp):
        tt = _round_up(tq, ATTN_TILE)
        if tt not in tables:
            tables[tt] = _bias_table_call(rel_bias, n_meta, tt)
        tables[tq] = tables[tt]

    tm_big = _pick_tile(M, 1088, 16)
    tm_res = _pick_tile(M, 544, 16)
    u = _rms_call(h, norm_mix_pre[0], tm_res)
    v_first = None
    for l in range(depth):
        lam_init = 0.8 - 0.6 * math.exp(-0.3 * l)
        wl = w_in[l]
        w_gate = wl[:, off_g:].astype(BF16)
        q_scale = diff_lq1.shape[1] ** -0.5
        w_qkv = jnp.concatenate([wl[:, off_q:off_q + DW] * q_scale, wl[:, off_q + DW:off_r]],
                                axis=1).astype(BF16)
        w_pool = wl[:, :PW].astype(BF16)
        w_rwkv = pad_lora(wl[:, off_r:off_g], 1).astype(BF16)
        mu_p = pad_lora(rwkv_mu[l], 0)

        zg = _mm_call(u, w_gate, BF16, tm_big, _pick_tile(3 * D, 768, LANES), "in_proj_gate")
        zqkv = _mm_call(u, w_qkv, BF16, tm_big, _pick_tile(3 * DW, 768, LANES), "in_proj_qkv")
        zp = _mm_call(u, w_pool, BF16, tm_big, _pick_tile(PW, 768, LANES), "in_proj_pool")
        zr = _mm_call(u, w_rwkv, F32, tm_big, _pick_tile(w_rwkv.shape[1], 768, LANES), "in_proj_rwkv")

        yp = _pool_call(zp, pool_w[l], pool_scale[l], B, Lp)
        yd = _attn_call(zqkv, tables, diff_lq1[l], diff_lk1[l], diff_lq2[l], diff_lk2[l],
                        diff_subln[l], B, L, Lp, lam_init)
        v_res = None if l == 0 else (rwkv_v0[l - 1], rwkv_v1[l - 1], rwkv_v2[l - 1])
        streams = _rwkv_prep_call(zr, B, Lp, RW, mu_p, rwkv_w0[l], rwkv_w2[l], rwkv_a0[l], rwkv_a2[l],
                                  rwkv_g2[l], rwkv_kk[l], rwkv_ka[l], ones_bd, v_first, v_res)
        if l == 0:
            v_first = streams[3]
        yr = _scan_call(streams, rwkv_rk[l], rwkv_lnx_w[l], rwkv_lnx_b[l], ones_bd, B, Lp, hd_r)

        merged = _merge_call(yp, yd, yr, zg, p_a[l].astype(BF16), p_b[l].astype(BF16),
                             p_c[l].astype(BF16), tm_res)
        h, u2 = _mm_res_call(merged, w_o[l].astype(BF16), h, norm_mix_post[l], norm_ffn_pre[l],
                             "out_proj_res")
        FF = ffn_down.shape[1]
        act = _ffn_up_call(u2, ffn_up[l].astype(BF16), ffn_conv_w[l], ffn_conv_b[l], B, Lp,
                           _pick_tile(FF, 512, LANES))
        g_next = norm_mix_pre[l + 1] if l + 1 < depth else norm_mix_pre[l]
        h, u = _mm_res_call(act, ffn_down[l].astype(BF16), h, norm_ffn_post[l], g_next,
                            "ffn_down_res")
    return h.reshape(B, Lp, D)[:, n_meta:n_meta + S]
```

```python
import functools
import math

import numpy as np
import jax
import jax.numpy as jnp
from jax import lax
from jax.experimental import pallas as pl
from jax.experimental.pallas import tpu as pltpu

F32 = jnp.float32
BF16 = jnp.bfloat16

CHUNK = 64
NORM_EPS = 1e-6
NEG_INF = -1e30
POOL_WINDOWS = (2, 4, 8, 16)
DIFF_SUBLN_EPS = 1e-5
REL_MAX_DIST = 128
RWKV_LNX_EPS = 64e-5
FFN_CONV = 3

LANES = 128
SUBLANES = 8
ATTN_TILE = 128
ATTN_Q_TILE = 512
ATTN_UNROLL = 3
SCAN_BATCH = 4
VMEM_LIMIT = 56 * 1024 * 1024


def _cparams(sem):
    return pltpu.CompilerParams(dimension_semantics=sem, vmem_limit_bytes=VMEM_LIMIT)


def _round_up(x, m):
    return (x + m - 1) // m * m


def _pick_tile(total, target, mult):
    best = None
    for t in range(mult, min(total, target) + 1, mult):
        if total % t == 0:
            best = t
    assert best is not None, (total, target, mult)
    return best


def _shift_down(x, d):
    row = lax.broadcasted_iota(jnp.int32, x.shape, 0)
    return jnp.where(row >= d, pltpu.roll(x, d, 0), 0.0)


def _sigmoid(x):
    return 1.0 / (1.0 + jnp.exp(-x))


def _dot(a, b):
    return jnp.dot(a.astype(BF16), b.astype(BF16), preferred_element_type=F32)


def _dot_nt(a, b):
    return lax.dot_general(a.astype(BF16), b.astype(BF16), (((1,), (1,)), ((), ())),
                           preferred_element_type=F32)


def _split3(x):
    hi = x.astype(BF16)
    r1 = x - hi.astype(F32)
    mid = r1.astype(BF16)
    lo = (r1 - mid.astype(F32)).astype(BF16)
    return hi, mid, lo


def _dot_exact_rhs(x, ones_bf16):
    hi, mid, lo = _split3(x)
    acc = jnp.dot(hi, ones_bf16, preferred_element_type=F32)
    acc = acc + jnp.dot(mid, ones_bf16, preferred_element_type=F32)
    return acc + jnp.dot(lo, ones_bf16, preferred_element_type=F32)


def _rms_kernel(x_ref, g_ref, o_ref):
    x = x_ref[...]
    y = x * lax.rsqrt(jnp.mean(x * x, axis=-1, keepdims=True) + NORM_EPS)
    o_ref[...] = (y * g_ref[...]).astype(o_ref.dtype)


def _rms_call(x, g, tm):
    M, D = x.shape
    return pl.pallas_call(
        _rms_kernel,
        grid=(M // tm,),
        in_specs=[pl.BlockSpec((tm, D), lambda i: (i, 0)),
                  pl.BlockSpec((1, D), lambda i: (0, 0))],
        out_specs=pl.BlockSpec((tm, D), lambda i: (i, 0)),
        out_shape=jax.ShapeDtypeStruct((M, D), BF16),
        compiler_params=_cparams(("parallel",)),
        name="rms_norm",
    )(x, g.reshape(1, D))


def _mm_kernel(a_ref, b_ref, o_ref):
    o_ref[...] = jnp.dot(a_ref[...], b_ref[...], preferred_element_type=F32).astype(o_ref.dtype)


def _mm_call(a, b, out_dtype, tm, tn, name):
    M, K = a.shape
    N = b.shape[1]
    return pl.pallas_call(
        _mm_kernel,
        grid=(M // tm, N // tn),
        in_specs=[pl.BlockSpec((tm, K), lambda i, j: (i, 0)),
                  pl.BlockSpec((K, tn), lambda i, j: (0, j))],
        out_specs=pl.BlockSpec((tm, tn), lambda i, j: (i, j)),
        out_shape=jax.ShapeDtypeStruct((M, N), out_dtype),
        compiler_params=_cparams(("parallel", "arbitrary")),
        name=name,
    )(a, b)


def _mm_res_kernel(a_ref, w_ref, h_ref, gp_ref, gn_ref, oh_ref, ou_ref):
    f = jnp.dot(a_ref[...], w_ref[...], preferred_element_type=F32)
    y = f * lax.rsqrt(jnp.mean(f * f, axis=-1, keepdims=True) + NORM_EPS) * gp_ref[...]
    hn = h_ref[...] + y
    oh_ref[...] = hn
    u = hn * lax.rsqrt(jnp.mean(hn * hn, axis=-1, keepdims=True) + NORM_EPS) * gn_ref[...]
    ou_ref[...] = u.astype(ou_ref.dtype)


def _mm_res_tile(M, K, D):
    budget = VMEM_LIMIT - 8 * 1024 * 1024
    best = None
    for tm in range(16, M + 1, 16):
        if M % tm:
            continue
        need = (K * D * 2
                + 2 * tm * K * 2
                + 2 * 2 * tm * D * 4
                + 2 * tm * D * 2
                + 2 * tm * D * 4)
        if need <= budget:
            best = tm
    assert best is not None
    return best


def _mm_res_call(a, w, h, g_post, g_next, name):
    M, K = a.shape
    D = w.shape[1]
    tm = _mm_res_tile(M, K, D)
    return pl.pallas_call(
        _mm_res_kernel,
        grid=(M // tm,),
        in_specs=[pl.BlockSpec((tm, K), lambda i: (i, 0)),
                  pl.BlockSpec((K, D), lambda i: (0, 0), pipeline_mode=pl.Buffered(1)),
                  pl.BlockSpec((tm, D), lambda i: (i, 0)),
                  pl.BlockSpec((1, D), lambda i: (0, 0)),
                  pl.BlockSpec((1, D), lambda i: (0, 0))],
        out_specs=[pl.BlockSpec((tm, D), lambda i: (i, 0)),
                   pl.BlockSpec((tm, D), lambda i: (i, 0))],
        out_shape=[jax.ShapeDtypeStruct((M, D), F32),
                   jax.ShapeDtypeStruct((M, D), BF16)],
        compiler_params=_cparams(("parallel",)),
        name=name,
    )(a, w, h, g_post.reshape(1, D), g_next.reshape(1, D))


def _pool_kernel(z_ref, w_ref, s_ref, o_ref, *, gw):
    L = z_ref.shape[1]
    row = lax.broadcasted_iota(jnp.int32, (L, gw), 0)
    for gi, win in enumerate(POOL_WINDOWS):
        sl = slice(gi * gw, (gi + 1) * gw)
        zg = z_ref[0, :, sl].astype(F32)
        s = zg
        d = 1
        while d < win:
            s = s + _shift_down(s, d)
            d *= 2
        cnt = jnp.minimum(row + 1, win).astype(F32)
        pooled = s / cnt - zg
        y = _dot(pooled, w_ref[gi]) * s_ref[:, sl]
        o_ref[0, :, sl] = y.astype(o_ref.dtype)


def _pool_call(z, w, scale, B, Lp):
    PW = z.shape[-1]
    G = len(POOL_WINDOWS)
    gw = PW // G
    z3 = z.reshape(B, Lp, PW)
    out = pl.pallas_call(
        functools.partial(_pool_kernel, gw=gw),
        grid=(B,),
        in_specs=[pl.BlockSpec((1, Lp, PW), lambda b: (b, 0, 0)),
                  pl.BlockSpec((G, gw, gw), lambda b: (0, 0, 0)),
                  pl.BlockSpec((1, PW), lambda b: (0, 0))],
        out_specs=pl.BlockSpec((1, Lp, PW), lambda b: (b, 0, 0)),
        out_shape=jax.ShapeDtypeStruct((B, Lp, PW), BF16),
        compiler_params=_cparams(("parallel",)),
        name="pool_mixer",
    )(z3, w.astype(BF16), scale.reshape(1, PW))
    return out.reshape(B * Lp, PW)


def _rel_bucket(rel, n_buckets):
    half = n_buckets // 2
    max_exact = half // 2
    n = np.abs(rel)
    large = max_exact + (np.log(np.maximum(n, 1) / max_exact) / math.log(REL_MAX_DIST / max_exact)
                         * (half - max_exact)).astype(np.int32)
    large = np.minimum(large, half - 1)
    return ((rel > 0) * half + np.where(n < max_exact, n, large)).astype(np.int32)


def _bias_table_kernel(rb_ref, bucket_ref, pen_ref, o_ref, *, n_buckets):
    s = pl.program_id(0)
    bucket = bucket_ref[...]
    acc = pen_ref[...]
    for b in range(n_buckets):
        acc = acc + jnp.where(bucket == b, rb_ref[b, s], 0.0)
    o_ref[0] = acc


def _bias_table_call(rel_bias, n_meta, tq):
    n_buckets, n_sub = rel_bias.shape
    KT = ATTN_TILE
    assert tq % KT == 0 and KT % CHUNK == 0 and n_meta <= KT
    n_near = tq // KT + 2
    n_slabs = n_near + 2
    far = n_buckets // 2 - 1
    r = np.arange(tq)[:, None]
    c = np.arange(KT)[None, :]
    assert np.all(_rel_bucket(-(KT + 1 + np.arange(4 * KT)), n_buckets) == far)
    bucket = np.full((n_slabs, tq, KT), far, np.int32)
    pen = np.zeros((n_slabs, tq, KT), np.float32)
    pen[-1] = NEG_INF
    for idx in range(n_near):
        key = (idx - 1) * KT + c
        bucket[1 + idx] = _rel_bucket(key - r, n_buckets)
        visible = np.floor_divide(key - n_meta, CHUNK) <= np.floor_divide(r - n_meta, CHUNK)
        pen[1 + idx] = np.where(visible, 0.0, NEG_INF)
    blk = pl.BlockSpec((n_slabs, tq, KT), lambda s: (0, 0, 0))
    return pl.pallas_call(
        functools.partial(_bias_table_kernel, n_buckets=n_buckets),
        grid=(n_sub,),
        in_specs=[pl.BlockSpec(memory_space=pltpu.SMEM), blk, blk],
        out_specs=pl.BlockSpec((1, n_slabs, tq, KT), lambda s: (s, 0, 0, 0)),
        out_shape=jax.ShapeDtypeStruct((n_sub, n_slabs, tq, KT), F32),
        compiler_params=_cparams(("arbitrary",)),
        name="bias_table_%d" % tq,
    )(rel_bias, jnp.asarray(bucket), jnp.asarray(pen))


def _attn_kernel(*refs, hd, lam_init, nt, tile0, live):
    (lq1_ref, lk1_ref, lq2_ref, lk2_ref, sub_ref, q_ref, k_ref, v_ref, w_ref,
     o_ref, s_sc, mx_sc, l_sc, acc_sc, k_sc, v_sc) = refs
    KT = ATTN_TILE
    tq = q_ref.shape[1]
    ratio = -(-tq // KT)
    Lk = k_ref.shape[1]

    @pl.when(pl.program_id(2) == 0)
    def _():
        k_sc[0:Lk, :] = k_ref[0]
        v_sc[0:Lk, :] = v_ref[0]
        if k_sc.shape[0] > Lk:
            k_sc[Lk:, :] = jnp.zeros((k_sc.shape[0] - Lk, k_sc.shape[1]), k_sc.dtype)
            v_sc[Lk:, :] = jnp.zeros((v_sc.shape[0] - Lk, v_sc.shape[1]), v_sc.dtype)
    n_slabs = w_ref.shape[1]
    i = pl.program_id(2)
    jt0 = tile0 + i * ratio
    n_need = jnp.minimum(jt0 + ratio + 1, nt)
    n_pairs = (n_need + 1) // 2
    q = q_ref[0, 0:live, :]

    def slab(j):
        return jnp.where(j < nt, jnp.clip(j - jt0 + 2, 0, n_slabs - 1), n_slabs - 1)

    def two_tiles(ref, jj):
        j0 = 2 * jj
        j1 = jnp.minimum(j0 + 1, nt - 1)
        t0 = ref[pl.ds(pl.multiple_of(j0 * KT, KT), KT), :]
        t1 = ref[pl.ds(pl.multiple_of(j1 * KT, KT), KT), :]
        return jnp.concatenate([t0, t1], axis=0)

    mx_sc[...] = jnp.full_like(mx_sc, NEG_INF)

    def unrolled(step):
        def group(t, carry):
            step([ATTN_UNROLL * t + r for r in range(ATTN_UNROLL)])
            return carry

        def single(jj, carry):
            step([jj])
            return carry
        n_groups = n_pairs // ATTN_UNROLL
        lax.fori_loop(0, n_groups, group, 0)
        lax.fori_loop(n_groups * ATTN_UNROLL, n_pairs, single, 0)

    def pass1(jjs):
        kts = [two_tiles(k_sc, jj) for jj in jjs]
        for s in range(2):
            mx = None
            for jj, kt in zip(jjs, kts):
                sc = _dot_nt(q[:, s * hd:(s + 1) * hd], kt[:, s * hd:(s + 1) * hd])
                sc = sc + jnp.concatenate([w_ref[s, slab(2 * jj), 0:live, :],
                                           w_ref[s, slab(2 * jj + 1), 0:live, :]], axis=1)
                s_sc[s, jj] = sc
                half = jnp.maximum(sc[:, :KT], sc[:, KT:])
                mx = half if mx is None else jnp.maximum(mx, half)
            mx_sc[s] = jnp.maximum(mx_sc[s], mx)

    unrolled(pass1)
    m = [jnp.max(mx_sc[s], axis=-1, keepdims=True) for s in range(2)]
    l_sc[...] = jnp.zeros_like(l_sc)
    acc_sc[...] = jnp.zeros_like(acc_sc)

    def pass2(jjs):
        for jj in jjs:
            vt = two_tiles(v_sc, jj)
            for s in range(2):
                p = jnp.exp(s_sc[s, jj] - m[s])
                l_sc[s] += p[:, :KT] + p[:, KT:]
                acc_sc[s] += jnp.dot(p.astype(BF16), vt, preferred_element_type=F32)

    unrolled(pass2)
    l = [jnp.sum(l_sc[s], axis=-1, keepdims=True) for s in range(2)]
    lam = (jnp.exp(jnp.sum(lq1_ref[...] * lk1_ref[...], axis=-1, keepdims=True))
           - jnp.exp(jnp.sum(lq2_ref[...] * lk2_ref[...], axis=-1, keepdims=True)) + lam_init)
    o = acc_sc[0] / l[0] - lam * (acc_sc[1] / l[1])
    y = o * lax.rsqrt(jnp.mean(o * o, axis=-1, keepdims=True) + DIFF_SUBLN_EPS)
    o_ref[0, 0:live, :] = (y * sub_ref[...] * (1.0 - lam_init)).astype(o_ref.dtype)
    if live < tq:
        o_ref[0, live:tq, :] = jnp.zeros((tq - live, o_ref.shape[2]), o_ref.dtype)


def _attn_segment(qkv3, table, params, lam_init, tq, row0, n_tiles, live):
    B, Lp, W3 = qkv3.shape
    DW = W3 // 3
    hd = params[0].shape[1]
    hw = 2 * hd
    n_heads = DW // hw
    KT = ATTN_TILE
    nt = -(-Lp // KT)
    blk0 = row0 // tq
    kern = functools.partial(_attn_kernel, hd=hd, lam_init=lam_init, nt=nt,
                             tile0=row0 // KT, live=live)
    par = lambda n: pl.BlockSpec((1, n), lambda b, h, i: (0, 0))
    in_specs = [par(hd)] * 4 + [
        par(hw),
        pl.BlockSpec((1, tq, hw), lambda b, h, i: (b, blk0 + i, h)),
        pl.BlockSpec((1, Lp, hw), lambda b, h, i: (b, 0, n_heads + h)),
        pl.BlockSpec((1, Lp, hw), lambda b, h, i: (b, 0, 2 * n_heads + h)),
        pl.BlockSpec((2, table.shape[1], tq, KT), lambda b, h, i: (h, 0, 0, 0))]
    args = list(params) + [qkv3, qkv3, qkv3, table]
    max_pairs = (nt + 1) // 2
    return pl.pallas_call(
        kern,
        grid=(B, n_heads, n_tiles),
        in_specs=in_specs,
        out_specs=pl.BlockSpec((1, tq, hw), lambda b, h, i: (b, i, h)),
        out_shape=jax.ShapeDtypeStruct((B, n_tiles * tq, DW), BF16),
        scratch_shapes=[pltpu.VMEM((2, max_pairs, live, 2 * KT), F32), pltpu.VMEM((2, live, KT), F32),
                        pltpu.VMEM((2, live, KT), F32), pltpu.VMEM((2, live, hw), F32),
                        pltpu.VMEM((nt * KT, hw), BF16), pltpu.VMEM((nt * KT, hw), BF16)],
        compiler_params=_cparams(("parallel", "parallel", "arbitrary")),
        name="diff_attention_%d" % tq,
    )(*args)


def _attn_segments(Lp):
    big = ATTN_Q_TILE
    n_big = Lp // big
    segs = []
    if n_big:
        segs.append((big, 0, n_big))
    row0 = n_big * big
    n_mid = (Lp - row0) // ATTN_TILE
    if n_mid:
        segs.append((ATTN_TILE, row0, n_mid))
        row0 += n_mid * ATTN_TILE
    if Lp > row0:
        segs.append((Lp - row0, row0, 1))
    return segs


def _attn_call(zqkv, tables, lq1, lk1, lq2, lk2, subln, B, L, Lp, lam_init):
    vec = lambda a: a.reshape(1, -1)
    params = [vec(lq1), vec(lk1), vec(lq2), vec(lk2), vec(subln)]
    qkv3 = zqkv.reshape(B, Lp, zqkv.shape[-1])
    outs = []
    for tq, row0, n_tiles in _attn_segments(Lp):
        live = min(tq, _round_up(L - row0, 2 * SUBLANES)) if n_tiles == 1 else tq
        outs.append(_attn_segment(qkv3, tables[tq], params, lam_init, tq, row0, n_tiles, live))
    out = outs[0] if len(outs) == 1 else jnp.concatenate(outs, axis=1)
    return out.reshape(B * Lp, -1)


def _rwkv_prep_kernel(*refs, rw, has_vres):
    if has_vres:
        (u_ref, up_ref, w_ref, mu_ref, w0_ref, w2_ref, a0_ref, a2_ref, g2_ref, kk_ref, ka_ref, ones_ref,
         vf_ref, v0_ref, v1_ref, v2_ref,
         r_o, lw_o, k_o, v_o, a_o, b_o, g_o) = refs
    else:
        (u_ref, up_ref, w_ref, mu_ref, w0_ref, w2_ref, a0_ref, a2_ref, g2_ref, kk_ref, ka_ref, ones_ref,
         r_o, lw_o, k_o, v_o, a_o, b_o, g_o) = refs
    i = pl.program_id(1)
    halo = up_ref.shape[1]
    zz = jnp.dot(jnp.concatenate([up_ref[0], u_ref[0]], axis=0), w_ref[...], preferred_element_type=F32)
    z = zz[halo:]
    last_prev = jnp.where(i > 0, zz[halo - 1:halo, :], 0.0)
    row = lax.broadcasted_iota(jnp.int32, z.shape, 0)
    prev = jnp.where(row >= 1, pltpu.roll(z, 1, 0), last_prev)
    zs = z + (prev - z) * mu_ref[...]
    r = zs[:, 0:rw]
    kx = zs[:, rw:2 * rw]
    vx = zs[:, 2 * rw:3 * rw]
    wd = zs[:, 3 * rw:3 * rw + LANES]
    ad = zs[:, 3 * rw + LANES:3 * rw + 2 * LANES]
    gd = zs[:, 3 * rw + 2 * LANES:3 * rw + 3 * LANES]
    wl = w0_ref[...] + _dot(jnp.tanh(wd), w2_ref[...])
    sp = jnp.maximum(-wl, 0.0) + jnp.log(1.0 + jnp.exp(-jnp.abs(wl)))
    lw = -jnp.exp(-sp - 0.5)
    if has_vres:
        gate_v = _sigmoid(v0_ref[...] + _dot(_dot(vx, v1_ref[...]), v2_ref[...]))
        vx = vx + (vf_ref[0] - vx) * gate_v
    a = _sigmoid(a0_ref[...] + _dot(ad, a2_ref[...]))
    g = _dot(_sigmoid(gd), g2_ref[...])
    kk = kx * kk_ref[...]
    ss = _dot_exact_rhs(kk * kk, ones_ref[...])
    kk = kk * lax.rsqrt(jnp.maximum(ss, 1e-24))
    kmod = kx * (1.0 + (a - 1.0) * ka_ref[...])
    r_o[0] = r
    lw_o[0] = lw
    k_o[0] = kmod
    v_o[0] = vx
    a_o[0] = -kk
    b_o[0] = kk * a
    g_o[0] = g


def _pad_rows(w, n):
    return jnp.pad(w, ((0, n - w.shape[0]), (0, 0)))


def _rwkv_prep_call(u, w_rwkv, B, Lp, rw, mu_p, w0, w2, a0, a2, g2, k_k, k_a, ones_bd, v_first, v_res):
    D = u.shape[-1]
    u3 = u.reshape(B, Lp, D)
    HALO = 2 * SUBLANES
    tl = _pick_tile(Lp, 384, HALO)
    nb = tl // HALO
    has_vres = v_res is not None
    vec = lambda a: a.reshape(1, -1)
    full = lambda a: pl.BlockSpec(a.shape, lambda b, i: (0,) * a.ndim)
    w2p = _pad_rows(w2, LANES).astype(BF16)
    a2p = _pad_rows(a2, LANES).astype(BF16)
    g2p = _pad_rows(g2, LANES).astype(BF16)
    args = [u3, u3, w_rwkv, vec(mu_p), vec(w0), w2p, vec(a0), a2p, g2p, vec(k_k), vec(k_a), ones_bd]
    in_specs = [pl.BlockSpec((1, tl, D), lambda b, i: (b, i, 0)),
                pl.BlockSpec((1, HALO, D), lambda b, i: (b, jnp.maximum(i * nb - 1, 0), 0)),
                pl.BlockSpec(w_rwkv.shape, lambda b, i: (0, 0), pipeline_mode=pl.Buffered(1))]
    in_specs += [full(a) for a in args[3:]]
    if has_vres:
        v0, v1, v2 = v_res
        v1p = jnp.pad(v1, ((0, 0), (0, LANES - v1.shape[1]))).astype(BF16)
        v2p = _pad_rows(v2, LANES).astype(BF16)
        extra = [v_first, vec(v0), v1p, v2p]
        args += extra
        in_specs += [pl.BlockSpec((1, tl, rw), lambda b, i: (b, i, 0))] + [full(a) for a in extra[1:]]
    o_spec = pl.BlockSpec((1, tl, rw), lambda b, i: (b, i, 0))
    o_shape = jax.ShapeDtypeStruct((B, Lp, rw), F32)
    return pl.pallas_call(
        functools.partial(_rwkv_prep_kernel, rw=rw, has_vres=has_vres),
        grid=(B, Lp // tl),
        in_specs=in_specs,
        out_specs=[o_spec] * 7,
        out_shape=[o_shape] * 7,
        compiler_params=_cparams(("parallel", "arbitrary")),
        name="rwkv_prep",
    )(*args)


def _scan_kernel(r_ref, lw_ref, k_ref, v_ref, a_ref, b_ref, g_ref, rk_ref, lnw_ref, lnb_ref, ones_ref,
                 o_ref, h_sc, *, hd):
    nb = r_ref.shape[0]
    C = r_ref.shape[1]
    C2 = 2 * C
    WB = r_ref.shape[2]
    W = nb * WB
    c = pl.program_id(1)

    @pl.when(c == 0)
    def _():
        h_sc[...] = jnp.zeros_like(h_sc)

    side_by_side = lambda ref: jnp.concatenate([ref[i] for i in range(nb)], axis=1)
    r = side_by_side(r_ref)
    lw = side_by_side(lw_ref)
    k = side_by_side(k_ref)
    v = side_by_side(v_ref)
    a = side_by_side(a_ref)
    b = side_by_side(b_ref)

    cum = lw
    d = 1
    while d < C:
        cum = cum + _shift_down(cum, d)
        d *= 2
    tot = cum[C - 1:C, :]
    e_end = jnp.exp(tot - cum)
    e_neg = jnp.exp(-cum)
    at = a * jnp.exp(cum - lw)
    rt = r * jnp.exp(cum)
    bt = b * e_neg
    kt = k * e_neg
    bb = b * e_end
    kb = k * e_end
    p_c = jnp.exp(tot)

    lane = lax.broadcasted_iota(jnp.int32, (C, LANES), 1)
    first = lane < hd
    rr = lax.broadcasted_iota(jnp.int32, (C2, C2), 0)
    cc = lax.broadcasted_iota(jnp.int32, (C2, C2), 1)
    tr = jnp.where(rr >= C, rr - C, rr)
    tc = jnp.where(cc >= C, cc - C, cc)
    strict = tr > tc
    lower = tr >= tc
    eye = rr == cc

    def stack(x):
        return jnp.concatenate([jnp.where(first, x, 0.0), jnp.where(first, 0.0, x)], axis=0)

    pairs = range(W // LANES)
    stacked = lambda x: [stack(x[:, p * LANES:(p + 1) * LANES]) for p in pairs]
    s_a, s_r, s_b, s_k = stacked(at), stacked(rt), stacked(bt), stacked(kt)
    s_bb, s_kb, s_v = stacked(bb), stacked(kb), stacked(v)
    sc = [_dot_nt(jnp.concatenate([s_a[p], s_r[p]], axis=0), jnp.concatenate([s_b[p], s_k[p]], axis=0))
          for p in pairs]
    n = [jnp.where(strict, sc[p][:C2, :C2], 0.0) for p in pairs]
    a_ak = [jnp.where(strict, sc[p][:C2, C2:], 0.0) for p in pairs]
    m_rb = [jnp.where(lower, sc[p][C2:, :C2], 0.0) for p in pairs]
    m_rk = [jnp.where(lower, sc[p][C2:, C2:], 0.0) for p in pairs]
    xv = [_dot(jnp.concatenate([a_ak[p], m_rk[p], s_kb[p].T], axis=0), s_v[p]) for p in pairs]
    t_inv = [jnp.where(eye, 1.0, 0.0) + n[p] for p in pairs]
    pw = [_dot(n[p], n[p]) for p in pairs]
    step = 4
    while step < C:
        tp = [_dot(jnp.concatenate([t_inv[p], pw[p]], axis=0), pw[p]) for p in pairs]
        t_inv = [t_inv[p] + tp[p][:C2] for p in pairs]
        pw = [tp[p][C2:] for p in pairs]
        step *= 2
    t_inv = [t_inv[p] + _dot(t_inv[p], pw[p]) for p in pairs]
    au = [_dot(t_inv[p], jnp.concatenate([s_a[p], xv[p][:C2]], axis=1)) for p in pairs]
    mbu = [_dot(jnp.concatenate([m_rb[p], s_bb[p].T], axis=0), au[p]) for p in pairs]
    ys = []
    for p in pairs:
        r_hat = s_r[p] + mbu[p][:C2, :LANES]
        y0 = mbu[p][:C2, LANES:] + xv[p][C2:2 * C2]
        g_mat = jnp.where(eye, p_c[:, p * LANES:(p + 1) * LANES], 0.0) + mbu[p][C2:, :LANES]
        h_add = mbu[p][C2:, LANES:] + xv[p][2 * C2:]
        yh = _dot(jnp.concatenate([r_hat, g_mat], axis=0), h_sc[p])
        y = yh[:C2] + y0
        h_sc[p] = yh[C2:] + h_add
        ys.append(y[0:C] + y[C:C2])
    ones = ones_ref[...]
    inv_hd = 1.0 / hd
    ppb = WB // LANES
    for i in range(nb):
        y = jnp.concatenate(ys[i * ppb:(i + 1) * ppb], axis=1)
        mean = _dot_exact_rhs(y, ones) * inv_hd
        yc = y - mean
        var = _dot_exact_rhs(yc * yc, ones) * inv_hd
        yn = yc * lax.rsqrt(var + RWKV_LNX_EPS) * lnw_ref[...] + lnb_ref[...]
        bonus = _dot_exact_rhs(r_ref[i] * k_ref[i] * rk_ref[...], ones) * v_ref[i]
        o_ref[i] = ((yn + bonus) * g_ref[i]).astype(o_ref.dtype)


def _scan_call(streams, r_k, lnx_w, lnx_b, ones_bd, B, Lp, hd):
    r, lw, k, v, a, b, g = streams
    rw = r.shape[-1]
    C = CHUNK
    assert 2 * hd == LANES and rw % LANES == 0 and Lp % C == 0
    vec = lambda x: x.reshape(1, -1)
    nb = SCAN_BATCH if B % SCAN_BATCH == 0 else 1
    blk = pl.BlockSpec((nb, C, rw), lambda bi, c: (bi, c, 0))
    par = pl.BlockSpec((1, rw), lambda bi, c: (0, 0))
    out = pl.pallas_call(
        functools.partial(_scan_kernel, hd=hd),
        grid=(B // nb, Lp // C),
        in_specs=[blk] * 7 + [par] * 3 + [pl.BlockSpec((rw, rw), lambda bi, c: (0, 0))],
        out_specs=blk,
        out_shape=jax.ShapeDtypeStruct((B, Lp, rw), BF16),
        scratch_shapes=[pltpu.VMEM((nb * rw // LANES, LANES, LANES), F32)],
        compiler_params=_cparams(("parallel", "arbitrary")),
        name="rwkv_scan",
    )(r, lw, k, v, a, b, g, vec(r_k), vec(lnx_w), vec(lnx_b), ones_bd)
    return out.reshape(B * Lp, rw)


def _merge_kernel(yp_ref, yd_ref, yr_ref, g0_ref, g1_ref, g2_ref, pa_ref, pb_ref, pc_ref, o_ref):
    dot = lambda x, w: jnp.dot(x[...], w[...], preferred_element_type=F32)
    m = _sigmoid(g0_ref[...].astype(F32)) * dot(yp_ref, pa_ref)
    m = m + _sigmoid(g1_ref[...].astype(F32)) * dot(yd_ref, pb_ref)
    m = m + _sigmoid(g2_ref[...].astype(F32)) * dot(yr_ref, pc_ref)
    o_ref[...] = m.astype(o_ref.dtype)


def _merge_call(yp, yd, yr, zg, pa, pb, pc, tm):
    M = yp.shape[0]
    D = pa.shape[1]
    rows = lambda a: pl.BlockSpec((tm, a.shape[1]), lambda i: (i, 0))
    full = lambda a: pl.BlockSpec(a.shape, lambda i: (0, 0))
    gate = lambda n: pl.BlockSpec((tm, D), lambda i: (i, n))
    return pl.pallas_call(
        _merge_kernel,
        grid=(M // tm,),
        in_specs=[rows(yp), rows(yd), rows(yr), gate(0), gate(1), gate(2), full(pa), full(pb), full(pc)],
        out_specs=pl.BlockSpec((tm, D), lambda i: (i, 0)),
        out_shape=jax.ShapeDtypeStruct((M, D), BF16),
        compiler_params=_cparams(("parallel",)),
        name="gated_merge",
    )(yp, yd, yr, zg, zg, zg, pa, pb, pc)


def _ffn_up_kernel(u_ref, wv_ref, wg_ref, cwv_ref, cwg_ref, cbv_ref, cbg_ref, o_ref, *scratch, n_sub):
    xg_scs, xv_scs, a_scs = scratch[:n_sub], scratch[n_sub:2 * n_sub], scratch[2 * n_sub:]
    L = u_ref.shape[0]
    mb = L // n_sub
    st = mb // SUBLANES
    PAD = SUBLANES
    n_slab = a_scs[0].shape[0]
    k_in = math.sqrt(2.0 / math.pi)
    zeros = jnp.zeros((PAD, LANES), F32)
    for s in range(n_slab):
        xv_scs[0][s, 0:PAD, :] = zeros
        xg_scs[0][s, 0:PAD, :] = zeros

    def taps(cw_ref, cb_ref, s):
        sl = slice(s * LANES, (s + 1) * LANES)
        rows = [jnp.broadcast_to(cw_ref[j:j + 1, sl], (SUBLANES, LANES)) for j in range(FFN_CONV)]
        return rows, jnp.broadcast_to(cb_ref[:, sl], (SUBLANES, LANES))

    def tail(sb):
        lo = sb * mb
        xg_sc, xv_sc, a_sc = xg_scs[sb], xv_scs[sb], a_scs[sb]
        for s in range(n_slab):
            wg, bg = taps(cwg_ref, cbg_ref, s)
            wv, bv = taps(cwv_ref, cbv_ref, s)
            ld = lambda ref, j: ref[s, pl.ds(PAD + j, SUBLANES, stride=st), :]
            xg = [ld(xg_sc, j) for j in range(1 - FFN_CONV, 0)]
            xv = [ld(xv_sc, j) for j in range(1 - FFN_CONV, 0)]
            for j in range(st):
                xg.append(ld(xg_sc, j))
                xv.append(ld(xv_sc, j))
                gate = bg
                val = bv
                for t in range(FFN_CONV):
                    gate = gate + xg[j + t] * wg[t]
                    val = val + xv[j + t] * wv[t]
                inner = gate * (2.0 * k_in + (2.0 * k_in * 0.044715) * (gate * gate))
                act = gate / (1.0 + jnp.exp(-inner)) * val
                a_sc[s, pl.ds(j, SUBLANES, stride=st), :] = act
        for s in range(n_slab):
            o_ref[lo:lo + mb, s * LANES:(s + 1) * LANES] = a_sc[s].astype(o_ref.dtype)

    for sb in range(n_sub):
        lo = sb * mb
        u = u_ref[lo:lo + mb, :]
        for x_scs, w_ref in ((xg_scs, wg_ref), (xv_scs, wv_ref)):
            x = jnp.dot(u, w_ref[...], preferred_element_type=F32)
            for s in range(n_slab):
                xs = x[:, s * LANES:(s + 1) * LANES]
                x_scs[sb][s, PAD:PAD + mb, :] = xs
                if sb + 1 < n_sub:
                    x_scs[sb + 1][s, 0:PAD, :] = xs[mb - PAD:mb]
        if sb:
            tail(sb - 1)
    tail(n_sub - 1)


def _ffn_up_call(u, w_up, conv_w, conv_b, B, Lp, tn):
    M, D = u.shape
    FF = w_up.shape[1] // 2
    nt = FF // tn
    n_sub = 4 if Lp % (4 * 2 * SUBLANES) == 0 else 1
    mb = Lp // n_sub
    return pl.pallas_call(
        functools.partial(_ffn_up_kernel, n_sub=n_sub),
        grid=(B, nt),
        in_specs=[pl.BlockSpec((Lp, D), lambda b, j: (b, 0)),
                  pl.BlockSpec((D, tn), lambda b, j: (0, j)),
                  pl.BlockSpec((D, tn), lambda b, j: (0, nt + j)),
                  pl.BlockSpec((FFN_CONV, tn), lambda b, j: (0, j)),
                  pl.BlockSpec((FFN_CONV, tn), lambda b, j: (0, nt + j)),
                  pl.BlockSpec((1, tn), lambda b, j: (0, j)),
                  pl.BlockSpec((1, tn), lambda b, j: (0, nt + j))],
        out_specs=pl.BlockSpec((Lp, tn), lambda b, j: (b, j)),
        out_shape=jax.ShapeDtypeStruct((M, FF), BF16),
        scratch_shapes=([pltpu.VMEM((tn // LANES, SUBLANES + mb, LANES), F32)] * (2 * n_sub)
                        + [pltpu.VMEM((tn // LANES, mb, LANES), F32)] * n_sub),
        compiler_params=_cparams(("parallel", "arbitrary")),
        name="ffn_up_conv_geglu",
    )(u, w_up, w_up, conv_w, conv_w, conv_b.reshape(1, -1), conv_b.reshape(1, -1))


def kernel(x, meta, rel_bias, norm_mix_pre, norm_mix_post, norm_ffn_pre, norm_ffn_post, w_in, pool_w, pool_scale, diff_lq1, diff_lk1, diff_lq2, diff_lk2, diff_subln, rwkv_mu, rwkv_w0, rwkv_w2, rwkv_a0, rwkv_a2, rwkv_g2, rwkv_kk, rwkv_ka, rwkv_rk, rwkv_lnx_w, rwkv_lnx_b, rwkv_v0, rwkv_v1, rwkv_v2, p_a, p_b, p_c, w_o, ffn_up, ffn_conv_w, ffn_conv_b, ffn_down):
    B, S, D = x.shape
    n_meta = meta.shape[0]
    depth = w_in.shape[0]
    L = S + n_meta
    Lp = _round_up(L, CHUNK)
    M = B * Lp

    PW = pool_scale.shape[1]
    DW = p_b.shape[1]
    RW = rwkv_w0.shape[1]
    hd_r = rwkv_rk.shape[2]
    dl, al, gl = rwkv_w2.shape[1], rwkv_a2.shape[1], rwkv_g2.shape[1]
    off_q = PW
    off_r = PW + 3 * DW
    off_g = off_r + 3 * RW + dl + al + gl
    assert off_g + 3 * D == w_in.shape[2]
    assert Lp == L or S % CHUNK == 0
    assert max(dl, al, gl, rwkv_v1.shape[2]) <= LANES

    h = jnp.concatenate([jnp.broadcast_to(meta.astype(x.dtype)[None], (B, n_meta, D)), x,
                         jnp.zeros((B, Lp - L, D), x.dtype)], axis=1).reshape(M, D)

    hid = np.arange(RW) // hd_r
    ones_bd = jnp.asarray((hid[:, None] == hid[None, :]).astype(np.float32), dtype=BF16)

    def pad_lora(a, axis):
        segs = [lax.slice_in_dim(a, 0, 3 * RW, axis=axis)]
        o = 3 * RW
        for n in (dl, al, gl):
            seg = lax.slice_in_dim(a, o, o + n, axis=axis)
            pad = [(0, 0)] * a.ndim
            pad[axis] = (0, LANES - n)
            segs.append(jnp.pad(seg, pad))
            o += n
        return jnp.concatenate(segs, axis=axis)

    tables = {}
    for tq, _, _ in _attn_segments(Lp):
        tt = _round_up(tq, ATTN_TILE)
        if tt not in tables:
            tables[tt] = _bias_table_call(rel_bias, n_meta, tt)
        tables[tq] = tables[tt]

    tm_big = _pick_tile(M, 1088, 16)
    tm_res = _pick_tile(M, 544, 16)
    u = _rms_call(h, norm_mix_pre[0], tm_res)
    v_first = None
    for l in range(depth):
        lam_init = 0.8 - 0.6 * math.exp(-0.3 * l)
        wl = w_in[l]
        w_gate = wl[:, off_g:].astype(BF16)
        q_scale = diff_lq1.shape[1] ** -0.5
        col_scale = np.concatenate([np.full(DW, q_scale, np.float32), np.ones(2 * DW, np.float32)])
        w_qkv = (wl[:, off_q:off_r] * col_scale[None, :]).astype(BF16)
        w_pool = wl[:, :PW].astype(BF16)
        w_rwkv = pad_lora(wl[:, off_r:off_g], 1).astype(BF16)
        mu_p = pad_lora(rwkv_mu[l], 0)

        zg = _mm_call(u, w_gate, BF16, tm_big, _pick_tile(3 * D, 768, LANES), "in_proj_gate")
        zqkv = _mm_call(u, w_qkv, BF16, tm_big, _pick_tile(3 * DW, 768, LANES), "in_proj_qkv")
        zp = _mm_call(u, w_pool, BF16, tm_big, _pick_tile(PW, 768, LANES), "in_proj_pool")

        yp = _pool_call(zp, pool_w[l], pool_scale[l], B, Lp)
        yd = _attn_call(zqkv, tables, diff_lq1[l], diff_lk1[l], diff_lq2[l], diff_lk2[l],
                        diff_subln[l], B, L, Lp, lam_init)
        v_res = None if l == 0 else (rwkv_v0[l - 1], rwkv_v1[l - 1], rwkv_v2[l - 1])
        streams = _rwkv_prep_call(u, w_rwkv, B, Lp, RW, mu_p, rwkv_w0[l], rwkv_w2[l], rwkv_a0[l], rwkv_a2[l],
                                  rwkv_g2[l], rwkv_kk[l], rwkv_ka[l], ones_bd, v_first, v_res)
        if l == 0:
            v_first = streams[3]
        yr = _scan_call(streams, rwkv_rk[l], rwkv_lnx_w[l], rwkv_lnx_b[l], ones_bd, B, Lp, hd_r)

        merged = _merge_call(yp, yd, yr, zg, p_a[l].astype(BF16), p_b[l].astype(BF16),
                             p_c[l].astype(BF16), tm_res)
        h, u2 = _mm_res_call(merged, w_o[l].astype(BF16), h, norm_mix_post[l], norm_ffn_pre[l],
                             "out_proj_res")
        FF = ffn_down.shape[1]
        act = _ffn_up_call(u2, ffn_up[l].astype(BF16), ffn_conv_w[l], ffn_conv_b[l], B, Lp,
                           _pick_tile(FF, 512, LANES))
        g_next = norm_mix_pre[l + 1] if l + 1 < depth else norm_mix_pre[l]
        h, u = _mm_res_call(act, ffn_down[l].astype(BF16), h, norm_ffn_post[l], g_next,
                            "ffn_down_res")
    return h.reshape(B, Lp, D)[:, n_meta:n_meta + S]
```

```python
import functools
import math

import numpy as np
import jax
import jax.numpy as jnp
from jax import lax
from jax.experimental import pallas as pl
from jax.experimental.pallas import tpu as pltpu

F32 = jnp.float32
BF16 = jnp.bfloat16

CHUNK = 64
NORM_EPS = 1e-6
NEG_INF = -1e30
POOL_WINDOWS = (2, 4, 8, 16)
DIFF_SUBLN_EPS = 1e-5
REL_MAX_DIST = 128
RWKV_LNX_EPS = 64e-5
FFN_CONV = 3

LANES = 128
SUBLANES = 8
ATTN_TILE = 128
ATTN_Q_TILE = 512
ATTN_UNROLL = 3
SCAN_BATCH = 4
VMEM_LIMIT = 56 * 1024 * 1024


def _cparams(sem):
    return pltpu.CompilerParams(dimension_semantics=sem, vmem_limit_bytes=VMEM_LIMIT)


def _round_up(x, m):
    return (x + m - 1) // m * m


def _pick_tile(total, target, mult):
    best = None
    for t in range(mult, min(total, target) + 1, mult):
        if total % t == 0:
            best = t
    assert best is not None, (total, target, mult)
    return best


def _shift_down(x, d):
    row = lax.broadcasted_iota(jnp.int32, x.shape, 0)
    return jnp.where(row >= d, pltpu.roll(x, d, 0), 0.0)


def _sigmoid(x):
    return 1.0 / (1.0 + jnp.exp(-x))


def _dot(a, b):
    return jnp.dot(a.astype(BF16), b.astype(BF16), preferred_element_type=F32)


def _dot_nt(a, b):
    return lax.dot_general(a.astype(BF16), b.astype(BF16), (((1,), (1,)), ((), ())),
                           preferred_element_type=F32)


def _split3(x):
    hi = x.astype(BF16)
    r1 = x - hi.astype(F32)
    mid = r1.astype(BF16)
    lo = (r1 - mid.astype(F32)).astype(BF16)
    return hi, mid, lo


def _dot_exact_rhs(x, ones_bf16):
    hi, mid, lo = _split3(x)
    acc = jnp.dot(hi, ones_bf16, preferred_element_type=F32)
    acc = acc + jnp.dot(mid, ones_bf16, preferred_element_type=F32)
    return acc + jnp.dot(lo, ones_bf16, preferred_element_type=F32)


def _rms_kernel(x_ref, g_ref, o_ref):
    x = x_ref[...]
    y = x * lax.rsqrt(jnp.mean(x * x, axis=-1, keepdims=True) + NORM_EPS)
    o_ref[...] = (y * g_ref[...]).astype(o_ref.dtype)


def _rms_call(x, g, tm):
    M, D = x.shape
    return pl.pallas_call(
        _rms_kernel,
        grid=(M // tm,),
        in_specs=[pl.BlockSpec((tm, D), lambda i: (i, 0)),
                  pl.BlockSpec((1, D), lambda i: (0, 0))],
        out_specs=pl.BlockSpec((tm, D), lambda i: (i, 0)),
        out_shape=jax.ShapeDtypeStruct((M, D), BF16),
        compiler_params=_cparams(("parallel",)),
        name="rms_norm",
    )(x, g.reshape(1, D))


def _mm_kernel(a_ref, b_ref, o_ref):
    o_ref[...] = jnp.dot(a_ref[...], b_ref[...], preferred_element_type=F32).astype(o_ref.dtype)


def _mm_call(a, b, out_dtype, tm, tn, name, layer=None, col0=0, n_cols=None):
    M, K = a.shape
    if layer is None:
        N = b.shape[1]
        b_spec = pl.BlockSpec((K, tn), lambda i, j: (0, j))
    else:
        N = n_cols
        assert col0 % tn == 0 and N % tn == 0
        b_spec = pl.BlockSpec((None, K, tn), lambda i, j: (layer, 0, col0 // tn + j))
    return pl.pallas_call(
        _mm_kernel,
        grid=(M // tm, N // tn),
        in_specs=[pl.BlockSpec((tm, K), lambda i, j: (i, 0)), b_spec],
        out_specs=pl.BlockSpec((tm, tn), lambda i, j: (i, j)),
        out_shape=jax.ShapeDtypeStruct((M, N), out_dtype),
        compiler_params=_cparams(("parallel", "arbitrary")),
        name=name,
    )(a, b)


def _mm_res_kernel(a_ref, w_ref, h_ref, gp_ref, gn_ref, oh_ref, ou_ref):
    f = jnp.dot(a_ref[...], w_ref[...], preferred_element_type=F32)
    y = f * lax.rsqrt(jnp.mean(f * f, axis=-1, keepdims=True) + NORM_EPS) * gp_ref[...]
    hn = h_ref[...] + y
    oh_ref[...] = hn
    u = hn * lax.rsqrt(jnp.mean(hn * hn, axis=-1, keepdims=True) + NORM_EPS) * gn_ref[...]
    ou_ref[...] = u.astype(ou_ref.dtype)


def _mm_res_tile(M, K, D):
    budget = VMEM_LIMIT - 8 * 1024 * 1024
    best = None
    for tm in range(16, M + 1, 16):
        if M % tm:
            continue
        need = (K * D * 2
                + 2 * tm * K * 2
                + 2 * 2 * tm * D * 4
                + 2 * tm * D * 2
                + 2 * tm * D * 4)
        if need <= budget:
            best = tm
    assert best is not None
    return best


def _mm_res_call(a, w, layer, h, g_post, g_next, name):
    M, K = a.shape
    D = w.shape[2]
    tm = _mm_res_tile(M, K, D)
    return pl.pallas_call(
        _mm_res_kernel,
        grid=(M // tm,),
        in_specs=[pl.BlockSpec((tm, K), lambda i: (i, 0)),
                  pl.BlockSpec((None, K, D), lambda i: (layer, 0, 0), pipeline_mode=pl.Buffered(1)),
                  pl.BlockSpec((tm, D), lambda i: (i, 0)),
                  pl.BlockSpec((1, D), lambda i: (0, 0)),
                  pl.BlockSpec((1, D), lambda i: (0, 0))],
        out_specs=[pl.BlockSpec((tm, D), lambda i: (i, 0)),
                   pl.BlockSpec((tm, D), lambda i: (i, 0))],
        out_shape=[jax.ShapeDtypeStruct((M, D), F32),
                   jax.ShapeDtypeStruct((M, D), BF16)],
        compiler_params=_cparams(("parallel",)),
        name=name,
    )(a, w, h, g_post.reshape(1, D), g_next.reshape(1, D))


def _pool_kernel(z_ref, w_ref, s_ref, o_ref, *, gw):
    L = z_ref.shape[1]
    row = lax.broadcasted_iota(jnp.int32, (L, gw), 0)
    for gi, win in enumerate(POOL_WINDOWS):
        sl = slice(gi * gw, (gi + 1) * gw)
        zg = z_ref[0, :, sl].astype(F32)
        s = zg
        d = 1
        while d < win:
            s = s + _shift_down(s, d)
            d *= 2
        cnt = jnp.minimum(row + 1, win).astype(F32)
        pooled = s / cnt - zg
        y = _dot(pooled, w_ref[gi]) * s_ref[:, sl]
        o_ref[0, :, sl] = y.astype(o_ref.dtype)


def _pool_call(z, w, scale, B, Lp):
    PW = z.shape[-1]
    G = len(POOL_WINDOWS)
    gw = PW // G
    z3 = z.reshape(B, Lp, PW)
    out = pl.pallas_call(
        functools.partial(_pool_kernel, gw=gw),
        grid=(B,),
        in_specs=[pl.BlockSpec((1, Lp, PW), lambda b: (b, 0, 0)),
                  pl.BlockSpec((G, gw, gw), lambda b: (0, 0, 0)),
                  pl.BlockSpec((1, PW), lambda b: (0, 0))],
        out_specs=pl.BlockSpec((1, Lp, PW), lambda b: (b, 0, 0)),
        out_shape=jax.ShapeDtypeStruct((B, Lp, PW), BF16),
        compiler_params=_cparams(("parallel",)),
        name="pool_mixer",
    )(z3, w.astype(BF16), scale.reshape(1, PW))
    return out.reshape(B * Lp, PW)


def _rel_bucket(rel, n_buckets):
    half = n_buckets // 2
    max_exact = half // 2
    n = np.abs(rel)
    large = max_exact + (np.log(np.maximum(n, 1) / max_exact) / math.log(REL_MAX_DIST / max_exact)
                         * (half - max_exact)).astype(np.int32)
    large = np.minimum(large, half - 1)
    return ((rel > 0) * half + np.where(n < max_exact, n, large)).astype(np.int32)


def _bias_table_kernel(rb_ref, bucket_ref, pen_ref, o_ref, *, n_buckets):
    s = pl.program_id(0)
    bucket = bucket_ref[...]
    acc = pen_ref[...]
    for b in range(n_buckets):
        acc = acc + jnp.where(bucket == b, rb_ref[b, s], 0.0)
    o_ref[0] = acc


def _bias_table_call(rel_bias, n_meta, tq):
    n_buckets, n_sub = rel_bias.shape
    KT = ATTN_TILE
    assert tq % KT == 0 and KT % CHUNK == 0 and n_meta <= KT
    n_near = tq // KT + 2
    n_slabs = n_near + 2
    far = n_buckets // 2 - 1
    r = np.arange(tq)[:, None]
    c = np.arange(KT)[None, :]
    assert np.all(_rel_bucket(-(KT + 1 + np.arange(4 * KT)), n_buckets) == far)
    bucket = np.full((n_slabs, tq, KT), far, np.int32)
    pen = np.zeros((n_slabs, tq, KT), np.float32)
    pen[-1] = NEG_INF
    for idx in range(n_near):
        key = (idx - 1) * KT + c
        bucket[1 + idx] = _rel_bucket(key - r, n_buckets)
        visible = np.floor_divide(key - n_meta, CHUNK) <= np.floor_divide(r - n_meta, CHUNK)
        pen[1 + idx] = np.where(visible, 0.0, NEG_INF)
    blk = pl.BlockSpec((n_slabs, tq, KT), lambda s: (0, 0, 0))
    return pl.pallas_call(
        functools.partial(_bias_table_kernel, n_buckets=n_buckets),
        grid=(n_sub,),
        in_specs=[pl.BlockSpec(memory_space=pltpu.SMEM), blk, blk],
        out_specs=pl.BlockSpec((1, n_slabs, tq, KT), lambda s: (s, 0, 0, 0)),
        out_shape=jax.ShapeDtypeStruct((n_sub, n_slabs, tq, KT), F32),
        compiler_params=_cparams(("arbitrary",)),
        name="bias_table_%d" % tq,
    )(rel_bias, jnp.asarray(bucket), jnp.asarray(pen))


def _attn_kernel(*refs, hd, lam_init, nt, tile0, live):
    scale = hd ** -0.5
    (lq1_ref, lk1_ref, lq2_ref, lk2_ref, sub_ref, q_ref, k_ref, v_ref, w_ref,
     o_ref, s_sc, mx_sc, l_sc, acc_sc, k_sc, v_sc) = refs
    KT = ATTN_TILE
    tq = q_ref.shape[1]
    ratio = -(-tq // KT)
    Lk = k_ref.shape[1]

    @pl.when(pl.program_id(2) == 0)
    def _():
        k_sc[0:Lk, :] = k_ref[0]
        v_sc[0:Lk, :] = v_ref[0]
        if k_sc.shape[0] > Lk:
            k_sc[Lk:, :] = jnp.zeros((k_sc.shape[0] - Lk, k_sc.shape[1]), k_sc.dtype)
            v_sc[Lk:, :] = jnp.zeros((v_sc.shape[0] - Lk, v_sc.shape[1]), v_sc.dtype)
    n_slabs = w_ref.shape[1]
    i = pl.program_id(2)
    jt0 = tile0 + i * ratio
    n_need = jnp.minimum(jt0 + ratio + 1, nt)
    n_pairs = (n_need + 1) // 2
    q = q_ref[0, 0:live, :]

    def slab(j):
        return jnp.where(j < nt, jnp.clip(j - jt0 + 2, 0, n_slabs - 1), n_slabs - 1)

    def two_tiles(ref, jj):
        j0 = 2 * jj
        j1 = jnp.minimum(j0 + 1, nt - 1)
        t0 = ref[pl.ds(pl.multiple_of(j0 * KT, KT), KT), :]
        t1 = ref[pl.ds(pl.multiple_of(j1 * KT, KT), KT), :]
        return jnp.concatenate([t0, t1], axis=0)

    mx_sc[...] = jnp.full_like(mx_sc, NEG_INF)

    def unrolled(step):
        def group(t, carry):
            step([ATTN_UNROLL * t + r for r in range(ATTN_UNROLL)])
            return carry

        def single(jj, carry):
            step([jj])
            return carry
        n_groups = n_pairs // ATTN_UNROLL
        lax.fori_loop(0, n_groups, group, 0)
        lax.fori_loop(n_groups * ATTN_UNROLL, n_pairs, single, 0)

    def pass1(jjs):
        kts = [two_tiles(k_sc, jj) for jj in jjs]
        for s in range(2):
            mx = None
            for jj, kt in zip(jjs, kts):
                sc = _dot_nt(q[:, s * hd:(s + 1) * hd], kt[:, s * hd:(s + 1) * hd]) * scale
                sc = sc + jnp.concatenate([w_ref[s, slab(2 * jj), 0:live, :],
                                           w_ref[s, slab(2 * jj + 1), 0:live, :]], axis=1)
                s_sc[s, jj] = sc
                half = jnp.maximum(sc[:, :KT], sc[:, KT:])
                mx = half if mx is None else jnp.maximum(mx, half)
            mx_sc[s] = jnp.maximum(mx_sc[s], mx)

    unrolled(pass1)
    m = [jnp.max(mx_sc[s], axis=-1, keepdims=True) for s in range(2)]
    l_sc[...] = jnp.zeros_like(l_sc)
    acc_sc[...] = jnp.zeros_like(acc_sc)

    def pass2(jjs):
        for jj in jjs:
            vt = two_tiles(v_sc, jj)
            for s in range(2):
                p = jnp.exp(s_sc[s, jj] - m[s])
                l_sc[s] += p[:, :KT] + p[:, KT:]
                acc_sc[s] += jnp.dot(p.astype(BF16), vt, preferred_element_type=F32)

    unrolled(pass2)
    l = [jnp.sum(l_sc[s], axis=-1, keepdims=True) for s in range(2)]
    lam = (jnp.exp(jnp.sum(lq1_ref[...] * lk1_ref[...], axis=-1, keepdims=True))
           - jnp.exp(jnp.sum(lq2_ref[...] * lk2_ref[...], axis=-1, keepdims=True)) + lam_init)
    o = acc_sc[0] / l[0] - lam * (acc_sc[1] / l[1])
    y = o * lax.rsqrt(jnp.mean(o * o, axis=-1, keepdims=True) + DIFF_SUBLN_EPS)
    o_ref[0, 0:live, :] = (y * sub_ref[...] * (1.0 - lam_init)).astype(o_ref.dtype)
    if live < tq:
        o_ref[0, live:tq, :] = jnp.zeros((tq - live, o_ref.shape[2]), o_ref.dtype)


def _attn_segment(qkv3, table, params, lam_init, tq, row0, n_tiles, live):
    B, Lp, W3 = qkv3.shape
    DW = W3 // 3
    hd = params[0].shape[1]
    hw = 2 * hd
    n_heads = DW // hw
    KT = ATTN_TILE
    nt = -(-Lp // KT)
    blk0 = row0 // tq
    kern = functools.partial(_attn_kernel, hd=hd, lam_init=lam_init, nt=nt,
                             tile0=row0 // KT, live=live)
    par = lambda n: pl.BlockSpec((1, n), lambda b, h, i: (0, 0))
    in_specs = [par(hd)] * 4 + [
        par(hw),
        pl.BlockSpec((1, tq, hw), lambda b, h, i: (b, blk0 + i, h)),
        pl.BlockSpec((1, Lp, hw), lambda b, h, i: (b, 0, n_heads + h)),
        pl.BlockSpec((1, Lp, hw), lambda b, h, i: (b, 0, 2 * n_heads + h)),
        pl.BlockSpec((2, table.shape[1], tq, KT), lambda b, h, i: (h, 0, 0, 0))]
    args = list(params) + [qkv3, qkv3, qkv3, table]
    max_pairs = (nt + 1) // 2
    return pl.pallas_call(
        kern,
        grid=(B, n_heads, n_tiles),
        in_specs=in_specs,
        out_specs=pl.BlockSpec((1, tq, hw), lambda b, h, i: (b, i, h)),
        out_shape=jax.ShapeDtypeStruct((B, n_tiles * tq, DW), BF16),
        scratch_shapes=[pltpu.VMEM((2, max_pairs, live, 2 * KT), F32), pltpu.VMEM((2, live, KT), F32),
                        pltpu.VMEM((2, live, KT), F32), pltpu.VMEM((2, live, hw), F32),
                        pltpu.VMEM((nt * KT, hw), BF16), pltpu.VMEM((nt * KT, hw), BF16)],
        compiler_params=_cparams(("parallel", "parallel", "arbitrary")),
        name="diff_attention_%d" % tq,
    )(*args)


def _attn_segments(Lp):
    big = ATTN_Q_TILE
    n_big = Lp // big
    segs = []
    if n_big:
        segs.append((big, 0, n_big))
    row0 = n_big * big
    n_mid = (Lp - row0) // ATTN_TILE
    if n_mid:
        segs.append((ATTN_TILE, row0, n_mid))
        row0 += n_mid * ATTN_TILE
    if Lp > row0:
        segs.append((Lp - row0, row0, 1))
    return segs


def _attn_call(zqkv, tables, lq1, lk1, lq2, lk2, subln, B, L, Lp, lam_init):
    vec = lambda a: a.reshape(1, -1)
    params = [vec(lq1), vec(lk1), vec(lq2), vec(lk2), vec(subln)]
    qkv3 = zqkv.reshape(B, Lp, zqkv.shape[-1])
    outs = []
    for tq, row0, n_tiles in _attn_segments(Lp):
        live = min(tq, _round_up(L - row0, 2 * SUBLANES)) if n_tiles == 1 else tq
        outs.append(_attn_segment(qkv3, tables[tq], params, lam_init, tq, row0, n_tiles, live))
    out = outs[0] if len(outs) == 1 else jnp.concatenate(outs, axis=1)
    return out.reshape(B * Lp, -1)


def _rwkv_prep_kernel(*refs, rw, has_vres):
    if has_vres:
        (u_ref, up_ref, w_ref, mu_ref, w0_ref, w2_ref, a0_ref, a2_ref, g2_ref, kk_ref, ka_ref, ones_ref,
         vf_ref, v0_ref, v1_ref, v2_ref,
         r_o, lw_o, k_o, v_o, a_o, b_o, g_o) = refs
    else:
        (u_ref, up_ref, w_ref, mu_ref, w0_ref, w2_ref, a0_ref, a2_ref, g2_ref, kk_ref, ka_ref, ones_ref,
         r_o, lw_o, k_o, v_o, a_o, b_o, g_o) = refs
    i = pl.program_id(1)
    halo = up_ref.shape[1]
    zz = jnp.dot(jnp.concatenate([up_ref[0], u_ref[0]], axis=0), w_ref[...], preferred_element_type=F32)
    z = zz[halo:]
    last_prev = jnp.where(i > 0, zz[halo - 1:halo, :], 0.0)
    row = lax.broadcasted_iota(jnp.int32, z.shape, 0)
    prev = jnp.where(row >= 1, pltpu.roll(z, 1, 0), last_prev)
    zs = z + (prev - z) * mu_ref[...]
    r = zs[:, 0:rw]
    kx = zs[:, rw:2 * rw]
    vx = zs[:, 2 * rw:3 * rw]
    wd = zs[:, 3 * rw:3 * rw + LANES]
    ad = zs[:, 3 * rw + LANES:3 * rw + 2 * LANES]
    gd = zs[:, 3 * rw + 2 * LANES:3 * rw + 3 * LANES]
    wl = w0_ref[...] + _dot(jnp.tanh(wd), w2_ref[...])
    sp = jnp.maximum(-wl, 0.0) + jnp.log(1.0 + jnp.exp(-jnp.abs(wl)))
    lw = -jnp.exp(-sp - 0.5)
    if has_vres:
        gate_v = _sigmoid(v0_ref[...] + _dot(_dot(vx, v1_ref[...]), v2_ref[...]))
        vx = vx + (vf_ref[0] - vx) * gate_v
    a = _sigmoid(a0_ref[...] + _dot(ad, a2_ref[...]))
    g = _dot(_sigmoid(gd), g2_ref[...])
    kk = kx * kk_ref[...]
    ss = _dot_exact_rhs(kk * kk, ones_ref[...])
    kk = kk * lax.rsqrt(jnp.maximum(ss, 1e-24))
    kmod = kx * (1.0 + (a - 1.0) * ka_ref[...])
    r_o[0] = r
    lw_o[0] = lw
    k_o[0] = kmod
    v_o[0] = vx
    a_o[0] = -kk
    b_o[0] = kk * a
    g_o[0] = g


def _pad_rows(w, n):
    return jnp.pad(w, ((0, n - w.shape[0]), (0, 0)))


def _rwkv_prep_call(u, w_rwkv, B, Lp, rw, mu_p, w0, w2, a0, a2, g2, k_k, k_a, ones_bd, v_first, v_res):
    D = u.shape[-1]
    u3 = u.reshape(B, Lp, D)
    HALO = 2 * SUBLANES
    tl = _pick_tile(Lp, 384, HALO)
    nb = tl // HALO
    has_vres = v_res is not None
    vec = lambda a: a.reshape(1, -1)
    full = lambda a: pl.BlockSpec(a.shape, lambda b, i: (0,) * a.ndim)
    w2p = _pad_rows(w2, LANES).astype(BF16)
    a2p = _pad_rows(a2, LANES).astype(BF16)
    g2p = _pad_rows(g2, LANES).astype(BF16)
    args = [u3, u3, w_rwkv, vec(mu_p), vec(w0), w2p, vec(a0), a2p, g2p, vec(k_k), vec(k_a), ones_bd]
    in_specs = [pl.BlockSpec((1, tl, D), lambda b, i: (b, i, 0)),
                pl.BlockSpec((1, HALO, D), lambda b, i: (b, jnp.maximum(i * nb - 1, 0), 0)),
                pl.BlockSpec(w_rwkv.shape, lambda b, i: (0, 0), pipeline_mode=pl.Buffered(1))]
    in_specs += [full(a) for a in args[3:]]
    if has_vres:
        v0, v1, v2 = v_res
        v1p = jnp.pad(v1, ((0, 0), (0, LANES - v1.shape[1]))).astype(BF16)
        v2p = _pad_rows(v2, LANES).astype(BF16)
        extra = [v_first, vec(v0), v1p, v2p]
        args += extra
        in_specs += [pl.BlockSpec((1, tl, rw), lambda b, i: (b, i, 0))] + [full(a) for a in extra[1:]]
    o_spec = pl.BlockSpec((1, tl, rw), lambda b, i: (b, i, 0))
    o_shape = jax.ShapeDtypeStruct((B, Lp, rw), F32)
    return pl.pallas_call(
        functools.partial(_rwkv_prep_kernel, rw=rw, has_vres=has_vres),
        grid=(B, Lp // tl),
        in_specs=in_specs,
        out_specs=[o_spec] * 7,
        out_shape=[o_shape] * 7,
        compiler_params=_cparams(("parallel", "arbitrary")),
        name="rwkv_prep",
    )(*args)


def _scan_kernel(r_ref, lw_ref, k_ref, v_ref, a_ref, b_ref, g_ref, rk_ref, lnw_ref, lnb_ref, ones_ref,
                 o_ref, h_sc, *, hd):
    nb = r_ref.shape[0]
    C = r_ref.shape[1]
    C2 = 2 * C
    WB = r_ref.shape[2]
    W = nb * WB
    c = pl.program_id(1)

    @pl.when(c == 0)
    def _():
        h_sc[...] = jnp.zeros_like(h_sc)

    side_by_side = lambda ref: jnp.concatenate([ref[i] for i in range(nb)], axis=1)
    r = side_by_side(r_ref)
    lw = side_by_side(lw_ref)
    k = side_by_side(k_ref)
    v = side_by_side(v_ref)
    a = side_by_side(a_ref)
    b = side_by_side(b_ref)

    cum = lw
    d = 1
    while d < C:
        cum = cum + _shift_down(cum, d)
        d *= 2
    tot = cum[C - 1:C, :]
    e_end = jnp.exp(tot - cum)
    e_neg = jnp.exp(-cum)
    at = a * jnp.exp(cum - lw)
    rt = r * jnp.exp(cum)
    bt = b * e_neg
    kt = k * e_neg
    bb = b * e_end
    kb = k * e_end
    p_c = jnp.exp(tot)

    lane = lax.broadcasted_iota(jnp.int32, (C, LANES), 1)
    first = lane < hd
    rr = lax.broadcasted_iota(jnp.int32, (C2, C2), 0)
    cc = lax.broadcasted_iota(jnp.int32, (C2, C2), 1)
    tr = jnp.where(rr >= C, rr - C, rr)
    tc = jnp.where(cc >= C, cc - C, cc)
    strict = tr > tc
    lower = tr >= tc
    eye = rr == cc

    def stack(x):
        return jnp.concatenate([jnp.where(first, x, 0.0), jnp.where(first, 0.0, x)], axis=0)

    pairs = range(W // LANES)
    stacked = lambda x: [stack(x[:, p * LANES:(p + 1) * LANES]) for p in pairs]
    s_a, s_r, s_b, s_k = stacked(at), stacked(rt), stacked(bt), stacked(kt)
    s_bb, s_kb, s_v = stacked(bb), stacked(kb), stacked(v)
    sc = [_dot_nt(jnp.concatenate([s_a[p], s_r[p]], axis=0), jnp.concatenate([s_b[p], s_k[p]], axis=0))
          for p in pairs]
    n = [jnp.where(strict, sc[p][:C2, :C2], 0.0) for p in pairs]
    a_ak = [jnp.where(strict, sc[p][:C2, C2:], 0.0) for p in pairs]
    m_rb = [jnp.where(lower, sc[p][C2:, :C2], 0.0) for p in pairs]
    m_rk = [jnp.where(lower, sc[p][C2:, C2:], 0.0) for p in pairs]
    xv = [_dot(jnp.concatenate([a_ak[p], m_rk[p], s_kb[p].T], axis=0), s_v[p]) for p in pairs]
    t_inv = [jnp.where(eye, 1.0, 0.0) + n[p] for p in pairs]
    pw = [_dot(n[p], n[p]) for p in pairs]
    step = 4
    while step < C:
        tp = [_dot(jnp.concatenate([t_inv[p], pw[p]], axis=0), pw[p]) for p in pairs]
        t_inv = [t_inv[p] + tp[p][:C2] for p in pairs]
        pw = [tp[p][C2:] for p in pairs]
        step *= 2
    t_inv = [t_inv[p] + _dot(t_inv[p], pw[p]) for p in pairs]
    au = [_dot(t_inv[p], jnp.concatenate([s_a[p], xv[p][:C2]], axis=1)) for p in pairs]
    mbu = [_dot(jnp.concatenate([m_rb[p], s_bb[p].T], axis=0), au[p]) for p in pairs]
    ys = []
    for p in pairs:
        r_hat = s_r[p] + mbu[p][:C2, :LANES]
        y0 = mbu[p][:C2, LANES:] + xv[p][C2:2 * C2]
        g_mat = jnp.where(eye, p_c[:, p * LANES:(p + 1) * LANES], 0.0) + mbu[p][C2:, :LANES]
        h_add = mbu[p][C2:, LANES:] + xv[p][2 * C2:]
        yh = _dot(jnp.concatenate([r_hat, g_mat], axis=0), h_sc[p])
        y = yh[:C2] + y0
        h_sc[p] = yh[C2:] + h_add
        ys.append(y[0:C] + y[C:C2])
    ones = ones_ref[...]
    inv_hd = 1.0 / hd
    ppb = WB // LANES
    for i in range(nb):
        y = jnp.concatenate(ys[i * ppb:(i + 1) * ppb], axis=1)
        mean = _dot_exact_rhs(y, ones) * inv_hd
        yc = y - mean
        var = _dot_exact_rhs(yc * yc, ones) * inv_hd
        yn = yc * lax.rsqrt(var + RWKV_LNX_EPS) * lnw_ref[...] + lnb_ref[...]
        bonus = _dot_exact_rhs(r_ref[i] * k_ref[i] * rk_ref[...], ones) * v_ref[i]
        o_ref[i] = ((yn + bonus) * g_ref[i]).astype(o_ref.dtype)


def _scan_call(streams, r_k, lnx_w, lnx_b, ones_bd, B, Lp, hd):
    r, lw, k, v, a, b, g = streams
    rw = r.shape[-1]
    C = CHUNK
    assert 2 * hd == LANES and rw % LANES == 0 and Lp % C == 0
    vec = lambda x: x.reshape(1, -1)
    nb = SCAN_BATCH if B % SCAN_BATCH == 0 else 1
    blk = pl.BlockSpec((nb, C, rw), lambda bi, c: (bi, c, 0))
    par = pl.BlockSpec((1, rw), lambda bi, c: (0, 0))
    out = pl.pallas_call(
        functools.partial(_scan_kernel, hd=hd),
        grid=(B // nb, Lp // C),
        in_specs=[blk] * 7 + [par] * 3 + [pl.BlockSpec((rw, rw), lambda bi, c: (0, 0))],
        out_specs=blk,
        out_shape=jax.ShapeDtypeStruct((B, Lp, rw), BF16),
        scratch_shapes=[pltpu.VMEM((nb * rw // LANES, LANES, LANES), F32)],
        compiler_params=_cparams(("parallel", "arbitrary")),
        name="rwkv_scan",
    )(r, lw, k, v, a, b, g, vec(r_k), vec(lnx_w), vec(lnx_b), ones_bd)
    return out.reshape(B * Lp, rw)


def _merge_kernel(yp_ref, yd_ref, yr_ref, g0_ref, g1_ref, g2_ref, pa_ref, pb_ref, pc_ref, o_ref):
    dot = lambda x, w: jnp.dot(x[...], w[...], preferred_element_type=F32)
    m = _sigmoid(g0_ref[...].astype(F32)) * dot(yp_ref, pa_ref)
    m = m + _sigmoid(g1_ref[...].astype(F32)) * dot(yd_ref, pb_ref)
    m = m + _sigmoid(g2_ref[...].astype(F32)) * dot(yr_ref, pc_ref)
    o_ref[...] = m.astype(o_ref.dtype)


def _merge_call(yp, yd, yr, zg, pa, pb, pc, layer, tm):
    M = yp.shape[0]
    D = pa.shape[2]
    rows = lambda a: pl.BlockSpec((tm, a.shape[1]), lambda i: (i, 0))
    full = lambda a: pl.BlockSpec((None,) + a.shape[1:], lambda i: (layer, 0, 0))
    gate = lambda n: pl.BlockSpec((tm, D), lambda i: (i, n))
    return pl.pallas_call(
        _merge_kernel,
        grid=(M // tm,),
        in_specs=[rows(yp), rows(yd), rows(yr), gate(0), gate(1), gate(2), full(pa), full(pb), full(pc)],
        out_specs=pl.BlockSpec((tm, D), lambda i: (i, 0)),
        out_shape=jax.ShapeDtypeStruct((M, D), BF16),
        compiler_params=_cparams(("parallel",)),
        name="gated_merge",
    )(yp, yd, yr, zg, zg, zg, pa, pb, pc)


def _ffn_up_kernel(u_ref, wv_ref, wg_ref, cwv_ref, cwg_ref, cbv_ref, cbg_ref, o_ref, *scratch, n_sub):
    xg_scs, xv_scs, a_scs = scratch[:n_sub], scratch[n_sub:2 * n_sub], scratch[2 * n_sub:]
    L = u_ref.shape[0]
    mb = L // n_sub
    st = mb // SUBLANES
    PAD = SUBLANES
    n_slab = a_scs[0].shape[0]
    k_in = math.sqrt(2.0 / math.pi)
    zeros = jnp.zeros((PAD, LANES), F32)
    for s in range(n_slab):
        xv_scs[0][s, 0:PAD, :] = zeros
        xg_scs[0][s, 0:PAD, :] = zeros

    def taps(cw_ref, cb_ref, s):
        sl = slice(s * LANES, (s + 1) * LANES)
        rows = [jnp.broadcast_to(cw_ref[j:j + 1, sl], (SUBLANES, LANES)) for j in range(FFN_CONV)]
        return rows, jnp.broadcast_to(cb_ref[:, sl], (SUBLANES, LANES))

    def tail(sb):
        lo = sb * mb
        xg_sc, xv_sc, a_sc = xg_scs[sb], xv_scs[sb], a_scs[sb]
        for s in range(n_slab):
            wg, bg = taps(cwg_ref, cbg_ref, s)
            wv, bv = taps(cwv_ref, cbv_ref, s)
            ld = lambda ref, j: ref[s, pl.ds(PAD + j, SUBLANES, stride=st), :]
            xg = [ld(xg_sc, j) for j in range(1 - FFN_CONV, 0)]
            xv = [ld(xv_sc, j) for j in range(1 - FFN_CONV, 0)]
            for j in range(st):
                xg.append(ld(xg_sc, j))
                xv.append(ld(xv_sc, j))
                gate = bg
                val = bv
                for t in range(FFN_CONV):
                    gate = gate + xg[j + t] * wg[t]
                    val = val + xv[j + t] * wv[t]
                inner = gate * (2.0 * k_in + (2.0 * k_in * 0.044715) * (gate * gate))
                act = gate / (1.0 + jnp.exp(-inner)) * val
                a_sc[s, pl.ds(j, SUBLANES, stride=st), :] = act
        for s in range(n_slab):
            o_ref[lo:lo + mb, s * LANES:(s + 1) * LANES] = a_sc[s].astype(o_ref.dtype)

    for sb in range(n_sub):
        lo = sb * mb
        u = u_ref[lo:lo + mb, :]
        for x_scs, w_ref in ((xg_scs, wg_ref), (xv_scs, wv_ref)):
            x = jnp.dot(u, w_ref[...], preferred_element_type=F32)
            for s in range(n_slab):
                xs = x[:, s * LANES:(s + 1) * LANES]
                x_scs[sb][s, PAD:PAD + mb, :] = xs
                if sb + 1 < n_sub:
                    x_scs[sb + 1][s, 0:PAD, :] = xs[mb - PAD:mb]
        if sb:
            tail(sb - 1)
    tail(n_sub - 1)


def _ffn_up_call(u, w_up, layer, conv_w, conv_b, B, Lp, tn):
    M, D = u.shape
    FF = w_up.shape[2] // 2
    nt = FF // tn
    n_sub = 4 if Lp % (4 * 2 * SUBLANES) == 0 else 1
    mb = Lp // n_sub
    return pl.pallas_call(
        functools.partial(_ffn_up_kernel, n_sub=n_sub),
        grid=(B, nt),
        in_specs=[pl.BlockSpec((Lp, D), lambda b, j: (b, 0)),
                  pl.BlockSpec((None, D, tn), lambda b, j: (layer, 0, j)),
                  pl.BlockSpec((None, D, tn), lambda b, j: (layer, 0, nt + j)),
                  pl.BlockSpec((FFN_CONV, tn), lambda b, j: (0, j)),
                  pl.BlockSpec((FFN_CONV, tn), lambda b, j: (0, nt + j)),
                  pl.BlockSpec((1, tn), lambda b, j: (0, j)),
                  pl.BlockSpec((1, tn), lambda b, j: (0, nt + j))],
        out_specs=pl.BlockSpec((Lp, tn), lambda b, j: (b, j)),
        out_shape=jax.ShapeDtypeStruct((M, FF), BF16),
        scratch_shapes=([pltpu.VMEM((tn // LANES, SUBLANES + mb, LANES), F32)] * (2 * n_sub)
                        + [pltpu.VMEM((tn // LANES, mb, LANES), F32)] * n_sub),
        compiler_params=_cparams(("parallel", "arbitrary")),
        name="ffn_up_conv_geglu",
    )(u, w_up, w_up, conv_w, conv_w, conv_b.reshape(1, -1), conv_b.reshape(1, -1))


def kernel(x, meta, rel_bias, norm_mix_pre, norm_mix_post, norm_ffn_pre, norm_ffn_post, w_in, pool_w, pool_scale, diff_lq1, diff_lk1, diff_lq2, diff_lk2, diff_subln, rwkv_mu, rwkv_w0, rwkv_w2, rwkv_a0, rwkv_a2, rwkv_g2, rwkv_kk, rwkv_ka, rwkv_rk, rwkv_lnx_w, rwkv_lnx_b, rwkv_v0, rwkv_v1, rwkv_v2, p_a, p_b, p_c, w_o, ffn_up, ffn_conv_w, ffn_conv_b, ffn_down):
    B, S, D = x.shape
    n_meta = meta.shape[0]
    depth = w_in.shape[0]
    L = S + n_meta
    Lp = _round_up(L, CHUNK)
    M = B * Lp

    PW = pool_scale.shape[1]
    DW = p_b.shape[1]
    RW = rwkv_w0.shape[1]
    hd_r = rwkv_rk.shape[2]
    dl, al, gl = rwkv_w2.shape[1], rwkv_a2.shape[1], rwkv_g2.shape[1]
    off_q = PW
    off_r = PW + 3 * DW
    off_g = off_r + 3 * RW + dl + al + gl
    assert off_g + 3 * D == w_in.shape[2]
    assert Lp == L or S % CHUNK == 0
    assert max(dl, al, gl, rwkv_v1.shape[2]) <= LANES

    h = jnp.concatenate([jnp.broadcast_to(meta.astype(x.dtype)[None], (B, n_meta, D)), x,
                         jnp.zeros((B, Lp - L, D), x.dtype)], axis=1).reshape(M, D)

    hid = np.arange(RW) // hd_r
    ones_bd = jnp.asarray((hid[:, None] == hid[None, :]).astype(np.float32), dtype=BF16)

    def pad_lora(a, axis):
        segs = [lax.slice_in_dim(a, 0, 3 * RW, axis=axis)]
        o = 3 * RW
        for n in (dl, al, gl):
            seg = lax.slice_in_dim(a, o, o + n, axis=axis)
            pad = [(0, 0)] * a.ndim
            pad[axis] = (0, LANES - n)
            segs.append(jnp.pad(seg, pad))
            o += n
        return jnp.concatenate(segs, axis=axis)

    tables = {}
    for tq, _, _ in _attn_segments(Lp):
        tt = _round_up(tq, ATTN_TILE)
        if tt not in tables:
            tables[tt] = _bias_table_call(rel_bias, n_meta, tt)
        tables[tq] = tables[tt]

    tm_big = _pick_tile(M, 1088, 16)
    tm_res = _pick_tile(M, 544, 16)
    u = _rms_call(h, norm_mix_pre[0], tm_res)
    v_first = None
    w_in_b, p_a_b, p_b_b, p_c_b, w_o_b, ffn_up_b, ffn_down_b = (
        a.astype(BF16) for a in (w_in, p_a, p_b, p_c, w_o, ffn_up, ffn_down))
    tn_qkv = _pick_tile(math.gcd(off_q, 3 * DW), 768, LANES)
    for l in range(depth):
        lam_init = 0.8 - 0.6 * math.exp(-0.3 * l)
        w_gate = w_in_b[l, :, off_g:]
        w_rwkv = pad_lora(w_in_b[l, :, off_r:off_g], 1)
        mu_p = pad_lora(rwkv_mu[l], 0)

        zg = _mm_call(u, w_gate, BF16, tm_big, _pick_tile(3 * D, 768, LANES), "in_proj_gate")
        zqkv = _mm_call(u, w_in_b, BF16, tm_big, tn_qkv, "in_proj_qkv", layer=l, col0=off_q, n_cols=3 * DW)
        zp = _mm_call(u, w_in_b, BF16, tm_big, _pick_tile(PW, 768, LANES), "in_proj_pool",
                      layer=l, col0=0, n_cols=PW)

        yp = _pool_call(zp, pool_w[l], pool_scale[l], B, Lp)
        yd = _attn_call(zqkv, tables, diff_lq1[l], diff_lk1[l], diff_lq2[l], diff_lk2[l],
                        diff_subln[l], B, L, Lp, lam_init)
        v_res = None if l == 0 else (rwkv_v0[l - 1], rwkv_v1[l - 1], rwkv_v2[l - 1])
        streams = _rwkv_prep_call(u, w_rwkv, B, Lp, RW, mu_p, rwkv_w0[l], rwkv_w2[l], rwkv_a0[l], rwkv_a2[l],
                                  rwkv_g2[l], rwkv_kk[l], rwkv_ka[l], ones_bd, v_first, v_res)
        if l == 0:
            v_first = streams[3]
        yr = _scan_call(streams, rwkv_rk[l], rwkv_lnx_w[l], rwkv_lnx_b[l], ones_bd, B, Lp, hd_r)

        merged = _merge_call(yp, yd, yr, zg, p_a_b, p_b_b, p_c_b, l, tm_res)
        h, u2 = _mm_res_call(merged, w_o_b, l, h, norm_mix_post[l], norm_ffn_pre[l], "out_proj_res")
        FF = ffn_down.shape[1]
        act = _ffn_up_call(u2, ffn_up_b, l, ffn_conv_w[l], ffn_conv_b[l], B, Lp,
                           _pick_tile(FF, 512, LANES))
        g_next = norm_mix_pre[l + 1] if l + 1 < depth else norm_mix_pre[l]
        h, u = _mm_res_call(act, ffn_down_b, l, h, norm_ffn_post[l], g_next, "ffn_down_res")
    return h.reshape(B, Lp, D)[:, n_meta:n_meta + S]
```

```python
import functools
import math

import numpy as np
import jax
import jax.numpy as jnp
from jax import lax
from jax.experimental import pallas as pl
from jax.experimental.pallas import tpu as pltpu

F32 = jnp.float32
BF16 = jnp.bfloat16

CHUNK = 64
NORM_EPS = 1e-6
NEG_INF = -1e30
POOL_WINDOWS = (2, 4, 8, 16)
DIFF_SUBLN_EPS = 1e-5
REL_MAX_DIST = 128
RWKV_LNX_EPS = 64e-5
FFN_CONV = 3

LANES = 128
SUBLANES = 8
ATTN_TILE = 128
ATTN_Q_TILE = 512
ATTN_UNROLL = 3
SCAN_BATCH = 4
VMEM_LIMIT = 56 * 1024 * 1024


def _cparams(sem):
    return pltpu.CompilerParams(dimension_semantics=sem, vmem_limit_bytes=VMEM_LIMIT)


def _round_up(x, m):
    return (x + m - 1) // m * m


def _pick_tile(total, target, mult):
    best = None
    for t in range(mult, min(total, target) + 1, mult):
        if total % t == 0:
            best = t
    assert best is not None, (total, target, mult)
    return best


def _shift_down(x, d):
    row = lax.broadcasted_iota(jnp.int32, x.shape, 0)
    return jnp.where(row >= d, pltpu.roll(x, d, 0), 0.0)


def _sigmoid(x):
    return 1.0 / (1.0 + jnp.exp(-x))


def _dot(a, b):
    return jnp.dot(a.astype(BF16), b.astype(BF16), preferred_element_type=F32)


def _dot_nt(a, b):
    return lax.dot_general(a.astype(BF16), b.astype(BF16), (((1,), (1,)), ((), ())),
                           preferred_element_type=F32)


def _split3(x):
    hi = x.astype(BF16)
    r1 = x - hi.astype(F32)
    mid = r1.astype(BF16)
    lo = (r1 - mid.astype(F32)).astype(BF16)
    return hi, mid, lo


def _dot_exact_rhs(x, ones_bf16):
    n = x.shape[0]
    pieces = jnp.concatenate(_split3(x), axis=0)
    res = jnp.dot(pieces, ones_bf16, preferred_element_type=F32)
    return res[:n] + res[n:2 * n] + res[2 * n:]


def _rms_kernel(x_ref, g_ref, o_ref):
    x = x_ref[...]
    y = x * lax.rsqrt(jnp.mean(x * x, axis=-1, keepdims=True) + NORM_EPS)
    o_ref[...] = (y * g_ref[...]).astype(o_ref.dtype)


def _rms_call(x, g, tm):
    M, D = x.shape
    return pl.pallas_call(
        _rms_kernel,
        grid=(M // tm,),
        in_specs=[pl.BlockSpec((tm, D), lambda i: (i, 0)),
                  pl.BlockSpec((1, D), lambda i: (0, 0))],
        out_specs=pl.BlockSpec((tm, D), lambda i: (i, 0)),
        out_shape=jax.ShapeDtypeStruct((M, D), BF16),
        compiler_params=_cparams(("parallel",)),
        name="rms_norm",
    )(x, g.reshape(1, D))


def _mm_kernel(a_ref, b_ref, o_ref):
    o_ref[...] = jnp.dot(a_ref[...], b_ref[...], preferred_element_type=F32).astype(o_ref.dtype)


def _mm_call(a, b, out_dtype, tm, tn, name, layer=None, col0=0, n_cols=None):
    M, K = a.shape
    if layer is None:
        N = b.shape[1]
        b_spec = pl.BlockSpec((K, tn), lambda i, j: (0, j))
    else:
        N = n_cols
        assert col0 % tn == 0 and N % tn == 0
        b_spec = pl.BlockSpec((None, K, tn), lambda i, j: (layer, 0, col0 // tn + j))
    return pl.pallas_call(
        _mm_kernel,
        grid=(M // tm, N // tn),
        in_specs=[pl.BlockSpec((tm, K), lambda i, j: (i, 0)), b_spec],
        out_specs=pl.BlockSpec((tm, tn), lambda i, j: (i, j)),
        out_shape=jax.ShapeDtypeStruct((M, N), out_dtype),
        compiler_params=_cparams(("parallel", "arbitrary")),
        name=name,
    )(a, b)


def _mm_res_kernel(a_ref, w_ref, h_ref, gp_ref, gn_ref, oh_ref, ou_ref):
    f = jnp.dot(a_ref[...], w_ref[...], preferred_element_type=F32)
    y = f * lax.rsqrt(jnp.mean(f * f, axis=-1, keepdims=True) + NORM_EPS) * gp_ref[...]
    hn = h_ref[...] + y
    oh_ref[...] = hn
    u = hn * lax.rsqrt(jnp.mean(hn * hn, axis=-1, keepdims=True) + NORM_EPS) * gn_ref[...]
    ou_ref[...] = u.astype(ou_ref.dtype)


def _mm_res_tile(M, K, D):
    budget = VMEM_LIMIT - 8 * 1024 * 1024
    best = None
    for tm in range(16, M + 1, 16):
        if M % tm:
            continue
        need = (K * D * 2
                + 2 * tm * K * 2
                + 2 * 2 * tm * D * 4
                + 2 * tm * D * 2
                + 2 * tm * D * 4)
        if need <= budget:
            best = tm
    assert best is not None
    return best


def _mm_res_call(a, w, layer, h, g_post, g_next, name):
    M, K = a.shape
    D = w.shape[2]
    tm = _mm_res_tile(M, K, D)
    return pl.pallas_call(
        _mm_res_kernel,
        grid=(M // tm,),
        in_specs=[pl.BlockSpec((tm, K), lambda i: (i, 0)),
                  pl.BlockSpec((None, K, D), lambda i: (layer, 0, 0), pipeline_mode=pl.Buffered(1)),
                  pl.BlockSpec((tm, D), lambda i: (i, 0)),
                  pl.BlockSpec((1, D), lambda i: (0, 0)),
                  pl.BlockSpec((1, D), lambda i: (0, 0))],
        out_specs=[pl.BlockSpec((tm, D), lambda i: (i, 0)),
                   pl.BlockSpec((tm, D), lambda i: (i, 0))],
        out_shape=[jax.ShapeDtypeStruct((M, D), F32),
                   jax.ShapeDtypeStruct((M, D), BF16)],
        compiler_params=_cparams(("parallel",)),
        name=name,
    )(a, w, h, g_post.reshape(1, D), g_next.reshape(1, D))


def _pool_kernel(z_ref, w_ref, s_ref, o_ref, *, gw):
    L = z_ref.shape[1]
    row = lax.broadcasted_iota(jnp.int32, (L, gw), 0)
    for gi, win in enumerate(POOL_WINDOWS):
        sl = slice(gi * gw, (gi + 1) * gw)
        zg = z_ref[0, :, sl].astype(F32)
        s = zg
        d = 1
        while d < win:
            s = s + _shift_down(s, d)
            d *= 2
        cnt = jnp.minimum(row + 1, win).astype(F32)
        pooled = s / cnt - zg
        y = _dot(pooled, w_ref[gi]) * s_ref[:, sl]
        o_ref[0, :, sl] = y.astype(o_ref.dtype)


def _pool_call(z, w, scale, B, Lp):
    PW = z.shape[-1]
    G = len(POOL_WINDOWS)
    gw = PW // G
    z3 = z.reshape(B, Lp, PW)
    out = pl.pallas_call(
        functools.partial(_pool_kernel, gw=gw),
        grid=(B,),
        in_specs=[pl.BlockSpec((1, Lp, PW), lambda b: (b, 0, 0)),
                  pl.BlockSpec((G, gw, gw), lambda b: (0, 0, 0)),
                  pl.BlockSpec((1, PW), lambda b: (0, 0))],
        out_specs=pl.BlockSpec((1, Lp, PW), lambda b: (b, 0, 0)),
        out_shape=jax.ShapeDtypeStruct((B, Lp, PW), BF16),
        compiler_params=_cparams(("parallel",)),
        name="pool_mixer",
    )(z3, w.astype(BF16), scale.reshape(1, PW))
    return out.reshape(B * Lp, PW)


def _rel_bucket(rel, n_buckets):
    half = n_buckets // 2
    max_exact = half // 2
    n = np.abs(rel)
    large = max_exact + (np.log(np.maximum(n, 1) / max_exact) / math.log(REL_MAX_DIST / max_exact)
                         * (half - max_exact)).astype(np.int32)
    large = np.minimum(large, half - 1)
    return ((rel > 0) * half + np.where(n < max_exact, n, large)).astype(np.int32)


def _bias_table_kernel(rb_ref, bucket_ref, pen_ref, o_ref, *, n_buckets):
    s = pl.program_id(0)
    bucket = bucket_ref[...]
    acc = pen_ref[...]
    for b in range(n_buckets):
        acc = acc + jnp.where(bucket == b, rb_ref[b, s], 0.0)
    o_ref[0] = acc


def _bias_table_call(rel_bias, n_meta, tq):
    n_buckets, n_sub = rel_bias.shape
    KT = ATTN_TILE
    assert tq % KT == 0 and KT % CHUNK == 0 and n_meta <= KT
    n_near = tq // KT + 2
    n_slabs = n_near + 2
    far = n_buckets // 2 - 1
    r = np.arange(tq)[:, None]
    c = np.arange(KT)[None, :]
    assert np.all(_rel_bucket(-(KT + 1 + np.arange(4 * KT)), n_buckets) == far)
    bucket = np.full((n_slabs, tq, KT), far, np.int32)
    pen = np.zeros((n_slabs, tq, KT), np.float32)
    pen[-1] = NEG_INF
    for idx in range(n_near):
        key = (idx - 1) * KT + c
        bucket[1 + idx] = _rel_bucket(key - r, n_buckets)
        visible = np.floor_divide(key - n_meta, CHUNK) <= np.floor_divide(r - n_meta, CHUNK)
        pen[1 + idx] = np.where(visible, 0.0, NEG_INF)
    blk = pl.BlockSpec((n_slabs, tq, KT), lambda s: (0, 0, 0))
    return pl.pallas_call(
        functools.partial(_bias_table_kernel, n_buckets=n_buckets),
        grid=(n_sub,),
        in_specs=[pl.BlockSpec(memory_space=pltpu.SMEM), blk, blk],
        out_specs=pl.BlockSpec((1, n_slabs, tq, KT), lambda s: (s, 0, 0, 0)),
        out_shape=jax.ShapeDtypeStruct((n_sub, n_slabs, tq, KT), F32),
        compiler_params=_cparams(("arbitrary",)),
        name="bias_table_%d" % tq,
    )(rel_bias, jnp.asarray(bucket), jnp.asarray(pen))


def _attn_kernel(*refs, hd, lam_init, nt, tile0, live):
    scale = hd ** -0.5
    (lq1_ref, lk1_ref, lq2_ref, lk2_ref, sub_ref, q_ref, k_ref, v_ref, w_ref,
     o_ref, s_sc, mx_sc, l_sc, acc_sc, k_sc, v_sc) = refs
    KT = ATTN_TILE
    tq = q_ref.shape[1]
    ratio = -(-tq // KT)
    Lk = k_ref.shape[1]

    @pl.when(pl.program_id(2) == 0)
    def _():
        k_sc[0:Lk, :] = k_ref[0]
        v_sc[0:Lk, :] = v_ref[0]
        if k_sc.shape[0] > Lk:
            k_sc[Lk:, :] = jnp.zeros((k_sc.shape[0] - Lk, k_sc.shape[1]), k_sc.dtype)
            v_sc[Lk:, :] = jnp.zeros((v_sc.shape[0] - Lk, v_sc.shape[1]), v_sc.dtype)
    n_slabs = w_ref.shape[1]
    i = pl.program_id(2)
    jt0 = tile0 + i * ratio
    n_need = jnp.minimum(jt0 + ratio + 1, nt)
    n_pairs = (n_need + 1) // 2
    q = q_ref[0, 0:live, :]

    def slab(j):
        return jnp.where(j < nt, jnp.clip(j - jt0 + 2, 0, n_slabs - 1), n_slabs - 1)

    def two_tiles(ref, jj):
        j0 = 2 * jj
        j1 = jnp.minimum(j0 + 1, nt - 1)
        t0 = ref[pl.ds(pl.multiple_of(j0 * KT, KT), KT), :]
        t1 = ref[pl.ds(pl.multiple_of(j1 * KT, KT), KT), :]
        return jnp.concatenate([t0, t1], axis=0)

    mx_sc[...] = jnp.full_like(mx_sc, NEG_INF)

    def unrolled(step):
        def group(t, carry):
            step([ATTN_UNROLL * t + r for r in range(ATTN_UNROLL)])
            return carry

        def single(jj, carry):
            step([jj])
            return carry
        n_groups = n_pairs // ATTN_UNROLL
        lax.fori_loop(0, n_groups, group, 0)
        lax.fori_loop(n_groups * ATTN_UNROLL, n_pairs, single, 0)

    def pass1(jjs):
        kts = [two_tiles(k_sc, jj) for jj in jjs]
        for s in range(2):
            mx = None
            for jj, kt in zip(jjs, kts):
                sc = _dot_nt(q[:, s * hd:(s + 1) * hd], kt[:, s * hd:(s + 1) * hd]) * scale
                sc = sc + jnp.concatenate([w_ref[s, slab(2 * jj), 0:live, :],
                                           w_ref[s, slab(2 * jj + 1), 0:live, :]], axis=1)
                s_sc[s, jj] = sc
                half = jnp.maximum(sc[:, :KT], sc[:, KT:])
                mx = half if mx is None else jnp.maximum(mx, half)
            mx_sc[s] = jnp.maximum(mx_sc[s], mx)

    unrolled(pass1)
    m = [jnp.max(mx_sc[s], axis=-1, keepdims=True) for s in range(2)]
    l_sc[...] = jnp.zeros_like(l_sc)
    acc_sc[...] = jnp.zeros_like(acc_sc)

    def pass2(jjs):
        for jj in jjs:
            vt = two_tiles(v_sc, jj)
            for s in range(2):
                p = jnp.exp(s_sc[s, jj] - m[s])
                l_sc[s] += p[:, :KT] + p[:, KT:]
                acc_sc[s] += jnp.dot(p.astype(BF16), vt, preferred_element_type=F32)

    unrolled(pass2)
    l = [jnp.sum(l_sc[s], axis=-1, keepdims=True) for s in range(2)]
    lam = (jnp.exp(jnp.sum(lq1_ref[...] * lk1_ref[...], axis=-1, keepdims=True))
           - jnp.exp(jnp.sum(lq2_ref[...] * lk2_ref[...], axis=-1, keepdims=True)) + lam_init)
    o = acc_sc[0] / l[0] - lam * (acc_sc[1] / l[1])
    y = o * lax.rsqrt(jnp.mean(o * o, axis=-1, keepdims=True) + DIFF_SUBLN_EPS)
    o_ref[0, 0:live, :] = (y * sub_ref[...] * (1.0 - lam_init)).astype(o_ref.dtype)
    if live < tq:
        o_ref[0, live:tq, :] = jnp.zeros((tq - live, o_ref.shape[2]), o_ref.dtype)


def _attn_segment(qkv3, table, params, lam_init, tq, row0, n_tiles, live):
    B, Lp, W3 = qkv3.shape
    DW = W3 // 3
    hd = params[0].shape[1]
    hw = 2 * hd
    n_heads = DW // hw
    KT = ATTN_TILE
    nt = -(-Lp // KT)
    blk0 = row0 // tq
    kern = functools.partial(_attn_kernel, hd=hd, lam_init=lam_init, nt=nt,
                             tile0=row0 // KT, live=live)
    par = lambda n: pl.BlockSpec((1, n), lambda b, h, i: (0, 0))
    in_specs = [par(hd)] * 4 + [
        par(hw),
        pl.BlockSpec((1, tq, hw), lambda b, h, i: (b, blk0 + i, h)),
        pl.BlockSpec((1, Lp, hw), lambda b, h, i: (b, 0, n_heads + h)),
        pl.BlockSpec((1, Lp, hw), lambda b, h, i: (b, 0, 2 * n_heads + h)),
        pl.BlockSpec((2, table.shape[1], tq, KT), lambda b, h, i: (h, 0, 0, 0))]
    args = list(params) + [qkv3, qkv3, qkv3, table]
    max_pairs = (nt + 1) // 2
    return pl.pallas_call(
        kern,
        grid=(B, n_heads, n_tiles),
        in_specs=in_specs,
        out_specs=pl.BlockSpec((1, tq, hw), lambda b, h, i: (b, i, h)),
        out_shape=jax.ShapeDtypeStruct((B, n_tiles * tq, DW), BF16),
        scratch_shapes=[pltpu.VMEM((2, max_pairs, live, 2 * KT), F32), pltpu.VMEM((2, live, KT), F32),
                        pltpu.VMEM((2, live, KT), F32), pltpu.VMEM((2, live, hw), F32),
                        pltpu.VMEM((nt * KT, hw), BF16), pltpu.VMEM((nt * KT, hw), BF16)],
        compiler_params=_cparams(("parallel", "parallel", "arbitrary")),
        name="diff_attention_%d" % tq,
    )(*args)


def _attn_segments(Lp):
    big = ATTN_Q_TILE
    n_big = Lp // big
    segs = []
    if n_big:
        segs.append((big, 0, n_big))
    row0 = n_big * big
    n_mid = (Lp - row0) // ATTN_TILE
    if n_mid:
        segs.append((ATTN_TILE, row0, n_mid))
        row0 += n_mid * ATTN_TILE
    if Lp > row0:
        segs.append((Lp - row0, row0, 1))
    return segs


def _attn_call(zqkv, tables, lq1, lk1, lq2, lk2, subln, B, L, Lp, lam_init):
    vec = lambda a: a.reshape(1, -1)
    params = [vec(lq1), vec(lk1), vec(lq2), vec(lk2), vec(subln)]
    qkv3 = zqkv.reshape(B, Lp, zqkv.shape[-1])
    outs = []
    for tq, row0, n_tiles in _attn_segments(Lp):
        live = min(tq, _round_up(L - row0, 2 * SUBLANES)) if n_tiles == 1 else tq
        outs.append(_attn_segment(qkv3, tables[tq], params, lam_init, tq, row0, n_tiles, live))
    out = outs[0] if len(outs) == 1 else jnp.concatenate(outs, axis=1)
    return out.reshape(B * Lp, -1)


def _rwkv_prep_kernel(*refs, rw, has_vres):
    if has_vres:
        (u_ref, up_ref, w_ref, mu_ref, w0_ref, w2_ref, a0_ref, a2_ref, g2_ref, kk_ref, ka_ref, ones_ref,
         vf_ref, v0_ref, v1_ref, v2_ref,
         r_o, lw_o, k_o, v_o, a_o, b_o, g_o) = refs
    else:
        (u_ref, up_ref, w_ref, mu_ref, w0_ref, w2_ref, a0_ref, a2_ref, g2_ref, kk_ref, ka_ref, ones_ref,
         r_o, lw_o, k_o, v_o, a_o, b_o, g_o) = refs
    i = pl.program_id(1)
    halo = up_ref.shape[1]
    zz = jnp.dot(jnp.concatenate([up_ref[0], u_ref[0]], axis=0), w_ref[...], preferred_element_type=F32)
    z = zz[halo:]
    last_prev = jnp.where(i > 0, zz[halo - 1:halo, :], 0.0)
    row = lax.broadcasted_iota(jnp.int32, z.shape, 0)
    prev = jnp.where(row >= 1, pltpu.roll(z, 1, 0), last_prev)
    zs = z + (prev - z) * mu_ref[...]
    r = zs[:, 0:rw]
    kx = zs[:, rw:2 * rw]
    vx = zs[:, 2 * rw:3 * rw]
    wd = zs[:, 3 * rw:3 * rw + LANES]
    ad = zs[:, 3 * rw + LANES:3 * rw + 2 * LANES]
    gd = zs[:, 3 * rw + 2 * LANES:3 * rw + 3 * LANES]
    wl = w0_ref[...] + _dot(jnp.tanh(wd), w2_ref[...])
    sp = jnp.maximum(-wl, 0.0) + jnp.log(1.0 + jnp.exp(-jnp.abs(wl)))
    lw = -jnp.exp(-sp - 0.5)
    if has_vres:
        gate_v = _sigmoid(v0_ref[...] + _dot(_dot(vx, v1_ref[...]), v2_ref[...]))
        vx = vx + (vf_ref[0] - vx) * gate_v
    a = _sigmoid(a0_ref[...] + _dot(ad, a2_ref[...]))
    g = _dot(_sigmoid(gd), g2_ref[...])
    kk = kx * kk_ref[...]
    ss = _dot_exact_rhs(kk * kk, ones_ref[...])
    kk = kk * lax.rsqrt(jnp.maximum(ss, 1e-24))
    kmod = kx * (1.0 + (a - 1.0) * ka_ref[...])
    r_o[0] = r
    lw_o[0] = lw
    k_o[0] = kmod
    v_o[0] = vx
    a_o[0] = -kk
    b_o[0] = kk * a
    g_o[0] = g


def _pad_rows(w, n):
    return jnp.pad(w, ((0, n - w.shape[0]), (0, 0)))


def _rwkv_prep_call(u, w_rwkv, B, Lp, rw, mu_p, w0, w2, a0, a2, g2, k_k, k_a, ones_bd, v_first, v_res):
    D = u.shape[-1]
    u3 = u.reshape(B, Lp, D)
    HALO = 2 * SUBLANES
    tl = _pick_tile(Lp, 384, HALO)
    nb = tl // HALO
    has_vres = v_res is not None
    vec = lambda a: a.reshape(1, -1)
    full = lambda a: pl.BlockSpec(a.shape, lambda b, i: (0,) * a.ndim)
    w2p = _pad_rows(w2, LANES).astype(BF16)
    a2p = _pad_rows(a2, LANES).astype(BF16)
    g2p = _pad_rows(g2, LANES).astype(BF16)
    args = [u3, u3, w_rwkv, vec(mu_p), vec(w0), w2p, vec(a0), a2p, g2p, vec(k_k), vec(k_a), ones_bd]
    in_specs = [pl.BlockSpec((1, tl, D), lambda b, i: (b, i, 0)),
                pl.BlockSpec((1, HALO, D), lambda b, i: (b, jnp.maximum(i * nb - 1, 0), 0)),
                pl.BlockSpec(w_rwkv.shape, lambda b, i: (0, 0), pipeline_mode=pl.Buffered(1))]
    in_specs += [full(a) for a in args[3:]]
    if has_vres:
        v0, v1, v2 = v_res
        v1p = jnp.pad(v1, ((0, 0), (0, LANES - v1.shape[1]))).astype(BF16)
        v2p = _pad_rows(v2, LANES).astype(BF16)
        extra = [v_first, vec(v0), v1p, v2p]
        args += extra
        in_specs += [pl.BlockSpec((1, tl, rw), lambda b, i: (b, i, 0))] + [full(a) for a in extra[1:]]
    o_spec = pl.BlockSpec((1, tl, rw), lambda b, i: (b, i, 0))
    o_shape = jax.ShapeDtypeStruct((B, Lp, rw), F32)
    return pl.pallas_call(
        functools.partial(_rwkv_prep_kernel, rw=rw, has_vres=has_vres),
        grid=(B, Lp // tl),
        in_specs=in_specs,
        out_specs=[o_spec] * 7,
        out_shape=[o_shape] * 7,
        compiler_params=_cparams(("parallel", "arbitrary")),
        name="rwkv_prep",
    )(*args)


def _scan_kernel(r_ref, lw_ref, k_ref, v_ref, a_ref, b_ref, g_ref, rk_ref, lnw_ref, lnb_ref, ones_ref,
                 o_ref, h_sc, *, hd):
    nb = r_ref.shape[0]
    C = r_ref.shape[1]
    C2 = 2 * C
    WB = r_ref.shape[2]
    W = nb * WB
    c = pl.program_id(1)

    @pl.when(c == 0)
    def _():
        h_sc[...] = jnp.zeros_like(h_sc)

    side_by_side = lambda ref: jnp.concatenate([ref[i] for i in range(nb)], axis=1)
    r = side_by_side(r_ref)
    lw = side_by_side(lw_ref)
    k = side_by_side(k_ref)
    v = side_by_side(v_ref)
    a = side_by_side(a_ref)
    b = side_by_side(b_ref)

    cum = lw
    d = 1
    while d < C:
        cum = cum + _shift_down(cum, d)
        d *= 2
    tot = cum[C - 1:C, :]
    e_end = jnp.exp(tot - cum)
    e_neg = jnp.exp(-cum)
    at = a * jnp.exp(cum - lw)
    rt = r * jnp.exp(cum)
    bt = b * e_neg
    kt = k * e_neg
    bb = b * e_end
    kb = k * e_end
    p_c = jnp.exp(tot)

    lane = lax.broadcasted_iota(jnp.int32, (C, LANES), 1)
    first = lane < hd
    rr = lax.broadcasted_iota(jnp.int32, (C2, C2), 0)
    cc = lax.broadcasted_iota(jnp.int32, (C2, C2), 1)
    tr = jnp.where(rr >= C, rr - C, rr)
    tc = jnp.where(cc >= C, cc - C, cc)
    strict = tr > tc
    lower = tr >= tc
    eye = rr == cc

    def stack(x):
        return jnp.concatenate([jnp.where(first, x, 0.0), jnp.where(first, 0.0, x)], axis=0)

    pairs = range(W // LANES)
    stacked = lambda x: [stack(x[:, p * LANES:(p + 1) * LANES]) for p in pairs]
    s_a, s_r, s_b, s_k = stacked(at), stacked(rt), stacked(bt), stacked(kt)
    s_bb, s_kb, s_v = stacked(bb), stacked(kb), stacked(v)
    sc = [_dot_nt(jnp.concatenate([s_a[p], s_r[p]], axis=0), jnp.concatenate([s_b[p], s_k[p]], axis=0))
          for p in pairs]
    n = [jnp.where(strict, sc[p][:C2, :C2], 0.0) for p in pairs]
    a_ak = [jnp.where(strict, sc[p][:C2, C2:], 0.0) for p in pairs]
    m_rb = [jnp.where(lower, sc[p][C2:, :C2], 0.0) for p in pairs]
    m_rk = [jnp.where(lower, sc[p][C2:, C2:], 0.0) for p in pairs]
    xv = [_dot(jnp.concatenate([a_ak[p], m_rk[p], s_kb[p].T], axis=0), s_v[p]) for p in pairs]
    t_inv = [jnp.where(eye, 1.0, 0.0) + n[p] for p in pairs]
    pw = [_dot(n[p], n[p]) for p in pairs]
    step = 4
    while step < C:
        tp = [_dot(jnp.concatenate([t_inv[p], pw[p]], axis=0), pw[p]) for p in pairs]
        t_inv = [t_inv[p] + tp[p][:C2] for p in pairs]
        pw = [tp[p][C2:] for p in pairs]
        step *= 2
    t_inv = [t_inv[p] + _dot(t_inv[p], pw[p]) for p in pairs]
    au = [_dot(t_inv[p], jnp.concatenate([s_a[p], xv[p][:C2]], axis=1)) for p in pairs]
    mbu = [_dot(jnp.concatenate([m_rb[p], s_bb[p].T], axis=0), au[p]) for p in pairs]
    ys = []
    for p in pairs:
        r_hat = s_r[p] + mbu[p][:C2, :LANES]
        y0 = mbu[p][:C2, LANES:] + xv[p][C2:2 * C2]
        g_mat = jnp.where(eye, p_c[:, p * LANES:(p + 1) * LANES], 0.0) + mbu[p][C2:, :LANES]
        h_add = mbu[p][C2:, LANES:] + xv[p][2 * C2:]
        yh = _dot(jnp.concatenate([r_hat, g_mat], axis=0), h_sc[p])
        y = yh[:C2] + y0
        h_sc[p] = yh[C2:] + h_add
        ys.append(y[0:C] + y[C:C2])
    ones = ones_ref[...]
    inv_hd = 1.0 / hd
    ppb = WB // LANES
    n = nb * C
    y = jnp.concatenate([jnp.concatenate(ys[i * ppb:(i + 1) * ppb], axis=1) for i in range(nb)], axis=0)
    rk = jnp.concatenate([r_ref[i] * k_ref[i] * rk_ref[...] for i in range(nb)], axis=0)
    sums = _dot_exact_rhs(jnp.concatenate([y, rk], axis=0), ones)
    yc = y - sums[:n] * inv_hd
    var = _dot_exact_rhs(yc * yc, ones) * inv_hd
    yn = yc * lax.rsqrt(var + RWKV_LNX_EPS) * lnw_ref[...] + lnb_ref[...]
    for i in range(nb):
        rows = slice(i * C, (i + 1) * C)
        bonus = sums[n:][rows] * v_ref[i]
        o_ref[i] = ((yn[rows] + bonus) * g_ref[i]).astype(o_ref.dtype)


def _scan_call(streams, r_k, lnx_w, lnx_b, ones_bd, B, Lp, hd):
    r, lw, k, v, a, b, g = streams
    rw = r.shape[-1]
    C = CHUNK
    assert 2 * hd == LANES and rw % LANES == 0 and Lp % C == 0
    vec = lambda x: x.reshape(1, -1)
    nb = SCAN_BATCH if B % SCAN_BATCH == 0 else 1
    blk = pl.BlockSpec((nb, C, rw), lambda bi, c: (bi, c, 0))
    par = pl.BlockSpec((1, rw), lambda bi, c: (0, 0))
    out = pl.pallas_call(
        functools.partial(_scan_kernel, hd=hd),
        grid=(B // nb, Lp // C),
        in_specs=[blk] * 7 + [par] * 3 + [pl.BlockSpec((rw, rw), lambda bi, c: (0, 0))],
        out_specs=blk,
        out_shape=jax.ShapeDtypeStruct((B, Lp, rw), BF16),
        scratch_shapes=[pltpu.VMEM((nb * rw // LANES, LANES, LANES), F32)],
        compiler_params=_cparams(("parallel", "arbitrary")),
        name="rwkv_scan",
    )(r, lw, k, v, a, b, g, vec(r_k), vec(lnx_w), vec(lnx_b), ones_bd)
    return out.reshape(B * Lp, rw)


def _merge_kernel(yp_ref, yd_ref, yr_ref, g0_ref, g1_ref, g2_ref, pa_ref, pb_ref, pc_ref, o_ref):
    dot = lambda x, w: jnp.dot(x[...], w[...], preferred_element_type=F32)
    m = _sigmoid(g0_ref[...].astype(F32)) * dot(yp_ref, pa_ref)
    m = m + _sigmoid(g1_ref[...].astype(F32)) * dot(yd_ref, pb_ref)
    m = m + _sigmoid(g2_ref[...].astype(F32)) * dot(yr_ref, pc_ref)
    o_ref[...] = m.astype(o_ref.dtype)


def _merge_call(yp, yd, yr, zg, pa, pb, pc, layer, tm):
    M = yp.shape[0]
    D = pa.shape[2]
    rows = lambda a: pl.BlockSpec((tm, a.shape[1]), lambda i: (i, 0))
    full = lambda a: pl.BlockSpec((None,) + a.shape[1:], lambda i: (layer, 0, 0))
    gate = lambda n: pl.BlockSpec((tm, D), lambda i: (i, n))
    return pl.pallas_call(
        _merge_kernel,
        grid=(M // tm,),
        in_specs=[rows(yp), rows(yd), rows(yr), gate(0), gate(1), gate(2), full(pa), full(pb), full(pc)],
        out_specs=pl.BlockSpec((tm, D), lambda i: (i, 0)),
        out_shape=jax.ShapeDtypeStruct((M, D), BF16),
        compiler_params=_cparams(("parallel",)),
        name="gated_merge",
    )(yp, yd, yr, zg, zg, zg, pa, pb, pc)


def _ffn_up_kernel(u_ref, wv_ref, wg_ref, cwv_ref, cwg_ref, cbv_ref, cbg_ref, o_ref, *scratch, n_sub):
    xg_scs, xv_scs, a_scs = scratch[:n_sub], scratch[n_sub:2 * n_sub], scratch[2 * n_sub:]
    L = u_ref.shape[0]
    mb = L // n_sub
    st = mb // SUBLANES
    PAD = SUBLANES
    n_slab = a_scs[0].shape[0]
    k_in = math.sqrt(2.0 / math.pi)
    zeros = jnp.zeros((PAD, LANES), F32)
    for s in range(n_slab):
        xv_scs[0][s, 0:PAD, :] = zeros
        xg_scs[0][s, 0:PAD, :] = zeros

    def taps(cw_ref, cb_ref, s):
        sl = slice(s * LANES, (s + 1) * LANES)
        rows = [jnp.broadcast_to(cw_ref[j:j + 1, sl], (SUBLANES, LANES)) for j in range(FFN_CONV)]
        return rows, jnp.broadcast_to(cb_ref[:, sl], (SUBLANES, LANES))

    def tail(sb):
        lo = sb * mb
        xg_sc, xv_sc, a_sc = xg_scs[sb], xv_scs[sb], a_scs[sb]
        for s in range(n_slab):
            wg, bg = taps(cwg_ref, cbg_ref, s)
            wv, bv = taps(cwv_ref, cbv_ref, s)
            ld = lambda ref, j: ref[s, pl.ds(PAD + j, SUBLANES, stride=st), :]
            xg = [ld(xg_sc, j) for j in range(1 - FFN_CONV, 0)]
            xv = [ld(xv_sc, j) for j in range(1 - FFN_CONV, 0)]
            for j in range(st):
                xg.append(ld(xg_sc, j))
                xv.append(ld(xv_sc, j))
                gate = bg
                val = bv
                for t in range(FFN_CONV):
                    gate = gate + xg[j + t] * wg[t]
                    val = val + xv[j + t] * wv[t]
                inner = gate * (2.0 * k_in + (2.0 * k_in * 0.044715) * (gate * gate))
                act = gate / (1.0 + jnp.exp(-inner)) * val
                a_sc[s, pl.ds(j, SUBLANES, stride=st), :] = act
        for s in range(n_slab):
            o_ref[lo:lo + mb, s * LANES:(s + 1) * LANES] = a_sc[s].astype(o_ref.dtype)

    for sb in range(n_sub):
        lo = sb * mb
        u = u_ref[lo:lo + mb, :]
        for x_scs, w_ref in ((xg_scs, wg_ref), (xv_scs, wv_ref)):
            x = jnp.dot(u, w_ref[...], preferred_element_type=F32)
            for s in range(n_slab):
                xs = x[:, s * LANES:(s + 1) * LANES]
                x_scs[sb][s, PAD:PAD + mb, :] = xs
                if sb + 1 < n_sub:
                    x_scs[sb + 1][s, 0:PAD, :] = xs[mb - PAD:mb]
        if sb:
            tail(sb - 1)
    tail(n_sub - 1)


def _ffn_up_call(u, w_up, layer, conv_w, conv_b, B, Lp, tn):
    M, D = u.shape
    FF = w_up.shape[2] // 2
    nt = FF // tn
    n_sub = 4 if Lp % (4 * 2 * SUBLANES) == 0 else 1
    mb = Lp // n_sub
    return pl.pallas_call(
        functools.partial(_ffn_up_kernel, n_sub=n_sub),
        grid=(B, nt),
        in_specs=[pl.BlockSpec((Lp, D), lambda b, j: (b, 0)),
                  pl.BlockSpec((None, D, tn), lambda b, j: (layer, 0, j)),
                  pl.BlockSpec((None, D, tn), lambda b, j: (layer, 0, nt + j)),
                  pl.BlockSpec((FFN_CONV, tn), lambda b, j: (0, j)),
                  pl.BlockSpec((FFN_CONV, tn), lambda b, j: (0, nt + j)),
                  pl.BlockSpec((1, tn), lambda b, j: (0, j)),
                  pl.BlockSpec((1, tn), lambda b, j: (0, nt + j))],
        out_specs=pl.BlockSpec((Lp, tn), lambda b, j: (b, j)),
        out_shape=jax.ShapeDtypeStruct((M, FF), BF16),
        scratch_shapes=([pltpu.VMEM((tn // LANES, SUBLANES + mb, LANES), F32)] * (2 * n_sub)
                        + [pltpu.VMEM((tn // LANES, mb, LANES), F32)] * n_sub),
        compiler_params=_cparams(("parallel", "arbitrary")),
        name="ffn_up_conv_geglu",
    )(u, w_up, w_up, conv_w, conv_w, conv_b.reshape(1, -1), conv_b.reshape(1, -1))


def kernel(x, meta, rel_bias, norm_mix_pre, norm_mix_post, norm_ffn_pre, norm_ffn_post, w_in, pool_w, pool_scale, diff_lq1, diff_lk1, diff_lq2, diff_lk2, diff_subln, rwkv_mu, rwkv_w0, rwkv_w2, rwkv_a0, rwkv_a2, rwkv_g2, rwkv_kk, rwkv_ka, rwkv_rk, rwkv_lnx_w, rwkv_lnx_b, rwkv_v0, rwkv_v1, rwkv_v2, p_a, p_b, p_c, w_o, ffn_up, ffn_conv_w, ffn_conv_b, ffn_down):
    B, S, D = x.shape
    n_meta = meta.shape[0]
    depth = w_in.shape[0]
    L = S + n_meta
    Lp = _round_up(L, CHUNK)
    M = B * Lp

    PW = pool_scale.shape[1]
    DW = p_b.shape[1]
    RW = rwkv_w0.shape[1]
    hd_r = rwkv_rk.shape[2]
    dl, al, gl = rwkv_w2.shape[1], rwkv_a2.shape[1], rwkv_g2.shape[1]
    off_q = PW
    off_r = PW + 3 * DW
    off_g = off_r + 3 * RW + dl + al + gl
    assert off_g + 3 * D == w_in.shape[2]
    assert Lp == L or S % CHUNK == 0
    assert max(dl, al, gl, rwkv_v1.shape[2]) <= LANES

    h = jnp.concatenate([jnp.broadcast_to(meta.astype(x.dtype)[None], (B, n_meta, D)), x,
                         jnp.zeros((B, Lp - L, D), x.dtype)], axis=1).reshape(M, D)

    hid = np.arange(RW) // hd_r
    ones_bd = jnp.asarray((hid[:, None] == hid[None, :]).astype(np.float32), dtype=BF16)

    def pad_lora(a, axis):
        segs = [lax.slice_in_dim(a, 0, 3 * RW, axis=axis)]
        o = 3 * RW
        for n in (dl, al, gl):
            seg = lax.slice_in_dim(a, o, o + n, axis=axis)
            pad = [(0, 0)] * a.ndim
            pad[axis] = (0, LANES - n)
            segs.append(jnp.pad(seg, pad))
            o += n
        return jnp.concatenate(segs, axis=axis)

    tables = {}
    for tq, _, _ in _attn_segments(Lp):
        tt = _round_up(tq, ATTN_TILE)
        if tt not in tables:
            tables[tt] = _bias_table_call(rel_bias, n_meta, tt)
        tables[tq] = tables[tt]

    tm_big = _pick_tile(M, 1088, 16)
    tm_res = _pick_tile(M, 544, 16)
    u = _rms_call(h, norm_mix_pre[0], tm_res)
    v_first = None
    w_in_b, p_a_b, p_b_b, p_c_b, w_o_b, ffn_up_b, ffn_down_b = (
        a.astype(BF16) for a in (w_in, p_a, p_b, p_c, w_o, ffn_up, ffn_down))
    tn_qkv = _pick_tile(math.gcd(off_q, 3 * DW), 768, LANES)
    for l in range(depth):
        lam_init = 0.8 - 0.6 * math.exp(-0.3 * l)
        w_gate = w_in_b[l, :, off_g:]
        w_rwkv = pad_lora(w_in_b[l, :, off_r:off_g], 1)
        mu_p = pad_lora(rwkv_mu[l], 0)

        zg = _mm_call(u, w_gate, BF16, tm_big, _pick_tile(3 * D, 768, LANES), "in_proj_gate")
        zqkv = _mm_call(u, w_in_b, BF16, tm_big, tn_qkv, "in_proj_qkv", layer=l, col0=off_q, n_cols=3 * DW)
        zp = _mm_call(u, w_in_b, BF16, tm_big, _pick_tile(PW, 768, LANES), "in_proj_pool",
                      layer=l, col0=0, n_cols=PW)

        yp = _pool_call(zp, pool_w[l], pool_scale[l], B, Lp)
        yd = _attn_call(zqkv, tables, diff_lq1[l], diff_lk1[l], diff_lq2[l], diff_lk2[l],
                        diff_subln[l], B, L, Lp, lam_init)
        v_res = None if l == 0 else (rwkv_v0[l - 1], rwkv_v1[l - 1], rwkv_v2[l - 1])
        streams = _rwkv_prep_call(u, w_rwkv, B, Lp, RW, mu_p, rwkv_w0[l], rwkv_w2[l], rwkv_a0[l], rwkv_a2[l],
                                  rwkv_g2[l], rwkv_kk[l], rwkv_ka[l], ones_bd, v_first, v_res)
        if l == 0:
            v_first = streams[3]
        yr = _scan_call(streams, rwkv_rk[l], rwkv_lnx_w[l], rwkv_lnx_b[l], ones_bd, B, Lp, hd_r)

        merged = _merge_call(yp, yd, yr, zg, p_a_b, p_b_b, p_c_b, l, tm_res)
        h, u2 = _mm_res_call(merged, w_o_b, l, h, norm_mix_post[l], norm_ffn_pre[l], "out_proj_res")
        FF = ffn_down.shape[1]
        act = _ffn_up_call(u2, ffn_up_b, l, ffn_conv_w[l], ffn_conv_b[l], B, Lp,
                           _pick_tile(FF, 512, LANES))
        g_next = norm_mix_pre[l + 1] if l + 1 < depth else norm_mix_pre[l]
        h, u = _mm_res_call(act, ffn_down_b, l, h, norm_ffn_post[l], g_next, "ffn_down_res")
    return h.reshape(B, Lp, D)[:, n_meta:n_meta + S]
```

```python
import functools
import math

import numpy as np
import jax
import jax.numpy as jnp
from jax import lax
from jax.experimental import pallas as pl
from jax.experimental.pallas import tpu as pltpu

F32 = jnp.float32
BF16 = jnp.bfloat16

CHUNK = 64
NORM_EPS = 1e-6
NEG_INF = -1e30
POOL_WINDOWS = (2, 4, 8, 16)
DIFF_SUBLN_EPS = 1e-5
REL_MAX_DIST = 128
RWKV_LNX_EPS = 64e-5
FFN_CONV = 3

LANES = 128
SUBLANES = 8
ATTN_TILE = 128
ATTN_Q_TILE = 512
ATTN_UNROLL = 3
SCAN_BATCH = 4
VMEM_LIMIT = 56 * 1024 * 1024


def _cparams(sem):
    return pltpu.CompilerParams(dimension_semantics=sem, vmem_limit_bytes=VMEM_LIMIT)


def _round_up(x, m):
    return (x + m - 1) // m * m


def _pick_tile(total, target, mult):
    best = None
    for t in range(mult, min(total, target) + 1, mult):
        if total % t == 0:
            best = t
    assert best is not None, (total, target, mult)
    return best


def _shift_down(x, d):
    row = lax.broadcasted_iota(jnp.int32, x.shape, 0)
    return jnp.where(row >= d, pltpu.roll(x, d, 0), 0.0)


def _sigmoid(x):
    return 1.0 / (1.0 + jnp.exp(-x))


def _dot(a, b):
    return jnp.dot(a.astype(BF16), b.astype(BF16), preferred_element_type=F32)


def _dot_nt(a, b):
    return lax.dot_general(a.astype(BF16), b.astype(BF16), (((1,), (1,)), ((), ())),
                           preferred_element_type=F32)


def _split3(x):
    hi = x.astype(BF16)
    r1 = x - hi.astype(F32)
    mid = r1.astype(BF16)
    lo = (r1 - mid.astype(F32)).astype(BF16)
    return hi, mid, lo


def _dot_exact_rhs(x, ones_bf16):
    n = x.shape[0]
    pieces = jnp.concatenate(_split3(x), axis=0)
    res = jnp.dot(pieces, ones_bf16, preferred_element_type=F32)
    return res[:n] + res[n:2 * n] + res[2 * n:]


def _rms_kernel(x_ref, g_ref, o_ref):
    x = x_ref[...]
    y = x * lax.rsqrt(jnp.mean(x * x, axis=-1, keepdims=True) + NORM_EPS)
    o_ref[...] = (y * g_ref[...]).astype(o_ref.dtype)


def _rms_call(x, g, tm):
    M, D = x.shape
    return pl.pallas_call(
        _rms_kernel,
        grid=(M // tm,),
        in_specs=[pl.BlockSpec((tm, D), lambda i: (i, 0)),
                  pl.BlockSpec((1, D), lambda i: (0, 0))],
        out_specs=pl.BlockSpec((tm, D), lambda i: (i, 0)),
        out_shape=jax.ShapeDtypeStruct((M, D), BF16),
        compiler_params=_cparams(("parallel",)),
        name="rms_norm",
    )(x, g.reshape(1, D))


def _mm_kernel(a_ref, b_ref, o_ref):
    o_ref[...] = jnp.dot(a_ref[...], b_ref[...], preferred_element_type=F32).astype(o_ref.dtype)


def _mm_call(a, b, out_dtype, tm, tn, name):
    M, K = a.shape
    N = b.shape[1]
    return pl.pallas_call(
        _mm_kernel,
        grid=(M // tm, N // tn),
        in_specs=[pl.BlockSpec((tm, K), lambda i, j: (i, 0)),
                  pl.BlockSpec((K, tn), lambda i, j: (0, j))],
        out_specs=pl.BlockSpec((tm, tn), lambda i, j: (i, j)),
        out_shape=jax.ShapeDtypeStruct((M, N), out_dtype),
        compiler_params=_cparams(("parallel", "arbitrary")),
        name=name,
    )(a, b)


def _mm_res_kernel(a_ref, w_ref, h_ref, gp_ref, gn_ref, oh_ref, ou_ref):
    f = jnp.dot(a_ref[...], w_ref[...], preferred_element_type=F32)
    y = f * lax.rsqrt(jnp.mean(f * f, axis=-1, keepdims=True) + NORM_EPS) * gp_ref[...]
    hn = h_ref[...] + y
    oh_ref[...] = hn
    u = hn * lax.rsqrt(jnp.mean(hn * hn, axis=-1, keepdims=True) + NORM_EPS) * gn_ref[...]
    ou_ref[...] = u.astype(ou_ref.dtype)


def _mm_res_tile(M, K, D):
    budget = VMEM_LIMIT - 8 * 1024 * 1024
    best = None
    for tm in range(16, M + 1, 16):
        if M % tm:
            continue
        need = (K * D * 2
                + 2 * tm * K * 2
                + 2 * 2 * tm * D * 4
                + 2 * tm * D * 2
                + 2 * tm * D * 4)
        if need <= budget:
            best = tm
    assert best is not None
    return best


def _mm_res_call(a, w, layer, h, g_post, g_next, name):
    M, K = a.shape
    D = w.shape[2]
    tm = _mm_res_tile(M, K, D)
    return pl.pallas_call(
        _mm_res_kernel,
        grid=(M // tm,),
        in_specs=[pl.BlockSpec((tm, K), lambda i: (i, 0)),
                  pl.BlockSpec((None, K, D), lambda i: (layer, 0, 0), pipeline_mode=pl.Buffered(1)),
                  pl.BlockSpec((tm, D), lambda i: (i, 0)),
                  pl.BlockSpec((1, D), lambda i: (0, 0)),
                  pl.BlockSpec((1, D), lambda i: (0, 0))],
        out_specs=[pl.BlockSpec((tm, D), lambda i: (i, 0)),
                   pl.BlockSpec((tm, D), lambda i: (i, 0))],
        out_shape=[jax.ShapeDtypeStruct((M, D), F32),
                   jax.ShapeDtypeStruct((M, D), BF16)],
        compiler_params=_cparams(("parallel",)),
        name=name,
    )(a, w, h, g_post.reshape(1, D), g_next.reshape(1, D))


def _pool_kernel(z_ref, w_ref, s_ref, o_ref, *, gw):
    L = z_ref.shape[1]
    row = lax.broadcasted_iota(jnp.int32, (L, gw), 0)
    for gi, win in enumerate(POOL_WINDOWS):
        sl = slice(gi * gw, (gi + 1) * gw)
        zg = z_ref[0, :, sl].astype(F32)
        s = zg
        d = 1
        while d < win:
            s = s + _shift_down(s, d)
            d *= 2
        cnt = jnp.minimum(row + 1, win).astype(F32)
        pooled = s / cnt - zg
        y = _dot(pooled, w_ref[gi]) * s_ref[:, sl]
        o_ref[0, :, sl] = y.astype(o_ref.dtype)


def _pool_call(z, w, scale, B, Lp):
    PW = z.shape[-1]
    G = len(POOL_WINDOWS)
    gw = PW // G
    z3 = z.reshape(B, Lp, PW)
    out = pl.pallas_call(
        functools.partial(_pool_kernel, gw=gw),
        grid=(B,),
        in_specs=[pl.BlockSpec((1, Lp, PW), lambda b: (b, 0, 0)),
                  pl.BlockSpec((G, gw, gw), lambda b: (0, 0, 0)),
                  pl.BlockSpec((1, PW), lambda b: (0, 0))],
        out_specs=pl.BlockSpec((1, Lp, PW), lambda b: (b, 0, 0)),
        out_shape=jax.ShapeDtypeStruct((B, Lp, PW), BF16),
        compiler_params=_cparams(("parallel",)),
        name="pool_mixer",
    )(z3, w.astype(BF16), scale.reshape(1, PW))
    return out.reshape(B * Lp, PW)


def _rel_bucket(rel, n_buckets):
    half = n_buckets // 2
    max_exact = half // 2
    n = np.abs(rel)
    large = max_exact + (np.log(np.maximum(n, 1) / max_exact) / math.log(REL_MAX_DIST / max_exact)
                         * (half - max_exact)).astype(np.int32)
    large = np.minimum(large, half - 1)
    return ((rel > 0) * half + np.where(n < max_exact, n, large)).astype(np.int32)


def _bias_table_kernel(rb_ref, bucket_ref, pen_ref, o_ref, *, n_buckets):
    s = pl.program_id(0)
    bucket = bucket_ref[...]
    acc = pen_ref[...]
    for b in range(n_buckets):
        acc = acc + jnp.where(bucket == b, rb_ref[b, s], 0.0)
    o_ref[0] = acc


def _bias_table_call(rel_bias, n_meta, tq):
    n_buckets, n_sub = rel_bias.shape
    KT = ATTN_TILE
    assert tq % KT == 0 and KT % CHUNK == 0 and n_meta <= KT
    n_near = tq // KT + 2
    n_slabs = n_near + 2
    far = n_buckets // 2 - 1
    r = np.arange(tq)[:, None]
    c = np.arange(KT)[None, :]
    assert np.all(_rel_bucket(-(KT + 1 + np.arange(4 * KT)), n_buckets) == far)
    bucket = np.full((n_slabs, tq, KT), far, np.int32)
    pen = np.zeros((n_slabs, tq, KT), np.float32)
    pen[-1] = NEG_INF
    for idx in range(n_near):
        key = (idx - 1) * KT + c
        bucket[1 + idx] = _rel_bucket(key - r, n_buckets)
        visible = np.floor_divide(key - n_meta, CHUNK) <= np.floor_divide(r - n_meta, CHUNK)
        pen[1 + idx] = np.where(visible, 0.0, NEG_INF)
    blk = pl.BlockSpec((n_slabs, tq, KT), lambda s: (0, 0, 0))
    return pl.pallas_call(
        functools.partial(_bias_table_kernel, n_buckets=n_buckets),
        grid=(n_sub,),
        in_specs=[pl.BlockSpec(memory_space=pltpu.SMEM), blk, blk],
        out_specs=pl.BlockSpec((1, n_slabs, tq, KT), lambda s: (s, 0, 0, 0)),
        out_shape=jax.ShapeDtypeStruct((n_sub, n_slabs, tq, KT), F32),
        compiler_params=_cparams(("arbitrary",)),
        name="bias_table_%d" % tq,
    )(rel_bias, jnp.asarray(bucket), jnp.asarray(pen))


def _attn_kernel(*refs, hd, lam_init, nt, tile0, live):
    scale = hd ** -0.5
    (lq1_ref, lk1_ref, lq2_ref, lk2_ref, sub_ref, q_ref, k_ref, v_ref, w_ref,
     o_ref, s_sc, mx_sc, l_sc, acc_sc, k_sc, v_sc) = refs
    KT = ATTN_TILE
    tq = q_ref.shape[1]
    ratio = -(-tq // KT)
    Lk = k_ref.shape[1]

    @pl.when(pl.program_id(2) == 0)
    def _():
        k_sc[0:Lk, :] = k_ref[0]
        v_sc[0:Lk, :] = v_ref[0]
        if k_sc.shape[0] > Lk:
            k_sc[Lk:, :] = jnp.zeros((k_sc.shape[0] - Lk, k_sc.shape[1]), k_sc.dtype)
            v_sc[Lk:, :] = jnp.zeros((v_sc.shape[0] - Lk, v_sc.shape[1]), v_sc.dtype)
    n_slabs = w_ref.shape[1]
    i = pl.program_id(2)
    jt0 = tile0 + i * ratio
    n_need = jnp.minimum(jt0 + ratio + 1, nt)
    n_pairs = (n_need + 1) // 2
    q = q_ref[0, 0:live, :]

    def slab(j):
        return jnp.where(j < nt, jnp.clip(j - jt0 + 2, 0, n_slabs - 1), n_slabs - 1)

    def two_tiles(ref, jj):
        j0 = 2 * jj
        j1 = jnp.minimum(j0 + 1, nt - 1)
        t0 = ref[pl.ds(pl.multiple_of(j0 * KT, KT), KT), :]
        t1 = ref[pl.ds(pl.multiple_of(j1 * KT, KT), KT), :]
        return jnp.concatenate([t0, t1], axis=0)

    mx_sc[...] = jnp.full_like(mx_sc, NEG_INF)

    def unrolled(step):
        def group(t, carry):
            step([ATTN_UNROLL * t + r for r in range(ATTN_UNROLL)])
            return carry

        def single(jj, carry):
            step([jj])
            return carry
        n_groups = n_pairs // ATTN_UNROLL
        lax.fori_loop(0, n_groups, group, 0)
        lax.fori_loop(n_groups * ATTN_UNROLL, n_pairs, single, 0)

    def pass1(jjs):
        kts = [two_tiles(k_sc, jj) for jj in jjs]
        for s in range(2):
            mx = None
            for jj, kt in zip(jjs, kts):
                sc = _dot_nt(q[:, s * hd:(s + 1) * hd], kt[:, s * hd:(s + 1) * hd]) * scale
                sc = sc + jnp.concatenate([w_ref[s, slab(2 * jj), 0:live, :],
                                           w_ref[s, slab(2 * jj + 1), 0:live, :]], axis=1)
                s_sc[s, jj] = sc
                half = jnp.maximum(sc[:, :KT], sc[:, KT:])
                mx = half if mx is None else jnp.maximum(mx, half)
            mx_sc[s] = jnp.maximum(mx_sc[s], mx)

    unrolled(pass1)
    m = [jnp.max(mx_sc[s], axis=-1, keepdims=True) for s in range(2)]
    l_sc[...] = jnp.zeros_like(l_sc)
    acc_sc[...] = jnp.zeros_like(acc_sc)

    def pass2(jjs):
        for jj in jjs:
            vt = two_tiles(v_sc, jj)
            for s in range(2):
                p = jnp.exp(s_sc[s, jj] - m[s])
                l_sc[s] += p[:, :KT] + p[:, KT:]
                acc_sc[s] += jnp.dot(p.astype(BF16), vt, preferred_element_type=F32)

    unrolled(pass2)
    l = [jnp.sum(l_sc[s], axis=-1, keepdims=True) for s in range(2)]
    lam = (jnp.exp(jnp.sum(lq1_ref[...] * lk1_ref[...], axis=-1, keepdims=True))
           - jnp.exp(jnp.sum(lq2_ref[...] * lk2_ref[...], axis=-1, keepdims=True)) + lam_init)
    o = acc_sc[0] / l[0] - lam * (acc_sc[1] / l[1])
    y = o * lax.rsqrt(jnp.mean(o * o, axis=-1, keepdims=True) + DIFF_SUBLN_EPS)
    o_ref[0, 0:live, :] = (y * sub_ref[...] * (1.0 - lam_init)).astype(o_ref.dtype)
    if live < tq:
        o_ref[0, live:tq, :] = jnp.zeros((tq - live, o_ref.shape[2]), o_ref.dtype)


def _attn_segment(qkv3, table, params, lam_init, tq, row0, n_tiles, live):
    B, Lp, W3 = qkv3.shape
    DW = W3 // 3
    hd = params[0].shape[1]
    hw = 2 * hd
    n_heads = DW // hw
    KT = ATTN_TILE
    nt = -(-Lp // KT)
    blk0 = row0 // tq
    kern = functools.partial(_attn_kernel, hd=hd, lam_init=lam_init, nt=nt,
                             tile0=row0 // KT, live=live)
    par = lambda n: pl.BlockSpec((1, n), lambda b, h, i: (0, 0))
    in_specs = [par(hd)] * 4 + [
        par(hw),
        pl.BlockSpec((1, tq, hw), lambda b, h, i: (b, blk0 + i, h)),
        pl.BlockSpec((1, Lp, hw), lambda b, h, i: (b, 0, n_heads + h)),
        pl.BlockSpec((1, Lp, hw), lambda b, h, i: (b, 0, 2 * n_heads + h)),
        pl.BlockSpec((2, table.shape[1], tq, KT), lambda b, h, i: (h, 0, 0, 0))]
    args = list(params) + [qkv3, qkv3, qkv3, table]
    max_pairs = (nt + 1) // 2
    return pl.pallas_call(
        kern,
        grid=(B, n_heads, n_tiles),
        in_specs=in_specs,
        out_specs=pl.BlockSpec((1, tq, hw), lambda b, h, i: (b, i, h)),
        out_shape=jax.ShapeDtypeStruct((B, n_tiles * tq, DW), BF16),
        scratch_shapes=[pltpu.VMEM((2, max_pairs, live, 2 * KT), F32), pltpu.VMEM((2, live, KT), F32),
                        pltpu.VMEM((2, live, KT), F32), pltpu.VMEM((2, live, hw), F32),
                        pltpu.VMEM((nt * KT, hw), BF16), pltpu.VMEM((nt * KT, hw), BF16)],
        compiler_params=_cparams(("parallel", "parallel", "arbitrary")),
        name="diff_attention_%d" % tq,
    )(*args)


def _attn_segments(Lp):
    big = ATTN_Q_TILE
    n_big = Lp // big
    segs = []
    if n_big:
        segs.append((big, 0, n_big))
    row0 = n_big * big
    n_mid = (Lp - row0) // ATTN_TILE
    if n_mid:
        segs.append((ATTN_TILE, row0, n_mid))
        row0 += n_mid * ATTN_TILE
    if Lp > row0:
        segs.append((Lp - row0, row0, 1))
    return segs


def _attn_call(zqkv, tables, lq1, lk1, lq2, lk2, subln, B, L, Lp, lam_init):
    vec = lambda a: a.reshape(1, -1)
    params = [vec(lq1), vec(lk1), vec(lq2), vec(lk2), vec(subln)]
    qkv3 = zqkv.reshape(B, Lp, zqkv.shape[-1])
    outs = []
    for tq, row0, n_tiles in _attn_segments(Lp):
        live = min(tq, _round_up(L - row0, 2 * SUBLANES)) if n_tiles == 1 else tq
        outs.append(_attn_segment(qkv3, tables[tq], params, lam_init, tq, row0, n_tiles, live))
    out = outs[0] if len(outs) == 1 else jnp.concatenate(outs, axis=1)
    return out.reshape(B * Lp, -1)


def _rwkv_prep_kernel(*refs, rw, has_vres):
    if has_vres:
        (u_ref, up_ref, w_ref, mu_ref, w0_ref, w2_ref, a0_ref, a2_ref, g2_ref, kk_ref, ka_ref, ones_ref,
         vf_ref, v0_ref, v1_ref, v2_ref,
         r_o, lw_o, k_o, v_o, a_o, b_o, g_o) = refs
    else:
        (u_ref, up_ref, w_ref, mu_ref, w0_ref, w2_ref, a0_ref, a2_ref, g2_ref, kk_ref, ka_ref, ones_ref,
         r_o, lw_o, k_o, v_o, a_o, b_o, g_o) = refs
    i = pl.program_id(1)
    halo = up_ref.shape[1]
    zz = jnp.dot(jnp.concatenate([up_ref[0], u_ref[0]], axis=0), w_ref[...], preferred_element_type=F32)
    z = zz[halo:]
    last_prev = jnp.where(i > 0, zz[halo - 1:halo, :], 0.0)
    row = lax.broadcasted_iota(jnp.int32, z.shape, 0)
    prev = jnp.where(row >= 1, pltpu.roll(z, 1, 0), last_prev)
    zs = z + (prev - z) * mu_ref[...]
    r = zs[:, 0:rw]
    kx = zs[:, rw:2 * rw]
    vx = zs[:, 2 * rw:3 * rw]
    wd = zs[:, 3 * rw:3 * rw + LANES]
    ad = zs[:, 3 * rw + LANES:3 * rw + 2 * LANES]
    gd = zs[:, 3 * rw + 2 * LANES:3 * rw + 3 * LANES]
    wl = w0_ref[...] + _dot(jnp.tanh(wd), w2_ref[...])
    sp = jnp.maximum(-wl, 0.0) + jnp.log(1.0 + jnp.exp(-jnp.abs(wl)))
    lw = -jnp.exp(-sp - 0.5)
    if has_vres:
        gate_v = _sigmoid(v0_ref[...] + _dot(_dot(vx, v1_ref[...]), v2_ref[...]))
        vx = vx + (vf_ref[0] - vx) * gate_v
    a = _sigmoid(a0_ref[...] + _dot(ad, a2_ref[...]))
    g = _dot(_sigmoid(gd), g2_ref[...])
    kk = kx * kk_ref[...]
    ss = _dot_exact_rhs(kk * kk, ones_ref[...])
    kk = kk * lax.rsqrt(jnp.maximum(ss, 1e-24))
    kmod = kx * (1.0 + (a - 1.0) * ka_ref[...])
    r_o[0] = r
    lw_o[0] = lw
    k_o[0] = kmod
    v_o[0] = vx
    a_o[0] = -kk
    b_o[0] = kk * a
    g_o[0] = g


def _pad_rows(w, n):
    return jnp.pad(w, ((0, n - w.shape[0]), (0, 0)))


def _rwkv_prep_call(u, w_rwkv, B, Lp, rw, mu_p, w0, w2, a0, a2, g2, k_k, k_a, ones_bd, v_first, v_res):
    D = u.shape[-1]
    u3 = u.reshape(B, Lp, D)
    HALO = 2 * SUBLANES
    tl = _pick_tile(Lp, 384, HALO)
    nb = tl // HALO
    has_vres = v_res is not None
    vec = lambda a: a.reshape(1, -1)
    full = lambda a: pl.BlockSpec(a.shape, lambda b, i: (0,) * a.ndim)
    w2p = _pad_rows(w2, LANES).astype(BF16)
    a2p = _pad_rows(a2, LANES).astype(BF16)
    g2p = _pad_rows(g2, LANES).astype(BF16)
    args = [u3, u3, w_rwkv, vec(mu_p), vec(w0), w2p, vec(a0), a2p, g2p, vec(k_k), vec(k_a), ones_bd]
    in_specs = [pl.BlockSpec((1, tl, D), lambda b, i: (b, i, 0)),
                pl.BlockSpec((1, HALO, D), lambda b, i: (b, jnp.maximum(i * nb - 1, 0), 0)),
                pl.BlockSpec(w_rwkv.shape, lambda b, i: (0, 0), pipeline_mode=pl.Buffered(1))]
    in_specs += [full(a) for a in args[3:]]
    if has_vres:
        v0, v1, v2 = v_res
        v1p = jnp.pad(v1, ((0, 0), (0, LANES - v1.shape[1]))).astype(BF16)
        v2p = _pad_rows(v2, LANES).astype(BF16)
        extra = [v_first, vec(v0), v1p, v2p]
        args += extra
        in_specs += [pl.BlockSpec((1, tl, rw), lambda b, i: (b, i, 0))] + [full(a) for a in extra[1:]]
    o_spec = pl.BlockSpec((1, tl, rw), lambda b, i: (b, i, 0))
    o_shape = jax.ShapeDtypeStruct((B, Lp, rw), F32)
    return pl.pallas_call(
        functools.partial(_rwkv_prep_kernel, rw=rw, has_vres=has_vres),
        grid=(B, Lp // tl),
        in_specs=in_specs,
        out_specs=[o_spec] * 7,
        out_shape=[o_shape] * 7,
        compiler_params=_cparams(("parallel", "arbitrary")),
        name="rwkv_prep",
    )(*args)


def _scan_kernel(r_ref, lw_ref, k_ref, v_ref, a_ref, b_ref, g_ref, rk_ref, lnw_ref, lnb_ref, ones_ref,
                 o_ref, h_sc, *, hd):
    nb = r_ref.shape[0]
    C = r_ref.shape[1]
    C2 = 2 * C
    WB = r_ref.shape[2]
    W = nb * WB
    c = pl.program_id(1)

    @pl.when(c == 0)
    def _():
        h_sc[...] = jnp.zeros_like(h_sc)

    side_by_side = lambda ref: jnp.concatenate([ref[i] for i in range(nb)], axis=1)
    r = side_by_side(r_ref)
    lw = side_by_side(lw_ref)
    k = side_by_side(k_ref)
    v = side_by_side(v_ref)
    a = side_by_side(a_ref)
    b = side_by_side(b_ref)

    cum = lw
    d = 1
    while d < C:
        cum = cum + _shift_down(cum, d)
        d *= 2
    tot = cum[C - 1:C, :]
    e_end = jnp.exp(tot - cum)
    e_neg = jnp.exp(-cum)
    at = a * jnp.exp(cum - lw)
    rt = r * jnp.exp(cum)
    bt = b * e_neg
    kt = k * e_neg
    bb = b * e_end
    kb = k * e_end
    p_c = jnp.exp(tot)

    lane = lax.broadcasted_iota(jnp.int32, (C, LANES), 1)
    first = lane < hd
    rr = lax.broadcasted_iota(jnp.int32, (C2, C2), 0)
    cc = lax.broadcasted_iota(jnp.int32, (C2, C2), 1)
    tr = jnp.where(rr >= C, rr - C, rr)
    tc = jnp.where(cc >= C, cc - C, cc)
    strict = tr > tc
    lower = tr >= tc
    eye = rr == cc

    def stack(x):
        return jnp.concatenate([jnp.where(first, x, 0.0), jnp.where(first, 0.0, x)], axis=0)

    pairs = range(W // LANES)
    stacked = lambda x: [stack(x[:, p * LANES:(p + 1) * LANES]) for p in pairs]
    s_a, s_r, s_b, s_k = stacked(at), stacked(rt), stacked(bt), stacked(kt)
    s_bb, s_kb, s_v = stacked(bb), stacked(kb), stacked(v)
    sc = [_dot_nt(jnp.concatenate([s_a[p], s_r[p]], axis=0), jnp.concatenate([s_b[p], s_k[p]], axis=0))
          for p in pairs]
    n = [jnp.where(strict, sc[p][:C2, :C2], 0.0) for p in pairs]
    a_ak = [jnp.where(strict, sc[p][:C2, C2:], 0.0) for p in pairs]
    m_rb = [jnp.where(lower, sc[p][C2:, :C2], 0.0) for p in pairs]
    m_rk = [jnp.where(lower, sc[p][C2:, C2:], 0.0) for p in pairs]
    xv = [_dot(jnp.concatenate([a_ak[p], m_rk[p], s_kb[p].T], axis=0), s_v[p]) for p in pairs]
    t_inv = [jnp.where(eye, 1.0, 0.0) + n[p] for p in pairs]
    pw = [_dot(n[p], n[p]) for p in pairs]
    step = 4
    while step < C:
        tp = [_dot(jnp.concatenate([t_inv[p], pw[p]], axis=0), pw[p]) for p in pairs]
        t_inv = [t_inv[p] + tp[p][:C2] for p in pairs]
        pw = [tp[p][C2:] for p in pairs]
        step *= 2
    t_inv = [t_inv[p] + _dot(t_inv[p], pw[p]) for p in pairs]
    au = [_dot(t_inv[p], jnp.concatenate([s_a[p], xv[p][:C2]], axis=1)) for p in pairs]
    mbu = [_dot(jnp.concatenate([m_rb[p], s_bb[p].T], axis=0), au[p]) for p in pairs]
    ys = []
    for p in pairs:
        r_hat = s_r[p] + mbu[p][:C2, :LANES]
        y0 = mbu[p][:C2, LANES:] + xv[p][C2:2 * C2]
        g_mat = jnp.where(eye, p_c[:, p * LANES:(p + 1) * LANES], 0.0) + mbu[p][C2:, :LANES]
        h_add = mbu[p][C2:, LANES:] + xv[p][2 * C2:]
        yh = _dot(jnp.concatenate([r_hat, g_mat], axis=0), h_sc[p])
        y = yh[:C2] + y0
        h_sc[p] = yh[C2:] + h_add
        ys.append(y[0:C] + y[C:C2])
    ones = ones_ref[...]
    inv_hd = 1.0 / hd
    ppb = WB // LANES
    n = nb * C
    y = jnp.concatenate([jnp.concatenate(ys[i * ppb:(i + 1) * ppb], axis=1) for i in range(nb)], axis=0)
    rk = jnp.concatenate([r_ref[i] * k_ref[i] * rk_ref[...] for i in range(nb)], axis=0)
    sums = _dot_exact_rhs(jnp.concatenate([y, rk], axis=0), ones)
    yc = y - sums[:n] * inv_hd
    var = _dot_exact_rhs(yc * yc, ones) * inv_hd
    yn = yc * lax.rsqrt(var + RWKV_LNX_EPS) * lnw_ref[...] + lnb_ref[...]
    for i in range(nb):
        rows = slice(i * C, (i + 1) * C)
        bonus = sums[n:][rows] * v_ref[i]
        o_ref[i] = ((yn[rows] + bonus) * g_ref[i]).astype(o_ref.dtype)


def _scan_call(streams, r_k, lnx_w, lnx_b, ones_bd, B, Lp, hd):
    r, lw, k, v, a, b, g = streams
    rw = r.shape[-1]
    C = CHUNK
    assert 2 * hd == LANES and rw % LANES == 0 and Lp % C == 0
    vec = lambda x: x.reshape(1, -1)
    nb = SCAN_BATCH if B % SCAN_BATCH == 0 else 1
    blk = pl.BlockSpec((nb, C, rw), lambda bi, c: (bi, c, 0))
    par = pl.BlockSpec((1, rw), lambda bi, c: (0, 0))
    out = pl.pallas_call(
        functools.partial(_scan_kernel, hd=hd),
        grid=(B // nb, Lp // C),
        in_specs=[blk] * 7 + [par] * 3 + [pl.BlockSpec((rw, rw), lambda bi, c: (0, 0))],
        out_specs=blk,
        out_shape=jax.ShapeDtypeStruct((B, Lp, rw), BF16),
        scratch_shapes=[pltpu.VMEM((nb * rw // LANES, LANES, LANES), F32)],
        compiler_params=_cparams(("parallel", "arbitrary")),
        name="rwkv_scan",
    )(r, lw, k, v, a, b, g, vec(r_k), vec(lnx_w), vec(lnx_b), ones_bd)
    return out.reshape(B * Lp, rw)


def _merge_kernel(yp_ref, yd_ref, yr_ref, g0_ref, g1_ref, g2_ref, pa_ref, pb_ref, pc_ref, o_ref):
    dot = lambda x, w: jnp.dot(x[...], w[...], preferred_element_type=F32)
    m = _sigmoid(g0_ref[...].astype(F32)) * dot(yp_ref, pa_ref)
    m = m + _sigmoid(g1_ref[...].astype(F32)) * dot(yd_ref, pb_ref)
    m = m + _sigmoid(g2_ref[...].astype(F32)) * dot(yr_ref, pc_ref)
    o_ref[...] = m.astype(o_ref.dtype)


def _merge_call(yp, yd, yr, zg, pa, pb, pc, layer, tm):
    M = yp.shape[0]
    D = pa.shape[2]
    rows = lambda a: pl.BlockSpec((tm, a.shape[1]), lambda i: (i, 0))
    full = lambda a: pl.BlockSpec((None,) + a.shape[1:], lambda i: (layer, 0, 0))
    gate = lambda n: pl.BlockSpec((tm, D), lambda i: (i, n))
    return pl.pallas_call(
        _merge_kernel,
        grid=(M // tm,),
        in_specs=[rows(yp), rows(yd), rows(yr), gate(0), gate(1), gate(2), full(pa), full(pb), full(pc)],
        out_specs=pl.BlockSpec((tm, D), lambda i: (i, 0)),
        out_shape=jax.ShapeDtypeStruct((M, D), BF16),
        compiler_params=_cparams(("parallel",)),
        name="gated_merge",
    )(yp, yd, yr, zg, zg, zg, pa, pb, pc)


def _ffn_up_kernel(u_ref, wv_ref, wg_ref, cwv_ref, cwg_ref, cbv_ref, cbg_ref, o_ref, *scratch, n_sub):
    xg_scs, xv_scs, a_scs = scratch[:n_sub], scratch[n_sub:2 * n_sub], scratch[2 * n_sub:]
    L = u_ref.shape[0]
    mb = L // n_sub
    st = mb // SUBLANES
    PAD = SUBLANES
    n_slab = a_scs[0].shape[0]
    k_in = math.sqrt(2.0 / math.pi)
    zeros = jnp.zeros((PAD, LANES), F32)
    for s in range(n_slab):
        xv_scs[0][s, 0:PAD, :] = zeros
        xg_scs[0][s, 0:PAD, :] = zeros

    def taps(cw_ref, cb_ref, s):
        sl = slice(s * LANES, (s + 1) * LANES)
        rows = [jnp.broadcast_to(cw_ref[j:j + 1, sl], (SUBLANES, LANES)) for j in range(FFN_CONV)]
        return rows, jnp.broadcast_to(cb_ref[:, sl], (SUBLANES, LANES))

    def tail(sb):
        lo = sb * mb
        xg_sc, xv_sc, a_sc = xg_scs[sb], xv_scs[sb], a_scs[sb]
        for s in range(n_slab):
            wg, bg = taps(cwg_ref, cbg_ref, s)
            wv, bv = taps(cwv_ref, cbv_ref, s)
            ld = lambda ref, j: ref[s, pl.ds(PAD + j, SUBLANES, stride=st), :]
            xg = [ld(xg_sc, j) for j in range(1 - FFN_CONV, 0)]
            xv = [ld(xv_sc, j) for j in range(1 - FFN_CONV, 0)]
            for j in range(st):
                xg.append(ld(xg_sc, j))
                xv.append(ld(xv_sc, j))
                gate = bg
                val = bv
                for t in range(FFN_CONV):
                    gate = gate + xg[j + t] * wg[t]
                    val = val + xv[j + t] * wv[t]
                inner = gate * (2.0 * k_in + (2.0 * k_in * 0.044715) * (gate * gate))
                act = gate / (1.0 + jnp.exp(-inner)) * val
                a_sc[s, pl.ds(j, SUBLANES, stride=st), :] = act
        for s in range(n_slab):
            o_ref[lo:lo + mb, s * LANES:(s + 1) * LANES] = a_sc[s].astype(o_ref.dtype)

    for sb in range(n_sub):
        lo = sb * mb
        u = u_ref[lo:lo + mb, :]
        for x_scs, w_ref in ((xg_scs, wg_ref), (xv_scs, wv_ref)):
            x = jnp.dot(u, w_ref[...], preferred_element_type=F32)
            for s in range(n_slab):
                xs = x[:, s * LANES:(s + 1) * LANES]
                x_scs[sb][s, PAD:PAD + mb, :] = xs
                if sb + 1 < n_sub:
                    x_scs[sb + 1][s, 0:PAD, :] = xs[mb - PAD:mb]
        if sb:
            tail(sb - 1)
    tail(n_sub - 1)


def _ffn_up_call(u, w_up, layer, conv_w, conv_b, B, Lp, tn):
    M, D = u.shape
    FF = w_up.shape[2] // 2
    nt = FF // tn
    n_sub = 4 if Lp % (4 * 2 * SUBLANES) == 0 else 1
    mb = Lp // n_sub
    return pl.pallas_call(
        functools.partial(_ffn_up_kernel, n_sub=n_sub),
        grid=(B, nt),
        in_specs=[pl.BlockSpec((Lp, D), lambda b, j: (b, 0)),
                  pl.BlockSpec((None, D, tn), lambda b, j: (layer, 0, j)),
                  pl.BlockSpec((None, D, tn), lambda b, j: (layer, 0, nt + j)),
                  pl.BlockSpec((FFN_CONV, tn), lambda b, j: (0, j)),
                  pl.BlockSpec((FFN_CONV, tn), lambda b, j: (0, nt + j)),
                  pl.BlockSpec((1, tn), lambda b, j: (0, j)),
                  pl.BlockSpec((1, tn), lambda b, j: (0, nt + j))],
        out_specs=pl.BlockSpec((Lp, tn), lambda b, j: (b, j)),
        out_shape=jax.ShapeDtypeStruct((M, FF), BF16),
        scratch_shapes=([pltpu.VMEM((tn // LANES, SUBLANES + mb, LANES), F32)] * (2 * n_sub)
                        + [pltpu.VMEM((tn // LANES, mb, LANES), F32)] * n_sub),
        compiler_params=_cparams(("parallel", "arbitrary")),
        name="ffn_up_conv_geglu",
    )(u, w_up, w_up, conv_w, conv_w, conv_b.reshape(1, -1), conv_b.reshape(1, -1))


def kernel(x, meta, rel_bias, norm_mix_pre, norm_mix_post, norm_ffn_pre, norm_ffn_post, w_in, pool_w, pool_scale, diff_lq1, diff_lk1, diff_lq2, diff_lk2, diff_subln, rwkv_mu, rwkv_w0, rwkv_w2, rwkv_a0, rwkv_a2, rwkv_g2, rwkv_kk, rwkv_ka, rwkv_rk, rwkv_lnx_w, rwkv_lnx_b, rwkv_v0, rwkv_v1, rwkv_v2, p_a, p_b, p_c, w_o, ffn_up, ffn_conv_w, ffn_conv_b, ffn_down):
    B, S, D = x.shape
    n_meta = meta.shape[0]
    depth = w_in.shape[0]
    L = S + n_meta
    Lp = _round_up(L, CHUNK)
    M = B * Lp

    PW = pool_scale.shape[1]
    DW = p_b.shape[1]
    RW = rwkv_w0.shape[1]
    hd_r = rwkv_rk.shape[2]
    dl, al, gl = rwkv_w2.shape[1], rwkv_a2.shape[1], rwkv_g2.shape[1]
    off_q = PW
    off_r = PW + 3 * DW
    off_g = off_r + 3 * RW + dl + al + gl
    assert off_g + 3 * D == w_in.shape[2]
    assert Lp == L or S % CHUNK == 0
    assert max(dl, al, gl, rwkv_v1.shape[2]) <= LANES

    h = jnp.concatenate([jnp.broadcast_to(meta.astype(x.dtype)[None], (B, n_meta, D)), x,
                         jnp.zeros((B, Lp - L, D), x.dtype)], axis=1).reshape(M, D)

    hid = np.arange(RW) // hd_r
    ones_bd = jnp.asarray((hid[:, None] == hid[None, :]).astype(np.float32), dtype=BF16)

    def pad_lora(a, axis):
        segs = [lax.slice_in_dim(a, 0, 3 * RW, axis=axis)]
        o = 3 * RW
        for n in (dl, al, gl):
            seg = lax.slice_in_dim(a, o, o + n, axis=axis)
            pad = [(0, 0)] * a.ndim
            pad[axis] = (0, LANES - n)
            segs.append(jnp.pad(seg, pad))
            o += n
        return jnp.concatenate(segs, axis=axis)

    tables = {}
    for tq, _, _ in _attn_segments(Lp):
        tt = _round_up(tq, ATTN_TILE)
        if tt not in tables:
            tables[tt] = _bias_table_call(rel_bias, n_meta, tt)
        tables[tq] = tables[tt]

    tm_big = _pick_tile(M, 1088, 16)
    tm_res = _pick_tile(M, 544, 16)
    u = _rms_call(h, norm_mix_pre[0], tm_res)
    v_first = None
    w_in_b, p_a_b, p_b_b, p_c_b, w_o_b, ffn_up_b, ffn_down_b = (
        a.astype(BF16) for a in (w_in, p_a, p_b, p_c, w_o, ffn_up, ffn_down))
    for l in range(depth):
        lam_init = 0.8 - 0.6 * math.exp(-0.3 * l)
        w_gate = w_in_b[l, :, off_g:]
        w_rwkv = pad_lora(w_in_b[l, :, off_r:off_g], 1)
        mu_p = pad_lora(rwkv_mu[l], 0)

        zg = _mm_call(u, w_gate, BF16, tm_big, _pick_tile(3 * D, 768, LANES), "in_proj_gate")
        zqkv = _mm_call(u, w_in_b[l, :, off_q:off_r], BF16, tm_big, _pick_tile(3 * DW, 768, LANES),
                        "in_proj_qkv")
        zp = _mm_call(u, w_in_b[l, :, :PW], BF16, tm_big, _pick_tile(PW, 768, LANES), "in_proj_pool")

        yp = _pool_call(zp, pool_w[l], pool_scale[l], B, Lp)
        yd = _attn_call(zqkv, tables, diff_lq1[l], diff_lk1[l], diff_lq2[l], diff_lk2[l],
                        diff_subln[l], B, L, Lp, lam_init)
        v_res = None if l == 0 else (rwkv_v0[l - 1], rwkv_v1[l - 1], rwkv_v2[l - 1])
        streams = _rwkv_prep_call(u, w_rwkv, B, Lp, RW, mu_p, rwkv_w0[l], rwkv_w2[l], rwkv_a0[l], rwkv_a2[l],
                                  rwkv_g2[l], rwkv_kk[l], rwkv_ka[l], ones_bd, v_first, v_res)
        if l == 0:
            v_first = streams[3]
        yr = _scan_call(streams, rwkv_rk[l], rwkv_lnx_w[l], rwkv_lnx_b[l], ones_bd, B, Lp, hd_r)

        merged = _merge_call(yp, yd, yr, zg, p_a_b, p_b_b, p_c_b, l, tm_res)
        h, u2 = _mm_res_call(merged, w_o_b, l, h, norm_mix_post[l], norm_ffn_pre[l], "out_proj_res")
        FF = ffn_down.shape[1]
        act = _ffn_up_call(u2, ffn_up_b, l, ffn_conv_w[l], ffn_conv_b[l], B, Lp,
                           _pick_tile(FF, 512, LANES))
        g_next = norm_mix_pre[l + 1] if l + 1 < depth else norm_mix_pre[l]
        h, u = _mm_res_call(act, ffn_down_b, l, h, norm_ffn_post[l], g_next, "ffn_down_res")
    return h.reshape(B, Lp, D)[:, n_meta:n_meta + S]
```

```python
import functools
import math

import numpy as np
import jax
import jax.numpy as jnp
from jax import lax
from jax.experimental import pallas as pl
from jax.experimental.pallas import tpu as pltpu

F32 = jnp.float32
BF16 = jnp.bfloat16

CHUNK = 64
NORM_EPS = 1e-6
NEG_INF = -1e30
POOL_WINDOWS = (2, 4, 8, 16)
DIFF_SUBLN_EPS = 1e-5
REL_MAX_DIST = 128
RWKV_LNX_EPS = 64e-5
FFN_CONV = 3

LANES = 128
SUBLANES = 8
ATTN_TILE = 128
ATTN_Q_TILE = 512
ATTN_UNROLL = 3
SCAN_BATCH = 4
BF16_ROWS = 2 * SUBLANES
VMEM_LIMIT = 56 * 1024 * 1024
VMEM_HEADROOM = 8 * 1024 * 1024

MM_ROWS = 2176
MM_COLS = 768
EW_ROWS = 544
PREP_ROWS = 384
FFN_COLS = 512
FFN_SUBBLOCKS = 4


def _cparams(sem):
    return pltpu.CompilerParams(dimension_semantics=sem, vmem_limit_bytes=VMEM_LIMIT)


def _round_up(x, m):
    return (x + m - 1) // m * m


def _pick_tile(total, target, mult):
    best = None
    for t in range(mult, min(total, target) + 1, mult):
        if total % t == 0:
            best = t
    assert best is not None, (total, target, mult)
    return best


def _shift_down(x, d):
    row = lax.broadcasted_iota(jnp.int32, x.shape, 0)
    return jnp.where(row >= d, pltpu.roll(x, d, 0), 0.0)


def _sigmoid(x):
    return 1.0 / (1.0 + jnp.exp(-x))


def _dot(a, b):
    return jnp.dot(a.astype(BF16), b.astype(BF16), preferred_element_type=F32)


def _dot_nt(a, b):
    return lax.dot_general(a.astype(BF16), b.astype(BF16), (((1,), (1,)), ((), ())),
                           preferred_element_type=F32)


def _split3(x):
    hi = x.astype(BF16)
    r1 = x - hi.astype(F32)
    mid = r1.astype(BF16)
    lo = (r1 - mid.astype(F32)).astype(BF16)
    return hi, mid, lo


def _dot_exact_rhs(x, ones_bf16):
    n = x.shape[0]
    pieces = jnp.concatenate(_split3(x), axis=0)
    res = jnp.dot(pieces, ones_bf16, preferred_element_type=F32)
    return res[:n] + res[n:2 * n] + res[2 * n:]


def _rms_kernel(x_ref, g_ref, o_ref):
    x = x_ref[...]
    y = x * lax.rsqrt(jnp.mean(x * x, axis=-1, keepdims=True) + NORM_EPS)
    o_ref[...] = (y * g_ref[...]).astype(o_ref.dtype)


def _rms_call(x, g, tm):
    M, D = x.shape
    return pl.pallas_call(
        _rms_kernel,
        grid=(M // tm,),
        in_specs=[pl.BlockSpec((tm, D), lambda i: (i, 0)),
                  pl.BlockSpec((1, D), lambda i: (0, 0))],
        out_specs=pl.BlockSpec((tm, D), lambda i: (i, 0)),
        out_shape=jax.ShapeDtypeStruct((M, D), BF16),
        compiler_params=_cparams(("parallel",)),
        name="rms_norm",
    )(x, g.reshape(1, D))


def _mm_kernel(a_ref, b_ref, o_ref):
    o_ref[...] = jnp.dot(a_ref[...], b_ref[...], preferred_element_type=F32).astype(o_ref.dtype)


def _mm_call(a, b, out_dtype, tm, tn, name):
    M, K = a.shape
    N = b.shape[1]
    return pl.pallas_call(
        _mm_kernel,
        grid=(M // tm, N // tn),
        in_specs=[pl.BlockSpec((tm, K), lambda i, j: (i, 0)),
                  pl.BlockSpec((K, tn), lambda i, j: (0, j))],
        out_specs=pl.BlockSpec((tm, tn), lambda i, j: (i, j)),
        out_shape=jax.ShapeDtypeStruct((M, N), out_dtype),
        compiler_params=_cparams(("parallel", "arbitrary")),
        name=name,
    )(a, b)


def _mm_res_kernel(a_ref, w_ref, h_ref, gp_ref, gn_ref, oh_ref, ou_ref):
    f = jnp.dot(a_ref[...], w_ref[...], preferred_element_type=F32)
    y = f * lax.rsqrt(jnp.mean(f * f, axis=-1, keepdims=True) + NORM_EPS) * gp_ref[...]
    hn = h_ref[...] + y
    oh_ref[...] = hn
    u = hn * lax.rsqrt(jnp.mean(hn * hn, axis=-1, keepdims=True) + NORM_EPS) * gn_ref[...]
    ou_ref[...] = u.astype(ou_ref.dtype)


def _mm_res_tile(M, K, D):
    budget = VMEM_LIMIT - VMEM_HEADROOM
    best = None
    for tm in range(BF16_ROWS, M + 1, BF16_ROWS):
        if M % tm:
            continue
        need = (K * D * 2
                + 2 * tm * K * 2
                + 2 * 2 * tm * D * 4
                + 2 * tm * D * 2
                + 2 * tm * D * 4)
        if need <= budget:
            best = tm
    assert best is not None
    return best


def _mm_res_call(a, w, layer, h, g_post, g_next, name):
    M, K = a.shape
    D = w.shape[2]
    tm = _mm_res_tile(M, K, D)
    return pl.pallas_call(
        _mm_res_kernel,
        grid=(M // tm,),
        in_specs=[pl.BlockSpec((tm, K), lambda i: (i, 0)),
                  pl.BlockSpec((None, K, D), lambda i: (layer, 0, 0), pipeline_mode=pl.Buffered(1)),
                  pl.BlockSpec((tm, D), lambda i: (i, 0)),
                  pl.BlockSpec((1, D), lambda i: (0, 0)),
                  pl.BlockSpec((1, D), lambda i: (0, 0))],
        out_specs=[pl.BlockSpec((tm, D), lambda i: (i, 0)),
                   pl.BlockSpec((tm, D), lambda i: (i, 0))],
        out_shape=[jax.ShapeDtypeStruct((M, D), F32),
                   jax.ShapeDtypeStruct((M, D), BF16)],
        compiler_params=_cparams(("parallel",)),
        name=name,
    )(a, w, h, g_post.reshape(1, D), g_next.reshape(1, D))


def _pool_kernel(z_ref, w_ref, s_ref, o_ref, *, gw):
    L = z_ref.shape[1]
    row = lax.broadcasted_iota(jnp.int32, (L, gw), 0)
    for gi, win in enumerate(POOL_WINDOWS):
        sl = slice(gi * gw, (gi + 1) * gw)
        zg = z_ref[0, :, sl].astype(F32)
        s = zg
        d = 1
        while d < win:
            s = s + _shift_down(s, d)
            d *= 2
        cnt = jnp.minimum(row + 1, win).astype(F32)
        pooled = s / cnt - zg
        y = _dot(pooled, w_ref[gi]) * s_ref[:, sl]
        o_ref[0, :, sl] = y.astype(o_ref.dtype)


def _pool_call(z, w, scale, B, Lp):
    PW = z.shape[-1]
    G = len(POOL_WINDOWS)
    gw = PW // G
    z3 = z.reshape(B, Lp, PW)
    out = pl.pallas_call(
        functools.partial(_pool_kernel, gw=gw),
        grid=(B,),
        in_specs=[pl.BlockSpec((1, Lp, PW), lambda b: (b, 0, 0)),
                  pl.BlockSpec((G, gw, gw), lambda b: (0, 0, 0)),
                  pl.BlockSpec((1, PW), lambda b: (0, 0))],
        out_specs=pl.BlockSpec((1, Lp, PW), lambda b: (b, 0, 0)),
        out_shape=jax.ShapeDtypeStruct((B, Lp, PW), BF16),
        compiler_params=_cparams(("parallel",)),
        name="pool_mixer",
    )(z3, w.astype(BF16), scale.reshape(1, PW))
    return out.reshape(B * Lp, PW)


def _rel_bucket(rel, n_buckets):
    half = n_buckets // 2
    max_exact = half // 2
    n = np.abs(rel)
    large = max_exact + (np.log(np.maximum(n, 1) / max_exact) / math.log(REL_MAX_DIST / max_exact)
                         * (half - max_exact)).astype(np.int32)
    large = np.minimum(large, half - 1)
    return ((rel > 0) * half + np.where(n < max_exact, n, large)).astype(np.int32)


def _bias_table_kernel(rb_ref, bucket_ref, pen_ref, o_ref, *, n_buckets):
    s = pl.program_id(0)
    bucket = bucket_ref[...]
    acc = pen_ref[...]
    for b in range(n_buckets):
        acc = acc + jnp.where(bucket == b, rb_ref[b, s], 0.0)
    o_ref[0] = acc


def _bias_table_call(rel_bias, n_meta, tq):
    n_buckets, n_sub = rel_bias.shape
    KT = ATTN_TILE
    assert tq % KT == 0 and KT % CHUNK == 0 and n_meta <= KT
    n_near = tq // KT + 2
    n_slabs = n_near + 2
    far = n_buckets // 2 - 1
    r = np.arange(tq)[:, None]
    c = np.arange(KT)[None, :]
    assert np.all(_rel_bucket(-(KT + 1 + np.arange(4 * KT)), n_buckets) == far)
    bucket = np.full((n_slabs, tq, KT), far, np.int32)
    pen = np.zeros((n_slabs, tq, KT), np.float32)
    pen[-1] = NEG_INF
    for idx in range(n_near):
        key = (idx - 1) * KT + c
        bucket[1 + idx] = _rel_bucket(key - r, n_buckets)
        visible = np.floor_divide(key - n_meta, CHUNK) <= np.floor_divide(r - n_meta, CHUNK)
        pen[1 + idx] = np.where(visible, 0.0, NEG_INF)
    blk = pl.BlockSpec((n_slabs, tq, KT), lambda s: (0, 0, 0))
    return pl.pallas_call(
        functools.partial(_bias_table_kernel, n_buckets=n_buckets),
        grid=(n_sub,),
        in_specs=[pl.BlockSpec(memory_space=pltpu.SMEM), blk, blk],
        out_specs=pl.BlockSpec((1, n_slabs, tq, KT), lambda s: (s, 0, 0, 0)),
        out_shape=jax.ShapeDtypeStruct((n_sub, n_slabs, tq, KT), F32),
        compiler_params=_cparams(("arbitrary",)),
        name="bias_table_%d" % tq,
    )(rel_bias, jnp.asarray(bucket), jnp.asarray(pen))


def _attn_kernel(*refs, hd, lam_init, nt, tile0, live):
    scale = hd ** -0.5
    (lq1_ref, lk1_ref, lq2_ref, lk2_ref, sub_ref, q_ref, k_ref, v_ref, w_ref,
     o_ref, s_sc, mx_sc, l_sc, acc_sc, k_sc, v_sc) = refs
    KT = ATTN_TILE
    tq = q_ref.shape[1]
    ratio = -(-tq // KT)
    Lk = k_ref.shape[1]

    @pl.when(pl.program_id(2) == 0)
    def _():
        k_sc[0:Lk, :] = k_ref[0]
        v_sc[0:Lk, :] = v_ref[0]
        if k_sc.shape[0] > Lk:
            k_sc[Lk:, :] = jnp.zeros((k_sc.shape[0] - Lk, k_sc.shape[1]), k_sc.dtype)
            v_sc[Lk:, :] = jnp.zeros((v_sc.shape[0] - Lk, v_sc.shape[1]), v_sc.dtype)
    n_slabs = w_ref.shape[1]
    i = pl.program_id(2)
    jt0 = tile0 + i * ratio
    n_need = jnp.minimum(jt0 + ratio + 1, nt)
    n_pairs = (n_need + 1) // 2
    q = q_ref[0, 0:live, :]

    def slab(j):
        return jnp.where(j < nt, jnp.clip(j - jt0 + 2, 0, n_slabs - 1), n_slabs - 1)

    def two_tiles(ref, jj):
        j0 = 2 * jj
        j1 = jnp.minimum(j0 + 1, nt - 1)
        t0 = ref[pl.ds(pl.multiple_of(j0 * KT, KT), KT), :]
        t1 = ref[pl.ds(pl.multiple_of(j1 * KT, KT), KT), :]
        return jnp.concatenate([t0, t1], axis=0)

    mx_sc[...] = jnp.full_like(mx_sc, NEG_INF)

    def unrolled(step):
        def group(t, carry):
            step([ATTN_UNROLL * t + r for r in range(ATTN_UNROLL)])
            return carry

        def single(jj, carry):
            step([jj])
            return carry
        n_groups = n_pairs // ATTN_UNROLL
        lax.fori_loop(0, n_groups, group, 0)
        lax.fori_loop(n_groups * ATTN_UNROLL, n_pairs, single, 0)

    def pass1(jjs):
        kts = [two_tiles(k_sc, jj) for jj in jjs]
        for s in range(2):
            mx = None
            for jj, kt in zip(jjs, kts):
                sc = _dot_nt(q[:, s * hd:(s + 1) * hd], kt[:, s * hd:(s + 1) * hd]) * scale
                sc = sc + jnp.concatenate([w_ref[s, slab(2 * jj), 0:live, :],
                                           w_ref[s, slab(2 * jj + 1), 0:live, :]], axis=1)
                s_sc[s, jj] = sc
                half = jnp.maximum(sc[:, :KT], sc[:, KT:])
                mx = half if mx is None else jnp.maximum(mx, half)
            mx_sc[s] = jnp.maximum(mx_sc[s], mx)

    unrolled(pass1)
    m = [jnp.max(mx_sc[s], axis=-1, keepdims=True) for s in range(2)]
    l_sc[...] = jnp.zeros_like(l_sc)
    acc_sc[...] = jnp.zeros_like(acc_sc)

    def pass2(jjs):
        for jj in jjs:
            vt = two_tiles(v_sc, jj)
            for s in range(2):
                p = jnp.exp(s_sc[s, jj] - m[s])
                l_sc[s] += p[:, :KT] + p[:, KT:]
                acc_sc[s] += jnp.dot(p.astype(BF16), vt, preferred_element_type=F32)

    unrolled(pass2)
    l = [jnp.sum(l_sc[s], axis=-1, keepdims=True) for s in range(2)]
    lam = (jnp.exp(jnp.sum(lq1_ref[...] * lk1_ref[...], axis=-1, keepdims=True))
           - jnp.exp(jnp.sum(lq2_ref[...] * lk2_ref[...], axis=-1, keepdims=True)) + lam_init)
    o = acc_sc[0] / l[0] - lam * (acc_sc[1] / l[1])
    y = o * lax.rsqrt(jnp.mean(o * o, axis=-1, keepdims=True) + DIFF_SUBLN_EPS)
    o_ref[0, 0:live, :] = (y * sub_ref[...] * (1.0 - lam_init)).astype(o_ref.dtype)
    if live < tq:
        o_ref[0, live:tq, :] = jnp.zeros((tq - live, o_ref.shape[2]), o_ref.dtype)


def _attn_segment(qkv3, table, params, lam_init, tq, row0, n_tiles, live):
    B, Lp, W3 = qkv3.shape
    DW = W3 // 3
    hd = params[0].shape[1]
    hw = 2 * hd
    n_heads = DW // hw
    KT = ATTN_TILE
    nt = -(-Lp // KT)
    blk0 = row0 // tq
    kern = functools.partial(_attn_kernel, hd=hd, lam_init=lam_init, nt=nt,
                             tile0=row0 // KT, live=live)
    par = lambda n: pl.BlockSpec((1, n), lambda b, h, i: (0, 0))
    in_specs = [par(hd)] * 4 + [
        par(hw),
        pl.BlockSpec((1, tq, hw), lambda b, h, i: (b, blk0 + i, h)),
        pl.BlockSpec((1, Lp, hw), lambda b, h, i: (b, 0, n_heads + h)),
        pl.BlockSpec((1, Lp, hw), lambda b, h, i: (b, 0, 2 * n_heads + h)),
        pl.BlockSpec((2, table.shape[1], tq, KT), lambda b, h, i: (h, 0, 0, 0))]
    args = list(params) + [qkv3, qkv3, qkv3, table]
    max_pairs = (nt + 1) // 2
    return pl.pallas_call(
        kern,
        grid=(B, n_heads, n_tiles),
        in_specs=in_specs,
        out_specs=pl.BlockSpec((1, tq, hw), lambda b, h, i: (b, i, h)),
        out_shape=jax.ShapeDtypeStruct((B, n_tiles * tq, DW), BF16),
        scratch_shapes=[pltpu.VMEM((2, max_pairs, live, 2 * KT), F32), pltpu.VMEM((2, live, KT), F32),
                        pltpu.VMEM((2, live, KT), F32), pltpu.VMEM((2, live, hw), F32),
                        pltpu.VMEM((nt * KT, hw), BF16), pltpu.VMEM((nt * KT, hw), BF16)],
        compiler_params=_cparams(("parallel", "parallel", "arbitrary")),
        name="diff_attention_%d" % tq,
    )(*args)


def _attn_segments(Lp):
    big = ATTN_Q_TILE
    n_big = Lp // big
    segs = []
    if n_big:
        segs.append((big, 0, n_big))
    row0 = n_big * big
    n_mid = (Lp - row0) // ATTN_TILE
    if n_mid:
        segs.append((ATTN_TILE, row0, n_mid))
        row0 += n_mid * ATTN_TILE
    if Lp > row0:
        segs.append((Lp - row0, row0, 1))
    return segs


def _attn_call(zqkv, tables, lq1, lk1, lq2, lk2, subln, B, L, Lp, lam_init):
    vec = lambda a: a.reshape(1, -1)
    params = [vec(lq1), vec(lk1), vec(lq2), vec(lk2), vec(subln)]
    qkv3 = zqkv.reshape(B, Lp, zqkv.shape[-1])
    outs = []
    for tq, row0, n_tiles in _attn_segments(Lp):
        live = min(tq, _round_up(L - row0, BF16_ROWS)) if n_tiles == 1 else tq
        outs.append(_attn_segment(qkv3, tables[tq], params, lam_init, tq, row0, n_tiles, live))
    out = outs[0] if len(outs) == 1 else jnp.concatenate(outs, axis=1)
    return out.reshape(B * Lp, -1)


def _rwkv_prep_kernel(*refs, rw, has_vres):
    if has_vres:
        (u_ref, up_ref, w_ref, mu_ref, w0_ref, w2_ref, a0_ref, a2_ref, g2_ref, kk_ref, ka_ref, ones_ref,
         vf_ref, v0_ref, v1_ref, v2_ref,
         r_o, lw_o, k_o, v_o, a_o, b_o, g_o) = refs
    else:
        (u_ref, up_ref, w_ref, mu_ref, w0_ref, w2_ref, a0_ref, a2_ref, g2_ref, kk_ref, ka_ref, ones_ref,
         r_o, lw_o, k_o, v_o, a_o, b_o, g_o) = refs
    i = pl.program_id(1)
    halo = up_ref.shape[1]
    zz = jnp.dot(jnp.concatenate([up_ref[0], u_ref[0]], axis=0), w_ref[...], preferred_element_type=F32)
    z = zz[halo:]
    last_prev = jnp.where(i > 0, zz[halo - 1:halo, :], 0.0)
    row = lax.broadcasted_iota(jnp.int32, z.shape, 0)
    prev = jnp.where(row >= 1, pltpu.roll(z, 1, 0), last_prev)
    zs = z + (prev - z) * mu_ref[...]
    r = zs[:, 0:rw]
    kx = zs[:, rw:2 * rw]
    vx = zs[:, 2 * rw:3 * rw]
    wd = zs[:, 3 * rw:3 * rw + LANES]
    ad = zs[:, 3 * rw + LANES:3 * rw + 2 * LANES]
    gd = zs[:, 3 * rw + 2 * LANES:3 * rw + 3 * LANES]
    wl = w0_ref[...] + _dot(jnp.tanh(wd), w2_ref[...])
    sp = jnp.maximum(-wl, 0.0) + jnp.log(1.0 + jnp.exp(-jnp.abs(wl)))
    lw = -jnp.exp(-sp - 0.5)
    if has_vres:
        gate_v = _sigmoid(v0_ref[...] + _dot(_dot(vx, v1_ref[...]), v2_ref[...]))
        vx = vx + (vf_ref[0] - vx) * gate_v
    a = _sigmoid(a0_ref[...] + _dot(ad, a2_ref[...]))
    g = _dot(_sigmoid(gd), g2_ref[...])
    kk = kx * kk_ref[...]
    ss = _dot_exact_rhs(kk * kk, ones_ref[...])
    kk = kk * lax.rsqrt(jnp.maximum(ss, 1e-24))
    kmod = kx * (1.0 + (a - 1.0) * ka_ref[...])
    r_o[0] = r
    lw_o[0] = lw
    k_o[0] = kmod
    v_o[0] = vx
    a_o[0] = -kk
    b_o[0] = kk * a
    g_o[0] = g


def _pad_rows(w, n):
    return jnp.pad(w, ((0, n - w.shape[0]), (0, 0)))


def _rwkv_prep_call(u, w_rwkv, B, Lp, rw, mu_p, w0, w2, a0, a2, g2, k_k, k_a, ones_bd, v_first, v_res):
    D = u.shape[-1]
    u3 = u.reshape(B, Lp, D)
    HALO = BF16_ROWS
    tl = _pick_tile(Lp, PREP_ROWS, HALO)
    nb = tl // HALO
    has_vres = v_res is not None
    vec = lambda a: a.reshape(1, -1)
    full = lambda a: pl.BlockSpec(a.shape, lambda b, i: (0,) * a.ndim)
    w2p = _pad_rows(w2, LANES).astype(BF16)
    a2p = _pad_rows(a2, LANES).astype(BF16)
    g2p = _pad_rows(g2, LANES).astype(BF16)
    args = [u3, u3, w_rwkv, vec(mu_p), vec(w0), w2p, vec(a0), a2p, g2p, vec(k_k), vec(k_a), ones_bd]
    in_specs = [pl.BlockSpec((1, tl, D), lambda b, i: (b, i, 0)),
                pl.BlockSpec((1, HALO, D), lambda b, i: (b, jnp.maximum(i * nb - 1, 0), 0)),
                pl.BlockSpec(w_rwkv.shape, lambda b, i: (0, 0), pipeline_mode=pl.Buffered(1))]
    in_specs += [full(a) for a in args[3:]]
    if has_vres:
        v0, v1, v2 = v_res
        v1p = jnp.pad(v1, ((0, 0), (0, LANES - v1.shape[1]))).astype(BF16)
        v2p = _pad_rows(v2, LANES).astype(BF16)
        extra = [v_first, vec(v0), v1p, v2p]
        args += extra
        in_specs += [pl.BlockSpec((1, tl, rw), lambda b, i: (b, i, 0))] + [full(a) for a in extra[1:]]
    o_spec = pl.BlockSpec((1, tl, rw), lambda b, i: (b, i, 0))
    o_shape = jax.ShapeDtypeStruct((B, Lp, rw), F32)
    return pl.pallas_call(
        functools.partial(_rwkv_prep_kernel, rw=rw, has_vres=has_vres),
        grid=(B, Lp // tl),
        in_specs=in_specs,
        out_specs=[o_spec] * 7,
        out_shape=[o_shape] * 7,
        compiler_params=_cparams(("parallel", "arbitrary")),
        name="rwkv_prep",
    )(*args)


def _scan_kernel(r_ref, lw_ref, k_ref, v_ref, a_ref, b_ref, g_ref, rk_ref, lnw_ref, lnb_ref, ones_ref,
                 o_ref, h_sc, *, hd):
    nb = r_ref.shape[0]
    C = r_ref.shape[1]
    C2 = 2 * C
    WB = r_ref.shape[2]
    W = nb * WB
    c = pl.program_id(1)

    @pl.when(c == 0)
    def _():
        h_sc[...] = jnp.zeros_like(h_sc)

    side_by_side = lambda ref: jnp.concatenate([ref[i] for i in range(nb)], axis=1)
    r = side_by_side(r_ref)
    lw = side_by_side(lw_ref)
    k = side_by_side(k_ref)
    v = side_by_side(v_ref)
    a = side_by_side(a_ref)
    b = side_by_side(b_ref)

    cum = lw
    d = 1
    while d < C:
        cum = cum + _shift_down(cum, d)
        d *= 2
    tot = cum[C - 1:C, :]
    e_end = jnp.exp(tot - cum)
    e_neg = jnp.exp(-cum)
    at = a * jnp.exp(cum - lw)
    rt = r * jnp.exp(cum)
    bt = b * e_neg
    kt = k * e_neg
    bb = b * e_end
    kb = k * e_end
    p_c = jnp.exp(tot)

    lane = lax.broadcasted_iota(jnp.int32, (C, LANES), 1)
    first = lane < hd
    rr = lax.broadcasted_iota(jnp.int32, (C2, C2), 0)
    cc = lax.broadcasted_iota(jnp.int32, (C2, C2), 1)
    tr = jnp.where(rr >= C, rr - C, rr)
    tc = jnp.where(cc >= C, cc - C, cc)
    strict = tr > tc
    lower = tr >= tc
    eye = rr == cc

    def stack(x):
        return jnp.concatenate([jnp.where(first, x, 0.0), jnp.where(first, 0.0, x)], axis=0)

    pairs = range(W // LANES)
    stacked = lambda x: [stack(x[:, p * LANES:(p + 1) * LANES]) for p in pairs]
    s_a, s_r, s_b, s_k = stacked(at), stacked(rt), stacked(bt), stacked(kt)
    s_bb, s_kb, s_v = stacked(bb), stacked(kb), stacked(v)
    sc = [_dot_nt(jnp.concatenate([s_a[p], s_r[p]], axis=0), jnp.concatenate([s_b[p], s_k[p]], axis=0))
          for p in pairs]
    n = [jnp.where(strict, sc[p][:C2, :C2], 0.0) for p in pairs]
    a_ak = [jnp.where(strict, sc[p][:C2, C2:], 0.0) for p in pairs]
    m_rb = [jnp.where(lower, sc[p][C2:, :C2], 0.0) for p in pairs]
    m_rk = [jnp.where(lower, sc[p][C2:, C2:], 0.0) for p in pairs]
    xv = [_dot(jnp.concatenate([a_ak[p], m_rk[p], s_kb[p].T], axis=0), s_v[p]) for p in pairs]
    t_inv = [jnp.where(eye, 1.0, 0.0) + n[p] for p in pairs]
    pw = [_dot(n[p], n[p]) for p in pairs]
    step = 4
    while step < C:
        tp = [_dot(jnp.concatenate([t_inv[p], pw[p]], axis=0), pw[p]) for p in pairs]
        t_inv = [t_inv[p] + tp[p][:C2] for p in pairs]
        pw = [tp[p][C2:] for p in pairs]
        step *= 2
    t_inv = [t_inv[p] + _dot(t_inv[p], pw[p]) for p in pairs]
    au = [_dot(t_inv[p], jnp.concatenate([s_a[p], xv[p][:C2]], axis=1)) for p in pairs]
    mbu = [_dot(jnp.concatenate([m_rb[p], s_bb[p].T], axis=0), au[p]) for p in pairs]
    ys = []
    for p in pairs:
        r_hat = s_r[p] + mbu[p][:C2, :LANES]
        y0 = mbu[p][:C2, LANES:] + xv[p][C2:2 * C2]
        g_mat = jnp.where(eye, p_c[:, p * LANES:(p + 1) * LANES], 0.0) + mbu[p][C2:, :LANES]
        h_add = mbu[p][C2:, LANES:] + xv[p][2 * C2:]
        yh = _dot(jnp.concatenate([r_hat, g_mat], axis=0), h_sc[p])
        y = yh[:C2] + y0
        h_sc[p] = yh[C2:] + h_add
        ys.append(y[0:C] + y[C:C2])
    ones = ones_ref[...]
    inv_hd = 1.0 / hd
    ppb = WB // LANES
    n = nb * C
    y = jnp.concatenate([jnp.concatenate(ys[i * ppb:(i + 1) * ppb], axis=1) for i in range(nb)], axis=0)
    rk = jnp.concatenate([r_ref[i] * k_ref[i] * rk_ref[...] for i in range(nb)], axis=0)
    sums = _dot_exact_rhs(jnp.concatenate([y, rk], axis=0), ones)
    yc = y - sums[:n] * inv_hd
    var = _dot_exact_rhs(yc * yc, ones) * inv_hd
    yn = yc * lax.rsqrt(var + RWKV_LNX_EPS) * lnw_ref[...] + lnb_ref[...]
    for i in range(nb):
        rows = slice(i * C, (i + 1) * C)
        bonus = sums[n:][rows] * v_ref[i]
        o_ref[i] = ((yn[rows] + bonus) * g_ref[i]).astype(o_ref.dtype)


def _scan_call(streams, r_k, lnx_w, lnx_b, ones_bd, B, Lp, hd):
    r, lw, k, v, a, b, g = streams
    rw = r.shape[-1]
    C = CHUNK
    assert 2 * hd == LANES and rw % LANES == 0 and Lp % C == 0
    vec = lambda x: x.reshape(1, -1)
    nb = SCAN_BATCH if B % SCAN_BATCH == 0 else 1
    blk = pl.BlockSpec((nb, C, rw), lambda bi, c: (bi, c, 0))
    par = pl.BlockSpec((1, rw), lambda bi, c: (0, 0))
    out = pl.pallas_call(
        functools.partial(_scan_kernel, hd=hd),
        grid=(B // nb, Lp // C),
        in_specs=[blk] * 7 + [par] * 3 + [pl.BlockSpec((rw, rw), lambda bi, c: (0, 0))],
        out_specs=blk,
        out_shape=jax.ShapeDtypeStruct((B, Lp, rw), BF16),
        scratch_shapes=[pltpu.VMEM((nb * rw // LANES, LANES, LANES), F32)],
        compiler_params=_cparams(("parallel", "arbitrary")),
        name="rwkv_scan",
    )(r, lw, k, v, a, b, g, vec(r_k), vec(lnx_w), vec(lnx_b), ones_bd)
    return out.reshape(B * Lp, rw)


def _merge_kernel(yp_ref, yd_ref, yr_ref, g0_ref, g1_ref, g2_ref, pa_ref, pb_ref, pc_ref, o_ref):
    dot = lambda x, w: jnp.dot(x[...], w[...], preferred_element_type=F32)
    m = _sigmoid(g0_ref[...].astype(F32)) * dot(yp_ref, pa_ref)
    m = m + _sigmoid(g1_ref[...].astype(F32)) * dot(yd_ref, pb_ref)
    m = m + _sigmoid(g2_ref[...].astype(F32)) * dot(yr_ref, pc_ref)
    o_ref[...] = m.astype(o_ref.dtype)


def _merge_call(yp, yd, yr, zg, pa, pb, pc, layer, tm):
    M = yp.shape[0]
    D = pa.shape[2]
    rows = lambda a: pl.BlockSpec((tm, a.shape[1]), lambda i: (i, 0))
    full = lambda a: pl.BlockSpec((None,) + a.shape[1:], lambda i: (layer, 0, 0))
    gate = lambda n: pl.BlockSpec((tm, D), lambda i: (i, n))
    return pl.pallas_call(
        _merge_kernel,
        grid=(M // tm,),
        in_specs=[rows(yp), rows(yd), rows(yr), gate(0), gate(1), gate(2), full(pa), full(pb), full(pc)],
        out_specs=pl.BlockSpec((tm, D), lambda i: (i, 0)),
        out_shape=jax.ShapeDtypeStruct((M, D), BF16),
        compiler_params=_cparams(("parallel",)),
        name="gated_merge",
    )(yp, yd, yr, zg, zg, zg, pa, pb, pc)


def _ffn_up_kernel(u_ref, wv_ref, wg_ref, cwv_ref, cwg_ref, cbv_ref, cbg_ref, o_ref, *scratch, n_sub):
    xg_scs, xv_scs, a_scs = scratch[:n_sub], scratch[n_sub:2 * n_sub], scratch[2 * n_sub:]
    L = u_ref.shape[0]
    mb = L // n_sub
    st = mb // SUBLANES
    PAD = SUBLANES
    n_slab = a_scs[0].shape[0]
    k_in = math.sqrt(2.0 / math.pi)
    zeros = jnp.zeros((PAD, LANES), F32)
    for s in range(n_slab):
        xv_scs[0][s, 0:PAD, :] = zeros
        xg_scs[0][s, 0:PAD, :] = zeros

    def taps(cw_ref, cb_ref, s):
        sl = slice(s * LANES, (s + 1) * LANES)
        rows = [jnp.broadcast_to(cw_ref[j:j + 1, sl], (SUBLANES, LANES)) for j in range(FFN_CONV)]
        return rows, jnp.broadcast_to(cb_ref[:, sl], (SUBLANES, LANES))

    def tail(sb):
        lo = sb * mb
        xg_sc, xv_sc, a_sc = xg_scs[sb], xv_scs[sb], a_scs[sb]
        for s in range(n_slab):
            wg, bg = taps(cwg_ref, cbg_ref, s)
            wv, bv = taps(cwv_ref, cbv_ref, s)
            ld = lambda ref, j: ref[s, pl.ds(PAD + j, SUBLANES, stride=st), :]
            xg = [ld(xg_sc, j) for j in range(1 - FFN_CONV, 0)]
            xv = [ld(xv_sc, j) for j in range(1 - FFN_CONV, 0)]
            for j in range(st):
                xg.append(ld(xg_sc, j))
                xv.append(ld(xv_sc, j))
                gate = bg
                val = bv
                for t in range(FFN_CONV):
                    gate = gate + xg[j + t] * wg[t]
                    val = val + xv[j + t] * wv[t]
                inner = gate * (2.0 * k_in + (2.0 * k_in * 0.044715) * (gate * gate))
                act = gate / (1.0 + jnp.exp(-inner)) * val
                a_sc[s, pl.ds(j, SUBLANES, stride=st), :] = act
        for s in range(n_slab):
            o_ref[lo:lo + mb, s * LANES:(s + 1) * LANES] = a_sc[s].astype(o_ref.dtype)

    for sb in range(n_sub):
        lo = sb * mb
        u = u_ref[lo:lo + mb, :]
        for x_scs, w_ref in ((xg_scs, wg_ref), (xv_scs, wv_ref)):
            x = jnp.dot(u, w_ref[...], preferred_element_type=F32)
            for s in range(n_slab):
                xs = x[:, s * LANES:(s + 1) * LANES]
                x_scs[sb][s, PAD:PAD + mb, :] = xs
                if sb + 1 < n_sub:
                    x_scs[sb + 1][s, 0:PAD, :] = xs[mb - PAD:mb]
        if sb:
            tail(sb - 1)
    tail(n_sub - 1)


def _ffn_up_call(u, w_up, layer, conv_w, conv_b, B, Lp, tn):
    M, D = u.shape
    FF = w_up.shape[2] // 2
    nt = FF // tn
    n_sub = FFN_SUBBLOCKS if Lp % (FFN_SUBBLOCKS * BF16_ROWS) == 0 else 1
    mb = Lp // n_sub
    return pl.pallas_call(
        functools.partial(_ffn_up_kernel, n_sub=n_sub),
        grid=(B, nt),
        in_specs=[pl.BlockSpec((Lp, D), lambda b, j: (b, 0)),
                  pl.BlockSpec((None, D, tn), lambda b, j: (layer, 0, j)),
                  pl.BlockSpec((None, D, tn), lambda b, j: (layer, 0, nt + j)),
                  pl.BlockSpec((FFN_CONV, tn), lambda b, j: (0, j)),
                  pl.BlockSpec((FFN_CONV, tn), lambda b, j: (0, nt + j)),
                  pl.BlockSpec((1, tn), lambda b, j: (0, j)),
                  pl.BlockSpec((1, tn), lambda b, j: (0, nt + j))],
        out_specs=pl.BlockSpec((Lp, tn), lambda b, j: (b, j)),
        out_shape=jax.ShapeDtypeStruct((M, FF), BF16),
        scratch_shapes=([pltpu.VMEM((tn // LANES, SUBLANES + mb, LANES), F32)] * (2 * n_sub)
                        + [pltpu.VMEM((tn // LANES, mb, LANES), F32)] * n_sub),
        compiler_params=_cparams(("parallel", "arbitrary")),
        name="ffn_up_conv_geglu",
    )(u, w_up, w_up, conv_w, conv_w, conv_b.reshape(1, -1), conv_b.reshape(1, -1))


def kernel(x, meta, rel_bias, norm_mix_pre, norm_mix_post, norm_ffn_pre, norm_ffn_post, w_in, pool_w, pool_scale, diff_lq1, diff_lk1, diff_lq2, diff_lk2, diff_subln, rwkv_mu, rwkv_w0, rwkv_w2, rwkv_a0, rwkv_a2, rwkv_g2, rwkv_kk, rwkv_ka, rwkv_rk, rwkv_lnx_w, rwkv_lnx_b, rwkv_v0, rwkv_v1, rwkv_v2, p_a, p_b, p_c, w_o, ffn_up, ffn_conv_w, ffn_conv_b, ffn_down):
    B, S, D = x.shape
    n_meta = meta.shape[0]
    depth = w_in.shape[0]
    L = S + n_meta
    Lp = _round_up(L, CHUNK)
    M = B * Lp

    PW = pool_scale.shape[1]
    DW = p_b.shape[1]
    RW = rwkv_w0.shape[1]
    hd_r = rwkv_rk.shape[2]
    dl, al, gl = rwkv_w2.shape[1], rwkv_a2.shape[1], rwkv_g2.shape[1]
    off_q = PW
    off_r = PW + 3 * DW
    off_g = off_r + 3 * RW + dl + al + gl
    assert off_g + 3 * D == w_in.shape[2]
    assert Lp == L or S % CHUNK == 0
    assert max(dl, al, gl, rwkv_v1.shape[2]) <= LANES

    h = jnp.concatenate([jnp.broadcast_to(meta.astype(x.dtype)[None], (B, n_meta, D)), x,
                         jnp.zeros((B, Lp - L, D), x.dtype)], axis=1).reshape(M, D)

    hid = np.arange(RW) // hd_r
    ones_bd = jnp.asarray((hid[:, None] == hid[None, :]).astype(np.float32), dtype=BF16)

    def pad_lora(a, axis):
        segs = [lax.slice_in_dim(a, 0, 3 * RW, axis=axis)]
        o = 3 * RW
        for n in (dl, al, gl):
            seg = lax.slice_in_dim(a, o, o + n, axis=axis)
            pad = [(0, 0)] * a.ndim
            pad[axis] = (0, LANES - n)
            segs.append(jnp.pad(seg, pad))
            o += n
        return jnp.concatenate(segs, axis=axis)

    tables = {}
    for tq, _, _ in _attn_segments(Lp):
        tt = _round_up(tq, ATTN_TILE)
        if tt not in tables:
            tables[tt] = _bias_table_call(rel_bias, n_meta, tt)
        tables[tq] = tables[tt]

    tm_big = _pick_tile(M, MM_ROWS, BF16_ROWS)
    tm_res = _pick_tile(M, EW_ROWS, BF16_ROWS)
    u = _rms_call(h, norm_mix_pre[0], tm_res)
    v_first = None
    w_in_b, p_a_b, p_b_b, p_c_b, w_o_b, ffn_up_b, ffn_down_b = (
        a.astype(BF16) for a in (w_in, p_a, p_b, p_c, w_o, ffn_up, ffn_down))
    for l in range(depth):
        lam_init = 0.8 - 0.6 * math.exp(-0.3 * l)
        w_gate = w_in_b[l, :, off_g:]
        w_rwkv = pad_lora(w_in_b[l, :, off_r:off_g], 1)
        mu_p = pad_lora(rwkv_mu[l], 0)

        zg = _mm_call(u, w_gate, BF16, tm_big, _pick_tile(3 * D, MM_COLS, LANES), "in_proj_gate")
        zqkv = _mm_call(u, w_in_b[l, :, off_q:off_r], BF16, tm_big, _pick_tile(3 * DW, MM_COLS, LANES),
                        "in_proj_qkv")
        zp = _mm_call(u, w_in_b[l, :, :PW], BF16, tm_big, _pick_tile(PW, MM_COLS, LANES), "in_proj_pool")

        yp = _pool_call(zp, pool_w[l], pool_scale[l], B, Lp)
        yd = _attn_call(zqkv, tables, diff_lq1[l], diff_lk1[l], diff_lq2[l], diff_lk2[l],
                        diff_subln[l], B, L, Lp, lam_init)
        v_res = None if l == 0 else (rwkv_v0[l - 1], rwkv_v1[l - 1], rwkv_v2[l - 1])
        streams = _rwkv_prep_call(u, w_rwkv, B, Lp, RW, mu_p, rwkv_w0[l], rwkv_w2[l], rwkv_a0[l], rwkv_a2[l],
                                  rwkv_g2[l], rwkv_kk[l], rwkv_ka[l], ones_bd, v_first, v_res)
        if l == 0:
            v_first = streams[3]
        yr = _scan_call(streams, rwkv_rk[l], rwkv_lnx_w[l], rwkv_lnx_b[l], ones_bd, B, Lp, hd_r)

        merged = _merge_call(yp, yd, yr, zg, p_a_b, p_b_b, p_c_b, l, tm_res)
        h, u2 = _mm_res_call(merged, w_o_b, l, h, norm_mix_post[l], norm_ffn_pre[l], "out_proj_res")
        FF = ffn_down.shape[1]
        act = _ffn_up_call(u2, ffn_up_b, l, ffn_conv_w[l], ffn_conv_b[l], B, Lp,
                           _pick_tile(FF, FFN_COLS, LANES))
        g_next = norm_mix_pre[l + 1] if l + 1 < depth else norm_mix_pre[l]
        h, u = _mm_res_call(act, ffn_down_b, l, h, norm_ffn_post[l], g_next, "ffn_down_res")
    return h.reshape(B, Lp, D)[:, n_meta:n_meta + S]
```

```python
import functools
import math

import numpy as np
import jax
import jax.numpy as jnp
from jax import lax
from jax.experimental import pallas as pl
from jax.experimental.pallas import tpu as pltpu

F32 = jnp.float32
BF16 = jnp.bfloat16

CHUNK = 64
NORM_EPS = 1e-6
NEG_INF = -1e30
POOL_WINDOWS = (2, 4, 8, 16)
DIFF_SUBLN_EPS = 1e-5
REL_MAX_DIST = 128
RWKV_LNX_EPS = 64e-5
FFN_CONV = 3

LANES = 128
SUBLANES = 8
ATTN_TILE = 128
ATTN_Q_TILE = 512
ATTN_UNROLL = 3
SCAN_BATCH = 4
BF16_ROWS = 2 * SUBLANES
VMEM_LIMIT = 56 * 1024 * 1024
VMEM_HEADROOM = 8 * 1024 * 1024

MM_ROWS = 2176
MM_COLS = 1536
EW_ROWS = 544
PREP_ROWS = 384
FFN_COLS = 512
FFN_SUBBLOCKS = 2


def _cparams(sem):
    return pltpu.CompilerParams(dimension_semantics=sem, vmem_limit_bytes=VMEM_LIMIT)


def _round_up(x, m):
    return (x + m - 1) // m * m


def _pick_tile(total, target, mult):
    best = None
    for t in range(mult, min(total, target) + 1, mult):
        if total % t == 0:
            best = t
    assert best is not None, (total, target, mult)
    return best


def _shift_down(x, d):
    row = lax.broadcasted_iota(jnp.int32, x.shape, 0)
    return jnp.where(row >= d, pltpu.roll(x, d, 0), 0.0)


def _sigmoid(x):
    return 1.0 / (1.0 + jnp.exp(-x))


def _dot(a, b):
    return jnp.dot(a.astype(BF16), b.astype(BF16), preferred_element_type=F32)


def _dot_nt(a, b):
    return lax.dot_general(a.astype(BF16), b.astype(BF16), (((1,), (1,)), ((), ())),
                           preferred_element_type=F32)


def _split3(x):
    hi = x.astype(BF16)
    r1 = x - hi.astype(F32)
    mid = r1.astype(BF16)
    lo = (r1 - mid.astype(F32)).astype(BF16)
    return hi, mid, lo


def _dot_exact_rhs(x, ones_bf16):
    n = x.shape[0]
    pieces = jnp.concatenate(_split3(x), axis=0)
    res = jnp.dot(pieces, ones_bf16, preferred_element_type=F32)
    return res[:n] + res[n:2 * n] + res[2 * n:]


def _rms_kernel(x_ref, g_ref, o_ref):
    x = x_ref[...]
    y = x * lax.rsqrt(jnp.mean(x * x, axis=-1, keepdims=True) + NORM_EPS)
    o_ref[...] = (y * g_ref[...]).astype(o_ref.dtype)


def _rms_call(x, g, tm):
    M, D = x.shape
    return pl.pallas_call(
        _rms_kernel,
        grid=(M // tm,),
        in_specs=[pl.BlockSpec((tm, D), lambda i: (i, 0)),
                  pl.BlockSpec((1, D), lambda i: (0, 0))],
        out_specs=pl.BlockSpec((tm, D), lambda i: (i, 0)),
        out_shape=jax.ShapeDtypeStruct((M, D), BF16),
        compiler_params=_cparams(("parallel",)),
        name="rms_norm",
    )(x, g.reshape(1, D))


def _mm_kernel(a_ref, b_ref, o_ref):
    o_ref[...] = jnp.dot(a_ref[...], b_ref[...], preferred_element_type=F32).astype(o_ref.dtype)


def _mm_call(a, b, out_dtype, tm, tn, name):
    M, K = a.shape
    N = b.shape[1]
    return pl.pallas_call(
        _mm_kernel,
        grid=(M // tm, N // tn),
        in_specs=[pl.BlockSpec((tm, K), lambda i, j: (i, 0)),
                  pl.BlockSpec((K, tn), lambda i, j: (0, j))],
        out_specs=pl.BlockSpec((tm, tn), lambda i, j: (i, j)),
        out_shape=jax.ShapeDtypeStruct((M, N), out_dtype),
        compiler_params=_cparams(("parallel", "arbitrary")),
        name=name,
    )(a, b)


def _mm_res_kernel(a_ref, w_ref, h_ref, gp_ref, gn_ref, oh_ref, ou_ref):
    f = jnp.dot(a_ref[...], w_ref[...], preferred_element_type=F32)
    y = f * lax.rsqrt(jnp.mean(f * f, axis=-1, keepdims=True) + NORM_EPS) * gp_ref[...]
    hn = h_ref[...] + y
    oh_ref[...] = hn
    u = hn * lax.rsqrt(jnp.mean(hn * hn, axis=-1, keepdims=True) + NORM_EPS) * gn_ref[...]
    ou_ref[...] = u.astype(ou_ref.dtype)


def _mm_res_tile(M, K, D):
    budget = VMEM_LIMIT - VMEM_HEADROOM
    best = None
    for tm in range(BF16_ROWS, M + 1, BF16_ROWS):
        if M % tm:
            continue
        need = (K * D * 2
                + 2 * tm * K * 2
                + 2 * 2 * tm * D * 4
                + 2 * tm * D * 2
                + 2 * tm * D * 4)
        if need <= budget:
            best = tm
    assert best is not None
    return best


def _mm_res_call(a, w, layer, h, g_post, g_next, name):
    M, K = a.shape
    D = w.shape[2]
    tm = _mm_res_tile(M, K, D)
    return pl.pallas_call(
        _mm_res_kernel,
        grid=(M // tm,),
        in_specs=[pl.BlockSpec((tm, K), lambda i: (i, 0)),
                  pl.BlockSpec((None, K, D), lambda i: (layer, 0, 0), pipeline_mode=pl.Buffered(1)),
                  pl.BlockSpec((tm, D), lambda i: (i, 0)),
                  pl.BlockSpec((1, D), lambda i: (0, 0)),
                  pl.BlockSpec((1, D), lambda i: (0, 0))],
        out_specs=[pl.BlockSpec((tm, D), lambda i: (i, 0)),
                   pl.BlockSpec((tm, D), lambda i: (i, 0))],
        out_shape=[jax.ShapeDtypeStruct((M, D), F32),
                   jax.ShapeDtypeStruct((M, D), BF16)],
        compiler_params=_cparams(("parallel",)),
        name=name,
    )(a, w, h, g_post.reshape(1, D), g_next.reshape(1, D))


def _pool_kernel(z_ref, w_ref, s_ref, o_ref, *, gw):
    L = z_ref.shape[1]
    row = lax.broadcasted_iota(jnp.int32, (L, gw), 0)
    for gi, win in enumerate(POOL_WINDOWS):
        sl = slice(gi * gw, (gi + 1) * gw)
        zg = z_ref[0, :, sl].astype(F32)
        s = zg
        d = 1
        while d < win:
            s = s + _shift_down(s, d)
            d *= 2
        cnt = jnp.minimum(row + 1, win).astype(F32)
        pooled = s / cnt - zg
        y = _dot(pooled, w_ref[gi]) * s_ref[:, sl]
        o_ref[0, :, sl] = y.astype(o_ref.dtype)


def _pool_call(z, w, scale, B, Lp):
    PW = z.shape[-1]
    G = len(POOL_WINDOWS)
    gw = PW // G
    z3 = z.reshape(B, Lp, PW)
    out = pl.pallas_call(
        functools.partial(_pool_kernel, gw=gw),
        grid=(B,),
        in_specs=[pl.BlockSpec((1, Lp, PW), lambda b: (b, 0, 0)),
                  pl.BlockSpec((G, gw, gw), lambda b: (0, 0, 0)),
                  pl.BlockSpec((1, PW), lambda b: (0, 0))],
        out_specs=pl.BlockSpec((1, Lp, PW), lambda b: (b, 0, 0)),
        out_shape=jax.ShapeDtypeStruct((B, Lp, PW), BF16),
        compiler_params=_cparams(("parallel",)),
        name="pool_mixer",
    )(z3, w.astype(BF16), scale.reshape(1, PW))
    return out.reshape(B * Lp, PW)


def _rel_bucket(rel, n_buckets):
    half = n_buckets // 2
    max_exact = half // 2
    n = np.abs(rel)
    large = max_exact + (np.log(np.maximum(n, 1) / max_exact) / math.log(REL_MAX_DIST / max_exact)
                         * (half - max_exact)).astype(np.int32)
    large = np.minimum(large, half - 1)
    return ((rel > 0) * half + np.where(n < max_exact, n, large)).astype(np.int32)


def _bias_table_kernel(rb_ref, bucket_ref, pen_ref, o_ref, *, n_buckets):
    s = pl.program_id(0)
    bucket = bucket_ref[...]
    acc = pen_ref[...]
    for b in range(n_buckets):
        acc = acc + jnp.where(bucket == b, rb_ref[b, s], 0.0)
    o_ref[0] = acc


def _bias_table_call(rel_bias, n_meta, tq):
    n_buckets, n_sub = rel_bias.shape
    KT = ATTN_TILE
    assert tq % KT == 0 and KT % CHUNK == 0 and n_meta <= KT
    n_near = tq // KT + 2
    n_slabs = n_near + 2
    far = n_buckets // 2 - 1
    r = np.arange(tq)[:, None]
    c = np.arange(KT)[None, :]
    assert np.all(_rel_bucket(-(KT + 1 + np.arange(4 * KT)), n_buckets) == far)
    bucket = np.full((n_slabs, tq, KT), far, np.int32)
    pen = np.zeros((n_slabs, tq, KT), np.float32)
    pen[-1] = NEG_INF
    for idx in range(n_near):
        key = (idx - 1) * KT + c
        bucket[1 + idx] = _rel_bucket(key - r, n_buckets)
        visible = np.floor_divide(key - n_meta, CHUNK) <= np.floor_divide(r - n_meta, CHUNK)
        pen[1 + idx] = np.where(visible, 0.0, NEG_INF)
    blk = pl.BlockSpec((n_slabs, tq, KT), lambda s: (0, 0, 0))
    return pl.pallas_call(
        functools.partial(_bias_table_kernel, n_buckets=n_buckets),
        grid=(n_sub,),
        in_specs=[pl.BlockSpec(memory_space=pltpu.SMEM), blk, blk],
        out_specs=pl.BlockSpec((1, n_slabs, tq, KT), lambda s: (s, 0, 0, 0)),
        out_shape=jax.ShapeDtypeStruct((n_sub, n_slabs, tq, KT), F32),
        compiler_params=_cparams(("arbitrary",)),
        name="bias_table_%d" % tq,
    )(rel_bias, jnp.asarray(bucket), jnp.asarray(pen))


def _attn_kernel(*refs, hd, lam_init, nt, tile0, live):
    scale = hd ** -0.5
    (lq1_ref, lk1_ref, lq2_ref, lk2_ref, sub_ref, q_ref, k_ref, v_ref, w_ref,
     o_ref, s_sc, mx_sc, l_sc, acc_sc, k_sc, v_sc) = refs
    KT = ATTN_TILE
    tq = q_ref.shape[1]
    ratio = -(-tq // KT)
    Lk = k_ref.shape[1]

    @pl.when(pl.program_id(2) == 0)
    def _():
        k_sc[0:Lk, :] = k_ref[0]
        v_sc[0:Lk, :] = v_ref[0]
        if k_sc.shape[0] > Lk:
            k_sc[Lk:, :] = jnp.zeros((k_sc.shape[0] - Lk, k_sc.shape[1]), k_sc.dtype)
            v_sc[Lk:, :] = jnp.zeros((v_sc.shape[0] - Lk, v_sc.shape[1]), v_sc.dtype)
    n_slabs = w_ref.shape[1]
    i = pl.program_id(2)
    jt0 = tile0 + i * ratio
    n_need = jnp.minimum(jt0 + ratio + 1, nt)
    n_pairs = (n_need + 1) // 2
    q = q_ref[0, 0:live, :]

    def slab(j):
        return jnp.where(j < nt, jnp.clip(j - jt0 + 2, 0, n_slabs - 1), n_slabs - 1)

    def two_tiles(ref, jj):
        j0 = 2 * jj
        j1 = jnp.minimum(j0 + 1, nt - 1)
        t0 = ref[pl.ds(pl.multiple_of(j0 * KT, KT), KT), :]
        t1 = ref[pl.ds(pl.multiple_of(j1 * KT, KT), KT), :]
        return jnp.concatenate([t0, t1], axis=0)

    mx_sc[...] = jnp.full_like(mx_sc, NEG_INF)

    def unrolled(step):
        def group(t, carry):
            step([ATTN_UNROLL * t + r for r in range(ATTN_UNROLL)])
            return carry

        def single(jj, carry):
            step([jj])
            return carry
        n_groups = n_pairs // ATTN_UNROLL
        lax.fori_loop(0, n_groups, group, 0)
        lax.fori_loop(n_groups * ATTN_UNROLL, n_pairs, single, 0)

    def pass1(jjs):
        kts = [two_tiles(k_sc, jj) for jj in jjs]
        for s in range(2):
            mx = None
            for jj, kt in zip(jjs, kts):
                sc = _dot_nt(q[:, s * hd:(s + 1) * hd], kt[:, s * hd:(s + 1) * hd]) * scale
                sc = sc + jnp.concatenate([w_ref[s, slab(2 * jj), 0:live, :],
                                           w_ref[s, slab(2 * jj + 1), 0:live, :]], axis=1)
                s_sc[s, jj] = sc
                half = jnp.maximum(sc[:, :KT], sc[:, KT:])
                mx = half if mx is None else jnp.maximum(mx, half)
            mx_sc[s] = jnp.maximum(mx_sc[s], mx)

    unrolled(pass1)
    m = [jnp.max(mx_sc[s], axis=-1, keepdims=True) for s in range(2)]
    l_sc[...] = jnp.zeros_like(l_sc)
    acc_sc[...] = jnp.zeros_like(acc_sc)

    def pass2(jjs):
        for jj in jjs:
            vt = two_tiles(v_sc, jj)
            for s in range(2):
                p = jnp.exp(s_sc[s, jj] - m[s])
                l_sc[s] += p[:, :KT] + p[:, KT:]
                acc_sc[s] += jnp.dot(p.astype(BF16), vt, preferred_element_type=F32)

    unrolled(pass2)
    l = [jnp.sum(l_sc[s], axis=-1, keepdims=True) for s in range(2)]
    lam = (jnp.exp(jnp.sum(lq1_ref[...] * lk1_ref[...], axis=-1, keepdims=True))
           - jnp.exp(jnp.sum(lq2_ref[...] * lk2_ref[...], axis=-1, keepdims=True)) + lam_init)
    o = acc_sc[0] / l[0] - lam * (acc_sc[1] / l[1])
    y = o * lax.rsqrt(jnp.mean(o * o, axis=-1, keepdims=True) + DIFF_SUBLN_EPS)
    o_ref[0, 0:live, :] = (y * sub_ref[...] * (1.0 - lam_init)).astype(o_ref.dtype)
    if live < tq:
        o_ref[0, live:tq, :] = jnp.zeros((tq - live, o_ref.shape[2]), o_ref.dtype)


def _attn_segment(qkv3, table, params, lam_init, tq, row0, n_tiles, live):
    B, Lp, W3 = qkv3.shape
    DW = W3 // 3
    hd = params[0].shape[1]
    hw = 2 * hd
    n_heads = DW // hw
    KT = ATTN_TILE
    nt = -(-Lp // KT)
    blk0 = row0 // tq
    kern = functools.partial(_attn_kernel, hd=hd, lam_init=lam_init, nt=nt,
                             tile0=row0 // KT, live=live)
    par = lambda n: pl.BlockSpec((1, n), lambda b, h, i: (0, 0))
    in_specs = [par(hd)] * 4 + [
        par(hw),
        pl.BlockSpec((1, tq, hw), lambda b, h, i: (b, blk0 + i, h)),
        pl.BlockSpec((1, Lp, hw), lambda b, h, i: (b, 0, n_heads + h)),
        pl.BlockSpec((1, Lp, hw), lambda b, h, i: (b, 0, 2 * n_heads + h)),
        pl.BlockSpec((2, table.shape[1], tq, KT), lambda b, h, i: (h, 0, 0, 0))]
    args = list(params) + [qkv3, qkv3, qkv3, table]
    max_pairs = (nt + 1) // 2
    return pl.pallas_call(
        kern,
        grid=(B, n_heads, n_tiles),
        in_specs=in_specs,
        out_specs=pl.BlockSpec((1, tq, hw), lambda b, h, i: (b, i, h)),
        out_shape=jax.ShapeDtypeStruct((B, n_tiles * tq, DW), BF16),
        scratch_shapes=[pltpu.VMEM((2, max_pairs, live, 2 * KT), F32), pltpu.VMEM((2, live, KT), F32),
                        pltpu.VMEM((2, live, KT), F32), pltpu.VMEM((2, live, hw), F32),
                        pltpu.VMEM((nt * KT, hw), BF16), pltpu.VMEM((nt * KT, hw), BF16)],
        compiler_params=_cparams(("parallel", "parallel", "arbitrary")),
        name="diff_attention_%d" % tq,
    )(*args)


def _attn_segments(Lp):
    big = ATTN_Q_TILE
    n_big = Lp // big
    segs = []
    if n_big:
        segs.append((big, 0, n_big))
    row0 = n_big * big
    n_mid = (Lp - row0) // ATTN_TILE
    if n_mid:
        segs.append((ATTN_TILE, row0, n_mid))
        row0 += n_mid * ATTN_TILE
    if Lp > row0:
        segs.append((Lp - row0, row0, 1))
    return segs


def _attn_call(zqkv, tables, lq1, lk1, lq2, lk2, subln, B, L, Lp, lam_init):
    vec = lambda a: a.reshape(1, -1)
    params = [vec(lq1), vec(lk1), vec(lq2), vec(lk2), vec(subln)]
    qkv3 = zqkv.reshape(B, Lp, zqkv.shape[-1])
    outs = []
    for tq, row0, n_tiles in _attn_segments(Lp):
        live = min(tq, _round_up(L - row0, BF16_ROWS)) if n_tiles == 1 else tq
        outs.append(_attn_segment(qkv3, tables[tq], params, lam_init, tq, row0, n_tiles, live))
    out = outs[0] if len(outs) == 1 else jnp.concatenate(outs, axis=1)
    return out.reshape(B * Lp, -1)


def _rwkv_prep_kernel(*refs, rw, has_vres):
    if has_vres:
        (u_ref, up_ref, w_ref, mu_ref, w0_ref, w2_ref, a0_ref, a2_ref, g2_ref, kk_ref, ka_ref, ones_ref,
         vf_ref, v0_ref, v1_ref, v2_ref,
         r_o, lw_o, k_o, v_o, a_o, b_o, g_o) = refs
    else:
        (u_ref, up_ref, w_ref, mu_ref, w0_ref, w2_ref, a0_ref, a2_ref, g2_ref, kk_ref, ka_ref, ones_ref,
         r_o, lw_o, k_o, v_o, a_o, b_o, g_o) = refs
    i = pl.program_id(1)
    halo = up_ref.shape[1]
    zz = jnp.dot(jnp.concatenate([up_ref[0], u_ref[0]], axis=0), w_ref[...], preferred_element_type=F32)
    z = zz[halo:]
    last_prev = jnp.where(i > 0, zz[halo - 1:halo, :], 0.0)
    row = lax.broadcasted_iota(jnp.int32, z.shape, 0)
    prev = jnp.where(row >= 1, pltpu.roll(z, 1, 0), last_prev)
    zs = z + (prev - z) * mu_ref[...]
    r = zs[:, 0:rw]
    kx = zs[:, rw:2 * rw]
    vx = zs[:, 2 * rw:3 * rw]
    wd = zs[:, 3 * rw:3 * rw + LANES]
    ad = zs[:, 3 * rw + LANES:3 * rw + 2 * LANES]
    gd = zs[:, 3 * rw + 2 * LANES:3 * rw + 3 * LANES]
    wl = w0_ref[...] + _dot(jnp.tanh(wd), w2_ref[...])
    sp = jnp.maximum(-wl, 0.0) + jnp.log(1.0 + jnp.exp(-jnp.abs(wl)))
    lw = -jnp.exp(-sp - 0.5)
    if has_vres:
        gate_v = _sigmoid(v0_ref[...] + _dot(_dot(vx, v1_ref[...]), v2_ref[...]))
        vx = vx + (vf_ref[0] - vx) * gate_v
    a = _sigmoid(a0_ref[...] + _dot(ad, a2_ref[...]))
    g = _dot(_sigmoid(gd), g2_ref[...])
    kk = kx * kk_ref[...]
    ss = _dot_exact_rhs(kk * kk, ones_ref[...])
    kk = kk * lax.rsqrt(jnp.maximum(ss, 1e-24))
    kmod = kx * (1.0 + (a - 1.0) * ka_ref[...])
    r_o[0] = r
    lw_o[0] = lw
    k_o[0] = kmod
    v_o[0] = vx
    a_o[0] = -kk
    b_o[0] = kk * a
    g_o[0] = g


def _pad_rows(w, n):
    return jnp.pad(w, ((0, n - w.shape[0]), (0, 0)))


def _rwkv_prep_call(u, w_rwkv, B, Lp, rw, mu_p, w0, w2, a0, a2, g2, k_k, k_a, ones_bd, v_first, v_res):
    D = u.shape[-1]
    u3 = u.reshape(B, Lp, D)
    HALO = BF16_ROWS
    tl = _pick_tile(Lp, PREP_ROWS, HALO)
    nb = tl // HALO
    has_vres = v_res is not None
    vec = lambda a: a.reshape(1, -1)
    full = lambda a: pl.BlockSpec(a.shape, lambda b, i: (0,) * a.ndim)
    w2p = _pad_rows(w2, LANES).astype(BF16)
    a2p = _pad_rows(a2, LANES).astype(BF16)
    g2p = _pad_rows(g2, LANES).astype(BF16)
    args = [u3, u3, w_rwkv, vec(mu_p), vec(w0), w2p, vec(a0), a2p, g2p, vec(k_k), vec(k_a), ones_bd]
    in_specs = [pl.BlockSpec((1, tl, D), lambda b, i: (b, i, 0)),
                pl.BlockSpec((1, HALO, D), lambda b, i: (b, jnp.maximum(i * nb - 1, 0), 0)),
                pl.BlockSpec(w_rwkv.shape, lambda b, i: (0, 0), pipeline_mode=pl.Buffered(1))]
    in_specs += [full(a) for a in args[3:]]
    if has_vres:
        v0, v1, v2 = v_res
        v1p = jnp.pad(v1, ((0, 0), (0, LANES - v1.shape[1]))).astype(BF16)
        v2p = _pad_rows(v2, LANES).astype(BF16)
        extra = [v_first, vec(v0), v1p, v2p]
        args += extra
        in_specs += [pl.BlockSpec((1, tl, rw), lambda b, i: (b, i, 0))] + [full(a) for a in extra[1:]]
    o_spec = pl.BlockSpec((1, tl, rw), lambda b, i: (b, i, 0))
    o_shape = jax.ShapeDtypeStruct((B, Lp, rw), F32)
    return pl.pallas_call(
        functools.partial(_rwkv_prep_kernel, rw=rw, has_vres=has_vres),
        grid=(B, Lp // tl),
        in_specs=in_specs,
        out_specs=[o_spec] * 7,
        out_shape=[o_shape] * 7,
        compiler_params=_cparams(("parallel", "arbitrary")),
        name="rwkv_prep",
    )(*args)


def _scan_kernel(r_ref, lw_ref, k_ref, v_ref, a_ref, b_ref, g_ref, rk_ref, lnw_ref, lnb_ref, ones_ref,
                 o_ref, h_sc, *, hd):
    nb = r_ref.shape[0]
    C = r_ref.shape[1]
    C2 = 2 * C
    WB = r_ref.shape[2]
    W = nb * WB
    c = pl.program_id(1)

    @pl.when(c == 0)
    def _():
        h_sc[...] = jnp.zeros_like(h_sc)

    side_by_side = lambda ref: jnp.concatenate([ref[i] for i in range(nb)], axis=1)
    r = side_by_side(r_ref)
    lw = side_by_side(lw_ref)
    k = side_by_side(k_ref)
    v = side_by_side(v_ref)
    a = side_by_side(a_ref)
    b = side_by_side(b_ref)

    cum = lw
    d = 1
    while d < C:
        cum = cum + _shift_down(cum, d)
        d *= 2
    tot = cum[C - 1:C, :]
    e_end = jnp.exp(tot - cum)
    e_neg = jnp.exp(-cum)
    at = a * jnp.exp(cum - lw)
    rt = r * jnp.exp(cum)
    bt = b * e_neg
    kt = k * e_neg
    bb = b * e_end
    kb = k * e_end
    p_c = jnp.exp(tot)

    lane = lax.broadcasted_iota(jnp.int32, (C, LANES), 1)
    first = lane < hd
    rr = lax.broadcasted_iota(jnp.int32, (C2, C2), 0)
    cc = lax.broadcasted_iota(jnp.int32, (C2, C2), 1)
    tr = jnp.where(rr >= C, rr - C, rr)
    tc = jnp.where(cc >= C, cc - C, cc)
    strict = tr > tc
    lower = tr >= tc
    eye = rr == cc

    def stack(x):
        return jnp.concatenate([jnp.where(first, x, 0.0), jnp.where(first, 0.0, x)], axis=0)

    pairs = range(W // LANES)
    stacked = lambda x: [stack(x[:, p * LANES:(p + 1) * LANES]) for p in pairs]
    s_a, s_r, s_b, s_k = stacked(at), stacked(rt), stacked(bt), stacked(kt)
    s_bb, s_kb, s_v = stacked(bb), stacked(kb), stacked(v)
    sc = [_dot_nt(jnp.concatenate([s_a[p], s_r[p]], axis=0), jnp.concatenate([s_b[p], s_k[p]], axis=0))
          for p in pairs]
    n = [jnp.where(strict, sc[p][:C2, :C2], 0.0) for p in pairs]
    a_ak = [jnp.where(strict, sc[p][:C2, C2:], 0.0) for p in pairs]
    m_rb = [jnp.where(lower, sc[p][C2:, :C2], 0.0) for p in pairs]
    m_rk = [jnp.where(lower, sc[p][C2:, C2:], 0.0) for p in pairs]
    xv = [_dot(jnp.concatenate([a_ak[p], m_rk[p], s_kb[p].T], axis=0), s_v[p]) for p in pairs]
    t_inv = [jnp.where(eye, 1.0, 0.0) + n[p] for p in pairs]
    pw = [_dot(n[p], n[p]) for p in pairs]
    step = 4
    while step < C:
        tp = [_dot(jnp.concatenate([t_inv[p], pw[p]], axis=0), pw[p]) for p in pairs]
        t_inv = [t_inv[p] + tp[p][:C2] for p in pairs]
        pw = [tp[p][C2:] for p in pairs]
        step *= 2
    t_inv = [t_inv[p] + _dot(t_inv[p], pw[p]) for p in pairs]
    au = [_dot(t_inv[p], jnp.concatenate([s_a[p], xv[p][:C2]], axis=1)) for p in pairs]
    mbu = [_dot(jnp.concatenate([m_rb[p], s_bb[p].T], axis=0), au[p]) for p in pairs]
    ys = []
    for p in pairs:
        r_hat = s_r[p] + mbu[p][:C2, :LANES]
        y0 = mbu[p][:C2, LANES:] + xv[p][C2:2 * C2]
        g_mat = jnp.where(eye, p_c[:, p * LANES:(p + 1) * LANES], 0.0) + mbu[p][C2:, :LANES]
        h_add = mbu[p][C2:, LANES:] + xv[p][2 * C2:]
        yh = _dot(jnp.concatenate([r_hat, g_mat], axis=0), h_sc[p])
        y = yh[:C2] + y0
        h_sc[p] = yh[C2:] + h_add
        ys.append(y[0:C] + y[C:C2])
    ones = ones_ref[...]
    inv_hd = 1.0 / hd
    ppb = WB // LANES
    n = nb * C
    y = jnp.concatenate([jnp.concatenate(ys[i * ppb:(i + 1) * ppb], axis=1) for i in range(nb)], axis=0)
    rk = jnp.concatenate([r_ref[i] * k_ref[i] * rk_ref[...] for i in range(nb)], axis=0)
    sums = _dot_exact_rhs(jnp.concatenate([y, rk], axis=0), ones)
    yc = y - sums[:n] * inv_hd
    var = _dot_exact_rhs(yc * yc, ones) * inv_hd
    yn = yc * lax.rsqrt(var + RWKV_LNX_EPS) * lnw_ref[...] + lnb_ref[...]
    for i in range(nb):
        rows = slice(i * C, (i + 1) * C)
        bonus = sums[n:][rows] * v_ref[i]
        o_ref[i] = ((yn[rows] + bonus) * g_ref[i]).astype(o_ref.dtype)


def _scan_call(streams, r_k, lnx_w, lnx_b, ones_bd, B, Lp, hd):
    r, lw, k, v, a, b, g = streams
    rw = r.shape[-1]
    C = CHUNK
    assert 2 * hd == LANES and rw % LANES == 0 and Lp % C == 0
    vec = lambda x: x.reshape(1, -1)
    nb = SCAN_BATCH if B % SCAN_BATCH == 0 else 1
    blk = pl.BlockSpec((nb, C, rw), lambda bi, c: (bi, c, 0))
    par = pl.BlockSpec((1, rw), lambda bi, c: (0, 0))
    out = pl.pallas_call(
        functools.partial(_scan_kernel, hd=hd),
        grid=(B // nb, Lp // C),
        in_specs=[blk] * 7 + [par] * 3 + [pl.BlockSpec((rw, rw), lambda bi, c: (0, 0))],
        out_specs=blk,
        out_shape=jax.ShapeDtypeStruct((B, Lp, rw), BF16),
        scratch_shapes=[pltpu.VMEM((nb * rw // LANES, LANES, LANES), F32)],
        compiler_params=_cparams(("parallel", "arbitrary")),
        name="rwkv_scan",
    )(r, lw, k, v, a, b, g, vec(r_k), vec(lnx_w), vec(lnx_b), ones_bd)
    return out.reshape(B * Lp, rw)


def _merge_kernel(yp_ref, yd_ref, yr_ref, g0_ref, g1_ref, g2_ref, pa_ref, pb_ref, pc_ref, o_ref):
    dot = lambda x, w: jnp.dot(x[...], w[...], preferred_element_type=F32)
    m = _sigmoid(g0_ref[...].astype(F32)) * dot(yp_ref, pa_ref)
    m = m + _sigmoid(g1_ref[...].astype(F32)) * dot(yd_ref, pb_ref)
    m = m + _sigmoid(g2_ref[...].astype(F32)) * dot(yr_ref, pc_ref)
    o_ref[...] = m.astype(o_ref.dtype)


def _merge_call(yp, yd, yr, zg, pa, pb, pc, layer, tm):
    M = yp.shape[0]
    D = pa.shape[2]
    rows = lambda a: pl.BlockSpec((tm, a.shape[1]), lambda i: (i, 0))
    full = lambda a: pl.BlockSpec((None,) + a.shape[1:], lambda i: (layer, 0, 0))
    gate = lambda n: pl.BlockSpec((tm, D), lambda i: (i, n))
    return pl.pallas_call(
        _merge_kernel,
        grid=(M // tm,),
        in_specs=[rows(yp), rows(yd), rows(yr), gate(0), gate(1), gate(2), full(pa), full(pb), full(pc)],
        out_specs=pl.BlockSpec((tm, D), lambda i: (i, 0)),
        out_shape=jax.ShapeDtypeStruct((M, D), BF16),
        compiler_params=_cparams(("parallel",)),
        name="gated_merge",
    )(yp, yd, yr, zg, zg, zg, pa, pb, pc)


def _ffn_up_kernel(u_ref, wv_ref, wg_ref, cwv_ref, cwg_ref, cbv_ref, cbg_ref, o_ref, *scratch, n_sub):
    xg_scs, xv_scs, a_scs = scratch[:n_sub], scratch[n_sub:2 * n_sub], scratch[2 * n_sub:]
    L = u_ref.shape[0]
    mb = L // n_sub
    st = mb // SUBLANES
    PAD = SUBLANES
    n_slab = a_scs[0].shape[0]
    k_in = math.sqrt(2.0 / math.pi)
    zeros = jnp.zeros((PAD, LANES), F32)
    for s in range(n_slab):
        xv_scs[0][s, 0:PAD, :] = zeros
        xg_scs[0][s, 0:PAD, :] = zeros

    def taps(cw_ref, cb_ref, s):
        sl = slice(s * LANES, (s + 1) * LANES)
        rows = [jnp.broadcast_to(cw_ref[j:j + 1, sl], (SUBLANES, LANES)) for j in range(FFN_CONV)]
        return rows, jnp.broadcast_to(cb_ref[:, sl], (SUBLANES, LANES))

    def tail(sb):
        lo = sb * mb
        xg_sc, xv_sc, a_sc = xg_scs[sb], xv_scs[sb], a_scs[sb]
        for s in range(n_slab):
            wg, bg = taps(cwg_ref, cbg_ref, s)
            wv, bv = taps(cwv_ref, cbv_ref, s)
            ld = lambda ref, j: ref[s, pl.ds(PAD + j, SUBLANES, stride=st), :]
            xg = [ld(xg_sc, j) for j in range(1 - FFN_CONV, 0)]
            xv = [ld(xv_sc, j) for j in range(1 - FFN_CONV, 0)]
            for j in range(st):
                xg.append(ld(xg_sc, j))
                xv.append(ld(xv_sc, j))
                gate = bg
                val = bv
                for t in range(FFN_CONV):
                    gate = gate + xg[j + t] * wg[t]
                    val = val + xv[j + t] * wv[t]
                inner = gate * (2.0 * k_in + (2.0 * k_in * 0.044715) * (gate * gate))
                act = gate / (1.0 + jnp.exp(-inner)) * val
                a_sc[s, pl.ds(j, SUBLANES, stride=st), :] = act
        for s in range(n_slab):
            o_ref[lo:lo + mb, s * LANES:(s + 1) * LANES] = a_sc[s].astype(o_ref.dtype)

    for sb in range(n_sub):
        lo = sb * mb
        u = u_ref[lo:lo + mb, :]
        for x_scs, w_ref in ((xg_scs, wg_ref), (xv_scs, wv_ref)):
            x = jnp.dot(u, w_ref[...], preferred_element_type=F32)
            for s in range(n_slab):
                xs = x[:, s * LANES:(s + 1) * LANES]
                x_scs[sb][s, PAD:PAD + mb, :] = xs
                if sb + 1 < n_sub:
                    x_scs[sb + 1][s, 0:PAD, :] = xs[mb - PAD:mb]
        if sb:
            tail(sb - 1)
    tail(n_sub - 1)


def _ffn_up_call(u, w_up, layer, conv_w, conv_b, B, Lp, tn):
    M, D = u.shape
    FF = w_up.shape[2] // 2
    nt = FF // tn
    n_sub = FFN_SUBBLOCKS if Lp % (FFN_SUBBLOCKS * BF16_ROWS) == 0 else 1
    mb = Lp // n_sub
    return pl.pallas_call(
        functools.partial(_ffn_up_kernel, n_sub=n_sub),
        grid=(B, nt),
        in_specs=[pl.BlockSpec((Lp, D), lambda b, j: (b, 0)),
                  pl.BlockSpec((None, D, tn), lambda b, j: (layer, 0, j)),
                  pl.BlockSpec((None, D, tn), lambda b, j: (layer, 0, nt + j)),
                  pl.BlockSpec((FFN_CONV, tn), lambda b, j: (0, j)),
                  pl.BlockSpec((FFN_CONV, tn), lambda b, j: (0, nt + j)),
                  pl.BlockSpec((1, tn), lambda b, j: (0, j)),
                  pl.BlockSpec((1, tn), lambda b, j: (0, nt + j))],
        out_specs=pl.BlockSpec((Lp, tn), lambda b, j: (b, j)),
        out_shape=jax.ShapeDtypeStruct((M, FF), BF16),
        scratch_shapes=([pltpu.VMEM((tn // LANES, SUBLANES + mb, LANES), F32)] * (2 * n_sub)
                        + [pltpu.VMEM((tn // LANES, mb, LANES), F32)] * n_sub),
        compiler_params=_cparams(("parallel", "arbitrary")),
        name="ffn_up_conv_geglu",
    )(u, w_up, w_up, conv_w, conv_w, conv_b.reshape(1, -1), conv_b.reshape(1, -1))


def kernel(x, meta, rel_bias, norm_mix_pre, norm_mix_post, norm_ffn_pre, norm_ffn_post, w_in, pool_w, pool_scale, diff_lq1, diff_lk1, diff_lq2, diff_lk2, diff_subln, rwkv_mu, rwkv_w0, rwkv_w2, rwkv_a0, rwkv_a2, rwkv_g2, rwkv_kk, rwkv_ka, rwkv_rk, rwkv_lnx_w, rwkv_lnx_b, rwkv_v0, rwkv_v1, rwkv_v2, p_a, p_b, p_c, w_o, ffn_up, ffn_conv_w, ffn_conv_b, ffn_down):
    B, S, D = x.shape
    n_meta = meta.shape[0]
    depth = w_in.shape[0]
    L = S + n_meta
    Lp = _round_up(L, CHUNK)
    M = B * Lp

    PW = pool_scale.shape[1]
    DW = p_b.shape[1]
    RW = rwkv_w0.shape[1]
    hd_r = rwkv_rk.shape[2]
    dl, al, gl = rwkv_w2.shape[1], rwkv_a2.shape[1], rwkv_g2.shape[1]
    off_q = PW
    off_r = PW + 3 * DW
    off_g = off_r + 3 * RW + dl + al + gl
    assert off_g + 3 * D == w_in.shape[2]
    assert Lp == L or S % CHUNK == 0
    assert max(dl, al, gl, rwkv_v1.shape[2]) <= LANES

    h = jnp.concatenate([jnp.broadcast_to(meta.astype(x.dtype)[None], (B, n_meta, D)), x,
                         jnp.zeros((B, Lp - L, D), x.dtype)], axis=1).reshape(M, D)

    hid = np.arange(RW) // hd_r
    ones_bd = jnp.asarray((hid[:, None] == hid[None, :]).astype(np.float32), dtype=BF16)

    def pad_lora(a, axis):
        segs = [lax.slice_in_dim(a, 0, 3 * RW, axis=axis)]
        o = 3 * RW
        for n in (dl, al, gl):
            seg = lax.slice_in_dim(a, o, o + n, axis=axis)
            pad = [(0, 0)] * a.ndim
            pad[axis] = (0, LANES - n)
            segs.append(jnp.pad(seg, pad))
            o += n
        return jnp.concatenate(segs, axis=axis)

    tables = {}
    for tq, _, _ in _attn_segments(Lp):
        tt = _round_up(tq, ATTN_TILE)
        if tt not in tables:
            tables[tt] = _bias_table_call(rel_bias, n_meta, tt)
        tables[tq] = tables[tt]

    tm_big = _pick_tile(M, MM_ROWS, BF16_ROWS)
    tm_res = _pick_tile(M, EW_ROWS, BF16_ROWS)
    u = _rms_call(h, norm_mix_pre[0], tm_res)
    v_first = None
    w_in_b, p_a_b, p_b_b, p_c_b, w_o_b, ffn_up_b, ffn_down_b = (
        a.astype(BF16) for a in (w_in, p_a, p_b, p_c, w_o, ffn_up, ffn_down))
    for l in range(depth):
        lam_init = 0.8 - 0.6 * math.exp(-0.3 * l)
        w_gate = w_in_b[l, :, off_g:]
        w_rwkv = pad_lora(w_in_b[l, :, off_r:off_g], 1)
        mu_p = pad_lora(rwkv_mu[l], 0)

        zg = _mm_call(u, w_gate, BF16, tm_big, _pick_tile(3 * D, MM_COLS, LANES), "in_proj_gate")
        zqkv = _mm_call(u, w_in_b[l, :, off_q:off_r], BF16, tm_big, _pick_tile(3 * DW, MM_COLS, LANES),
                        "in_proj_qkv")
        zp = _mm_call(u, w_in_b[l, :, :PW], BF16, tm_big, _pick_tile(PW, MM_COLS, LANES), "in_proj_pool")

        yp = _pool_call(zp, pool_w[l], pool_scale[l], B, Lp)
        yd = _attn_call(zqkv, tables, diff_lq1[l], diff_lk1[l], diff_lq2[l], diff_lk2[l],
                        diff_subln[l], B, L, Lp, lam_init)
        v_res = None if l == 0 else (rwkv_v0[l - 1], rwkv_v1[l - 1], rwkv_v2[l - 1])
        streams = _rwkv_prep_call(u, w_rwkv, B, Lp, RW, mu_p, rwkv_w0[l], rwkv_w2[l], rwkv_a0[l], rwkv_a2[l],
                                  rwkv_g2[l], rwkv_kk[l], rwkv_ka[l], ones_bd, v_first, v_res)
        if l == 0:
            v_first = streams[3]
        yr = _scan_call(streams, rwkv_rk[l], rwkv_lnx_w[l], rwkv_lnx_b[l], ones_bd, B, Lp, hd_r)

        merged = _merge_call(yp, yd, yr, zg, p_a_b, p_b_b, p_c_b, l, tm_res)
        h, u2 = _mm_res_call(merged, w_o_b, l, h, norm_mix_post[l], norm_ffn_pre[l], "out_proj_res")
        FF = ffn_down.shape[1]
        act = _ffn_up_call(u2, ffn_up_b, l, ffn_conv_w[l], ffn_conv_b[l], B, Lp,
                           _pick_tile(FF, FFN_COLS, LANES))
        g_next = norm_mix_pre[l + 1] if l + 1 < depth else norm_mix_pre[l]
        h, u = _mm_res_call(act, ffn_down_b, l, h, norm_ffn_post[l], g_next, "ffn_down_res")
    return h.reshape(B, Lp, D)[:, n_meta:n_meta + S]
```

```python
import functools
import math

import numpy as np
import jax
import jax.numpy as jnp
from jax import lax
from jax.experimental import pallas as pl
from jax.experimental.pallas import tpu as pltpu

F32 = jnp.float32
BF16 = jnp.bfloat16

CHUNK = 64
NORM_EPS = 1e-6
NEG_INF = -1e30
POOL_WINDOWS = (2, 4, 8, 16)
DIFF_SUBLN_EPS = 1e-5
REL_MAX_DIST = 128
RWKV_LNX_EPS = 64e-5
FFN_CONV = 3

LANES = 128
SUBLANES = 8
ATTN_TILE = 128
ATTN_Q_TILE = 512
ATTN_UNROLL = 3
SCAN_BATCH = 4
BF16_ROWS = 2 * SUBLANES
VMEM_LIMIT = 56 * 1024 * 1024
VMEM_HEADROOM = 8 * 1024 * 1024

MM_ROWS = 2176
MM_COLS = 1536
EW_ROWS = 544
PREP_ROWS = 384
FFN_COLS = 512
FFN_SUBBLOCKS = 2


def _cparams(sem):
    return pltpu.CompilerParams(dimension_semantics=sem, vmem_limit_bytes=VMEM_LIMIT)


def _round_up(x, m):
    return (x + m - 1) // m * m


def _pick_tile(total, target, mult):
    best = None
    for t in range(mult, min(total, target) + 1, mult):
        if total % t == 0:
            best = t
    assert best is not None, (total, target, mult)
    return best


def _shift_down(x, d):
    row = lax.broadcasted_iota(jnp.int32, x.shape, 0)
    return jnp.where(row >= d, pltpu.roll(x, d, 0), 0.0)


def _sigmoid(x):
    return 1.0 / (1.0 + jnp.exp(-x))


def _dot(a, b):
    return jnp.dot(a.astype(BF16), b.astype(BF16), preferred_element_type=F32)


def _dot_nt(a, b):
    return lax.dot_general(a.astype(BF16), b.astype(BF16), (((1,), (1,)), ((), ())),
                           preferred_element_type=F32)


def _split3(x):
    hi = x.astype(BF16)
    r1 = x - hi.astype(F32)
    mid = r1.astype(BF16)
    lo = (r1 - mid.astype(F32)).astype(BF16)
    return hi, mid, lo


def _dot_exact_rhs(x, ones_bf16):
    n = x.shape[0]
    pieces = jnp.concatenate(_split3(x), axis=0)
    res = jnp.dot(pieces, ones_bf16, preferred_element_type=F32)
    return res[:n] + res[n:2 * n] + res[2 * n:]


def _rms_kernel(x_ref, g_ref, o_ref):
    x = x_ref[...]
    y = x * lax.rsqrt(jnp.mean(x * x, axis=-1, keepdims=True) + NORM_EPS)
    o_ref[...] = (y * g_ref[...]).astype(o_ref.dtype)


def _rms_call(x, g, tm):
    M, D = x.shape
    return pl.pallas_call(
        _rms_kernel,
        grid=(M // tm,),
        in_specs=[pl.BlockSpec((tm, D), lambda i: (i, 0)),
                  pl.BlockSpec((1, D), lambda i: (0, 0))],
        out_specs=pl.BlockSpec((tm, D), lambda i: (i, 0)),
        out_shape=jax.ShapeDtypeStruct((M, D), BF16),
        compiler_params=_cparams(("parallel",)),
        name="rms_norm",
    )(x, g.reshape(1, D))


def _mm_kernel(a_ref, b_ref, o_ref):
    o_ref[...] = jnp.dot(a_ref[...], b_ref[...], preferred_element_type=F32).astype(o_ref.dtype)


def _mm_call(a, b, out_dtype, tm, tn, name):
    M, K = a.shape
    N = b.shape[1]
    return pl.pallas_call(
        _mm_kernel,
        grid=(M // tm, N // tn),
        in_specs=[pl.BlockSpec((tm, K), lambda i, j: (i, 0)),
                  pl.BlockSpec((K, tn), lambda i, j: (0, j))],
        out_specs=pl.BlockSpec((tm, tn), lambda i, j: (i, j)),
        out_shape=jax.ShapeDtypeStruct((M, N), out_dtype),
        compiler_params=_cparams(("parallel", "arbitrary")),
        name=name,
    )(a, b)


def _mm_res_kernel(a_ref, w_ref, h_ref, gp_ref, gn_ref, oh_ref, ou_ref):
    f = jnp.dot(a_ref[...], w_ref[...], preferred_element_type=F32)
    y = f * lax.rsqrt(jnp.mean(f * f, axis=-1, keepdims=True) + NORM_EPS) * gp_ref[...]
    hn = h_ref[...] + y
    oh_ref[...] = hn
    u = hn * lax.rsqrt(jnp.mean(hn * hn, axis=-1, keepdims=True) + NORM_EPS) * gn_ref[...]
    ou_ref[...] = u.astype(ou_ref.dtype)


def _mm_res_tile(M, K, D):
    budget = VMEM_LIMIT - VMEM_HEADROOM
    best = None
    for tm in range(BF16_ROWS, M + 1, BF16_ROWS):
        if M % tm:
            continue
        need = (K * D * 2
                + 2 * tm * K * 2
                + 2 * 2 * tm * D * 4
                + 2 * tm * D * 2
                + 2 * tm * D * 4)
        if need <= budget:
            best = tm
    assert best is not None
    return best


def _mm_res_call(a, w, layer, h, g_post, g_next, name):
    M, K = a.shape
    D = w.shape[2]
    tm = _mm_res_tile(M, K, D)
    return pl.pallas_call(
        _mm_res_kernel,
        grid=(M // tm,),
        in_specs=[pl.BlockSpec((tm, K), lambda i: (i, 0)),
                  pl.BlockSpec((None, K, D), lambda i: (layer, 0, 0), pipeline_mode=pl.Buffered(1)),
                  pl.BlockSpec((tm, D), lambda i: (i, 0)),
                  pl.BlockSpec((1, D), lambda i: (0, 0)),
                  pl.BlockSpec((1, D), lambda i: (0, 0))],
        out_specs=[pl.BlockSpec((tm, D), lambda i: (i, 0)),
                   pl.BlockSpec((tm, D), lambda i: (i, 0))],
        out_shape=[jax.ShapeDtypeStruct((M, D), F32),
                   jax.ShapeDtypeStruct((M, D), BF16)],
        compiler_params=_cparams(("parallel",)),
        name=name,
    )(a, w, h, g_post.reshape(1, D), g_next.reshape(1, D))


def _pool_kernel(z_ref, w_ref, s_ref, o_ref, *, gw):
    L = z_ref.shape[1]
    row = lax.broadcasted_iota(jnp.int32, (L, gw), 0)
    for gi, win in enumerate(POOL_WINDOWS):
        sl = slice(gi * gw, (gi + 1) * gw)
        zg = z_ref[0, :, sl].astype(F32)
        s = zg
        d = 1
        while d < win:
            s = s + _shift_down(s, d)
            d *= 2
        cnt = jnp.minimum(row + 1, win).astype(F32)
        pooled = s / cnt - zg
        y = _dot(pooled, w_ref[gi]) * s_ref[:, sl]
        o_ref[0, :, sl] = y.astype(o_ref.dtype)


def _pool_call(z, w, scale, B, Lp):
    PW = z.shape[-1]
    G = len(POOL_WINDOWS)
    gw = PW // G
    z3 = z.reshape(B, Lp, PW)
    out = pl.pallas_call(
        functools.partial(_pool_kernel, gw=gw),
        grid=(B,),
        in_specs=[pl.BlockSpec((1, Lp, PW), lambda b: (b, 0, 0)),
                  pl.BlockSpec((G, gw, gw), lambda b: (0, 0, 0)),
                  pl.BlockSpec((1, PW), lambda b: (0, 0))],
        out_specs=pl.BlockSpec((1, Lp, PW), lambda b: (b, 0, 0)),
        out_shape=jax.ShapeDtypeStruct((B, Lp, PW), BF16),
        compiler_params=_cparams(("parallel",)),
        name="pool_mixer",
    )(z3, w.astype(BF16), scale.reshape(1, PW))
    return out.reshape(B * Lp, PW)


def _rel_bucket(rel, n_buckets):
    half = n_buckets // 2
    max_exact = half // 2
    n = np.abs(rel)
    large = max_exact + (np.log(np.maximum(n, 1) / max_exact) / math.log(REL_MAX_DIST / max_exact)
                         * (half - max_exact)).astype(np.int32)
    large = np.minimum(large, half - 1)
    return ((rel > 0) * half + np.where(n < max_exact, n, large)).astype(np.int32)


def _bias_table_kernel(rb_ref, bucket_ref, pen_ref, o_ref, *, n_buckets):
    s = pl.program_id(0)
    bucket = bucket_ref[...]
    acc = pen_ref[...]
    for b in range(n_buckets):
        acc = acc + jnp.where(bucket == b, rb_ref[b, s], 0.0)
    o_ref[0] = acc


def _bias_table_call(rel_bias, n_meta, tq):
    n_buckets, n_sub = rel_bias.shape
    KT = ATTN_TILE
    assert tq % KT == 0 and KT % CHUNK == 0 and n_meta <= KT
    n_near = tq // KT + 2
    n_slabs = n_near + 2
    far = n_buckets // 2 - 1
    r = np.arange(tq)[:, None]
    c = np.arange(KT)[None, :]
    assert np.all(_rel_bucket(-(KT + 1 + np.arange(4 * KT)), n_buckets) == far)
    bucket = np.full((n_slabs, tq, KT), far, np.int32)
    pen = np.zeros((n_slabs, tq, KT), np.float32)
    pen[-1] = NEG_INF
    for idx in range(n_near):
        key = (idx - 1) * KT + c
        bucket[1 + idx] = _rel_bucket(key - r, n_buckets)
        visible = np.floor_divide(key - n_meta, CHUNK) <= np.floor_divide(r - n_meta, CHUNK)
        pen[1 + idx] = np.where(visible, 0.0, NEG_INF)
    blk = pl.BlockSpec((n_slabs, tq, KT), lambda s: (0, 0, 0))
    return pl.pallas_call(
        functools.partial(_bias_table_kernel, n_buckets=n_buckets),
        grid=(n_sub,),
        in_specs=[pl.BlockSpec(memory_space=pltpu.SMEM), blk, blk],
        out_specs=pl.BlockSpec((1, n_slabs, tq, KT), lambda s: (s, 0, 0, 0)),
        out_shape=jax.ShapeDtypeStruct((n_sub, n_slabs, tq, KT), F32),
        compiler_params=_cparams(("arbitrary",)),
        name="bias_table_%d" % tq,
    )(rel_bias, jnp.asarray(bucket), jnp.asarray(pen))


def _attn_kernel(*refs, hd, lam_init, nt, tile0, live):
    scale = hd ** -0.5
    (lq1_ref, lk1_ref, lq2_ref, lk2_ref, sub_ref, q_ref, k_ref, v_ref, w_ref,
     o_ref, s_sc, mx_sc, l_sc, acc_sc, k_sc, v_sc) = refs
    KT = ATTN_TILE
    tq = q_ref.shape[1]
    ratio = -(-tq // KT)
    Lk = k_ref.shape[1]

    @pl.when(pl.program_id(2) == 0)
    def _():
        k_sc[0:Lk, :] = k_ref[0]
        v_sc[0:Lk, :] = v_ref[0]
        if k_sc.shape[0] > Lk:
            k_sc[Lk:, :] = jnp.zeros((k_sc.shape[0] - Lk, k_sc.shape[1]), k_sc.dtype)
            v_sc[Lk:, :] = jnp.zeros((v_sc.shape[0] - Lk, v_sc.shape[1]), v_sc.dtype)
    n_slabs = w_ref.shape[1]
    i = pl.program_id(2)
    jt0 = tile0 + i * ratio
    n_need = jnp.minimum(jt0 + ratio + 1, nt)
    n_pairs = (n_need + 1) // 2
    q = q_ref[0, 0:live, :]

    def bias(s, jj):
        rows = []
        for c in range(-(-live // KT)):
            n = min(KT, live - c * KT)
            tiles = []
            for j in (2 * jj, 2 * jj + 1):
                idx = jnp.where(j < nt, jnp.clip(j - (jt0 + c) + 2, 0, n_slabs - 1), n_slabs - 1)
                tiles.append(w_ref[s, idx, 0:n, :])
            rows.append(jnp.concatenate(tiles, axis=1))
        return rows[0] if len(rows) == 1 else jnp.concatenate(rows, axis=0)

    def two_tiles(ref, jj):
        j0 = 2 * jj
        j1 = jnp.minimum(j0 + 1, nt - 1)
        t0 = ref[pl.ds(pl.multiple_of(j0 * KT, KT), KT), :]
        t1 = ref[pl.ds(pl.multiple_of(j1 * KT, KT), KT), :]
        return jnp.concatenate([t0, t1], axis=0)

    mx_sc[...] = jnp.full_like(mx_sc, NEG_INF)

    def unrolled(step):
        def group(t, carry):
            step([ATTN_UNROLL * t + r for r in range(ATTN_UNROLL)])
            return carry

        def single(jj, carry):
            step([jj])
            return carry
        n_groups = n_pairs // ATTN_UNROLL
        lax.fori_loop(0, n_groups, group, 0)
        lax.fori_loop(n_groups * ATTN_UNROLL, n_pairs, single, 0)

    def pass1(jjs):
        kts = [two_tiles(k_sc, jj) for jj in jjs]
        for s in range(2):
            mx = None
            for jj, kt in zip(jjs, kts):
                sc = _dot_nt(q[:, s * hd:(s + 1) * hd], kt[:, s * hd:(s + 1) * hd]) * scale
                sc = sc + bias(s, jj)
                s_sc[s, jj] = sc
                half = jnp.maximum(sc[:, :KT], sc[:, KT:])
                mx = half if mx is None else jnp.maximum(mx, half)
            mx_sc[s] = jnp.maximum(mx_sc[s], mx)

    unrolled(pass1)
    m = [jnp.max(mx_sc[s], axis=-1, keepdims=True) for s in range(2)]
    l_sc[...] = jnp.zeros_like(l_sc)
    acc_sc[...] = jnp.zeros_like(acc_sc)

    def pass2(jjs):
        for jj in jjs:
            vt = two_tiles(v_sc, jj)
            for s in range(2):
                p = jnp.exp(s_sc[s, jj] - m[s])
                l_sc[s] += p[:, :KT] + p[:, KT:]
                acc_sc[s] += jnp.dot(p.astype(BF16), vt, preferred_element_type=F32)

    unrolled(pass2)
    l = [jnp.sum(l_sc[s], axis=-1, keepdims=True) for s in range(2)]
    lam = (jnp.exp(jnp.sum(lq1_ref[...] * lk1_ref[...], axis=-1, keepdims=True))
           - jnp.exp(jnp.sum(lq2_ref[...] * lk2_ref[...], axis=-1, keepdims=True)) + lam_init)
    o = acc_sc[0] / l[0] - lam * (acc_sc[1] / l[1])
    y = o * lax.rsqrt(jnp.mean(o * o, axis=-1, keepdims=True) + DIFF_SUBLN_EPS)
    o_ref[0, 0:live, :] = (y * sub_ref[...] * (1.0 - lam_init)).astype(o_ref.dtype)
    if live < tq:
        o_ref[0, live:tq, :] = jnp.zeros((tq - live, o_ref.shape[2]), o_ref.dtype)


def _attn_segment(qkv3, table, params, lam_init, tq, row0, n_tiles, live):
    B, Lp, W3 = qkv3.shape
    DW = W3 // 3
    hd = params[0].shape[1]
    hw = 2 * hd
    n_heads = DW // hw
    KT = ATTN_TILE
    nt = -(-Lp // KT)
    blk0 = row0 // tq
    kern = functools.partial(_attn_kernel, hd=hd, lam_init=lam_init, nt=nt,
                             tile0=row0 // KT, live=live)
    par = lambda n: pl.BlockSpec((1, n), lambda b, h, i: (0, 0))
    in_specs = [par(hd)] * 4 + [
        par(hw),
        pl.BlockSpec((1, tq, hw), lambda b, h, i: (b, blk0 + i, h)),
        pl.BlockSpec((1, Lp, hw), lambda b, h, i: (b, 0, n_heads + h)),
        pl.BlockSpec((1, Lp, hw), lambda b, h, i: (b, 0, 2 * n_heads + h)),
        pl.BlockSpec((2,) + table.shape[1:], lambda b, h, i: (h, 0, 0, 0))]
    args = list(params) + [qkv3, qkv3, qkv3, table]
    max_pairs = (nt + 1) // 2
    return pl.pallas_call(
        kern,
        grid=(B, n_heads, n_tiles),
        in_specs=in_specs,
        out_specs=pl.BlockSpec((1, tq, hw), lambda b, h, i: (b, i, h)),
        out_shape=jax.ShapeDtypeStruct((B, n_tiles * tq, DW), BF16),
        scratch_shapes=[pltpu.VMEM((2, max_pairs, live, 2 * KT), F32), pltpu.VMEM((2, live, KT), F32),
                        pltpu.VMEM((2, live, KT), F32), pltpu.VMEM((2, live, hw), F32),
                        pltpu.VMEM((nt * KT, hw), BF16), pltpu.VMEM((nt * KT, hw), BF16)],
        compiler_params=_cparams(("parallel", "parallel", "arbitrary")),
        name="diff_attention_%d" % tq,
    )(*args)


def _attn_segments(Lp):
    big = ATTN_Q_TILE
    n_big = Lp // big
    segs = []
    if n_big:
        segs.append((big, 0, n_big))
    row0 = n_big * big
    n_mid = (Lp - row0) // ATTN_TILE
    if n_mid:
        segs.append((ATTN_TILE, row0, n_mid))
        row0 += n_mid * ATTN_TILE
    if Lp > row0:
        segs.append((Lp - row0, row0, 1))
    return segs


def _attn_call(zqkv, tables, lq1, lk1, lq2, lk2, subln, B, L, Lp, lam_init):
    vec = lambda a: a.reshape(1, -1)
    params = [vec(lq1), vec(lk1), vec(lq2), vec(lk2), vec(subln)]
    qkv3 = zqkv.reshape(B, Lp, zqkv.shape[-1])
    outs = []
    for tq, row0, n_tiles in _attn_segments(Lp):
        live = min(tq, _round_up(L - row0, BF16_ROWS)) if n_tiles == 1 else tq
        outs.append(_attn_segment(qkv3, tables[tq], params, lam_init, tq, row0, n_tiles, live))
    out = outs[0] if len(outs) == 1 else jnp.concatenate(outs, axis=1)
    return out.reshape(B * Lp, -1)


def _rwkv_prep_kernel(*refs, rw, has_vres):
    if has_vres:
        (u_ref, up_ref, w_ref, mu_ref, w0_ref, w2_ref, a0_ref, a2_ref, g2_ref, kk_ref, ka_ref, ones_ref,
         vf_ref, v0_ref, v1_ref, v2_ref,
         r_o, lw_o, k_o, v_o, a_o, b_o, g_o) = refs
    else:
        (u_ref, up_ref, w_ref, mu_ref, w0_ref, w2_ref, a0_ref, a2_ref, g2_ref, kk_ref, ka_ref, ones_ref,
         r_o, lw_o, k_o, v_o, a_o, b_o, g_o) = refs
    i = pl.program_id(1)
    halo = up_ref.shape[1]
    zz = jnp.dot(jnp.concatenate([up_ref[0], u_ref[0]], axis=0), w_ref[...], preferred_element_type=F32)
    z = zz[halo:]
    last_prev = jnp.where(i > 0, zz[halo - 1:halo, :], 0.0)
    row = lax.broadcasted_iota(jnp.int32, z.shape, 0)
    prev = jnp.where(row >= 1, pltpu.roll(z, 1, 0), last_prev)
    zs = z + (prev - z) * mu_ref[...]
    r = zs[:, 0:rw]
    kx = zs[:, rw:2 * rw]
    vx = zs[:, 2 * rw:3 * rw]
    wd = zs[:, 3 * rw:3 * rw + LANES]
    ad = zs[:, 3 * rw + LANES:3 * rw + 2 * LANES]
    gd = zs[:, 3 * rw + 2 * LANES:3 * rw + 3 * LANES]
    wl = w0_ref[...] + _dot(jnp.tanh(wd), w2_ref[...])
    sp = jnp.maximum(-wl, 0.0) + jnp.log(1.0 + jnp.exp(-jnp.abs(wl)))
    lw = -jnp.exp(-sp - 0.5)
    if has_vres:
        gate_v = _sigmoid(v0_ref[...] + _dot(_dot(vx, v1_ref[...]), v2_ref[...]))
        vx = vx + (vf_ref[0] - vx) * gate_v
    a = _sigmoid(a0_ref[...] + _dot(ad, a2_ref[...]))
    g = _dot(_sigmoid(gd), g2_ref[...])
    kk = kx * kk_ref[...]
    ss = _dot_exact_rhs(kk * kk, ones_ref[...])
    kk = kk * lax.rsqrt(jnp.maximum(ss, 1e-24))
    kmod = kx * (1.0 + (a - 1.0) * ka_ref[...])
    r_o[0] = r
    lw_o[0] = lw
    k_o[0] = kmod
    v_o[0] = vx
    a_o[0] = -kk
    b_o[0] = kk * a
    g_o[0] = g


def _pad_rows(w, n):
    return jnp.pad(w, ((0, n - w.shape[0]), (0, 0)))


def _rwkv_prep_call(u, w_rwkv, B, Lp, rw, mu_p, w0, w2, a0, a2, g2, k_k, k_a, ones_bd, v_first, v_res):
    D = u.shape[-1]
    u3 = u.reshape(B, Lp, D)
    HALO = BF16_ROWS
    tl = _pick_tile(Lp, PREP_ROWS, HALO)
    nb = tl // HALO
    has_vres = v_res is not None
    vec = lambda a: a.reshape(1, -1)
    full = lambda a: pl.BlockSpec(a.shape, lambda b, i: (0,) * a.ndim)
    w2p = _pad_rows(w2, LANES).astype(BF16)
    a2p = _pad_rows(a2, LANES).astype(BF16)
    g2p = _pad_rows(g2, LANES).astype(BF16)
    args = [u3, u3, w_rwkv, vec(mu_p), vec(w0), w2p, vec(a0), a2p, g2p, vec(k_k), vec(k_a), ones_bd]
    in_specs = [pl.BlockSpec((1, tl, D), lambda b, i: (b, i, 0)),
                pl.BlockSpec((1, HALO, D), lambda b, i: (b, jnp.maximum(i * nb - 1, 0), 0)),
                pl.BlockSpec(w_rwkv.shape, lambda b, i: (0, 0), pipeline_mode=pl.Buffered(1))]
    in_specs += [full(a) for a in args[3:]]
    if has_vres:
        v0, v1, v2 = v_res
        v1p = jnp.pad(v1, ((0, 0), (0, LANES - v1.shape[1]))).astype(BF16)
        v2p = _pad_rows(v2, LANES).astype(BF16)
        extra = [v_first, vec(v0), v1p, v2p]
        args += extra
        in_specs += [pl.BlockSpec((1, tl, rw), lambda b, i: (b, i, 0))] + [full(a) for a in extra[1:]]
    o_spec = pl.BlockSpec((1, tl, rw), lambda b, i: (b, i, 0))
    o_shape = jax.ShapeDtypeStruct((B, Lp, rw), F32)
    return pl.pallas_call(
        functools.partial(_rwkv_prep_kernel, rw=rw, has_vres=has_vres),
        grid=(B, Lp // tl),
        in_specs=in_specs,
        out_specs=[o_spec] * 7,
        out_shape=[o_shape] * 7,
        compiler_params=_cparams(("parallel", "arbitrary")),
        name="rwkv_prep",
    )(*args)


def _scan_kernel(r_ref, lw_ref, k_ref, v_ref, a_ref, b_ref, g_ref, rk_ref, lnw_ref, lnb_ref, ones_ref,
                 o_ref, h_sc, *, hd):
    nb = r_ref.shape[0]
    C = r_ref.shape[1]
    C2 = 2 * C
    WB = r_ref.shape[2]
    W = nb * WB
    c = pl.program_id(1)

    @pl.when(c == 0)
    def _():
        h_sc[...] = jnp.zeros_like(h_sc)

    side_by_side = lambda ref: jnp.concatenate([ref[i] for i in range(nb)], axis=1)
    r = side_by_side(r_ref)
    lw = side_by_side(lw_ref)
    k = side_by_side(k_ref)
    v = side_by_side(v_ref)
    a = side_by_side(a_ref)
    b = side_by_side(b_ref)

    cum = lw
    d = 1
    while d < C:
        cum = cum + _shift_down(cum, d)
        d *= 2
    tot = cum[C - 1:C, :]
    e_end = jnp.exp(tot - cum)
    e_neg = jnp.exp(-cum)
    at = a * jnp.exp(cum - lw)
    rt = r * jnp.exp(cum)
    bt = b * e_neg
    kt = k * e_neg
    bb = b * e_end
    kb = k * e_end
    p_c = jnp.exp(tot)

    lane = lax.broadcasted_iota(jnp.int32, (C, LANES), 1)
    first = lane < hd
    rr = lax.broadcasted_iota(jnp.int32, (C2, C2), 0)
    cc = lax.broadcasted_iota(jnp.int32, (C2, C2), 1)
    tr = jnp.where(rr >= C, rr - C, rr)
    tc = jnp.where(cc >= C, cc - C, cc)
    strict = tr > tc
    lower = tr >= tc
    eye = rr == cc

    def stack(x):
        return jnp.concatenate([jnp.where(first, x, 0.0), jnp.where(first, 0.0, x)], axis=0)

    pairs = range(W // LANES)
    stacked = lambda x: [stack(x[:, p * LANES:(p + 1) * LANES]) for p in pairs]
    s_a, s_r, s_b, s_k = stacked(at), stacked(rt), stacked(bt), stacked(kt)
    s_bb, s_kb, s_v = stacked(bb), stacked(kb), stacked(v)
    sc = [_dot_nt(jnp.concatenate([s_a[p], s_r[p]], axis=0), jnp.concatenate([s_b[p], s_k[p]], axis=0))
          for p in pairs]
    n = [jnp.where(strict, sc[p][:C2, :C2], 0.0) for p in pairs]
    a_ak = [jnp.where(strict, sc[p][:C2, C2:], 0.0) for p in pairs]
    m_rb = [jnp.where(lower, sc[p][C2:, :C2], 0.0) for p in pairs]
    m_rk = [jnp.where(lower, sc[p][C2:, C2:], 0.0) for p in pairs]
    xv = [_dot(jnp.concatenate([a_ak[p], m_rk[p], s_kb[p].T], axis=0), s_v[p]) for p in pairs]
    t_inv = [jnp.where(eye, 1.0, 0.0) + n[p] for p in pairs]
    pw = [_dot(n[p], n[p]) for p in pairs]
    step = 4
    while step < C:
        tp = [_dot(jnp.concatenate([t_inv[p], pw[p]], axis=0), pw[p]) for p in pairs]
        t_inv = [t_inv[p] + tp[p][:C2] for p in pairs]
        pw = [tp[p][C2:] for p in pairs]
        step *= 2
    t_inv = [t_inv[p] + _dot(t_inv[p], pw[p]) for p in pairs]
    au = [_dot(t_inv[p], jnp.concatenate([s_a[p], xv[p][:C2]], axis=1)) for p in pairs]
    mbu = [_dot(jnp.concatenate([m_rb[p], s_bb[p].T], axis=0), au[p]) for p in pairs]
    ys = []
    for p in pairs:
        r_hat = s_r[p] + mbu[p][:C2, :LANES]
        y0 = mbu[p][:C2, LANES:] + xv[p][C2:2 * C2]
        g_mat = jnp.where(eye, p_c[:, p * LANES:(p + 1) * LANES], 0.0) + mbu[p][C2:, :LANES]
        h_add = mbu[p][C2:, LANES:] + xv[p][2 * C2:]
        yh = _dot(jnp.concatenate([r_hat, g_mat], axis=0), h_sc[p])
        y = yh[:C2] + y0
        h_sc[p] = yh[C2:] + h_add
        ys.append(y[0:C] + y[C:C2])
    ones = ones_ref[...]
    inv_hd = 1.0 / hd
    ppb = WB // LANES
    n = nb * C
    y = jnp.concatenate([jnp.concatenate(ys[i * ppb:(i + 1) * ppb], axis=1) for i in range(nb)], axis=0)
    rk = jnp.concatenate([r_ref[i] * k_ref[i] * rk_ref[...] for i in range(nb)], axis=0)
    sums = _dot_exact_rhs(jnp.concatenate([y, rk], axis=0), ones)
    yc = y - sums[:n] * inv_hd
    var = _dot_exact_rhs(yc * yc, ones) * inv_hd
    yn = yc * lax.rsqrt(var + RWKV_LNX_EPS) * lnw_ref[...] + lnb_ref[...]
    for i in range(nb):
        rows = slice(i * C, (i + 1) * C)
        bonus = sums[n:][rows] * v_ref[i]
        o_ref[i] = ((yn[rows] + bonus) * g_ref[i]).astype(o_ref.dtype)


def _scan_call(streams, r_k, lnx_w, lnx_b, ones_bd, B, Lp, hd):
    r, lw, k, v, a, b, g = streams
    rw = r.shape[-1]
    C = CHUNK
    assert 2 * hd == LANES and rw % LANES == 0 and Lp % C == 0
    vec = lambda x: x.reshape(1, -1)
    nb = SCAN_BATCH if B % SCAN_BATCH == 0 else 1
    blk = pl.BlockSpec((nb, C, rw), lambda bi, c: (bi, c, 0))
    par = pl.BlockSpec((1, rw), lambda bi, c: (0, 0))
    out = pl.pallas_call(
        functools.partial(_scan_kernel, hd=hd),
        grid=(B // nb, Lp // C),
        in_specs=[blk] * 7 + [par] * 3 + [pl.BlockSpec((rw, rw), lambda bi, c: (0, 0))],
        out_specs=blk,
        out_shape=jax.ShapeDtypeStruct((B, Lp, rw), BF16),
        scratch_shapes=[pltpu.VMEM((nb * rw // LANES, LANES, LANES), F32)],
        compiler_params=_cparams(("parallel", "arbitrary")),
        name="rwkv_scan",
    )(r, lw, k, v, a, b, g, vec(r_k), vec(lnx_w), vec(lnx_b), ones_bd)
    return out.reshape(B * Lp, rw)


def _merge_kernel(yp_ref, yd_ref, yr_ref, g0_ref, g1_ref, g2_ref, pa_ref, pb_ref, pc_ref, o_ref):
    dot = lambda x, w: jnp.dot(x[...], w[...], preferred_element_type=F32)
    m = _sigmoid(g0_ref[...].astype(F32)) * dot(yp_ref, pa_ref)
    m = m + _sigmoid(g1_ref[...].astype(F32)) * dot(yd_ref, pb_ref)
    m = m + _sigmoid(g2_ref[...].astype(F32)) * dot(yr_ref, pc_ref)
    o_ref[...] = m.astype(o_ref.dtype)


def _merge_call(yp, yd, yr, zg, pa, pb, pc, layer, tm):
    M = yp.shape[0]
    D = pa.shape[2]
    rows = lambda a: pl.BlockSpec((tm, a.shape[1]), lambda i: (i, 0))
    full = lambda a: pl.BlockSpec((None,) + a.shape[1:], lambda i: (layer, 0, 0))
    gate = lambda n: pl.BlockSpec((tm, D), lambda i: (i, n))
    return pl.pallas_call(
        _merge_kernel,
        grid=(M // tm,),
        in_specs=[rows(yp), rows(yd), rows(yr), gate(0), gate(1), gate(2), full(pa), full(pb), full(pc)],
        out_specs=pl.BlockSpec((tm, D), lambda i: (i, 0)),
        out_shape=jax.ShapeDtypeStruct((M, D), BF16),
        compiler_params=_cparams(("parallel",)),
        name="gated_merge",
    )(yp, yd, yr, zg, zg, zg, pa, pb, pc)


def _ffn_up_kernel(u_ref, wv_ref, wg_ref, cwv_ref, cwg_ref, cbv_ref, cbg_ref, o_ref, *scratch, n_sub):
    xg_scs, xv_scs, a_scs = scratch[:n_sub], scratch[n_sub:2 * n_sub], scratch[2 * n_sub:]
    L = u_ref.shape[0]
    mb = L // n_sub
    st = mb // SUBLANES
    PAD = SUBLANES
    n_slab = a_scs[0].shape[0]
    k_in = math.sqrt(2.0 / math.pi)
    zeros = jnp.zeros((PAD, LANES), F32)
    for s in range(n_slab):
        xv_scs[0][s, 0:PAD, :] = zeros
        xg_scs[0][s, 0:PAD, :] = zeros

    def taps(cw_ref, cb_ref, s):
        sl = slice(s * LANES, (s + 1) * LANES)
        rows = [jnp.broadcast_to(cw_ref[j:j + 1, sl], (SUBLANES, LANES)) for j in range(FFN_CONV)]
        return rows, jnp.broadcast_to(cb_ref[:, sl], (SUBLANES, LANES))

    def tail(sb):
        lo = sb * mb
        xg_sc, xv_sc, a_sc = xg_scs[sb], xv_scs[sb], a_scs[sb]
        for s in range(n_slab):
            wg, bg = taps(cwg_ref, cbg_ref, s)
            wv, bv = taps(cwv_ref, cbv_ref, s)
            ld = lambda ref, j: ref[s, pl.ds(PAD + j, SUBLANES, stride=st), :]
            xg = [ld(xg_sc, j) for j in range(1 - FFN_CONV, 0)]
            xv = [ld(xv_sc, j) for j in range(1 - FFN_CONV, 0)]
            for j in range(st):
                xg.append(ld(xg_sc, j))
                xv.append(ld(xv_sc, j))
                gate = bg
                val = bv
                for t in range(FFN_CONV):
                    gate = gate + xg[j + t] * wg[t]
                    val = val + xv[j + t] * wv[t]
                inner = gate * (2.0 * k_in + (2.0 * k_in * 0.044715) * (gate * gate))
                act = gate / (1.0 + jnp.exp(-inner)) * val
                a_sc[s, pl.ds(j, SUBLANES, stride=st), :] = act
        for s in range(n_slab):
            o_ref[lo:lo + mb, s * LANES:(s + 1) * LANES] = a_sc[s].astype(o_ref.dtype)

    for sb in range(n_sub):
        lo = sb * mb
        u = u_ref[lo:lo + mb, :]
        for x_scs, w_ref in ((xg_scs, wg_ref), (xv_scs, wv_ref)):
            x = jnp.dot(u, w_ref[...], preferred_element_type=F32)
            for s in range(n_slab):
                xs = x[:, s * LANES:(s + 1) * LANES]
                x_scs[sb][s, PAD:PAD + mb, :] = xs
                if sb + 1 < n_sub:
                    x_scs[sb + 1][s, 0:PAD, :] = xs[mb - PAD:mb]
        if sb:
            tail(sb - 1)
    tail(n_sub - 1)


def _ffn_up_call(u, w_up, layer, conv_w, conv_b, B, Lp, tn):
    M, D = u.shape
    FF = w_up.shape[2] // 2
    nt = FF // tn
    n_sub = FFN_SUBBLOCKS if Lp % (FFN_SUBBLOCKS * BF16_ROWS) == 0 else 1
    mb = Lp // n_sub
    return pl.pallas_call(
        functools.partial(_ffn_up_kernel, n_sub=n_sub),
        grid=(B, nt),
        in_specs=[pl.BlockSpec((Lp, D), lambda b, j: (b, 0)),
                  pl.BlockSpec((None, D, tn), lambda b, j: (layer, 0, j)),
                  pl.BlockSpec((None, D, tn), lambda b, j: (layer, 0, nt + j)),
                  pl.BlockSpec((FFN_CONV, tn), lambda b, j: (0, j)),
                  pl.BlockSpec((FFN_CONV, tn), lambda b, j: (0, nt + j)),
                  pl.BlockSpec((1, tn), lambda b, j: (0, j)),
                  pl.BlockSpec((1, tn), lambda b, j: (0, nt + j))],
        out_specs=pl.BlockSpec((Lp, tn), lambda b, j: (b, j)),
        out_shape=jax.ShapeDtypeStruct((M, FF), BF16),
        scratch_shapes=([pltpu.VMEM((tn // LANES, SUBLANES + mb, LANES), F32)] * (2 * n_sub)
                        + [pltpu.VMEM((tn // LANES, mb, LANES), F32)] * n_sub),
        compiler_params=_cparams(("parallel", "arbitrary")),
        name="ffn_up_conv_geglu",
    )(u, w_up, w_up, conv_w, conv_w, conv_b.reshape(1, -1), conv_b.reshape(1, -1))


def kernel(x, meta, rel_bias, norm_mix_pre, norm_mix_post, norm_ffn_pre, norm_ffn_post, w_in, pool_w, pool_scale, diff_lq1, diff_lk1, diff_lq2, diff_lk2, diff_subln, rwkv_mu, rwkv_w0, rwkv_w2, rwkv_a0, rwkv_a2, rwkv_g2, rwkv_kk, rwkv_ka, rwkv_rk, rwkv_lnx_w, rwkv_lnx_b, rwkv_v0, rwkv_v1, rwkv_v2, p_a, p_b, p_c, w_o, ffn_up, ffn_conv_w, ffn_conv_b, ffn_down):
    B, S, D = x.shape
    n_meta = meta.shape[0]
    depth = w_in.shape[0]
    L = S + n_meta
    Lp = _round_up(L, CHUNK)
    M = B * Lp

    PW = pool_scale.shape[1]
    DW = p_b.shape[1]
    RW = rwkv_w0.shape[1]
    hd_r = rwkv_rk.shape[2]
    dl, al, gl = rwkv_w2.shape[1], rwkv_a2.shape[1], rwkv_g2.shape[1]
    off_q = PW
    off_r = PW + 3 * DW
    off_g = off_r + 3 * RW + dl + al + gl
    assert off_g + 3 * D == w_in.shape[2]
    assert Lp == L or S % CHUNK == 0
    assert max(dl, al, gl, rwkv_v1.shape[2]) <= LANES

    h = jnp.concatenate([jnp.broadcast_to(meta.astype(x.dtype)[None], (B, n_meta, D)), x,
                         jnp.zeros((B, Lp - L, D), x.dtype)], axis=1).reshape(M, D)

    hid = np.arange(RW) // hd_r
    ones_bd = jnp.asarray((hid[:, None] == hid[None, :]).astype(np.float32), dtype=BF16)

    def pad_lora(a, axis):
        segs = [lax.slice_in_dim(a, 0, 3 * RW, axis=axis)]
        o = 3 * RW
        for n in (dl, al, gl):
            seg = lax.slice_in_dim(a, o, o + n, axis=axis)
            pad = [(0, 0)] * a.ndim
            pad[axis] = (0, LANES - n)
            segs.append(jnp.pad(seg, pad))
            o += n
        return jnp.concatenate(segs, axis=axis)

    table = _bias_table_call(rel_bias, n_meta, ATTN_TILE)
    tables = {tq: table for tq, _, _ in _attn_segments(Lp)}

    tm_big = _pick_tile(M, MM_ROWS, BF16_ROWS)
    tm_res = _pick_tile(M, EW_ROWS, BF16_ROWS)
    u = _rms_call(h, norm_mix_pre[0], tm_res)
    v_first = None
    w_in_b, p_a_b, p_b_b, p_c_b, w_o_b, ffn_up_b, ffn_down_b = (
        a.astype(BF16) for a in (w_in, p_a, p_b, p_c, w_o, ffn_up, ffn_down))
    for l in range(depth):
        lam_init = 0.8 - 0.6 * math.exp(-0.3 * l)
        w_gate = w_in_b[l, :, off_g:]
        w_rwkv = pad_lora(w_in_b[l, :, off_r:off_g], 1)
        mu_p = pad_lora(rwkv_mu[l], 0)

        zg = _mm_call(u, w_gate, BF16, tm_big, _pick_tile(3 * D, MM_COLS, LANES), "in_proj_gate")
        zqkv = _mm_call(u, w_in_b[l, :, off_q:off_r], BF16, tm_big, _pick_tile(3 * DW, MM_COLS, LANES),
                        "in_proj_qkv")
        zp = _mm_call(u, w_in_b[l, :, :PW], BF16, tm_big, _pick_tile(PW, MM_COLS, LANES), "in_proj_pool")

        yp = _pool_call(zp, pool_w[l], pool_scale[l], B, Lp)
        yd = _attn_call(zqkv, tables, diff_lq1[l], diff_lk1[l], diff_lq2[l], diff_lk2[l],
                        diff_subln[l], B, L, Lp, lam_init)
        v_res = None if l == 0 else (rwkv_v0[l - 1], rwkv_v1[l - 1], rwkv_v2[l - 1])
        streams = _rwkv_prep_call(u, w_rwkv, B, Lp, RW, mu_p, rwkv_w0[l], rwkv_w2[l], rwkv_a0[l], rwkv_a2[l],
                                  rwkv_g2[l], rwkv_kk[l], rwkv_ka[l], ones_bd, v_first, v_res)
        if l == 0:
            v_first = streams[3]
        yr = _scan_call(streams, rwkv_rk[l], rwkv_lnx_w[l], rwkv_lnx_b[l], ones_bd, B, Lp, hd_r)

        merged = _merge_call(yp, yd, yr, zg, p_a_b, p_b_b, p_c_b, l, tm_res)
        h, u2 = _mm_res_call(merged, w_o_b, l, h, norm_mix_post[l], norm_ffn_pre[l], "out_proj_res")
        FF = ffn_down.shape[1]
        act = _ffn_up_call(u2, ffn_up_b, l, ffn_conv_w[l], ffn_conv_b[l], B, Lp,
                           _pick_tile(FF, FFN_COLS, LANES))
        g_next = norm_mix_pre[l + 1] if l + 1 < depth else norm_mix_pre[l]
        h, u = _mm_res_call(act, ffn_down_b, l, h, norm_ffn_post[l], g_next, "ffn_down_res")
    return h.reshape(B, Lp, D)[:, n_meta:n_meta + S]
```

```python
import functools
import math

import numpy as np
import jax
import jax.numpy as jnp
from jax import lax
from jax.experimental import pallas as pl
from jax.experimental.pallas import tpu as pltpu

F32 = jnp.float32
BF16 = jnp.bfloat16

CHUNK = 64
NORM_EPS = 1e-6
NEG_INF = -1e30
POOL_WINDOWS = (2, 4, 8, 16)
DIFF_SUBLN_EPS = 1e-5
REL_MAX_DIST = 128
RWKV_LNX_EPS = 64e-5
FFN_CONV = 3

LANES = 128
SUBLANES = 8
ATTN_TILE = 128
ATTN_Q_TILE = 512
ATTN_UNROLL = 3
SCAN_BATCH = 4
BF16_ROWS = 2 * SUBLANES
VMEM_LIMIT = 56 * 1024 * 1024
VMEM_HEADROOM = 8 * 1024 * 1024

MM_ROWS = 2176
MM_COLS = 1536
EW_ROWS = 544
PREP_ROWS = 704
FFN_COLS = 512
FFN_SUBBLOCKS = 2


def _cparams(sem):
    return pltpu.CompilerParams(dimension_semantics=sem, vmem_limit_bytes=VMEM_LIMIT)


def _round_up(x, m):
    return (x + m - 1) // m * m


def _pick_tile(total, target, mult):
    best = None
    for t in range(mult, min(total, target) + 1, mult):
        if total % t == 0:
            best = t
    assert best is not None, (total, target, mult)
    return best


def _shift_down(x, d):
    row = lax.broadcasted_iota(jnp.int32, x.shape, 0)
    return jnp.where(row >= d, pltpu.roll(x, d, 0), 0.0)


def _sigmoid(x):
    return 1.0 / (1.0 + jnp.exp(-x))


def _dot(a, b):
    return jnp.dot(a.astype(BF16), b.astype(BF16), preferred_element_type=F32)


def _dot_nt(a, b):
    return lax.dot_general(a.astype(BF16), b.astype(BF16), (((1,), (1,)), ((), ())),
                           preferred_element_type=F32)


def _split3(x):
    hi = x.astype(BF16)
    r1 = x - hi.astype(F32)
    mid = r1.astype(BF16)
    lo = (r1 - mid.astype(F32)).astype(BF16)
    return hi, mid, lo


def _dot_exact_rhs(x, ones_bf16):
    n = x.shape[0]
    pieces = jnp.concatenate(_split3(x), axis=0)
    res = jnp.dot(pieces, ones_bf16, preferred_element_type=F32)
    return res[:n] + res[n:2 * n] + res[2 * n:]


def _rms_kernel(x_ref, g_ref, o_ref):
    x = x_ref[...]
    y = x * lax.rsqrt(jnp.mean(x * x, axis=-1, keepdims=True) + NORM_EPS)
    o_ref[...] = (y * g_ref[...]).astype(o_ref.dtype)


def _rms_call(x, g, tm):
    M, D = x.shape
    return pl.pallas_call(
        _rms_kernel,
        grid=(M // tm,),
        in_specs=[pl.BlockSpec((tm, D), lambda i: (i, 0)),
                  pl.BlockSpec((1, D), lambda i: (0, 0))],
        out_specs=pl.BlockSpec((tm, D), lambda i: (i, 0)),
        out_shape=jax.ShapeDtypeStruct((M, D), BF16),
        compiler_params=_cparams(("parallel",)),
        name="rms_norm",
    )(x, g.reshape(1, D))


def _mm_kernel(a_ref, b_ref, o_ref):
    o_ref[...] = jnp.dot(a_ref[...], b_ref[...], preferred_element_type=F32).astype(o_ref.dtype)


def _mm_call(a, b, out_dtype, tm, tn, name):
    M, K = a.shape
    N = b.shape[1]
    return pl.pallas_call(
        _mm_kernel,
        grid=(M // tm, N // tn),
        in_specs=[pl.BlockSpec((tm, K), lambda i, j: (i, 0)),
                  pl.BlockSpec((K, tn), lambda i, j: (0, j))],
        out_specs=pl.BlockSpec((tm, tn), lambda i, j: (i, j)),
        out_shape=jax.ShapeDtypeStruct((M, N), out_dtype),
        compiler_params=_cparams(("parallel", "arbitrary")),
        name=name,
    )(a, b)


def _mm_res_kernel(a_ref, w_ref, h_ref, gp_ref, gn_ref, oh_ref, ou_ref):
    f = jnp.dot(a_ref[...], w_ref[...], preferred_element_type=F32)
    y = f * lax.rsqrt(jnp.mean(f * f, axis=-1, keepdims=True) + NORM_EPS) * gp_ref[...]
    hn = h_ref[...] + y
    oh_ref[...] = hn
    u = hn * lax.rsqrt(jnp.mean(hn * hn, axis=-1, keepdims=True) + NORM_EPS) * gn_ref[...]
    ou_ref[...] = u.astype(ou_ref.dtype)


def _mm_res_tile(M, K, D):
    budget = VMEM_LIMIT - VMEM_HEADROOM
    best = None
    for tm in range(BF16_ROWS, M + 1, BF16_ROWS):
        if M % tm:
            continue
        need = (K * D * 2
                + 2 * tm * K * 2
                + 2 * 2 * tm * D * 4
                + 2 * tm * D * 2
                + 2 * tm * D * 4)
        if need <= budget:
            best = tm
    assert best is not None
    return best


def _mm_res_call(a, w, layer, h, g_post, g_next, name):
    M, K = a.shape
    D = w.shape[2]
    tm = _mm_res_tile(M, K, D)
    return pl.pallas_call(
        _mm_res_kernel,
        grid=(M // tm,),
        in_specs=[pl.BlockSpec((tm, K), lambda i: (i, 0)),
                  pl.BlockSpec((None, K, D), lambda i: (layer, 0, 0), pipeline_mode=pl.Buffered(1)),
                  pl.BlockSpec((tm, D), lambda i: (i, 0)),
                  pl.BlockSpec((1, D), lambda i: (0, 0)),
                  pl.BlockSpec((1, D), lambda i: (0, 0))],
        out_specs=[pl.BlockSpec((tm, D), lambda i: (i, 0)),
                   pl.BlockSpec((tm, D), lambda i: (i, 0))],
        out_shape=[jax.ShapeDtypeStruct((M, D), F32),
                   jax.ShapeDtypeStruct((M, D), BF16)],
        compiler_params=_cparams(("parallel",)),
        name=name,
    )(a, w, h, g_post.reshape(1, D), g_next.reshape(1, D))


def _pool_kernel(z_ref, w_ref, s_ref, o_ref, *, gw):
    L = z_ref.shape[1]
    row = lax.broadcasted_iota(jnp.int32, (L, gw), 0)
    for gi, win in enumerate(POOL_WINDOWS):
        sl = slice(gi * gw, (gi + 1) * gw)
        zg = z_ref[0, :, sl].astype(F32)
        s = zg
        d = 1
        while d < win:
            s = s + _shift_down(s, d)
            d *= 2
        cnt = jnp.minimum(row + 1, win).astype(F32)
        pooled = s / cnt - zg
        y = _dot(pooled, w_ref[gi]) * s_ref[:, sl]
        o_ref[0, :, sl] = y.astype(o_ref.dtype)


def _pool_call(z, w, scale, B, Lp):
    PW = z.shape[-1]
    G = len(POOL_WINDOWS)
    gw = PW // G
    z3 = z.reshape(B, Lp, PW)
    out = pl.pallas_call(
        functools.partial(_pool_kernel, gw=gw),
        grid=(B,),
        in_specs=[pl.BlockSpec((1, Lp, PW), lambda b: (b, 0, 0)),
                  pl.BlockSpec((G, gw, gw), lambda b: (0, 0, 0)),
                  pl.BlockSpec((1, PW), lambda b: (0, 0))],
        out_specs=pl.BlockSpec((1, Lp, PW), lambda b: (b, 0, 0)),
        out_shape=jax.ShapeDtypeStruct((B, Lp, PW), BF16),
        compiler_params=_cparams(("parallel",)),
        name="pool_mixer",
    )(z3, w.astype(BF16), scale.reshape(1, PW))
    return out.reshape(B * Lp, PW)


def _rel_bucket(rel, n_buckets):
    half = n_buckets // 2
    max_exact = half // 2
    n = np.abs(rel)
    large = max_exact + (np.log(np.maximum(n, 1) / max_exact) / math.log(REL_MAX_DIST / max_exact)
                         * (half - max_exact)).astype(np.int32)
    large = np.minimum(large, half - 1)
    return ((rel > 0) * half + np.where(n < max_exact, n, large)).astype(np.int32)


def _bias_table_kernel(rb_ref, bucket_ref, pen_ref, o_ref, *, n_buckets):
    s = pl.program_id(0)
    bucket = bucket_ref[...]
    acc = pen_ref[...]
    for b in range(n_buckets):
        acc = acc + jnp.where(bucket == b, rb_ref[b, s], 0.0)
    o_ref[0] = acc


def _bias_table_call(rel_bias, n_meta, tq):
    n_buckets, n_sub = rel_bias.shape
    KT = ATTN_TILE
    assert tq % KT == 0 and KT % CHUNK == 0 and n_meta <= KT
    n_near = tq // KT + 2
    n_slabs = n_near + 2
    far = n_buckets // 2 - 1
    r = np.arange(tq)[:, None]
    c = np.arange(KT)[None, :]
    assert np.all(_rel_bucket(-(KT + 1 + np.arange(4 * KT)), n_buckets) == far)
    bucket = np.full((n_slabs, tq, KT), far, np.int32)
    pen = np.zeros((n_slabs, tq, KT), np.float32)
    pen[-1] = NEG_INF
    for idx in range(n_near):
        key = (idx - 1) * KT + c
        bucket[1 + idx] = _rel_bucket(key - r, n_buckets)
        visible = np.floor_divide(key - n_meta, CHUNK) <= np.floor_divide(r - n_meta, CHUNK)
        pen[1 + idx] = np.where(visible, 0.0, NEG_INF)
    blk = pl.BlockSpec((n_slabs, tq, KT), lambda s: (0, 0, 0))
    return pl.pallas_call(
        functools.partial(_bias_table_kernel, n_buckets=n_buckets),
        grid=(n_sub,),
        in_specs=[pl.BlockSpec(memory_space=pltpu.SMEM), blk, blk],
        out_specs=pl.BlockSpec((1, n_slabs, tq, KT), lambda s: (s, 0, 0, 0)),
        out_shape=jax.ShapeDtypeStruct((n_sub, n_slabs, tq, KT), F32),
        compiler_params=_cparams(("arbitrary",)),
        name="bias_table_%d" % tq,
    )(rel_bias, jnp.asarray(bucket), jnp.asarray(pen))


def _attn_kernel(*refs, hd, lam_init, nt, tile0, live):
    scale = hd ** -0.5
    (lq1_ref, lk1_ref, lq2_ref, lk2_ref, sub_ref, q_ref, k_ref, v_ref, w_ref,
     o_ref, s_sc, mx_sc, l_sc, acc_sc, k_sc, v_sc) = refs
    KT = ATTN_TILE
    tq = q_ref.shape[1]
    ratio = -(-tq // KT)
    Lk = k_ref.shape[1]

    @pl.when(pl.program_id(2) == 0)
    def _():
        k_sc[0:Lk, :] = k_ref[0]
        v_sc[0:Lk, :] = v_ref[0]
        if k_sc.shape[0] > Lk:
            k_sc[Lk:, :] = jnp.zeros((k_sc.shape[0] - Lk, k_sc.shape[1]), k_sc.dtype)
            v_sc[Lk:, :] = jnp.zeros((v_sc.shape[0] - Lk, v_sc.shape[1]), v_sc.dtype)
    n_slabs = w_ref.shape[1]
    i = pl.program_id(2)
    jt0 = tile0 + i * ratio
    n_need = jnp.minimum(jt0 + ratio + 1, nt)
    n_pairs = (n_need + 1) // 2
    q = q_ref[0, 0:live, :]

    def bias(s, jj):
        rows = []
        for c in range(-(-live // KT)):
            n = min(KT, live - c * KT)
            tiles = []
            for j in (2 * jj, 2 * jj + 1):
                idx = jnp.where(j < nt, jnp.clip(j - (jt0 + c) + 2, 0, n_slabs - 1), n_slabs - 1)
                tiles.append(w_ref[s, idx, 0:n, :])
            rows.append(jnp.concatenate(tiles, axis=1))
        return rows[0] if len(rows) == 1 else jnp.concatenate(rows, axis=0)

    def two_tiles(ref, jj):
        j0 = 2 * jj
        j1 = jnp.minimum(j0 + 1, nt - 1)
        t0 = ref[pl.ds(pl.multiple_of(j0 * KT, KT), KT), :]
        t1 = ref[pl.ds(pl.multiple_of(j1 * KT, KT), KT), :]
        return jnp.concatenate([t0, t1], axis=0)

    mx_sc[...] = jnp.full_like(mx_sc, NEG_INF)

    def unrolled(step):
        def group(t, carry):
            step([ATTN_UNROLL * t + r for r in range(ATTN_UNROLL)])
            return carry

        def single(jj, carry):
            step([jj])
            return carry
        n_groups = n_pairs // ATTN_UNROLL
        lax.fori_loop(0, n_groups, group, 0)
        lax.fori_loop(n_groups * ATTN_UNROLL, n_pairs, single, 0)

    def pass1(jjs):
        kts = [two_tiles(k_sc, jj) for jj in jjs]
        for s in range(2):
            mx = None
            for jj, kt in zip(jjs, kts):
                sc = _dot_nt(q[:, s * hd:(s + 1) * hd], kt[:, s * hd:(s + 1) * hd]) * scale
                sc = sc + bias(s, jj)
                s_sc[s, jj] = sc
                half = jnp.maximum(sc[:, :KT], sc[:, KT:])
                mx = half if mx is None else jnp.maximum(mx, half)
            mx_sc[s] = jnp.maximum(mx_sc[s], mx)

    unrolled(pass1)
    m = [jnp.max(mx_sc[s], axis=-1, keepdims=True) for s in range(2)]
    l_sc[...] = jnp.zeros_like(l_sc)
    acc_sc[...] = jnp.zeros_like(acc_sc)

    def pass2(jjs):
        for jj in jjs:
            vt = two_tiles(v_sc, jj)
            for s in range(2):
                p = jnp.exp(s_sc[s, jj] - m[s])
                l_sc[s] += p[:, :KT] + p[:, KT:]
                acc_sc[s] += jnp.dot(p.astype(BF16), vt, preferred_element_type=F32)

    unrolled(pass2)
    l = [jnp.sum(l_sc[s], axis=-1, keepdims=True) for s in range(2)]
    lam = (jnp.exp(jnp.sum(lq1_ref[...] * lk1_ref[...], axis=-1, keepdims=True))
           - jnp.exp(jnp.sum(lq2_ref[...] * lk2_ref[...], axis=-1, keepdims=True)) + lam_init)
    o = acc_sc[0] / l[0] - lam * (acc_sc[1] / l[1])
    y = o * lax.rsqrt(jnp.mean(o * o, axis=-1, keepdims=True) + DIFF_SUBLN_EPS)
    o_ref[0, 0:live, :] = (y * sub_ref[...] * (1.0 - lam_init)).astype(o_ref.dtype)
    if live < tq:
        o_ref[0, live:tq, :] = jnp.zeros((tq - live, o_ref.shape[2]), o_ref.dtype)


def _attn_segment(qkv3, table, params, lam_init, tq, row0, n_tiles, live):
    B, Lp, W3 = qkv3.shape
    DW = W3 // 3
    hd = params[0].shape[1]
    hw = 2 * hd
    n_heads = DW // hw
    KT = ATTN_TILE
    nt = -(-Lp // KT)
    blk0 = row0 // tq
    kern = functools.partial(_attn_kernel, hd=hd, lam_init=lam_init, nt=nt,
                             tile0=row0 // KT, live=live)
    par = lambda n: pl.BlockSpec((1, n), lambda b, h, i: (0, 0))
    in_specs = [par(hd)] * 4 + [
        par(hw),
        pl.BlockSpec((1, tq, hw), lambda b, h, i: (b, blk0 + i, h)),
        pl.BlockSpec((1, Lp, hw), lambda b, h, i: (b, 0, n_heads + h)),
        pl.BlockSpec((1, Lp, hw), lambda b, h, i: (b, 0, 2 * n_heads + h)),
        pl.BlockSpec((2,) + table.shape[1:], lambda b, h, i: (h, 0, 0, 0))]
    args = list(params) + [qkv3, qkv3, qkv3, table]
    max_pairs = (nt + 1) // 2
    return pl.pallas_call(
        kern,
        grid=(B, n_heads, n_tiles),
        in_specs=in_specs,
        out_specs=pl.BlockSpec((1, tq, hw), lambda b, h, i: (b, i, h)),
        out_shape=jax.ShapeDtypeStruct((B, n_tiles * tq, DW), BF16),
        scratch_shapes=[pltpu.VMEM((2, max_pairs, live, 2 * KT), F32), pltpu.VMEM((2, live, KT), F32),
                        pltpu.VMEM((2, live, KT), F32), pltpu.VMEM((2, live, hw), F32),
                        pltpu.VMEM((nt * KT, hw), BF16), pltpu.VMEM((nt * KT, hw), BF16)],
        compiler_params=_cparams(("parallel", "parallel", "arbitrary")),
        name="diff_attention_%d" % tq,
    )(*args)


def _attn_segments(Lp):
    big = ATTN_Q_TILE
    n_big = Lp // big
    segs = []
    if n_big:
        segs.append((big, 0, n_big))
    row0 = n_big * big
    n_mid = (Lp - row0) // ATTN_TILE
    if n_mid:
        segs.append((ATTN_TILE, row0, n_mid))
        row0 += n_mid * ATTN_TILE
    if Lp > row0:
        segs.append((Lp - row0, row0, 1))
    return segs


def _attn_call(zqkv, tables, lq1, lk1, lq2, lk2, subln, B, L, Lp, lam_init):
    vec = lambda a: a.reshape(1, -1)
    params = [vec(lq1), vec(lk1), vec(lq2), vec(lk2), vec(subln)]
    qkv3 = zqkv.reshape(B, Lp, zqkv.shape[-1])
    outs = []
    for tq, row0, n_tiles in _attn_segments(Lp):
        live = min(tq, _round_up(L - row0, BF16_ROWS)) if n_tiles == 1 else tq
        outs.append(_attn_segment(qkv3, tables[tq], params, lam_init, tq, row0, n_tiles, live))
    out = outs[0] if len(outs) == 1 else jnp.concatenate(outs, axis=1)
    return out.reshape(B * Lp, -1)


def _rwkv_prep_kernel(*refs, rw, has_vres):
    if has_vres:
        (u_ref, up_ref, w_ref, mu_ref, w0_ref, w2_ref, a0_ref, a2_ref, g2_ref, kk_ref, ka_ref, ones_ref,
         vf_ref, v0_ref, v1_ref, v2_ref,
         r_o, lw_o, k_o, v_o, a_o, b_o, g_o) = refs
    else:
        (u_ref, up_ref, w_ref, mu_ref, w0_ref, w2_ref, a0_ref, a2_ref, g2_ref, kk_ref, ka_ref, ones_ref,
         r_o, lw_o, k_o, v_o, a_o, b_o, g_o) = refs
    i = pl.program_id(1)
    halo = up_ref.shape[1]
    zz = jnp.dot(jnp.concatenate([up_ref[0], u_ref[0]], axis=0), w_ref[...], preferred_element_type=F32)
    z = zz[halo:]
    last_prev = jnp.where(i > 0, zz[halo - 1:halo, :], 0.0)
    row = lax.broadcasted_iota(jnp.int32, z.shape, 0)
    prev = jnp.where(row >= 1, pltpu.roll(z, 1, 0), last_prev)
    zs = z + (prev - z) * mu_ref[...]
    r = zs[:, 0:rw]
    kx = zs[:, rw:2 * rw]
    vx = zs[:, 2 * rw:3 * rw]
    wd = zs[:, 3 * rw:3 * rw + LANES]
    ad = zs[:, 3 * rw + LANES:3 * rw + 2 * LANES]
    gd = zs[:, 3 * rw + 2 * LANES:3 * rw + 3 * LANES]
    wl = w0_ref[...] + _dot(jnp.tanh(wd), w2_ref[...])
    sp = jnp.maximum(-wl, 0.0) + jnp.log(1.0 + jnp.exp(-jnp.abs(wl)))
    lw = -jnp.exp(-sp - 0.5)
    if has_vres:
        gate_v = _sigmoid(v0_ref[...] + _dot(_dot(vx, v1_ref[...]), v2_ref[...]))
        vx = vx + (vf_ref[0] - vx) * gate_v
    a = _sigmoid(a0_ref[...] + _dot(ad, a2_ref[...]))
    g = _dot(_sigmoid(gd), g2_ref[...])
    kk = kx * kk_ref[...]
    ss = _dot_exact_rhs(kk * kk, ones_ref[...])
    kk = kk * lax.rsqrt(jnp.maximum(ss, 1e-24))
    kmod = kx * (1.0 + (a - 1.0) * ka_ref[...])
    r_o[0] = r
    lw_o[0] = lw
    k_o[0] = kmod
    v_o[0] = vx
    a_o[0] = -kk
    b_o[0] = kk * a
    g_o[0] = g


def _pad_rows(w, n):
    return jnp.pad(w, ((0, n - w.shape[0]), (0, 0)))


def _rwkv_prep_call(u, w_rwkv, B, Lp, rw, mu_p, w0, w2, a0, a2, g2, k_k, k_a, ones_bd, v_first, v_res):
    D = u.shape[-1]
    u3 = u.reshape(B, Lp, D)
    HALO = BF16_ROWS
    tl = _pick_tile(Lp, PREP_ROWS, HALO)
    nb = tl // HALO
    has_vres = v_res is not None
    vec = lambda a: a.reshape(1, -1)
    full = lambda a: pl.BlockSpec(a.shape, lambda b, i: (0,) * a.ndim)
    w2p = _pad_rows(w2, LANES).astype(BF16)
    a2p = _pad_rows(a2, LANES).astype(BF16)
    g2p = _pad_rows(g2, LANES).astype(BF16)
    args = [u3, u3, w_rwkv, vec(mu_p), vec(w0), w2p, vec(a0), a2p, g2p, vec(k_k), vec(k_a), ones_bd]
    in_specs = [pl.BlockSpec((1, tl, D), lambda b, i: (b, i, 0)),
                pl.BlockSpec((1, HALO, D), lambda b, i: (b, jnp.maximum(i * nb - 1, 0), 0)),
                pl.BlockSpec(w_rwkv.shape, lambda b, i: (0, 0), pipeline_mode=pl.Buffered(1))]
    in_specs += [full(a) for a in args[3:]]
    if has_vres:
        v0, v1, v2 = v_res
        v1p = jnp.pad(v1, ((0, 0), (0, LANES - v1.shape[1]))).astype(BF16)
        v2p = _pad_rows(v2, LANES).astype(BF16)
        extra = [v_first, vec(v0), v1p, v2p]
        args += extra
        in_specs += [pl.BlockSpec((1, tl, rw), lambda b, i: (b, i, 0))] + [full(a) for a in extra[1:]]
    o_spec = pl.BlockSpec((1, tl, rw), lambda b, i: (b, i, 0))
    o_shape = jax.ShapeDtypeStruct((B, Lp, rw), F32)
    return pl.pallas_call(
        functools.partial(_rwkv_prep_kernel, rw=rw, has_vres=has_vres),
        grid=(B, Lp // tl),
        in_specs=in_specs,
        out_specs=[o_spec] * 7,
        out_shape=[o_shape] * 7,
        compiler_params=_cparams(("parallel", "arbitrary")),
        name="rwkv_prep",
    )(*args)


def _scan_kernel(r_ref, lw_ref, k_ref, v_ref, a_ref, b_ref, g_ref, rk_ref, lnw_ref, lnb_ref, ones_ref,
                 o_ref, h_sc, *, hd):
    nb = r_ref.shape[0]
    C = r_ref.shape[1]
    C2 = 2 * C
    WB = r_ref.shape[2]
    W = nb * WB
    c = pl.program_id(1)

    @pl.when(c == 0)
    def _():
        h_sc[...] = jnp.zeros_like(h_sc)

    side_by_side = lambda ref: jnp.concatenate([ref[i] for i in range(nb)], axis=1)
    r = side_by_side(r_ref)
    lw = side_by_side(lw_ref)
    k = side_by_side(k_ref)
    v = side_by_side(v_ref)
    a = side_by_side(a_ref)
    b = side_by_side(b_ref)

    cum = lw
    d = 1
    while d < C:
        cum = cum + _shift_down(cum, d)
        d *= 2
    tot = cum[C - 1:C, :]
    e_end = jnp.exp(tot - cum)
    e_neg = jnp.exp(-cum)
    at = a * jnp.exp(cum - lw)
    rt = r * jnp.exp(cum)
    bt = b * e_neg
    kt = k * e_neg
    bb = b * e_end
    kb = k * e_end
    p_c = jnp.exp(tot)

    lane = lax.broadcasted_iota(jnp.int32, (C, LANES), 1)
    first = lane < hd
    rr = lax.broadcasted_iota(jnp.int32, (C2, C2), 0)
    cc = lax.broadcasted_iota(jnp.int32, (C2, C2), 1)
    tr = jnp.where(rr >= C, rr - C, rr)
    tc = jnp.where(cc >= C, cc - C, cc)
    strict = tr > tc
    lower = tr >= tc
    eye = rr == cc

    def stack(x):
        return jnp.concatenate([jnp.where(first, x, 0.0), jnp.where(first, 0.0, x)], axis=0)

    pairs = range(W // LANES)
    stacked = lambda x: [stack(x[:, p * LANES:(p + 1) * LANES]) for p in pairs]
    s_a, s_r, s_b, s_k = stacked(at), stacked(rt), stacked(bt), stacked(kt)
    s_bb, s_kb, s_v = stacked(bb), stacked(kb), stacked(v)
    sc = [_dot_nt(jnp.concatenate([s_a[p], s_r[p]], axis=0), jnp.concatenate([s_b[p], s_k[p]], axis=0))
          for p in pairs]
    n = [jnp.where(strict, sc[p][:C2, :C2], 0.0) for p in pairs]
    a_ak = [jnp.where(strict, sc[p][:C2, C2:], 0.0) for p in pairs]
    m_rb = [jnp.where(lower, sc[p][C2:, :C2], 0.0) for p in pairs]
    m_rk = [jnp.where(lower, sc[p][C2:, C2:], 0.0) for p in pairs]
    xv = [_dot(jnp.concatenate([a_ak[p], m_rk[p], s_kb[p].T], axis=0), s_v[p]) for p in pairs]
    t_inv = [jnp.where(eye, 1.0, 0.0) + n[p] for p in pairs]
    pw = [_dot(n[p], n[p]) for p in pairs]
    step = 4
    while step < C:
        tp = [_dot(jnp.concatenate([t_inv[p], pw[p]], axis=0), pw[p]) for p in pairs]
        t_inv = [t_inv[p] + tp[p][:C2] for p in pairs]
        pw = [tp[p][C2:] for p in pairs]
        step *= 2
    t_inv = [t_inv[p] + _dot(t_inv[p], pw[p]) for p in pairs]
    au = [_dot(t_inv[p], jnp.concatenate([s_a[p], xv[p][:C2]], axis=1)) for p in pairs]
    mbu = [_dot(jnp.concatenate([m_rb[p], s_bb[p].T], axis=0), au[p]) for p in pairs]
    ys = []
    for p in pairs:
        r_hat = s_r[p] + mbu[p][:C2, :LANES]
        y0 = mbu[p][:C2, LANES:] + xv[p][C2:2 * C2]
        g_mat = jnp.where(eye, p_c[:, p * LANES:(p + 1) * LANES], 0.0) + mbu[p][C2:, :LANES]
        h_add = mbu[p][C2:, LANES:] + xv[p][2 * C2:]
        yh = _dot(jnp.concatenate([r_hat, g_mat], axis=0), h_sc[p])
        y = yh[:C2] + y0
        h_sc[p] = yh[C2:] + h_add
        ys.append(y[0:C] + y[C:C2])
    ones = ones_ref[...]
    inv_hd = 1.0 / hd
    ppb = WB // LANES
    n = nb * C
    y = jnp.concatenate([jnp.concatenate(ys[i * ppb:(i + 1) * ppb], axis=1) for i in range(nb)], axis=0)
    rk = jnp.concatenate([r_ref[i] * k_ref[i] * rk_ref[...] for i in range(nb)], axis=0)
    sums = _dot_exact_rhs(jnp.concatenate([y, rk], axis=0), ones)
    yc = y - sums[:n] * inv_hd
    var = _dot_exact_rhs(yc * yc, ones) * inv_hd
    yn = yc * lax.rsqrt(var + RWKV_LNX_EPS) * lnw_ref[...] + lnb_ref[...]
    for i in range(nb):
        rows = slice(i * C, (i + 1) * C)
        bonus = sums[n:][rows] * v_ref[i]
        o_ref[i] = ((yn[rows] + bonus) * g_ref[i]).astype(o_ref.dtype)


def _scan_call(streams, r_k, lnx_w, lnx_b, ones_bd, B, Lp, hd):
    r, lw, k, v, a, b, g = streams
    rw = r.shape[-1]
    C = CHUNK
    assert 2 * hd == LANES and rw % LANES == 0 and Lp % C == 0
    vec = lambda x: x.reshape(1, -1)
    nb = SCAN_BATCH if B % SCAN_BATCH == 0 else 1
    blk = pl.BlockSpec((nb, C, rw), lambda bi, c: (bi, c, 0))
    par = pl.BlockSpec((1, rw), lambda bi, c: (0, 0))
    out = pl.pallas_call(
        functools.partial(_scan_kernel, hd=hd),
        grid=(B // nb, Lp // C),
        in_specs=[blk] * 7 + [par] * 3 + [pl.BlockSpec((rw, rw), lambda bi, c: (0, 0))],
        out_specs=blk,
        out_shape=jax.ShapeDtypeStruct((B, Lp, rw), BF16),
        scratch_shapes=[pltpu.VMEM((nb * rw // LANES, LANES, LANES), F32)],
        compiler_params=_cparams(("parallel", "arbitrary")),
        name="rwkv_scan",
    )(r, lw, k, v, a, b, g, vec(r_k), vec(lnx_w), vec(lnx_b), ones_bd)
    return out.reshape(B * Lp, rw)


def _merge_kernel(yp_ref, yd_ref, yr_ref, g0_ref, g1_ref, g2_ref, pa_ref, pb_ref, pc_ref, o_ref):
    dot = lambda x, w: jnp.dot(x[...], w[...], preferred_element_type=F32)
    m = _sigmoid(g0_ref[...].astype(F32)) * dot(yp_ref, pa_ref)
    m = m + _sigmoid(g1_ref[...].astype(F32)) * dot(yd_ref, pb_ref)
    m = m + _sigmoid(g2_ref[...].astype(F32)) * dot(yr_ref, pc_ref)
    o_ref[...] = m.astype(o_ref.dtype)


def _merge_call(yp, yd, yr, zg, pa, pb, pc, layer, tm):
    M = yp.shape[0]
    D = pa.shape[2]
    rows = lambda a: pl.BlockSpec((tm, a.shape[1]), lambda i: (i, 0))
    full = lambda a: pl.BlockSpec((None,) + a.shape[1:], lambda i: (layer, 0, 0))
    gate = lambda n: pl.BlockSpec((tm, D), lambda i: (i, n))
    return pl.pallas_call(
        _merge_kernel,
        grid=(M // tm,),
        in_specs=[rows(yp), rows(yd), rows(yr), gate(0), gate(1), gate(2), full(pa), full(pb), full(pc)],
        out_specs=pl.BlockSpec((tm, D), lambda i: (i, 0)),
        out_shape=jax.ShapeDtypeStruct((M, D), BF16),
        compiler_params=_cparams(("parallel",)),
        name="gated_merge",
    )(yp, yd, yr, zg, zg, zg, pa, pb, pc)


def _ffn_up_kernel(u_ref, wv_ref, wg_ref, cwv_ref, cwg_ref, cbv_ref, cbg_ref, o_ref, *scratch, n_sub):
    xg_scs, xv_scs, a_scs = scratch[:n_sub], scratch[n_sub:2 * n_sub], scratch[2 * n_sub:]
    L = u_ref.shape[0]
    mb = L // n_sub
    st = mb // SUBLANES
    PAD = SUBLANES
    n_slab = a_scs[0].shape[0]
    k_in = math.sqrt(2.0 / math.pi)
    zeros = jnp.zeros((PAD, LANES), F32)
    for s in range(n_slab):
        xv_scs[0][s, 0:PAD, :] = zeros
        xg_scs[0][s, 0:PAD, :] = zeros

    def taps(cw_ref, cb_ref, s):
        sl = slice(s * LANES, (s + 1) * LANES)
        rows = [jnp.broadcast_to(cw_ref[j:j + 1, sl], (SUBLANES, LANES)) for j in range(FFN_CONV)]
        return rows, jnp.broadcast_to(cb_ref[:, sl], (SUBLANES, LANES))

    def tail(sb):
        lo = sb * mb
        xg_sc, xv_sc, a_sc = xg_scs[sb], xv_scs[sb], a_scs[sb]
        for s in range(n_slab):
            wg, bg = taps(cwg_ref, cbg_ref, s)
            wv, bv = taps(cwv_ref, cbv_ref, s)
            ld = lambda ref, j: ref[s, pl.ds(PAD + j, SUBLANES, stride=st), :]
            xg = [ld(xg_sc, j) for j in range(1 - FFN_CONV, 0)]
            xv = [ld(xv_sc, j) for j in range(1 - FFN_CONV, 0)]
            for j in range(st):
                xg.append(ld(xg_sc, j))
                xv.append(ld(xv_sc, j))
                gate = bg
                val = bv
                for t in range(FFN_CONV):
                    gate = gate + xg[j + t] * wg[t]
                    val = val + xv[j + t] * wv[t]
                inner = gate * (2.0 * k_in + (2.0 * k_in * 0.044715) * (gate * gate))
                act = gate / (1.0 + jnp.exp(-inner)) * val
                a_sc[s, pl.ds(j, SUBLANES, stride=st), :] = act
        for s in range(n_slab):
            o_ref[lo:lo + mb, s * LANES:(s + 1) * LANES] = a_sc[s].astype(o_ref.dtype)

    for sb in range(n_sub):
        lo = sb * mb
        u = u_ref[lo:lo + mb, :]
        for x_scs, w_ref in ((xg_scs, wg_ref), (xv_scs, wv_ref)):
            x = jnp.dot(u, w_ref[...], preferred_element_type=F32)
            for s in range(n_slab):
                xs = x[:, s * LANES:(s + 1) * LANES]
                x_scs[sb][s, PAD:PAD + mb, :] = xs
                if sb + 1 < n_sub:
                    x_scs[sb + 1][s, 0:PAD, :] = xs[mb - PAD:mb]
        if sb:
            tail(sb - 1)
    tail(n_sub - 1)


def _ffn_up_call(u, w_up, layer, conv_w, conv_b, B, Lp, tn):
    M, D = u.shape
    FF = w_up.shape[2] // 2
    nt = FF // tn
    n_sub = FFN_SUBBLOCKS if Lp % (FFN_SUBBLOCKS * BF16_ROWS) == 0 else 1
    mb = Lp // n_sub
    return pl.pallas_call(
        functools.partial(_ffn_up_kernel, n_sub=n_sub),
        grid=(B, nt),
        in_specs=[pl.BlockSpec((Lp, D), lambda b, j: (b, 0)),
                  pl.BlockSpec((None, D, tn), lambda b, j: (layer, 0, j)),
                  pl.BlockSpec((None, D, tn), lambda b, j: (layer, 0, nt + j)),
                  pl.BlockSpec((FFN_CONV, tn), lambda b, j: (0, j)),
                  pl.BlockSpec((FFN_CONV, tn), lambda b, j: (0, nt + j)),
                  pl.BlockSpec((1, tn), lambda b, j: (0, j)),
                  pl.BlockSpec((1, tn), lambda b, j: (0, nt + j))],
        out_specs=pl.BlockSpec((Lp, tn), lambda b, j: (b, j)),
        out_shape=jax.ShapeDtypeStruct((M, FF), BF16),
        scratch_shapes=([pltpu.VMEM((tn // LANES, SUBLANES + mb, LANES), F32)] * (2 * n_sub)
                        + [pltpu.VMEM((tn // LANES, mb, LANES), F32)] * n_sub),
        compiler_params=_cparams(("parallel", "arbitrary")),
        name="ffn_up_conv_geglu",
    )(u, w_up, w_up, conv_w, conv_w, conv_b.reshape(1, -1), conv_b.reshape(1, -1))


def kernel(x, meta, rel_bias, norm_mix_pre, norm_mix_post, norm_ffn_pre, norm_ffn_post, w_in, pool_w, pool_scale, diff_lq1, diff_lk1, diff_lq2, diff_lk2, diff_subln, rwkv_mu, rwkv_w0, rwkv_w2, rwkv_a0, rwkv_a2, rwkv_g2, rwkv_kk, rwkv_ka, rwkv_rk, rwkv_lnx_w, rwkv_lnx_b, rwkv_v0, rwkv_v1, rwkv_v2, p_a, p_b, p_c, w_o, ffn_up, ffn_conv_w, ffn_conv_b, ffn_down):
    B, S, D = x.shape
    n_meta = meta.shape[0]
    depth = w_in.shape[0]
    L = S + n_meta
    Lp = _round_up(L, CHUNK)
    M = B * Lp

    PW = pool_scale.shape[1]
    DW = p_b.shape[1]
    RW = rwkv_w0.shape[1]
    hd_r = rwkv_rk.shape[2]
    dl, al, gl = rwkv_w2.shape[1], rwkv_a2.shape[1], rwkv_g2.shape[1]
    off_q = PW
    off_r = PW + 3 * DW
    off_g = off_r + 3 * RW + dl + al + gl
    assert off_g + 3 * D == w_in.shape[2]
    assert Lp == L or S % CHUNK == 0
    assert max(dl, al, gl, rwkv_v1.shape[2]) <= LANES

    h = jnp.concatenate([jnp.broadcast_to(meta.astype(x.dtype)[None], (B, n_meta, D)), x,
                         jnp.zeros((B, Lp - L, D), x.dtype)], axis=1).reshape(M, D)

    hid = np.arange(RW) // hd_r
    ones_bd = jnp.asarray((hid[:, None] == hid[None, :]).astype(np.float32), dtype=BF16)

    def pad_lora(a, axis):
        segs = [lax.slice_in_dim(a, 0, 3 * RW, axis=axis)]
        o = 3 * RW
        for n in (dl, al, gl):
            seg = lax.slice_in_dim(a, o, o + n, axis=axis)
            pad = [(0, 0)] * a.ndim
            pad[axis] = (0, LANES - n)
            segs.append(jnp.pad(seg, pad))
            o += n
        return jnp.concatenate(segs, axis=axis)

    table = _bias_table_call(rel_bias, n_meta, ATTN_TILE)
    tables = {tq: table for tq, _, _ in _attn_segments(Lp)}

    tm_big = _pick_tile(M, MM_ROWS, BF16_ROWS)
    tm_res = _pick_tile(M, EW_ROWS, BF16_ROWS)
    u = _rms_call(h, norm_mix_pre[0], tm_res)
    v_first = None
    w_in_b, p_a_b, p_b_b, p_c_b, w_o_b, ffn_up_b, ffn_down_b = (
        a.astype(BF16) for a in (w_in, p_a, p_b, p_c, w_o, ffn_up, ffn_down))
    for l in range(depth):
        lam_init = 0.8 - 0.6 * math.exp(-0.3 * l)
        w_gate = w_in_b[l, :, off_g:]
        w_rwkv = pad_lora(w_in_b[l, :, off_r:off_g], 1)
        mu_p = pad_lora(rwkv_mu[l], 0)

        zg = _mm_call(u, w_gate, BF16, tm_big, _pick_tile(3 * D, MM_COLS, LANES), "in_proj_gate")
        zqkv = _mm_call(u, w_in_b[l, :, off_q:off_r], BF16, tm_big, _pick_tile(3 * DW, MM_COLS, LANES),
                        "in_proj_qkv")
        zp = _mm_call(u, w_in_b[l, :, :PW], BF16, tm_big, _pick_tile(PW, MM_COLS, LANES), "in_proj_pool")

        yp = _pool_call(zp, pool_w[l], pool_scale[l], B, Lp)
        yd = _attn_call(zqkv, tables, diff_lq1[l], diff_lk1[l], diff_lq2[l], diff_lk2[l],
                        diff_subln[l], B, L, Lp, lam_init)
        v_res = None if l == 0 else (rwkv_v0[l - 1], rwkv_v1[l - 1], rwkv_v2[l - 1])
        streams = _rwkv_prep_call(u, w_rwkv, B, Lp, RW, mu_p, rwkv_w0[l], rwkv_w2[l], rwkv_a0[l], rwkv_a2[l],
                                  rwkv_g2[l], rwkv_kk[l], rwkv_ka[l], ones_bd, v_first, v_res)
        if l == 0:
            v_first = streams[3]
        yr = _scan_call(streams, rwkv_rk[l], rwkv_lnx_w[l], rwkv_lnx_b[l], ones_bd, B, Lp, hd_r)

        merged = _merge_call(yp, yd, yr, zg, p_a_b, p_b_b, p_c_b, l, tm_res)
        h, u2 = _mm_res_call(merged, w_o_b, l, h, norm_mix_post[l], norm_ffn_pre[l], "out_proj_res")
        FF = ffn_down.shape[1]
        act = _ffn_up_call(u2, ffn_up_b, l, ffn_conv_w[l], ffn_conv_b[l], B, Lp,
                           _pick_tile(FF, FFN_COLS, LANES))
        g_next = norm_mix_pre[l + 1] if l + 1 < depth else norm_mix_pre[l]
        h, u = _mm_res_call(act, ffn_down_b, l, h, norm_ffn_post[l], g_next, "ffn_down_res")
    return h.reshape(B, Lp, D)[:, n_meta:n_meta + S]
```

```python
import functools
import math

import numpy as np
import jax
import jax.numpy as jnp
from jax import lax
from jax.experimental import pallas as pl
from jax.experimental.pallas import tpu as pltpu

F32 = jnp.float32
BF16 = jnp.bfloat16

CHUNK = 64
NORM_EPS = 1e-6
NEG_INF = -1e30
POOL_WINDOWS = (2, 4, 8, 16)
DIFF_SUBLN_EPS = 1e-5
REL_MAX_DIST = 128
RWKV_LNX_EPS = 64e-5
FFN_CONV = 3

LANES = 128
SUBLANES = 8
ATTN_TILE = 128
ATTN_Q_TILE = 512
ATTN_UNROLL = 3
SCAN_BATCH = 4
BF16_ROWS = 2 * SUBLANES
VMEM_LIMIT = 56 * 1024 * 1024
VMEM_HEADROOM = 8 * 1024 * 1024

MM_ROWS = 2176
MM_COLS = 1536
EW_ROWS = 544
PREP_ROWS = 384
FFN_COLS = 512
FFN_SUBBLOCKS = 2


def _cparams(sem):
    return pltpu.CompilerParams(dimension_semantics=sem, vmem_limit_bytes=VMEM_LIMIT)


def _round_up(x, m):
    return (x + m - 1) // m * m


def _pick_tile(total, target, mult):
    best = None
    for t in range(mult, min(total, target) + 1, mult):
        if total % t == 0:
            best = t
    assert best is not None, (total, target, mult)
    return best


def _shift_down(x, d):
    row = lax.broadcasted_iota(jnp.int32, x.shape, 0)
    return jnp.where(row >= d, pltpu.roll(x, d, 0), 0.0)


def _sigmoid(x):
    return 1.0 / (1.0 + jnp.exp(-x))


def _dot(a, b):
    return jnp.dot(a.astype(BF16), b.astype(BF16), preferred_element_type=F32)


def _dot_nt(a, b):
    return lax.dot_general(a.astype(BF16), b.astype(BF16), (((1,), (1,)), ((), ())),
                           preferred_element_type=F32)


def _split3(x):
    hi = x.astype(BF16)
    r1 = x - hi.astype(F32)
    mid = r1.astype(BF16)
    lo = (r1 - mid.astype(F32)).astype(BF16)
    return hi, mid, lo


def _dot_exact_rhs(x, ones_bf16):
    n = x.shape[0]
    pieces = jnp.concatenate(_split3(x), axis=0)
    res = jnp.dot(pieces, ones_bf16, preferred_element_type=F32)
    return res[:n] + res[n:2 * n] + res[2 * n:]


def _rms_kernel(x_ref, g_ref, o_ref):
    x = x_ref[...]
    y = x * lax.rsqrt(jnp.mean(x * x, axis=-1, keepdims=True) + NORM_EPS)
    o_ref[...] = (y * g_ref[...]).astype(o_ref.dtype)


def _rms_call(x, g, tm):
    M, D = x.shape
    return pl.pallas_call(
        _rms_kernel,
        grid=(M // tm,),
        in_specs=[pl.BlockSpec((tm, D), lambda i: (i, 0)),
                  pl.BlockSpec((1, D), lambda i: (0, 0))],
        out_specs=pl.BlockSpec((tm, D), lambda i: (i, 0)),
        out_shape=jax.ShapeDtypeStruct((M, D), BF16),
        compiler_params=_cparams(("parallel",)),
        name="rms_norm",
    )(x, g.reshape(1, D))


def _mm_kernel(a_ref, b_ref, o_ref):
    o_ref[...] = jnp.dot(a_ref[...], b_ref[...], preferred_element_type=F32).astype(o_ref.dtype)


def _mm_call(a, b, out_dtype, tm, tn, name):
    M, K = a.shape
    N = b.shape[1]
    return pl.pallas_call(
        _mm_kernel,
        grid=(M // tm, N // tn),
        in_specs=[pl.BlockSpec((tm, K), lambda i, j: (i, 0)),
                  pl.BlockSpec((K, tn), lambda i, j: (0, j))],
        out_specs=pl.BlockSpec((tm, tn), lambda i, j: (i, j)),
        out_shape=jax.ShapeDtypeStruct((M, N), out_dtype),
        compiler_params=_cparams(("parallel", "arbitrary")),
        name=name,
    )(a, b)


def _mm_res_kernel(a_ref, w_ref, h_ref, gp_ref, gn_ref, oh_ref, ou_ref):
    f = jnp.dot(a_ref[...], w_ref[...], preferred_element_type=F32)
    y = f * lax.rsqrt(jnp.mean(f * f, axis=-1, keepdims=True) + NORM_EPS) * gp_ref[...]
    hn = h_ref[...] + y
    oh_ref[...] = hn
    u = hn * lax.rsqrt(jnp.mean(hn * hn, axis=-1, keepdims=True) + NORM_EPS) * gn_ref[...]
    ou_ref[...] = u.astype(ou_ref.dtype)


def _mm_res_tile(M, K, D):
    budget = VMEM_LIMIT - VMEM_HEADROOM
    best = None
    for tm in range(BF16_ROWS, M + 1, BF16_ROWS):
        if M % tm:
            continue
        need = (K * D * 2
                + 2 * tm * K * 2
                + 2 * 2 * tm * D * 4
                + 2 * tm * D * 2
                + 2 * tm * D * 4)
        if need <= budget:
            best = tm
    assert best is not None
    return best


def _mm_res_call(a, w, layer, h, g_post, g_next, name):
    M, K = a.shape
    D = w.shape[2]
    tm = _mm_res_tile(M, K, D)
    return pl.pallas_call(
        _mm_res_kernel,
        grid=(M // tm,),
        in_specs=[pl.BlockSpec((tm, K), lambda i: (i, 0)),
                  pl.BlockSpec((None, K, D), lambda i: (layer, 0, 0), pipeline_mode=pl.Buffered(1)),
                  pl.BlockSpec((tm, D), lambda i: (i, 0)),
                  pl.BlockSpec((1, D), lambda i: (0, 0)),
                  pl.BlockSpec((1, D), lambda i: (0, 0))],
        out_specs=[pl.BlockSpec((tm, D), lambda i: (i, 0)),
                   pl.BlockSpec((tm, D), lambda i: (i, 0))],
        out_shape=[jax.ShapeDtypeStruct((M, D), F32),
                   jax.ShapeDtypeStruct((M, D), BF16)],
        compiler_params=_cparams(("parallel",)),
        name=name,
    )(a, w, h, g_post.reshape(1, D), g_next.reshape(1, D))


def _pool_kernel(z_ref, w_ref, s_ref, o_ref, *, gw):
    L = z_ref.shape[1]
    row = lax.broadcasted_iota(jnp.int32, (L, gw), 0)
    for gi, win in enumerate(POOL_WINDOWS):
        sl = slice(gi * gw, (gi + 1) * gw)
        zg = z_ref[0, :, sl].astype(F32)
        s = zg
        d = 1
        while d < win:
            s = s + _shift_down(s, d)
            d *= 2
        cnt = jnp.minimum(row + 1, win).astype(F32)
        pooled = s / cnt - zg
        y = _dot(pooled, w_ref[gi]) * s_ref[:, sl]
        o_ref[0, :, sl] = y.astype(o_ref.dtype)


def _pool_call(z, w, scale, B, Lp):
    PW = z.shape[-1]
    G = len(POOL_WINDOWS)
    gw = PW // G
    z3 = z.reshape(B, Lp, PW)
    out = pl.pallas_call(
        functools.partial(_pool_kernel, gw=gw),
        grid=(B,),
        in_specs=[pl.BlockSpec((1, Lp, PW), lambda b: (b, 0, 0)),
                  pl.BlockSpec((G, gw, gw), lambda b: (0, 0, 0)),
                  pl.BlockSpec((1, PW), lambda b: (0, 0))],
        out_specs=pl.BlockSpec((1, Lp, PW), lambda b: (b, 0, 0)),
        out_shape=jax.ShapeDtypeStruct((B, Lp, PW), BF16),
        compiler_params=_cparams(("parallel",)),
        name="pool_mixer",
    )(z3, w.astype(BF16), scale.reshape(1, PW))
    return out.reshape(B * Lp, PW)


def _rel_bucket(rel, n_buckets):
    half = n_buckets // 2
    max_exact = half // 2
    n = np.abs(rel)
    large = max_exact + (np.log(np.maximum(n, 1) / max_exact) / math.log(REL_MAX_DIST / max_exact)
                         * (half - max_exact)).astype(np.int32)
    large = np.minimum(large, half - 1)
    return ((rel > 0) * half + np.where(n < max_exact, n, large)).astype(np.int32)


def _bias_table_kernel(rb_ref, bucket_ref, pen_ref, o_ref, *, n_buckets):
    s = pl.program_id(0)
    bucket = bucket_ref[...]
    acc = pen_ref[...]
    for b in range(n_buckets):
        acc = acc + jnp.where(bucket == b, rb_ref[b, s], 0.0)
    o_ref[0] = acc


def _bias_table_call(rel_bias, n_meta, tq):
    n_buckets, n_sub = rel_bias.shape
    KT = ATTN_TILE
    assert tq % KT == 0 and KT % CHUNK == 0 and n_meta <= KT
    n_near = tq // KT + 2
    n_slabs = n_near + 2
    far = n_buckets // 2 - 1
    r = np.arange(tq)[:, None]
    c = np.arange(KT)[None, :]
    assert np.all(_rel_bucket(-(KT + 1 + np.arange(4 * KT)), n_buckets) == far)
    bucket = np.full((n_slabs, tq, KT), far, np.int32)
    pen = np.zeros((n_slabs, tq, KT), np.float32)
    pen[-1] = NEG_INF
    for idx in range(n_near):
        key = (idx - 1) * KT + c
        bucket[1 + idx] = _rel_bucket(key - r, n_buckets)
        visible = np.floor_divide(key - n_meta, CHUNK) <= np.floor_divide(r - n_meta, CHUNK)
        pen[1 + idx] = np.where(visible, 0.0, NEG_INF)
    blk = pl.BlockSpec((n_slabs, tq, KT), lambda s: (0, 0, 0))
    return pl.pallas_call(
        functools.partial(_bias_table_kernel, n_buckets=n_buckets),
        grid=(n_sub,),
        in_specs=[pl.BlockSpec(memory_space=pltpu.SMEM), blk, blk],
        out_specs=pl.BlockSpec((1, n_slabs, tq, KT), lambda s: (s, 0, 0, 0)),
        out_shape=jax.ShapeDtypeStruct((n_sub, n_slabs, tq, KT), F32),
        compiler_params=_cparams(("arbitrary",)),
        name="bias_table_%d" % tq,
    )(rel_bias, jnp.asarray(bucket), jnp.asarray(pen))


def _attn_kernel(*refs, hd, lam_init, nt, tile0, live):
    scale = hd ** -0.5
    (lq1_ref, lk1_ref, lq2_ref, lk2_ref, sub_ref, q_ref, k_ref, v_ref, w_ref,
     o_ref, s_sc, mx_sc, l_sc, acc_sc, k_sc, v_sc) = refs
    KT = ATTN_TILE
    tq = q_ref.shape[1]
    ratio = -(-tq // KT)
    Lk = k_ref.shape[1]

    @pl.when(pl.program_id(2) == 0)
    def _():
        k_sc[0:Lk, :] = k_ref[0]
        v_sc[0:Lk, :] = v_ref[0]
        if k_sc.shape[0] > Lk:
            k_sc[Lk:, :] = jnp.zeros((k_sc.shape[0] - Lk, k_sc.shape[1]), k_sc.dtype)
            v_sc[Lk:, :] = jnp.zeros((v_sc.shape[0] - Lk, v_sc.shape[1]), v_sc.dtype)
    n_slabs = w_ref.shape[1]
    i = pl.program_id(2)
    jt0 = tile0 + i * ratio
    n_need = jnp.minimum(jt0 + ratio + 1, nt)
    n_pairs = (n_need + 1) // 2
    q = q_ref[0, 0:live, :]

    def bias(s, jj):
        rows = []
        for c in range(-(-live // KT)):
            n = min(KT, live - c * KT)
            tiles = []
            for j in (2 * jj, 2 * jj + 1):
                idx = jnp.where(j < nt, jnp.clip(j - (jt0 + c) + 2, 0, n_slabs - 1), n_slabs - 1)
                tiles.append(w_ref[s, idx, 0:n, :])
            rows.append(jnp.concatenate(tiles, axis=1))
        return rows[0] if len(rows) == 1 else jnp.concatenate(rows, axis=0)

    def two_tiles(ref, jj):
        j0 = 2 * jj
        j1 = jnp.minimum(j0 + 1, nt - 1)
        t0 = ref[pl.ds(pl.multiple_of(j0 * KT, KT), KT), :]
        t1 = ref[pl.ds(pl.multiple_of(j1 * KT, KT), KT), :]
        return jnp.concatenate([t0, t1], axis=0)

    mx_sc[...] = jnp.full_like(mx_sc, NEG_INF)

    def unrolled(step):
        def group(t, carry):
            step([ATTN_UNROLL * t + r for r in range(ATTN_UNROLL)])
            return carry

        def single(jj, carry):
            step([jj])
            return carry
        n_groups = n_pairs // ATTN_UNROLL
        lax.fori_loop(0, n_groups, group, 0)
        lax.fori_loop(n_groups * ATTN_UNROLL, n_pairs, single, 0)

    def pass1(jjs):
        kts = [two_tiles(k_sc, jj) for jj in jjs]
        for s in range(2):
            mx = None
            for jj, kt in zip(jjs, kts):
                sc = _dot_nt(q[:, s * hd:(s + 1) * hd], kt[:, s * hd:(s + 1) * hd]) * scale
                sc = sc + bias(s, jj)
                s_sc[s, jj] = sc
                half = jnp.maximum(sc[:, :KT], sc[:, KT:])
                mx = half if mx is None else jnp.maximum(mx, half)
            mx_sc[s] = jnp.maximum(mx_sc[s], mx)

    unrolled(pass1)
    m = [jnp.max(mx_sc[s], axis=-1, keepdims=True) for s in range(2)]
    l_sc[...] = jnp.zeros_like(l_sc)
    acc_sc[...] = jnp.zeros_like(acc_sc)

    def pass2(jjs):
        for jj in jjs:
            vt = two_tiles(v_sc, jj)
            for s in range(2):
                p = jnp.exp(s_sc[s, jj] - m[s])
                l_sc[s] += p[:, :KT] + p[:, KT:]
                acc_sc[s] += jnp.dot(p.astype(BF16), vt, preferred_element_type=F32)

    unrolled(pass2)
    l = [jnp.sum(l_sc[s], axis=-1, keepdims=True) for s in range(2)]
    lam = (jnp.exp(jnp.sum(lq1_ref[...] * lk1_ref[...], axis=-1, keepdims=True))
           - jnp.exp(jnp.sum(lq2_ref[...] * lk2_ref[...], axis=-1, keepdims=True)) + lam_init)
    o = acc_sc[0] / l[0] - lam * (acc_sc[1] / l[1])
    y = o * lax.rsqrt(jnp.mean(o * o, axis=-1, keepdims=True) + DIFF_SUBLN_EPS)
    o_ref[0, 0:live, :] = (y * sub_ref[...] * (1.0 - lam_init)).astype(o_ref.dtype)
    if live < tq:
        o_ref[0, live:tq, :] = jnp.zeros((tq - live, o_ref.shape[2]), o_ref.dtype)


def _attn_segment(qkv3, table, params, lam_init, tq, row0, n_tiles, live):
    B, Lp, W3 = qkv3.shape
    DW = W3 // 3
    hd = params[0].shape[1]
    hw = 2 * hd
    n_heads = DW // hw
    KT = ATTN_TILE
    nt = -(-Lp // KT)
    blk0 = row0 // tq
    kern = functools.partial(_attn_kernel, hd=hd, lam_init=lam_init, nt=nt,
                             tile0=row0 // KT, live=live)
    par = lambda n: pl.BlockSpec((1, n), lambda b, h, i: (0, 0))
    in_specs = [par(hd)] * 4 + [
        par(hw),
        pl.BlockSpec((1, tq, hw), lambda b, h, i: (b, blk0 + i, h)),
        pl.BlockSpec((1, Lp, hw), lambda b, h, i: (b, 0, n_heads + h)),
        pl.BlockSpec((1, Lp, hw), lambda b, h, i: (b, 0, 2 * n_heads + h)),
        pl.BlockSpec((2,) + table.shape[1:], lambda b, h, i: (h, 0, 0, 0))]
    args = list(params) + [qkv3, qkv3, qkv3, table]
    max_pairs = (nt + 1) // 2
    return pl.pallas_call(
        kern,
        grid=(B, n_heads, n_tiles),
        in_specs=in_specs,
        out_specs=pl.BlockSpec((1, tq, hw), lambda b, h, i: (b, i, h)),
        out_shape=jax.ShapeDtypeStruct((B, n_tiles * tq, DW), BF16),
        scratch_shapes=[pltpu.VMEM((2, max_pairs, live, 2 * KT), F32), pltpu.VMEM((2, live, KT), F32),
                        pltpu.VMEM((2, live, KT), F32), pltpu.VMEM((2, live, hw), F32),
                        pltpu.VMEM((nt * KT, hw), BF16), pltpu.VMEM((nt * KT, hw), BF16)],
        compiler_params=_cparams(("parallel", "parallel", "arbitrary")),
        name="diff_attention_%d" % tq,
    )(*args)


def _attn_segments(Lp):
    big = ATTN_Q_TILE
    n_big = Lp // big
    segs = []
    if n_big:
        segs.append((big, 0, n_big))
    row0 = n_big * big
    n_mid = (Lp - row0) // ATTN_TILE
    if n_mid:
        segs.append((ATTN_TILE, row0, n_mid))
        row0 += n_mid * ATTN_TILE
    if Lp > row0:
        segs.append((Lp - row0, row0, 1))
    return segs


def _attn_call(zqkv, tables, lq1, lk1, lq2, lk2, subln, B, L, Lp, lam_init):
    vec = lambda a: a.reshape(1, -1)
    params = [vec(lq1), vec(lk1), vec(lq2), vec(lk2), vec(subln)]
    qkv3 = zqkv.reshape(B, Lp, zqkv.shape[-1])
    outs = []
    for tq, row0, n_tiles in _attn_segments(Lp):
        live = min(tq, _round_up(L - row0, BF16_ROWS)) if n_tiles == 1 else tq
        outs.append(_attn_segment(qkv3, tables[tq], params, lam_init, tq, row0, n_tiles, live))
    out = outs[0] if len(outs) == 1 else jnp.concatenate(outs, axis=1)
    return out.reshape(B * Lp, -1)


def _rwkv_prep_kernel(*refs, rw, has_vres):
    if has_vres:
        (u_ref, up_ref, w_ref, mu_ref, w0_ref, w2_ref, a0_ref, a2_ref, g2_ref, kk_ref, ka_ref, ones_ref,
         vf_ref, v0_ref, v1_ref, v2_ref,
         r_o, lw_o, k_o, v_o, a_o, b_o, g_o) = refs
    else:
        (u_ref, up_ref, w_ref, mu_ref, w0_ref, w2_ref, a0_ref, a2_ref, g2_ref, kk_ref, ka_ref, ones_ref,
         r_o, lw_o, k_o, v_o, a_o, b_o, g_o) = refs
    i = pl.program_id(1)
    halo = up_ref.shape[1]
    zz = jnp.dot(jnp.concatenate([up_ref[0], u_ref[0]], axis=0), w_ref[...], preferred_element_type=F32)
    z = zz[halo:]
    last_prev = jnp.where(i > 0, zz[halo - 1:halo, :], 0.0)
    row = lax.broadcasted_iota(jnp.int32, z.shape, 0)
    prev = jnp.where(row >= 1, pltpu.roll(z, 1, 0), last_prev)
    zs = z + (prev - z) * mu_ref[...]
    r = zs[:, 0:rw]
    kx = zs[:, rw:2 * rw]
    vx = zs[:, 2 * rw:3 * rw]
    wd = zs[:, 3 * rw:3 * rw + LANES]
    ad = zs[:, 3 * rw + LANES:3 * rw + 2 * LANES]
    gd = zs[:, 3 * rw + 2 * LANES:3 * rw + 3 * LANES]
    wl = w0_ref[...] + _dot(jnp.tanh(wd), w2_ref[...])
    sp = jnp.maximum(-wl, 0.0) + jnp.log(1.0 + jnp.exp(-jnp.abs(wl)))
    lw = -jnp.exp(-sp - 0.5)
    if has_vres:
        gate_v = _sigmoid(v0_ref[...] + _dot(_dot(vx, v1_ref[...]), v2_ref[...]))
        vx = vx + (vf_ref[0] - vx) * gate_v
    a = _sigmoid(a0_ref[...] + _dot(ad, a2_ref[...]))
    g = _dot(_sigmoid(gd), g2_ref[...])
    kk = kx * kk_ref[...]
    ss = _dot_exact_rhs(kk * kk, ones_ref[...])
    kk = kk * lax.rsqrt(jnp.maximum(ss, 1e-24))
    kmod = kx * (1.0 + (a - 1.0) * ka_ref[...])
    r_o[0] = r
    lw_o[0] = lw
    k_o[0] = kmod
    v_o[0] = vx
    a_o[0] = -kk
    b_o[0] = kk * a
    g_o[0] = g


def _pad_rows(w, n):
    return jnp.pad(w, ((0, n - w.shape[0]), (0, 0)))


def _rwkv_prep_call(u, w_rwkv, B, Lp, rw, mu_p, w0, w2, a0, a2, g2, k_k, k_a, ones_bd, v_first, v_res):
    D = u.shape[-1]
    u3 = u.reshape(B, Lp, D)
    HALO = BF16_ROWS
    tl = _pick_tile(Lp, PREP_ROWS, HALO)
    nb = tl // HALO
    has_vres = v_res is not None
    vec = lambda a: a.reshape(1, -1)
    full = lambda a: pl.BlockSpec(a.shape, lambda b, i: (0,) * a.ndim)
    w2p = _pad_rows(w2, LANES).astype(BF16)
    a2p = _pad_rows(a2, LANES).astype(BF16)
    g2p = _pad_rows(g2, LANES).astype(BF16)
    args = [u3, u3, w_rwkv, vec(mu_p), vec(w0), w2p, vec(a0), a2p, g2p, vec(k_k), vec(k_a), ones_bd]
    in_specs = [pl.BlockSpec((1, tl, D), lambda b, i: (b, i, 0)),
                pl.BlockSpec((1, HALO, D), lambda b, i: (b, jnp.maximum(i * nb - 1, 0), 0)),
                pl.BlockSpec(w_rwkv.shape, lambda b, i: (0, 0), pipeline_mode=pl.Buffered(1))]
    in_specs += [full(a) for a in args[3:]]
    if has_vres:
        v0, v1, v2 = v_res
        v1p = jnp.pad(v1, ((0, 0), (0, LANES - v1.shape[1]))).astype(BF16)
        v2p = _pad_rows(v2, LANES).astype(BF16)
        extra = [v_first, vec(v0), v1p, v2p]
        args += extra
        in_specs += [pl.BlockSpec((1, tl, rw), lambda b, i: (b, i, 0))] + [full(a) for a in extra[1:]]
    o_spec = pl.BlockSpec((1, tl, rw), lambda b, i: (b, i, 0))
    o_shape = jax.ShapeDtypeStruct((B, Lp, rw), F32)
    return pl.pallas_call(
        functools.partial(_rwkv_prep_kernel, rw=rw, has_vres=has_vres),
        grid=(B, Lp // tl),
        in_specs=in_specs,
        out_specs=[o_spec] * 7,
        out_shape=[o_shape] * 7,
        compiler_params=_cparams(("parallel", "arbitrary")),
        name="rwkv_prep",
    )(*args)


def _scan_kernel(r_ref, lw_ref, k_ref, v_ref, a_ref, b_ref, g_ref, rk_ref, lnw_ref, lnb_ref, ones_ref,
                 o_ref, h_sc, *, hd):
    nb = r_ref.shape[0]
    C = r_ref.shape[1]
    C2 = 2 * C
    WB = r_ref.shape[2]
    W = nb * WB
    c = pl.program_id(1)

    @pl.when(c == 0)
    def _():
        h_sc[...] = jnp.zeros_like(h_sc)

    side_by_side = lambda ref: jnp.concatenate([ref[i] for i in range(nb)], axis=1)
    r = side_by_side(r_ref)
    lw = side_by_side(lw_ref)
    k = side_by_side(k_ref)
    v = side_by_side(v_ref)
    a = side_by_side(a_ref)
    b = side_by_side(b_ref)

    cum = lw
    d = 1
    while d < C:
        cum = cum + _shift_down(cum, d)
        d *= 2
    tot = cum[C - 1:C, :]
    e_end = jnp.exp(tot - cum)
    e_neg = jnp.exp(-cum)
    at = a * jnp.exp(cum - lw)
    rt = r * jnp.exp(cum)
    bt = b * e_neg
    kt = k * e_neg
    bb = b * e_end
    kb = k * e_end
    p_c = jnp.exp(tot)

    lane = lax.broadcasted_iota(jnp.int32, (C, LANES), 1)
    first = lane < hd
    rr = lax.broadcasted_iota(jnp.int32, (C2, C2), 0)
    cc = lax.broadcasted_iota(jnp.int32, (C2, C2), 1)
    tr = jnp.where(rr >= C, rr - C, rr)
    tc = jnp.where(cc >= C, cc - C, cc)
    strict = tr > tc
    lower = tr >= tc
    eye = rr == cc

    def stack(x):
        return jnp.concatenate([jnp.where(first, x, 0.0), jnp.where(first, 0.0, x)], axis=0)

    pairs = range(W // LANES)
    stacked = lambda x: [stack(x[:, p * LANES:(p + 1) * LANES]) for p in pairs]
    s_a, s_r, s_b, s_k = stacked(at), stacked(rt), stacked(bt), stacked(kt)
    s_bb, s_kb, s_v = stacked(bb), stacked(kb), stacked(v)
    sc = [_dot_nt(jnp.concatenate([s_a[p], s_r[p]], axis=0), jnp.concatenate([s_b[p], s_k[p]], axis=0))
          for p in pairs]
    n = [jnp.where(strict, sc[p][:C2, :C2], 0.0) for p in pairs]
    a_ak = [jnp.where(strict, sc[p][:C2, C2:], 0.0) for p in pairs]
    m_rb = [jnp.where(lower, sc[p][C2:, :C2], 0.0) for p in pairs]
    m_rk = [jnp.where(lower, sc[p][C2:, C2:], 0.0) for p in pairs]
    xv = [_dot(jnp.concatenate([a_ak[p], m_rk[p], s_kb[p].T], axis=0), s_v[p]) for p in pairs]
    t_inv = [jnp.where(eye, 1.0, 0.0) + n[p] for p in pairs]
    pw = [_dot(n[p], n[p]) for p in pairs]
    step = 4
    while step < C:
        tp = [_dot(jnp.concatenate([t_inv[p], pw[p]], axis=0), pw[p]) for p in pairs]
        t_inv = [t_inv[p] + tp[p][:C2] for p in pairs]
        pw = [tp[p][C2:] for p in pairs]
        step *= 2
    t_inv = [t_inv[p] + _dot(t_inv[p], pw[p]) for p in pairs]
    au = [_dot(t_inv[p], jnp.concatenate([s_a[p], xv[p][:C2]], axis=1)) for p in pairs]
    mbu = [_dot(jnp.concatenate([m_rb[p], s_bb[p].T], axis=0), au[p]) for p in pairs]
    ys = []
    for p in pairs:
        r_hat = s_r[p] + mbu[p][:C2, :LANES]
        y0 = mbu[p][:C2, LANES:] + xv[p][C2:2 * C2]
        g_mat = jnp.where(eye, p_c[:, p * LANES:(p + 1) * LANES], 0.0) + mbu[p][C2:, :LANES]
        h_add = mbu[p][C2:, LANES:] + xv[p][2 * C2:]
        yh = _dot(jnp.concatenate([r_hat, g_mat], axis=0), h_sc[p])
        y = yh[:C2] + y0
        h_sc[p] = yh[C2:] + h_add
        ys.append(y[0:C] + y[C:C2])
    ones = ones_ref[...]
    inv_hd = 1.0 / hd
    ppb = WB // LANES
    n = nb * C
    y = jnp.concatenate([jnp.concatenate(ys[i * ppb:(i + 1) * ppb], axis=1) for i in range(nb)], axis=0)
    rk = jnp.concatenate([r_ref[i] * k_ref[i] * rk_ref[...] for i in range(nb)], axis=0)
    sums = _dot_exact_rhs(jnp.concatenate([y, rk], axis=0), ones)
    yc = y - sums[:n] * inv_hd
    var = _dot_exact_rhs(yc * yc, ones) * inv_hd
    yn = yc * lax.rsqrt(var + RWKV_LNX_EPS) * lnw_ref[...] + lnb_ref[...]
    for i in range(nb):
        rows = slice(i * C, (i + 1) * C)
        bonus = sums[n:][rows] * v_ref[i]
        o_ref[i] = ((yn[rows] + bonus) * g_ref[i]).astype(o_ref.dtype)


def _scan_call(streams, r_k, lnx_w, lnx_b, ones_bd, B, Lp, hd):
    r, lw, k, v, a, b, g = streams
    rw = r.shape[-1]
    C = CHUNK
    assert 2 * hd == LANES and rw % LANES == 0 and Lp % C == 0
    vec = lambda x: x.reshape(1, -1)
    nb = SCAN_BATCH if B % SCAN_BATCH == 0 else 1
    blk = pl.BlockSpec((nb, C, rw), lambda bi, c: (bi, c, 0))
    par = pl.BlockSpec((1, rw), lambda bi, c: (0, 0))
    out = pl.pallas_call(
        functools.partial(_scan_kernel, hd=hd),
        grid=(B // nb, Lp // C),
        in_specs=[blk] * 7 + [par] * 3 + [pl.BlockSpec((rw, rw), lambda bi, c: (0, 0))],
        out_specs=blk,
        out_shape=jax.ShapeDtypeStruct((B, Lp, rw), BF16),
        scratch_shapes=[pltpu.VMEM((nb * rw // LANES, LANES, LANES), F32)],
        compiler_params=_cparams(("parallel", "arbitrary")),
        name="rwkv_scan",
    )(r, lw, k, v, a, b, g, vec(r_k), vec(lnx_w), vec(lnx_b), ones_bd)
    return out.reshape(B * Lp, rw)


def _merge_kernel(yp_ref, yd_ref, yr_ref, g0_ref, g1_ref, g2_ref, pa_ref, pb_ref, pc_ref, o_ref):
    dot = lambda x, w: jnp.dot(x[...], w[...], preferred_element_type=F32)
    m = _sigmoid(g0_ref[...].astype(F32)) * dot(yp_ref, pa_ref)
    m = m + _sigmoid(g1_ref[...].astype(F32)) * dot(yd_ref, pb_ref)
    m = m + _sigmoid(g2_ref[...].astype(F32)) * dot(yr_ref, pc_ref)
    o_ref[...] = m.astype(o_ref.dtype)


def _merge_call(yp, yd, yr, zg, pa, pb, pc, layer, tm):
    M = yp.shape[0]
    D = pa.shape[2]
    rows = lambda a: pl.BlockSpec((tm, a.shape[1]), lambda i: (i, 0))
    full = lambda a: pl.BlockSpec((None,) + a.shape[1:], lambda i: (layer, 0, 0))
    gate = lambda n: pl.BlockSpec((tm, D), lambda i: (i, n))
    return pl.pallas_call(
        _merge_kernel,
        grid=(M // tm,),
        in_specs=[rows(yp), rows(yd), rows(yr), gate(0), gate(1), gate(2), full(pa), full(pb), full(pc)],
        out_specs=pl.BlockSpec((tm, D), lambda i: (i, 0)),
        out_shape=jax.ShapeDtypeStruct((M, D), BF16),
        compiler_params=_cparams(("parallel",)),
        name="gated_merge",
    )(yp, yd, yr, zg, zg, zg, pa, pb, pc)


def _ffn_up_kernel(u_ref, wv_ref, wg_ref, cwv_ref, cwg_ref, cbv_ref, cbg_ref, o_ref, *scratch, n_sub):
    xg_scs, xv_scs, a_scs = scratch[:n_sub], scratch[n_sub:2 * n_sub], scratch[2 * n_sub:]
    L = u_ref.shape[0]
    mb = L // n_sub
    st = mb // SUBLANES
    PAD = SUBLANES
    n_slab = a_scs[0].shape[0]
    k_in = math.sqrt(2.0 / math.pi)
    zeros = jnp.zeros((PAD, LANES), F32)
    for s in range(n_slab):
        xv_scs[0][s, 0:PAD, :] = zeros
        xg_scs[0][s, 0:PAD, :] = zeros

    def taps(cw_ref, cb_ref, s):
        sl = slice(s * LANES, (s + 1) * LANES)
        rows = [jnp.broadcast_to(cw_ref[j:j + 1, sl], (SUBLANES, LANES)) for j in range(FFN_CONV)]
        return rows, jnp.broadcast_to(cb_ref[:, sl], (SUBLANES, LANES))

    def tail(sb):
        lo = sb * mb
        xg_sc, xv_sc, a_sc = xg_scs[sb], xv_scs[sb], a_scs[sb]
        for s in range(n_slab):
            wg, bg = taps(cwg_ref, cbg_ref, s)
            wv, bv = taps(cwv_ref, cbv_ref, s)
            ld = lambda ref, j: ref[s, pl.ds(PAD + j, SUBLANES, stride=st), :]
            xg = [ld(xg_sc, j) for j in range(1 - FFN_CONV, 0)]
            xv = [ld(xv_sc, j) for j in range(1 - FFN_CONV, 0)]
            for j in range(st):
                xg.append(ld(xg_sc, j))
                xv.append(ld(xv_sc, j))
                gate = bg
                val = bv
                for t in range(FFN_CONV):
                    gate = gate + xg[j + t] * wg[t]
                    val = val + xv[j + t] * wv[t]
                inner = gate * (2.0 * k_in + (2.0 * k_in * 0.044715) * (gate * gate))
                act = gate / (1.0 + jnp.exp(-inner)) * val
                a_sc[s, pl.ds(j, SUBLANES, stride=st), :] = act
        for s in range(n_slab):
            o_ref[lo:lo + mb, s * LANES:(s + 1) * LANES] = a_sc[s].astype(o_ref.dtype)

    for sb in range(n_sub):
        lo = sb * mb
        u = u_ref[lo:lo + mb, :]
        for x_scs, w_ref in ((xg_scs, wg_ref), (xv_scs, wv_ref)):
            x = jnp.dot(u, w_ref[...], preferred_element_type=F32)
            for s in range(n_slab):
                xs = x[:, s * LANES:(s + 1) * LANES]
                x_scs[sb][s, PAD:PAD + mb, :] = xs
                if sb + 1 < n_sub:
                    x_scs[sb + 1][s, 0:PAD, :] = xs[mb - PAD:mb]
        if sb:
            tail(sb - 1)
    tail(n_sub - 1)


def _ffn_up_call(u, w_up, layer, conv_w, conv_b, B, Lp, tn):
    M, D = u.shape
    FF = w_up.shape[2] // 2
    nt = FF // tn
    n_sub = FFN_SUBBLOCKS if Lp % (FFN_SUBBLOCKS * BF16_ROWS) == 0 else 1
    mb = Lp // n_sub
    return pl.pallas_call(
        functools.partial(_ffn_up_kernel, n_sub=n_sub),
        grid=(B, nt),
        in_specs=[pl.BlockSpec((Lp, D), lambda b, j: (b, 0)),
                  pl.BlockSpec((None, D, tn), lambda b, j: (layer, 0, j)),
                  pl.BlockSpec((None, D, tn), lambda b, j: (layer, 0, nt + j)),
                  pl.BlockSpec((FFN_CONV, tn), lambda b, j: (0, j)),
                  pl.BlockSpec((FFN_CONV, tn), lambda b, j: (0, nt + j)),
                  pl.BlockSpec((1, tn), lambda b, j: (0, j)),
                  pl.BlockSpec((1, tn), lambda b, j: (0, nt + j))],
        out_specs=pl.BlockSpec((Lp, tn), lambda b, j: (b, j)),
        out_shape=jax.ShapeDtypeStruct((M, FF), BF16),
        scratch_shapes=([pltpu.VMEM((tn // LANES, SUBLANES + mb, LANES), F32)] * (2 * n_sub)
                        + [pltpu.VMEM((tn // LANES, mb, LANES), F32)] * n_sub),
        compiler_params=_cparams(("parallel", "arbitrary")),
        name="ffn_up_conv_geglu",
    )(u, w_up, w_up, conv_w, conv_w, conv_b.reshape(1, -1), conv_b.reshape(1, -1))


def kernel(x, meta, rel_bias, norm_mix_pre, norm_mix_post, norm_ffn_pre, norm_ffn_post, w_in, pool_w, pool_scale, diff_lq1, diff_lk1, diff_lq2, diff_lk2, diff_subln, rwkv_mu, rwkv_w0, rwkv_w2, rwkv_a0, rwkv_a2, rwkv_g2, rwkv_kk, rwkv_ka, rwkv_rk, rwkv_lnx_w, rwkv_lnx_b, rwkv_v0, rwkv_v1, rwkv_v2, p_a, p_b, p_c, w_o, ffn_up, ffn_conv_w, ffn_conv_b, ffn_down):
    B, S, D = x.shape
    n_meta = meta.shape[0]
    depth = w_in.shape[0]
    L = S + n_meta
    Lp = _round_up(L, CHUNK)
    M = B * Lp

    PW = pool_scale.shape[1]
    DW = p_b.shape[1]
    RW = rwkv_w0.shape[1]
    hd_r = rwkv_rk.shape[2]
    dl, al, gl = rwkv_w2.shape[1], rwkv_a2.shape[1], rwkv_g2.shape[1]
    off_q = PW
    off_r = PW + 3 * DW
    off_g = off_r + 3 * RW + dl + al + gl
    assert off_g + 3 * D == w_in.shape[2]
    assert Lp == L or S % CHUNK == 0
    assert max(dl, al, gl, rwkv_v1.shape[2]) <= LANES

    h = jnp.concatenate([jnp.broadcast_to(meta.astype(x.dtype)[None], (B, n_meta, D)), x,
                         jnp.zeros((B, Lp - L, D), x.dtype)], axis=1).reshape(M, D)

    hid = np.arange(RW) // hd_r
    ones_bd = jnp.asarray((hid[:, None] == hid[None, :]).astype(np.float32), dtype=BF16)

    def pad_lora(a, axis):
        segs = [lax.slice_in_dim(a, 0, 3 * RW, axis=axis)]
        o = 3 * RW
        for n in (dl, al, gl):
            seg = lax.slice_in_dim(a, o, o + n, axis=axis)
            pad = [(0, 0)] * a.ndim
            pad[axis] = (0, LANES - n)
            segs.append(jnp.pad(seg, pad))
            o += n
        return jnp.concatenate(segs, axis=axis)

    table = _bias_table_call(rel_bias, n_meta, ATTN_TILE)
    tables = {tq: table for tq, _, _ in _attn_segments(Lp)}

    tm_big = _pick_tile(M, MM_ROWS, BF16_ROWS)
    tm_res = _pick_tile(M, EW_ROWS, BF16_ROWS)
    u = _rms_call(h, norm_mix_pre[0], tm_res)
    v_first = None
    w_in_b, p_a_b, p_b_b, p_c_b, w_o_b, ffn_up_b, ffn_down_b = (
        a.astype(BF16) for a in (w_in, p_a, p_b, p_c, w_o, ffn_up, ffn_down))
    for l in range(depth):
        lam_init = 0.8 - 0.6 * math.exp(-0.3 * l)
        w_gate = w_in_b[l, :, off_g:]
        w_rwkv = pad_lora(w_in_b[l, :, off_r:off_g], 1)
        mu_p = pad_lora(rwkv_mu[l], 0)

        zg = _mm_call(u, w_gate, BF16, tm_big, _pick_tile(3 * D, MM_COLS, LANES), "in_proj_gate")
        zqkv = _mm_call(u, w_in_b[l, :, off_q:off_r], BF16, tm_big, _pick_tile(3 * DW, MM_COLS, LANES),
                        "in_proj_qkv")
        zp = _mm_call(u, w_in_b[l, :, :PW], BF16, tm_big, _pick_tile(PW, MM_COLS, LANES), "in_proj_pool")

        yp = _pool_call(zp, pool_w[l], pool_scale[l], B, Lp)
        yd = _attn_call(zqkv, tables, diff_lq1[l], diff_lk1[l], diff_lq2[l], diff_lk2[l],
                        diff_subln[l], B, L, Lp, lam_init)
        v_res = None if l == 0 else (rwkv_v0[l - 1], rwkv_v1[l - 1], rwkv_v2[l - 1])
        streams = _rwkv_prep_call(u, w_rwkv, B, Lp, RW, mu_p, rwkv_w0[l], rwkv_w2[l], rwkv_a0[l], rwkv_a2[l],
                                  rwkv_g2[l], rwkv_kk[l], rwkv_ka[l], ones_bd, v_first, v_res)
        if l == 0:
            v_first = streams[3]
        yr = _scan_call(streams, rwkv_rk[l], rwkv_lnx_w[l], rwkv_lnx_b[l], ones_bd, B, Lp, hd_r)

        merged = _merge_call(yp, yd, yr, zg, p_a_b, p_b_b, p_c_b, l, tm_res)
        h, u2 = _mm_res_call(merged, w_o_b, l, h, norm_mix_post[l], norm_ffn_pre[l], "out_proj_res")
        FF = ffn_down.shape[1]
        act = _ffn_up_call(u2, ffn_up_b, l, ffn_conv_w[l], ffn_conv_b[l], B, Lp,
                           _pick_tile(FF, FFN_COLS, LANES))
        g_next = norm_mix_pre[l + 1] if l + 1 < depth else norm_mix_pre[l]
        h, u = _mm_res_call(act, ffn_down_b, l, h, norm_ffn_post[l], g_next, "ffn_down_res")
    return h.reshape(B, Lp, D)[:, n_meta:n_meta + S]
```

```python
import functools
import math

import numpy as np
import jax
import jax.numpy as jnp
from jax import lax
from jax.experimental import pallas as pl
from jax.experimental.pallas import tpu as pltpu

F32 = jnp.float32
BF16 = jnp.bfloat16

CHUNK = 64
NORM_EPS = 1e-6
NEG_INF = -1e30
POOL_WINDOWS = (2, 4, 8, 16)
DIFF_SUBLN_EPS = 1e-5
REL_MAX_DIST = 128
RWKV_LNX_EPS = 64e-5
FFN_CONV = 3

LANES = 128
SUBLANES = 8
ATTN_TILE = 128
ATTN_Q_TILE = 512
ATTN_UNROLL = 3
SCAN_BATCH = 4
BF16_ROWS = 2 * SUBLANES
VMEM_LIMIT = 56 * 1024 * 1024
VMEM_HEADROOM = 8 * 1024 * 1024

MM_ROWS = 2176
MM_COLS = 1536
EW_ROWS = 544
PREP_ROWS = 384
FFN_COLS = 512
FFN_SUBBLOCKS = 2


def _cparams(sem):
    return pltpu.CompilerParams(dimension_semantics=sem, vmem_limit_bytes=VMEM_LIMIT)


def _round_up(x, m):
    return (x + m - 1) // m * m


def _pick_tile(total, target, mult):
    best = None
    for t in range(mult, min(total, target) + 1, mult):
        if total % t == 0:
            best = t
    assert best is not None, (total, target, mult)
    return best


def _shift_down(x, d):
    row = lax.broadcasted_iota(jnp.int32, x.shape, 0)
    return jnp.where(row >= d, pltpu.roll(x, d, 0), 0.0)


def _sigmoid(x):
    return 1.0 / (1.0 + jnp.exp(-x))


def _dot(a, b):
    return jnp.dot(a.astype(BF16), b.astype(BF16), preferred_element_type=F32)


def _dot_nt(a, b):
    return lax.dot_general(a.astype(BF16), b.astype(BF16), (((1,), (1,)), ((), ())),
                           preferred_element_type=F32)


def _split3(x):
    hi = x.astype(BF16)
    r1 = x - hi.astype(F32)
    mid = r1.astype(BF16)
    lo = (r1 - mid.astype(F32)).astype(BF16)
    return hi, mid, lo


def _dot_exact_rhs(x, ones_bf16):
    n = x.shape[0]
    pieces = jnp.concatenate(_split3(x), axis=0)
    res = jnp.dot(pieces, ones_bf16, preferred_element_type=F32)
    return res[:n] + res[n:2 * n] + res[2 * n:]


def _rms_kernel(x_ref, g_ref, o_ref):
    x = x_ref[...]
    y = x * lax.rsqrt(jnp.mean(x * x, axis=-1, keepdims=True) + NORM_EPS)
    o_ref[...] = (y * g_ref[...]).astype(o_ref.dtype)


def _rms_call(x, g, tm):
    M, D = x.shape
    return pl.pallas_call(
        _rms_kernel,
        grid=(M // tm,),
        in_specs=[pl.BlockSpec((tm, D), lambda i: (i, 0)),
                  pl.BlockSpec((1, D), lambda i: (0, 0))],
        out_specs=pl.BlockSpec((tm, D), lambda i: (i, 0)),
        out_shape=jax.ShapeDtypeStruct((M, D), BF16),
        compiler_params=_cparams(("parallel",)),
        name="rms_norm",
    )(x, g.reshape(1, D))


def _mm_kernel(a_ref, b_ref, o_ref):
    o_ref[...] = jnp.dot(a_ref[...], b_ref[...], preferred_element_type=F32).astype(o_ref.dtype)


def _mm_call(a, b, out_dtype, tm, tn, name):
    M, K = a.shape
    N = b.shape[1]
    return pl.pallas_call(
        _mm_kernel,
        grid=(M // tm, N // tn),
        in_specs=[pl.BlockSpec((tm, K), lambda i, j: (i, 0)),
                  pl.BlockSpec((K, tn), lambda i, j: (0, j))],
        out_specs=pl.BlockSpec((tm, tn), lambda i, j: (i, j)),
        out_shape=jax.ShapeDtypeStruct((M, N), out_dtype),
        compiler_params=_cparams(("parallel", "arbitrary")),
        name=name,
    )(a, b)


def _mm_res_kernel(a_ref, w_ref, h_ref, gp_ref, gn_ref, oh_ref, ou_ref):
    f = jnp.dot(a_ref[...], w_ref[...], preferred_element_type=F32)
    y = f * lax.rsqrt(jnp.mean(f * f, axis=-1, keepdims=True) + NORM_EPS) * gp_ref[...]
    hn = h_ref[...] + y
    oh_ref[...] = hn
    u = hn * lax.rsqrt(jnp.mean(hn * hn, axis=-1, keepdims=True) + NORM_EPS) * gn_ref[...]
    ou_ref[...] = u.astype(ou_ref.dtype)


def _mm_res_tile(M, K, D):
    budget = VMEM_LIMIT - VMEM_HEADROOM
    best = None
    for tm in range(BF16_ROWS, M + 1, BF16_ROWS):
        if M % tm:
            continue
        need = (K * D * 2
                + 2 * tm * K * 2
                + 2 * 2 * tm * D * 4
                + 2 * tm * D * 2
                + 2 * tm * D * 4)
        if need <= budget:
            best = tm
    assert best is not None
    return best


def _mm_res_call(a, w, layer, h, g_post, g_next, name):
    M, K = a.shape
    D = w.shape[2]
    tm = _mm_res_tile(M, K, D)
    return pl.pallas_call(
        _mm_res_kernel,
        grid=(M // tm,),
        in_specs=[pl.BlockSpec((tm, K), lambda i: (i, 0)),
                  pl.BlockSpec((None, K, D), lambda i: (layer, 0, 0), pipeline_mode=pl.Buffered(1)),
                  pl.BlockSpec((tm, D), lambda i: (i, 0)),
                  pl.BlockSpec((1, D), lambda i: (0, 0)),
                  pl.BlockSpec((1, D), lambda i: (0, 0))],
        out_specs=[pl.BlockSpec((tm, D), lambda i: (i, 0)),
                   pl.BlockSpec((tm, D), lambda i: (i, 0))],
        out_shape=[jax.ShapeDtypeStruct((M, D), F32),
                   jax.ShapeDtypeStruct((M, D), BF16)],
        compiler_params=_cparams(("parallel",)),
        name=name,
    )(a, w, h, g_post.reshape(1, D), g_next.reshape(1, D))


def _pool_kernel(z_ref, w_ref, s_ref, o_ref, *, gw):
    L = z_ref.shape[1]
    row = lax.broadcasted_iota(jnp.int32, (L, gw), 0)
    for gi, win in enumerate(POOL_WINDOWS):
        sl = slice(gi * gw, (gi + 1) * gw)
        zg = z_ref[0, :, sl].astype(F32)
        s = zg
        d = 1
        while d < win:
            s = s + _shift_down(s, d)
            d *= 2
        cnt = jnp.minimum(row + 1, win).astype(F32)
        pooled = s / cnt - zg
        y = _dot(pooled, w_ref[gi]) * s_ref[:, sl]
        o_ref[0, :, sl] = y.astype(o_ref.dtype)


def _pool_call(z, w, scale, B, Lp):
    PW = z.shape[-1]
    G = len(POOL_WINDOWS)
    gw = PW // G
    z3 = z.reshape(B, Lp, PW)
    out = pl.pallas_call(
        functools.partial(_pool_kernel, gw=gw),
        grid=(B,),
        in_specs=[pl.BlockSpec((1, Lp, PW), lambda b: (b, 0, 0)),
                  pl.BlockSpec((G, gw, gw), lambda b: (0, 0, 0)),
                  pl.BlockSpec((1, PW), lambda b: (0, 0))],
        out_specs=pl.BlockSpec((1, Lp, PW), lambda b: (b, 0, 0)),
        out_shape=jax.ShapeDtypeStruct((B, Lp, PW), BF16),
        compiler_params=_cparams(("parallel",)),
        name="pool_mixer",
    )(z3, w.astype(BF16), scale.reshape(1, PW))
    return out.reshape(B * Lp, PW)


def _rel_bucket(rel, n_buckets):
    half = n_buckets // 2
    max_exact = half // 2
    n = np.abs(rel)
    large = max_exact + (np.log(np.maximum(n, 1) / max_exact) / math.log(REL_MAX_DIST / max_exact)
                         * (half - max_exact)).astype(np.int32)
    large = np.minimum(large, half - 1)
    return ((rel > 0) * half + np.where(n < max_exact, n, large)).astype(np.int32)


def _bias_table_kernel(rb_ref, bucket_ref, pen_ref, o_ref, *, n_buckets):
    s = pl.program_id(0)
    bucket = bucket_ref[...]
    acc = pen_ref[...]
    for b in range(n_buckets):
        acc = acc + jnp.where(bucket == b, rb_ref[b, s], 0.0)
    o_ref[0] = acc


def _bias_table_call(rel_bias, n_meta, tq):
    n_buckets, n_sub = rel_bias.shape
    KT = ATTN_TILE
    assert tq % KT == 0 and KT % CHUNK == 0 and n_meta <= KT
    n_near = tq // KT + 2
    n_slabs = n_near + 2
    far = n_buckets // 2 - 1
    r = np.arange(tq)[:, None]
    c = np.arange(KT)[None, :]
    assert np.all(_rel_bucket(-(KT + 1 + np.arange(4 * KT)), n_buckets) == far)
    bucket = np.full((n_slabs, tq, KT), far, np.int32)
    pen = np.zeros((n_slabs, tq, KT), np.float32)
    pen[-1] = NEG_INF
    for idx in range(n_near):
        key = (idx - 1) * KT + c
        bucket[1 + idx] = _rel_bucket(key - r, n_buckets)
        visible = np.floor_divide(key - n_meta, CHUNK) <= np.floor_divide(r - n_meta, CHUNK)
        pen[1 + idx] = np.where(visible, 0.0, NEG_INF)
    blk = pl.BlockSpec((n_slabs, tq, KT), lambda s: (0, 0, 0))
    return pl.pallas_call(
        functools.partial(_bias_table_kernel, n_buckets=n_buckets),
        grid=(n_sub,),
        in_specs=[pl.BlockSpec(memory_space=pltpu.SMEM), blk, blk],
        out_specs=pl.BlockSpec((1, n_slabs, tq, KT), lambda s: (s, 0, 0, 0)),
        out_shape=jax.ShapeDtypeStruct((n_sub, n_slabs, tq, KT), F32),
        compiler_params=_cparams(("arbitrary",)),
        name="bias_table_%d" % tq,
    )(rel_bias, jnp.asarray(bucket), jnp.asarray(pen))


def _attn_kernel(*refs, hd, lam_init, nt, tile0, live):
    scale = hd ** -0.5
    (lq1_ref, lk1_ref, lq2_ref, lk2_ref, sub_ref, q_ref, k_ref, v_ref, w_ref,
     o_ref, s_sc, mx_sc, l_sc, acc_sc, k_sc, v_sc) = refs
    KT = ATTN_TILE
    tq = q_ref.shape[1]
    ratio = -(-tq // KT)
    Lk = k_ref.shape[1]

    @pl.when(pl.program_id(2) == 0)
    def _():
        k_sc[0:Lk, :] = k_ref[0]
        v_sc[0:Lk, :] = v_ref[0]
        if k_sc.shape[0] > Lk:
            k_sc[Lk:, :] = jnp.zeros((k_sc.shape[0] - Lk, k_sc.shape[1]), k_sc.dtype)
            v_sc[Lk:, :] = jnp.zeros((v_sc.shape[0] - Lk, v_sc.shape[1]), v_sc.dtype)
    n_slabs = w_ref.shape[1]
    i = pl.program_id(2)
    jt0 = tile0 + i * ratio
    n_need = jnp.minimum(jt0 + ratio + 1, nt)
    n_pairs = (n_need + 1) // 2
    q = q_ref[0, 0:live, :]

    def bias(s, jj):
        rows = []
        for c in range(-(-live // KT)):
            n = min(KT, live - c * KT)
            tiles = []
            for j in (2 * jj, 2 * jj + 1):
                idx = jnp.where(j < nt, jnp.clip(j - (jt0 + c) + 2, 0, n_slabs - 1), n_slabs - 1)
                tiles.append(w_ref[s, idx, 0:n, :])
            rows.append(jnp.concatenate(tiles, axis=1))
        return rows[0] if len(rows) == 1 else jnp.concatenate(rows, axis=0)

    def two_tiles(ref, jj):
        j0 = 2 * jj
        j1 = jnp.minimum(j0 + 1, nt - 1)
        t0 = ref[pl.ds(pl.multiple_of(j0 * KT, KT), KT), :]
        t1 = ref[pl.ds(pl.multiple_of(j1 * KT, KT), KT), :]
        return jnp.concatenate([t0, t1], axis=0)

    mx_sc[...] = jnp.full_like(mx_sc, NEG_INF)

    def unrolled(step):
        def group(t, carry):
            step([ATTN_UNROLL * t + r for r in range(ATTN_UNROLL)])
            return carry

        def single(jj, carry):
            step([jj])
            return carry
        n_groups = n_pairs // ATTN_UNROLL
        lax.fori_loop(0, n_groups, group, 0)
        lax.fori_loop(n_groups * ATTN_UNROLL, n_pairs, single, 0)

    def pass1(jjs):
        kts = [two_tiles(k_sc, jj) for jj in jjs]
        for s in range(2):
            mx = None
            for jj, kt in zip(jjs, kts):
                sc = _dot_nt(q[:, s * hd:(s + 1) * hd], kt[:, s * hd:(s + 1) * hd]) * scale
                sc = sc + bias(s, jj)
                s_sc[s, jj] = sc
                half = jnp.maximum(sc[:, :KT], sc[:, KT:])
                mx = half if mx is None else jnp.maximum(mx, half)
            mx_sc[s] = jnp.maximum(mx_sc[s], mx)

    unrolled(pass1)
    m = [jnp.max(mx_sc[s], axis=-1, keepdims=True) for s in range(2)]
    l_sc[...] = jnp.zeros_like(l_sc)
    acc_sc[...] = jnp.zeros_like(acc_sc)

    def pass2(jjs):
        for jj in jjs:
            vt = two_tiles(v_sc, jj)
            for s in range(2):
                p = jnp.exp(s_sc[s, jj] - m[s])
                l_sc[s] += p[:, :KT] + p[:, KT:]
                acc_sc[s] += jnp.dot(p.astype(BF16), vt, preferred_element_type=F32)

    unrolled(pass2)
    l = [jnp.sum(l_sc[s], axis=-1, keepdims=True) for s in range(2)]
    lam = (jnp.exp(jnp.sum(lq1_ref[...] * lk1_ref[...], axis=-1, keepdims=True))
           - jnp.exp(jnp.sum(lq2_ref[...] * lk2_ref[...], axis=-1, keepdims=True)) + lam_init)
    o = acc_sc[0] / l[0] - lam * (acc_sc[1] / l[1])
    y = o * lax.rsqrt(jnp.mean(o * o, axis=-1, keepdims=True) + DIFF_SUBLN_EPS)
    o_ref[0, 0:live, :] = (y * sub_ref[...] * (1.0 - lam_init)).astype(o_ref.dtype)
    if live < tq:
        o_ref[0, live:tq, :] = jnp.zeros((tq - live, o_ref.shape[2]), o_ref.dtype)


def _attn_segment(qkv3, table, params, lam_init, tq, row0, n_tiles, live):
    B, Lp, W3 = qkv3.shape
    DW = W3 // 3
    hd = params[0].shape[1]
    hw = 2 * hd
    n_heads = DW // hw
    KT = ATTN_TILE
    nt = -(-Lp // KT)
    blk0 = row0 // tq
    kern = functools.partial(_attn_kernel, hd=hd, lam_init=lam_init, nt=nt,
                             tile0=row0 // KT, live=live)
    par = lambda n: pl.BlockSpec((1, n), lambda b, h, i: (0, 0))
    in_specs = [par(hd)] * 4 + [
        par(hw),
        pl.BlockSpec((1, tq, hw), lambda b, h, i: (b, blk0 + i, h)),
        pl.BlockSpec((1, Lp, hw), lambda b, h, i: (b, 0, n_heads + h)),
        pl.BlockSpec((1, Lp, hw), lambda b, h, i: (b, 0, 2 * n_heads + h)),
        pl.BlockSpec((2,) + table.shape[1:], lambda b, h, i: (h, 0, 0, 0))]
    args = list(params) + [qkv3, qkv3, qkv3, table]
    max_pairs = (nt + 1) // 2
    return pl.pallas_call(
        kern,
        grid=(B, n_heads, n_tiles),
        in_specs=in_specs,
        out_specs=pl.BlockSpec((1, tq, hw), lambda b, h, i: (b, i, h)),
        out_shape=jax.ShapeDtypeStruct((B, n_tiles * tq, DW), BF16),
        scratch_shapes=[pltpu.VMEM((2, max_pairs, live, 2 * KT), F32), pltpu.VMEM((2, live, KT), F32),
                        pltpu.VMEM((2, live, KT), F32), pltpu.VMEM((2, live, hw), F32),
                        pltpu.VMEM((nt * KT, hw), BF16), pltpu.VMEM((nt * KT, hw), BF16)],
        compiler_params=_cparams(("parallel", "parallel", "arbitrary")),
        name="diff_attention_%d" % tq,
    )(*args)


def _attn_tail_kernel(lq1_ref, lk1_ref, lq2_ref, lk2_ref, sub_ref, q_ref, k_ref, v_ref, w_ref,
                      o_ref, k_sc, v_sc, *, hd, lam_init, jt0, live):
    KT = ATTN_TILE
    scale = hd ** -0.5
    tq = q_ref.shape[1]
    Lk = k_ref.shape[1]
    nt = k_sc.shape[0] // KT
    n_slabs = w_ref.shape[1]
    hw = 2 * hd
    n_heads = q_ref.shape[2] // hw
    k_sc[0:Lk, :] = k_ref[0]
    v_sc[0:Lk, :] = v_ref[0]
    if k_sc.shape[0] > Lk:
        pad = jnp.zeros((k_sc.shape[0] - Lk, k_sc.shape[1]), k_sc.dtype)
        k_sc[Lk:, :] = pad
        v_sc[Lk:, :] = pad
    slabs = [min(max(j - jt0 + 2, 0), n_slabs - 1) for j in range(nt)]
    lam = (jnp.exp(jnp.sum(lq1_ref[...] * lk1_ref[...], axis=-1, keepdims=True))
           - jnp.exp(jnp.sum(lq2_ref[...] * lk2_ref[...], axis=-1, keepdims=True)) + lam_init)
    for h in range(n_heads):
        v_h = v_sc[:, h * hw:(h + 1) * hw]
        outs = []
        for s in range(2):
            cols = slice((2 * h + s) * hd, (2 * h + s + 1) * hd)
            sc = _dot_nt(q_ref[0, 0:live, cols], k_sc[:, cols]) * scale
            sc = sc + jnp.concatenate([w_ref[2 * h + s, i, 0:live, :] for i in slabs], axis=1)
            p = jnp.exp(sc - jnp.max(sc, axis=-1, keepdims=True))
            l = jnp.sum(p, axis=-1, keepdims=True)
            outs.append(jnp.dot(p.astype(BF16), v_h, preferred_element_type=F32) / l)
        o = outs[0] - lam * outs[1]
        y = o * lax.rsqrt(jnp.mean(o * o, axis=-1, keepdims=True) + DIFF_SUBLN_EPS)
        o_ref[0, 0:live, h * hw:(h + 1) * hw] = (y * sub_ref[...] * (1.0 - lam_init)).astype(o_ref.dtype)
    if live < tq:
        o_ref[0, live:tq, :] = jnp.zeros((tq - live, o_ref.shape[2]), o_ref.dtype)


def _attn_tail(qkv3, table, params, lam_init, tq, row0, live):
    B, Lp, W3 = qkv3.shape
    DW = W3 // 3
    hd = params[0].shape[1]
    KT = ATTN_TILE
    nt = -(-Lp // KT)
    assert row0 % KT == 0 and row0 // KT == nt - 1 and row0 % tq == 0
    par = lambda n: pl.BlockSpec((1, n), lambda b: (0, 0))
    return pl.pallas_call(
        functools.partial(_attn_tail_kernel, hd=hd, lam_init=lam_init, jt0=row0 // KT, live=live),
        grid=(B,),
        in_specs=[par(hd)] * 4 + [
            par(2 * hd),
            pl.BlockSpec((1, tq, DW), lambda b: (b, row0 // tq, 0)),
            pl.BlockSpec((1, Lp, DW), lambda b: (b, 0, 1)),
            pl.BlockSpec((1, Lp, DW), lambda b: (b, 0, 2)),
            pl.BlockSpec(table.shape, lambda b: (0, 0, 0, 0))],
        out_specs=pl.BlockSpec((1, tq, DW), lambda b: (b, 0, 0)),
        out_shape=jax.ShapeDtypeStruct((B, tq, DW), BF16),
        scratch_shapes=[pltpu.VMEM((nt * KT, DW), BF16), pltpu.VMEM((nt * KT, DW), BF16)],
        compiler_params=_cparams(("parallel",)),
        name="diff_attention_tail",
    )(*(list(params) + [qkv3, qkv3, qkv3, table]))


def _attn_segments(Lp):
    big = ATTN_Q_TILE
    n_big = Lp // big
    segs = []
    if n_big:
        segs.append((big, 0, n_big))
    row0 = n_big * big
    n_mid = (Lp - row0) // ATTN_TILE
    if n_mid:
        segs.append((ATTN_TILE, row0, n_mid))
        row0 += n_mid * ATTN_TILE
    if Lp > row0:
        segs.append((Lp - row0, row0, 1))
    return segs


def _attn_call(zqkv, tables, lq1, lk1, lq2, lk2, subln, B, L, Lp, lam_init):
    vec = lambda a: a.reshape(1, -1)
    params = [vec(lq1), vec(lk1), vec(lq2), vec(lk2), vec(subln)]
    qkv3 = zqkv.reshape(B, Lp, zqkv.shape[-1])
    outs = []
    for tq, row0, n_tiles in _attn_segments(Lp):
        live = min(tq, _round_up(L - row0, BF16_ROWS)) if n_tiles == 1 else tq
        if tq < ATTN_TILE:
            outs.append(_attn_tail(qkv3, tables[tq], params, lam_init, tq, row0, live))
        else:
            outs.append(_attn_segment(qkv3, tables[tq], params, lam_init, tq, row0, n_tiles, live))
    out = outs[0] if len(outs) == 1 else jnp.concatenate(outs, axis=1)
    return out.reshape(B * Lp, -1)


def _rwkv_prep_kernel(*refs, rw, has_vres):
    if has_vres:
        (u_ref, up_ref, w_ref, mu_ref, w0_ref, w2_ref, a0_ref, a2_ref, g2_ref, kk_ref, ka_ref, ones_ref,
         vf_ref, v0_ref, v1_ref, v2_ref,
         r_o, lw_o, k_o, v_o, a_o, b_o, g_o) = refs
    else:
        (u_ref, up_ref, w_ref, mu_ref, w0_ref, w2_ref, a0_ref, a2_ref, g2_ref, kk_ref, ka_ref, ones_ref,
         r_o, lw_o, k_o, v_o, a_o, b_o, g_o) = refs
    i = pl.program_id(1)
    halo = up_ref.shape[1]
    zz = jnp.dot(jnp.concatenate([up_ref[0], u_ref[0]], axis=0), w_ref[...], preferred_element_type=F32)
    z = zz[halo:]
    last_prev = jnp.where(i > 0, zz[halo - 1:halo, :], 0.0)
    row = lax.broadcasted_iota(jnp.int32, z.shape, 0)
    prev = jnp.where(row >= 1, pltpu.roll(z, 1, 0), last_prev)
    zs = z + (prev - z) * mu_ref[...]
    r = zs[:, 0:rw]
    kx = zs[:, rw:2 * rw]
    vx = zs[:, 2 * rw:3 * rw]
    wd = zs[:, 3 * rw:3 * rw + LANES]
    ad = zs[:, 3 * rw + LANES:3 * rw + 2 * LANES]
    gd = zs[:, 3 * rw + 2 * LANES:3 * rw + 3 * LANES]
    wl = w0_ref[...] + _dot(jnp.tanh(wd), w2_ref[...])
    sp = jnp.maximum(-wl, 0.0) + jnp.log(1.0 + jnp.exp(-jnp.abs(wl)))
    lw = -jnp.exp(-sp - 0.5)
    if has_vres:
        gate_v = _sigmoid(v0_ref[...] + _dot(_dot(vx, v1_ref[...]), v2_ref[...]))
        vx = vx + (vf_ref[0] - vx) * gate_v
    a = _sigmoid(a0_ref[...] + _dot(ad, a2_ref[...]))
    g = _dot(_sigmoid(gd), g2_ref[...])
    kk = kx * kk_ref[...]
    ss = _dot_exact_rhs(kk * kk, ones_ref[...])
    kk = kk * lax.rsqrt(jnp.maximum(ss, 1e-24))
    kmod = kx * (1.0 + (a - 1.0) * ka_ref[...])
    r_o[0] = r
    lw_o[0] = lw
    k_o[0] = kmod
    v_o[0] = vx
    a_o[0] = -kk
    b_o[0] = kk * a
    g_o[0] = g


def _pad_rows(w, n):
    return jnp.pad(w, ((0, n - w.shape[0]), (0, 0)))


def _rwkv_prep_call(u, w_rwkv, B, Lp, rw, mu_p, w0, w2, a0, a2, g2, k_k, k_a, ones_bd, v_first, v_res):
    D = u.shape[-1]
    u3 = u.reshape(B, Lp, D)
    HALO = BF16_ROWS
    tl = _pick_tile(Lp, PREP_ROWS, HALO)
    nb = tl // HALO
    has_vres = v_res is not None
    vec = lambda a: a.reshape(1, -1)
    full = lambda a: pl.BlockSpec(a.shape, lambda b, i: (0,) * a.ndim)
    w2p = _pad_rows(w2, LANES).astype(BF16)
    a2p = _pad_rows(a2, LANES).astype(BF16)
    g2p = _pad_rows(g2, LANES).astype(BF16)
    args = [u3, u3, w_rwkv, vec(mu_p), vec(w0), w2p, vec(a0), a2p, g2p, vec(k_k), vec(k_a), ones_bd]
    in_specs = [pl.BlockSpec((1, tl, D), lambda b, i: (b, i, 0)),
                pl.BlockSpec((1, HALO, D), lambda b, i: (b, jnp.maximum(i * nb - 1, 0), 0)),
                pl.BlockSpec(w_rwkv.shape, lambda b, i: (0, 0), pipeline_mode=pl.Buffered(1))]
    in_specs += [full(a) for a in args[3:]]
    if has_vres:
        v0, v1, v2 = v_res
        v1p = jnp.pad(v1, ((0, 0), (0, LANES - v1.shape[1]))).astype(BF16)
        v2p = _pad_rows(v2, LANES).astype(BF16)
        extra = [v_first, vec(v0), v1p, v2p]
        args += extra
        in_specs += [pl.BlockSpec((1, tl, rw), lambda b, i: (b, i, 0))] + [full(a) for a in extra[1:]]
    o_spec = pl.BlockSpec((1, tl, rw), lambda b, i: (b, i, 0))
    o_shape = jax.ShapeDtypeStruct((B, Lp, rw), F32)
    return pl.pallas_call(
        functools.partial(_rwkv_prep_kernel, rw=rw, has_vres=has_vres),
        grid=(B, Lp // tl),
        in_specs=in_specs,
        out_specs=[o_spec] * 7,
        out_shape=[o_shape] * 7,
        compiler_params=_cparams(("parallel", "arbitrary")),
        name="rwkv_prep",
    )(*args)


def _scan_kernel(r_ref, lw_ref, k_ref, v_ref, a_ref, b_ref, g_ref, rk_ref, lnw_ref, lnb_ref, ones_ref,
                 o_ref, h_sc, *, hd):
    nb = r_ref.shape[0]
    C = r_ref.shape[1]
    C2 = 2 * C
    WB = r_ref.shape[2]
    W = nb * WB
    c = pl.program_id(1)

    @pl.when(c == 0)
    def _():
        h_sc[...] = jnp.zeros_like(h_sc)

    side_by_side = lambda ref: jnp.concatenate([ref[i] for i in range(nb)], axis=1)
    r = side_by_side(r_ref)
    lw = side_by_side(lw_ref)
    k = side_by_side(k_ref)
    v = side_by_side(v_ref)
    a = side_by_side(a_ref)
    b = side_by_side(b_ref)

    cum = lw
    d = 1
    while d < C:
        cum = cum + _shift_down(cum, d)
        d *= 2
    tot = cum[C - 1:C, :]
    e_end = jnp.exp(tot - cum)
    e_neg = jnp.exp(-cum)
    at = a * jnp.exp(cum - lw)
    rt = r * jnp.exp(cum)
    bt = b * e_neg
    kt = k * e_neg
    bb = b * e_end
    kb = k * e_end
    p_c = jnp.exp(tot)

    lane = lax.broadcasted_iota(jnp.int32, (C, LANES), 1)
    first = lane < hd
    rr = lax.broadcasted_iota(jnp.int32, (C2, C2), 0)
    cc = lax.broadcasted_iota(jnp.int32, (C2, C2), 1)
    tr = jnp.where(rr >= C, rr - C, rr)
    tc = jnp.where(cc >= C, cc - C, cc)
    strict = tr > tc
    lower = tr >= tc
    eye = rr == cc

    def stack(x):
        return jnp.concatenate([jnp.where(first, x, 0.0), jnp.where(first, 0.0, x)], axis=0)

    pairs = range(W // LANES)
    stacked = lambda x: [stack(x[:, p * LANES:(p + 1) * LANES]) for p in pairs]
    s_a, s_r, s_b, s_k = stacked(at), stacked(rt), stacked(bt), stacked(kt)
    s_bb, s_kb, s_v = stacked(bb), stacked(kb), stacked(v)
    sc = [_dot_nt(jnp.concatenate([s_a[p], s_r[p]], axis=0), jnp.concatenate([s_b[p], s_k[p]], axis=0))
          for p in pairs]
    n = [jnp.where(strict, sc[p][:C2, :C2], 0.0) for p in pairs]
    a_ak = [jnp.where(strict, sc[p][:C2, C2:], 0.0) for p in pairs]
    m_rb = [jnp.where(lower, sc[p][C2:, :C2], 0.0) for p in pairs]
    m_rk = [jnp.where(lower, sc[p][C2:, C2:], 0.0) for p in pairs]
    xv = [_dot(jnp.concatenate([a_ak[p], m_rk[p], s_kb[p].T], axis=0), s_v[p]) for p in pairs]
    t_inv = [jnp.where(eye, 1.0, 0.0) + n[p] for p in pairs]
    pw = [_dot(n[p], n[p]) for p in pairs]
    step = 4
    while step < C:
        tp = [_dot(jnp.concatenate([t_inv[p], pw[p]], axis=0), pw[p]) for p in pairs]
        t_inv = [t_inv[p] + tp[p][:C2] for p in pairs]
        pw = [tp[p][C2:] for p in pairs]
        step *= 2
    t_inv = [t_inv[p] + _dot(t_inv[p], pw[p]) for p in pairs]
    au = [_dot(t_inv[p], jnp.concatenate([s_a[p], xv[p][:C2]], axis=1)) for p in pairs]
    mbu = [_dot(jnp.concatenate([m_rb[p], s_bb[p].T], axis=0), au[p]) for p in pairs]
    ys = []
    for p in pairs:
        r_hat = s_r[p] + mbu[p][:C2, :LANES]
        y0 = mbu[p][:C2, LANES:] + xv[p][C2:2 * C2]
        g_mat = jnp.where(eye, p_c[:, p * LANES:(p + 1) * LANES], 0.0) + mbu[p][C2:, :LANES]
        h_add = mbu[p][C2:, LANES:] + xv[p][2 * C2:]
        yh = _dot(jnp.concatenate([r_hat, g_mat], axis=0), h_sc[p])
        y = yh[:C2] + y0
        h_sc[p] = yh[C2:] + h_add
        ys.append(y[0:C] + y[C:C2])
    ones = ones_ref[...]
    inv_hd = 1.0 / hd
    ppb = WB // LANES
    n = nb * C
    y = jnp.concatenate([jnp.concatenate(ys[i * ppb:(i + 1) * ppb], axis=1) for i in range(nb)], axis=0)
    rk = jnp.concatenate([r_ref[i] * k_ref[i] * rk_ref[...] for i in range(nb)], axis=0)
    sums = _dot_exact_rhs(jnp.concatenate([y, rk], axis=0), ones)
    yc = y - sums[:n] * inv_hd
    var = _dot_exact_rhs(yc * yc, ones) * inv_hd
    yn = yc * lax.rsqrt(var + RWKV_LNX_EPS) * lnw_ref[...] + lnb_ref[...]
    for i in range(nb):
        rows = slice(i * C, (i + 1) * C)
        bonus = sums[n:][rows] * v_ref[i]
        o_ref[i] = ((yn[rows] + bonus) * g_ref[i]).astype(o_ref.dtype)


def _scan_call(streams, r_k, lnx_w, lnx_b, ones_bd, B, Lp, hd):
    r, lw, k, v, a, b, g = streams
    rw = r.shape[-1]
    C = CHUNK
    assert 2 * hd == LANES and rw % LANES == 0 and Lp % C == 0
    vec = lambda x: x.reshape(1, -1)
    nb = SCAN_BATCH if B % SCAN_BATCH == 0 else 1
    blk = pl.BlockSpec((nb, C, rw), lambda bi, c: (bi, c, 0))
    par = pl.BlockSpec((1, rw), lambda bi, c: (0, 0))
    out = pl.pallas_call(
        functools.partial(_scan_kernel, hd=hd),
        grid=(B // nb, Lp // C),
        in_specs=[blk] * 7 + [par] * 3 + [pl.BlockSpec((rw, rw), lambda bi, c: (0, 0))],
        out_specs=blk,
        out_shape=jax.ShapeDtypeStruct((B, Lp, rw), BF16),
        scratch_shapes=[pltpu.VMEM((nb * rw // LANES, LANES, LANES), F32)],
        compiler_params=_cparams(("parallel", "arbitrary")),
        name="rwkv_scan",
    )(r, lw, k, v, a, b, g, vec(r_k), vec(lnx_w), vec(lnx_b), ones_bd)
    return out.reshape(B * Lp, rw)


def _merge_kernel(yp_ref, yd_ref, yr_ref, g0_ref, g1_ref, g2_ref, pa_ref, pb_ref, pc_ref, o_ref):
    dot = lambda x, w: jnp.dot(x[...], w[...], preferred_element_type=F32)
    m = _sigmoid(g0_ref[...].astype(F32)) * dot(yp_ref, pa_ref)
    m = m + _sigmoid(g1_ref[...].astype(F32)) * dot(yd_ref, pb_ref)
    m = m + _sigmoid(g2_ref[...].astype(F32)) * dot(yr_ref, pc_ref)
    o_ref[...] = m.astype(o_ref.dtype)


def _merge_call(yp, yd, yr, zg, pa, pb, pc, layer, tm):
    M = yp.shape[0]
    D = pa.shape[2]
    rows = lambda a: pl.BlockSpec((tm, a.shape[1]), lambda i: (i, 0))
    full = lambda a: pl.BlockSpec((None,) + a.shape[1:], lambda i: (layer, 0, 0))
    gate = lambda n: pl.BlockSpec((tm, D), lambda i: (i, n))
    return pl.pallas_call(
        _merge_kernel,
        grid=(M // tm,),
        in_specs=[rows(yp), rows(yd), rows(yr), gate(0), gate(1), gate(2), full(pa), full(pb), full(pc)],
        out_specs=pl.BlockSpec((tm, D), lambda i: (i, 0)),
        out_shape=jax.ShapeDtypeStruct((M, D), BF16),
        compiler_params=_cparams(("parallel",)),
        name="gated_merge",
    )(yp, yd, yr, zg, zg, zg, pa, pb, pc)


def _ffn_up_kernel(u_ref, wv_ref, wg_ref, cwv_ref, cwg_ref, cbv_ref, cbg_ref, o_ref, *scratch, n_sub):
    xg_scs, xv_scs, a_scs = scratch[:n_sub], scratch[n_sub:2 * n_sub], scratch[2 * n_sub:]
    L = u_ref.shape[0]
    mb = L // n_sub
    st = mb // SUBLANES
    PAD = SUBLANES
    n_slab = a_scs[0].shape[0]
    k_in = math.sqrt(2.0 / math.pi)
    zeros = jnp.zeros((PAD, LANES), F32)
    for s in range(n_slab):
        xv_scs[0][s, 0:PAD, :] = zeros
        xg_scs[0][s, 0:PAD, :] = zeros

    def taps(cw_ref, cb_ref, s):
        sl = slice(s * LANES, (s + 1) * LANES)
        rows = [jnp.broadcast_to(cw_ref[j:j + 1, sl], (SUBLANES, LANES)) for j in range(FFN_CONV)]
        return rows, jnp.broadcast_to(cb_ref[:, sl], (SUBLANES, LANES))

    def tail(sb):
        lo = sb * mb
        xg_sc, xv_sc, a_sc = xg_scs[sb], xv_scs[sb], a_scs[sb]
        for s in range(n_slab):
            wg, bg = taps(cwg_ref, cbg_ref, s)
            wv, bv = taps(cwv_ref, cbv_ref, s)
            ld = lambda ref, j: ref[s, pl.ds(PAD + j, SUBLANES, stride=st), :]
            xg = [ld(xg_sc, j) for j in range(1 - FFN_CONV, 0)]
            xv = [ld(xv_sc, j) for j in range(1 - FFN_CONV, 0)]
            for j in range(st):
                xg.append(ld(xg_sc, j))
                xv.append(ld(xv_sc, j))
                gate = bg
                val = bv
                for t in range(FFN_CONV):
                    gate = gate + xg[j + t] * wg[t]
                    val = val + xv[j + t] * wv[t]
                inner = gate * (2.0 * k_in + (2.0 * k_in * 0.044715) * (gate * gate))
                act = gate / (1.0 + jnp.exp(-inner)) * val
                a_sc[s, pl.ds(j, SUBLANES, stride=st), :] = act
        for s in range(n_slab):
            o_ref[lo:lo + mb, s * LANES:(s + 1) * LANES] = a_sc[s].astype(o_ref.dtype)

    for sb in range(n_sub):
        lo = sb * mb
        u = u_ref[lo:lo + mb, :]
        for x_scs, w_ref in ((xg_scs, wg_ref), (xv_scs, wv_ref)):
            x = jnp.dot(u, w_ref[...], preferred_element_type=F32)
            for s in range(n_slab):
                xs = x[:, s * LANES:(s + 1) * LANES]
                x_scs[sb][s, PAD:PAD + mb, :] = xs
                if sb + 1 < n_sub:
                    x_scs[sb + 1][s, 0:PAD, :] = xs[mb - PAD:mb]
        if sb:
            tail(sb - 1)
    tail(n_sub - 1)


def _ffn_up_call(u, w_up, layer, conv_w, conv_b, B, Lp, tn):
    M, D = u.shape
    FF = w_up.shape[2] // 2
    nt = FF // tn
    n_sub = FFN_SUBBLOCKS if Lp % (FFN_SUBBLOCKS * BF16_ROWS) == 0 else 1
    mb = Lp // n_sub
    return pl.pallas_call(
        functools.partial(_ffn_up_kernel, n_sub=n_sub),
        grid=(B, nt),
        in_specs=[pl.BlockSpec((Lp, D), lambda b, j: (b, 0)),
                  pl.BlockSpec((None, D, tn), lambda b, j: (layer, 0, j)),
                  pl.BlockSpec((None, D, tn), lambda b, j: (layer, 0, nt + j)),
                  pl.BlockSpec((FFN_CONV, tn), lambda b, j: (0, j)),
                  pl.BlockSpec((FFN_CONV, tn), lambda b, j: (0, nt + j)),
                  pl.BlockSpec((1, tn), lambda b, j: (0, j)),
                  pl.BlockSpec((1, tn), lambda b, j: (0, nt + j))],
        out_specs=pl.BlockSpec((Lp, tn), lambda b, j: (b, j)),
        out_shape=jax.ShapeDtypeStruct((M, FF), BF16),
        scratch_shapes=([pltpu.VMEM((tn // LANES, SUBLANES + mb, LANES), F32)] * (2 * n_sub)
                        + [pltpu.VMEM((tn // LANES, mb, LANES), F32)] * n_sub),
        compiler_params=_cparams(("parallel", "arbitrary")),
        name="ffn_up_conv_geglu",
    )(u, w_up, w_up, conv_w, conv_w, conv_b.reshape(1, -1), conv_b.reshape(1, -1))


def kernel(x, meta, rel_bias, norm_mix_pre, norm_mix_post, norm_ffn_pre, norm_ffn_post, w_in, pool_w, pool_scale, diff_lq1, diff_lk1, diff_lq2, diff_lk2, diff_subln, rwkv_mu, rwkv_w0, rwkv_w2, rwkv_a0, rwkv_a2, rwkv_g2, rwkv_kk, rwkv_ka, rwkv_rk, rwkv_lnx_w, rwkv_lnx_b, rwkv_v0, rwkv_v1, rwkv_v2, p_a, p_b, p_c, w_o, ffn_up, ffn_conv_w, ffn_conv_b, ffn_down):
    B, S, D = x.shape
    n_meta = meta.shape[0]
    depth = w_in.shape[0]
    L = S + n_meta
    Lp = _round_up(L, CHUNK)
    M = B * Lp

    PW = pool_scale.shape[1]
    DW = p_b.shape[1]
    RW = rwkv_w0.shape[1]
    hd_r = rwkv_rk.shape[2]
    dl, al, gl = rwkv_w2.shape[1], rwkv_a2.shape[1], rwkv_g2.shape[1]
    off_q = PW
    off_r = PW + 3 * DW
    off_g = off_r + 3 * RW + dl + al + gl
    assert off_g + 3 * D == w_in.shape[2]
    assert Lp == L or S % CHUNK == 0
    assert max(dl, al, gl, rwkv_v1.shape[2]) <= LANES

    h = jnp.concatenate([jnp.broadcast_to(meta.astype(x.dtype)[None], (B, n_meta, D)), x,
                         jnp.zeros((B, Lp - L, D), x.dtype)], axis=1).reshape(M, D)

    hid = np.arange(RW) // hd_r
    ones_bd = jnp.asarray((hid[:, None] == hid[None, :]).astype(np.float32), dtype=BF16)

    def pad_lora(a, axis):
        segs = [lax.slice_in_dim(a, 0, 3 * RW, axis=axis)]
        o = 3 * RW
        for n in (dl, al, gl):
            seg = lax.slice_in_dim(a, o, o + n, axis=axis)
            pad = [(0, 0)] * a.ndim
            pad[axis] = (0, LANES - n)
            segs.append(jnp.pad(seg, pad))
            o += n
        return jnp.concatenate(segs, axis=axis)

    table = _bias_table_call(rel_bias, n_meta, ATTN_TILE)
    tables = {tq: table for tq, _, _ in _attn_segments(Lp)}

    tm_big = _pick_tile(M, MM_ROWS, BF16_ROWS)
    tm_res = _pick_tile(M, EW_ROWS, BF16_ROWS)
    u = _rms_call(h, norm_mix_pre[0], tm_res)
    v_first = None
    w_in_b, p_a_b, p_b_b, p_c_b, w_o_b, ffn_up_b, ffn_down_b = (
        a.astype(BF16) for a in (w_in, p_a, p_b, p_c, w_o, ffn_up, ffn_down))
    for l in range(depth):
        lam_init = 0.8 - 0.6 * math.exp(-0.3 * l)
        w_gate = w_in_b[l, :, off_g:]
        w_rwkv = pad_lora(w_in_b[l, :, off_r:off_g], 1)
        mu_p = pad_lora(rwkv_mu[l], 0)

        zg = _mm_call(u, w_gate, BF16, tm_big, _pick_tile(3 * D, MM_COLS, LANES), "in_proj_gate")
        zqkv = _mm_call(u, w_in_b[l, :, off_q:off_r], BF16, tm_big, _pick_tile(3 * DW, MM_COLS, LANES),
                        "in_proj_qkv")
        zp = _mm_call(u, w_in_b[l, :, :PW], BF16, tm_big, _pick_tile(PW, MM_COLS, LANES), "in_proj_pool")

        yp = _pool_call(zp, pool_w[l], pool_scale[l], B, Lp)
        yd = _attn_call(zqkv, tables, diff_lq1[l], diff_lk1[l], diff_lq2[l], diff_lk2[l],
                        diff_subln[l], B, L, Lp, lam_init)
        v_res = None if l == 0 else (rwkv_v0[l - 1], rwkv_v1[l - 1], rwkv_v2[l - 1])
        streams = _rwkv_prep_call(u, w_rwkv, B, Lp, RW, mu_p, rwkv_w0[l], rwkv_w2[l], rwkv_a0[l], rwkv_a2[l],
                                  rwkv_g2[l], rwkv_kk[l], rwkv_ka[l], ones_bd, v_first, v_res)
        if l == 0:
            v_first = streams[3]
        yr = _scan_call(streams, rwkv_rk[l], rwkv_lnx_w[l], rwkv_lnx_b[l], ones_bd, B, Lp, hd_r)

        merged = _merge_call(yp, yd, yr, zg, p_a_b, p_b_b, p_c_b, l, tm_res)
        h, u2 = _mm_res_call(merged, w_o_b, l, h, norm_mix_post[l], norm_ffn_pre[l], "out_proj_res")
        FF = ffn_down.shape[1]
        act = _ffn_up_call(u2, ffn_up_b, l, ffn_conv_w[l], ffn_conv_b[l], B, Lp,
                           _pick_tile(FF, FFN_COLS, LANES))
        g_next = norm_mix_pre[l + 1] if l + 1 < depth else norm_mix_pre[l]
        h, u = _mm_res_call(act, ffn_down_b, l, h, norm_ffn_post[l], g_next, "ffn_down_res")
    return h.reshape(B, Lp, D)[:, n_meta:n_meta + S]
```

```python
import functools
import math

import numpy as np
import jax
import jax.numpy as jnp
from jax import lax
from jax.experimental import pallas as pl
from jax.experimental.pallas import tpu as pltpu

F32 = jnp.float32
BF16 = jnp.bfloat16

CHUNK = 64
NORM_EPS = 1e-6
NEG_INF = -1e30
POOL_WINDOWS = (2, 4, 8, 16)
DIFF_SUBLN_EPS = 1e-5
REL_MAX_DIST = 128
RWKV_LNX_EPS = 64e-5
FFN_CONV = 3

LANES = 128
SUBLANES = 8
ATTN_TILE = 128
ATTN_Q_TILE = 512
ATTN_UNROLL = 3
SCAN_BATCH = 4
BF16_ROWS = 2 * SUBLANES
VMEM_LIMIT = 56 * 1024 * 1024
VMEM_HEADROOM = 8 * 1024 * 1024

MM_ROWS = 2176
MM_COLS = 1536
EW_ROWS = 544
PREP_ROWS = 384
FFN_COLS = 512
FFN_SUBBLOCKS = 2


def _cparams(sem):
    return pltpu.CompilerParams(dimension_semantics=sem, vmem_limit_bytes=VMEM_LIMIT)


def _round_up(x, m):
    return (x + m - 1) // m * m


def _pick_tile(total, target, mult):
    best = None
    for t in range(mult, min(total, target) + 1, mult):
        if total % t == 0:
            best = t
    assert best is not None, (total, target, mult)
    return best


def _shift_down(x, d):
    row = lax.broadcasted_iota(jnp.int32, x.shape, 0)
    return jnp.where(row >= d, pltpu.roll(x, d, 0), 0.0)


def _sigmoid(x):
    return 1.0 / (1.0 + jnp.exp(-x))


def _dot(a, b):
    return jnp.dot(a.astype(BF16), b.astype(BF16), preferred_element_type=F32)


def _dot_nt(a, b):
    return lax.dot_general(a.astype(BF16), b.astype(BF16), (((1,), (1,)), ((), ())),
                           preferred_element_type=F32)


def _split3(x):
    hi = x.astype(BF16)
    r1 = x - hi.astype(F32)
    mid = r1.astype(BF16)
    lo = (r1 - mid.astype(F32)).astype(BF16)
    return hi, mid, lo


def _dot_exact_rhs(x, ones_bf16):
    n = x.shape[0]
    pieces = jnp.concatenate(_split3(x), axis=0)
    res = jnp.dot(pieces, ones_bf16, preferred_element_type=F32)
    return res[:n] + res[n:2 * n] + res[2 * n:]


def _rms_kernel(x_ref, g_ref, o_ref):
    x = x_ref[...]
    y = x * lax.rsqrt(jnp.mean(x * x, axis=-1, keepdims=True) + NORM_EPS)
    o_ref[...] = (y * g_ref[...]).astype(o_ref.dtype)


def _rms_call(x, g, tm):
    M, D = x.shape
    return pl.pallas_call(
        _rms_kernel,
        grid=(M // tm,),
        in_specs=[pl.BlockSpec((tm, D), lambda i: (i, 0)),
                  pl.BlockSpec((1, D), lambda i: (0, 0))],
        out_specs=pl.BlockSpec((tm, D), lambda i: (i, 0)),
        out_shape=jax.ShapeDtypeStruct((M, D), BF16),
        compiler_params=_cparams(("parallel",)),
        name="rms_norm",
    )(x, g.reshape(1, D))


def _mm_kernel(a_ref, b_ref, o_ref):
    o_ref[...] = jnp.dot(a_ref[...], b_ref[...], preferred_element_type=F32).astype(o_ref.dtype)


def _mm_call(a, b, out_dtype, tm, tn, name):
    M, K = a.shape
    N = b.shape[1]
    return pl.pallas_call(
        _mm_kernel,
        grid=(M // tm, N // tn),
        in_specs=[pl.BlockSpec((tm, K), lambda i, j: (i, 0)),
                  pl.BlockSpec((K, tn), lambda i, j: (0, j))],
        out_specs=pl.BlockSpec((tm, tn), lambda i, j: (i, j)),
        out_shape=jax.ShapeDtypeStruct((M, N), out_dtype),
        compiler_params=_cparams(("parallel", "arbitrary")),
        name=name,
    )(a, b)


def _mm_res_kernel(a_ref, w_ref, h_ref, gp_ref, gn_ref, oh_ref, ou_ref):
    f = jnp.dot(a_ref[...], w_ref[...], preferred_element_type=F32)
    y = f * lax.rsqrt(jnp.mean(f * f, axis=-1, keepdims=True) + NORM_EPS) * gp_ref[...]
    hn = h_ref[...] + y
    oh_ref[...] = hn
    u = hn * lax.rsqrt(jnp.mean(hn * hn, axis=-1, keepdims=True) + NORM_EPS) * gn_ref[...]
    ou_ref[...] = u.astype(ou_ref.dtype)


def _mm_res_tile(M, K, D):
    budget = VMEM_LIMIT - VMEM_HEADROOM
    best = None
    for tm in range(BF16_ROWS, M + 1, BF16_ROWS):
        if M % tm:
            continue
        need = (K * D * 2
                + 2 * tm * K * 2
                + 2 * 2 * tm * D * 4
                + 2 * tm * D * 2
                + 2 * tm * D * 4)
        if need <= budget:
            best = tm
    assert best is not None
    return best


def _mm_res_call(a, w, layer, h, g_post, g_next, name):
    M, K = a.shape
    D = w.shape[2]
    tm = _mm_res_tile(M, K, D)
    return pl.pallas_call(
        _mm_res_kernel,
        grid=(M // tm,),
        in_specs=[pl.BlockSpec((tm, K), lambda i: (i, 0)),
                  pl.BlockSpec((None, K, D), lambda i: (layer, 0, 0), pipeline_mode=pl.Buffered(1)),
                  pl.BlockSpec((tm, D), lambda i: (i, 0)),
                  pl.BlockSpec((1, D), lambda i: (0, 0)),
                  pl.BlockSpec((1, D), lambda i: (0, 0))],
        out_specs=[pl.BlockSpec((tm, D), lambda i: (i, 0)),
                   pl.BlockSpec((tm, D), lambda i: (i, 0))],
        out_shape=[jax.ShapeDtypeStruct((M, D), F32),
                   jax.ShapeDtypeStruct((M, D), BF16)],
        compiler_params=_cparams(("parallel",)),
        name=name,
    )(a, w, h, g_post.reshape(1, D), g_next.reshape(1, D))


def _pool_kernel(z_ref, w_ref, s_ref, o_ref, *, gw):
    L = z_ref.shape[1]
    row = lax.broadcasted_iota(jnp.int32, (L, gw), 0)
    for gi, win in enumerate(POOL_WINDOWS):
        sl = slice(gi * gw, (gi + 1) * gw)
        zg = z_ref[0, :, sl].astype(F32)
        s = zg
        d = 1
        while d < win:
            s = s + _shift_down(s, d)
            d *= 2
        cnt = jnp.minimum(row + 1, win).astype(F32)
        pooled = s / cnt - zg
        y = _dot(pooled, w_ref[gi]) * s_ref[:, sl]
        o_ref[0, :, sl] = y.astype(o_ref.dtype)


def _pool_call(z, w, scale, B, Lp):
    PW = z.shape[-1]
    G = len(POOL_WINDOWS)
    gw = PW // G
    z3 = z.reshape(B, Lp, PW)
    out = pl.pallas_call(
        functools.partial(_pool_kernel, gw=gw),
        grid=(B,),
        in_specs=[pl.BlockSpec((1, Lp, PW), lambda b: (b, 0, 0)),
                  pl.BlockSpec((G, gw, gw), lambda b: (0, 0, 0)),
                  pl.BlockSpec((1, PW), lambda b: (0, 0))],
        out_specs=pl.BlockSpec((1, Lp, PW), lambda b: (b, 0, 0)),
        out_shape=jax.ShapeDtypeStruct((B, Lp, PW), BF16),
        compiler_params=_cparams(("parallel",)),
        name="pool_mixer",
    )(z3, w.astype(BF16), scale.reshape(1, PW))
    return out.reshape(B * Lp, PW)


def _rel_bucket(rel, n_buckets):
    half = n_buckets // 2
    max_exact = half // 2
    n = np.abs(rel)
    large = max_exact + (np.log(np.maximum(n, 1) / max_exact) / math.log(REL_MAX_DIST / max_exact)
                         * (half - max_exact)).astype(np.int32)
    large = np.minimum(large, half - 1)
    return ((rel > 0) * half + np.where(n < max_exact, n, large)).astype(np.int32)


def _bias_table_kernel(rb_ref, bucket_ref, pen_ref, o_ref, *, n_buckets):
    s = pl.program_id(0)
    bucket = bucket_ref[...]
    acc = pen_ref[...]
    for b in range(n_buckets):
        acc = acc + jnp.where(bucket == b, rb_ref[b, s], 0.0)
    o_ref[0] = acc


def _bias_table_call(rel_bias, n_meta, tq):
    n_buckets, n_sub = rel_bias.shape
    KT = ATTN_TILE
    assert tq % KT == 0 and KT % CHUNK == 0 and n_meta <= KT
    n_near = tq // KT + 2
    n_slabs = n_near + 2
    far = n_buckets // 2 - 1
    r = np.arange(tq)[:, None]
    c = np.arange(KT)[None, :]
    assert np.all(_rel_bucket(-(KT + 1 + np.arange(4 * KT)), n_buckets) == far)
    bucket = np.full((n_slabs, tq, KT), far, np.int32)
    pen = np.zeros((n_slabs, tq, KT), np.float32)
    pen[-1] = NEG_INF
    for idx in range(n_near):
        key = (idx - 1) * KT + c
        bucket[1 + idx] = _rel_bucket(key - r, n_buckets)
        visible = np.floor_divide(key - n_meta, CHUNK) <= np.floor_divide(r - n_meta, CHUNK)
        pen[1 + idx] = np.where(visible, 0.0, NEG_INF)
    blk = pl.BlockSpec((n_slabs, tq, KT), lambda s: (0, 0, 0))
    return pl.pallas_call(
        functools.partial(_bias_table_kernel, n_buckets=n_buckets),
        grid=(n_sub,),
        in_specs=[pl.BlockSpec(memory_space=pltpu.SMEM), blk, blk],
        out_specs=pl.BlockSpec((1, n_slabs, tq, KT), lambda s: (s, 0, 0, 0)),
        out_shape=jax.ShapeDtypeStruct((n_sub, n_slabs, tq, KT), F32),
        compiler_params=_cparams(("arbitrary",)),
        name="bias_table_%d" % tq,
    )(rel_bias, jnp.asarray(bucket), jnp.asarray(pen))


def _attn_kernel(*refs, hd, lam_init, nt, tile0, live):
    scale = hd ** -0.5
    (lq1_ref, lk1_ref, lq2_ref, lk2_ref, sub_ref, q_ref, k_ref, v_ref, w_ref,
     o_ref, s_sc, mx_sc, l_sc, acc_sc, k_sc, v_sc) = refs
    KT = ATTN_TILE
    tq = q_ref.shape[1]
    ratio = -(-tq // KT)
    Lk = k_ref.shape[1]

    @pl.when(pl.program_id(2) == 0)
    def _():
        k_sc[0:Lk, :] = k_ref[0]
        v_sc[0:Lk, :] = v_ref[0]
        if k_sc.shape[0] > Lk:
            k_sc[Lk:, :] = jnp.zeros((k_sc.shape[0] - Lk, k_sc.shape[1]), k_sc.dtype)
            v_sc[Lk:, :] = jnp.zeros((v_sc.shape[0] - Lk, v_sc.shape[1]), v_sc.dtype)
    n_slabs = w_ref.shape[1]
    i = pl.program_id(2)
    jt0 = tile0 + i * ratio
    n_need = jnp.minimum(jt0 + ratio + 1, nt)
    n_pairs = (n_need + 1) // 2
    q = q_ref[0, 0:live, :]

    def bias(s, jj):
        rows = []
        for c in range(-(-live // KT)):
            n = min(KT, live - c * KT)
            tiles = []
            for j in (2 * jj, 2 * jj + 1):
                idx = jnp.where(j < nt, jnp.clip(j - (jt0 + c) + 2, 0, n_slabs - 1), n_slabs - 1)
                tiles.append(w_ref[s, idx, 0:n, :])
            rows.append(jnp.concatenate(tiles, axis=1))
        return rows[0] if len(rows) == 1 else jnp.concatenate(rows, axis=0)

    def two_tiles(ref, jj):
        j0 = 2 * jj
        j1 = jnp.minimum(j0 + 1, nt - 1)
        t0 = ref[pl.ds(pl.multiple_of(j0 * KT, KT), KT), :]
        t1 = ref[pl.ds(pl.multiple_of(j1 * KT, KT), KT), :]
        return jnp.concatenate([t0, t1], axis=0)

    mx_sc[...] = jnp.full_like(mx_sc, NEG_INF)

    def unrolled(step):
        def group(t, carry):
            step([ATTN_UNROLL * t + r for r in range(ATTN_UNROLL)])
            return carry

        def single(jj, carry):
            step([jj])
            return carry
        n_groups = n_pairs // ATTN_UNROLL
        lax.fori_loop(0, n_groups, group, 0)
        lax.fori_loop(n_groups * ATTN_UNROLL, n_pairs, single, 0)

    def pass1(jjs):
        kts = [two_tiles(k_sc, jj) for jj in jjs]
        for s in range(2):
            mx = None
            for jj, kt in zip(jjs, kts):
                sc = _dot_nt(q[:, s * hd:(s + 1) * hd], kt[:, s * hd:(s + 1) * hd]) * scale
                sc = sc + bias(s, jj)
                s_sc[s, jj] = sc
                half = jnp.maximum(sc[:, :KT], sc[:, KT:])
                mx = half if mx is None else jnp.maximum(mx, half)
            mx_sc[s] = jnp.maximum(mx_sc[s], mx)

    unrolled(pass1)
    m = [jnp.max(mx_sc[s], axis=-1, keepdims=True) for s in range(2)]
    l_sc[...] = jnp.zeros_like(l_sc)
    acc_sc[...] = jnp.zeros_like(acc_sc)

    def pass2(jjs):
        for jj in jjs:
            vt = two_tiles(v_sc, jj)
            for s in range(2):
                p = jnp.exp(s_sc[s, jj] - m[s])
                l_sc[s] += p[:, :KT] + p[:, KT:]
                acc_sc[s] += jnp.dot(p.astype(BF16), vt, preferred_element_type=F32)

    unrolled(pass2)
    l = [jnp.sum(l_sc[s], axis=-1, keepdims=True) for s in range(2)]
    lam = (jnp.exp(jnp.sum(lq1_ref[...] * lk1_ref[...], axis=-1, keepdims=True))
           - jnp.exp(jnp.sum(lq2_ref[...] * lk2_ref[...], axis=-1, keepdims=True)) + lam_init)
    o = acc_sc[0] / l[0] - lam * (acc_sc[1] / l[1])
    y = o * lax.rsqrt(jnp.mean(o * o, axis=-1, keepdims=True) + DIFF_SUBLN_EPS)
    o_ref[0, 0:live, :] = (y * sub_ref[...] * (1.0 - lam_init)).astype(o_ref.dtype)
    if live < tq:
        o_ref[0, live:tq, :] = jnp.zeros((tq - live, o_ref.shape[2]), o_ref.dtype)


def _attn_segment(qkv3, table, params, lam_init, tq, row0, n_tiles, live):
    B, Lp, W3 = qkv3.shape
    DW = W3 // 3
    hd = params[0].shape[1]
    hw = 2 * hd
    n_heads = DW // hw
    KT = ATTN_TILE
    nt = -(-Lp // KT)
    blk0 = row0 // tq
    kern = functools.partial(_attn_kernel, hd=hd, lam_init=lam_init, nt=nt,
                             tile0=row0 // KT, live=live)
    par = lambda n: pl.BlockSpec((1, n), lambda b, h, i: (0, 0))
    in_specs = [par(hd)] * 4 + [
        par(hw),
        pl.BlockSpec((1, tq, hw), lambda b, h, i: (b, blk0 + i, h)),
        pl.BlockSpec((1, Lp, hw), lambda b, h, i: (b, 0, n_heads + h)),
        pl.BlockSpec((1, Lp, hw), lambda b, h, i: (b, 0, 2 * n_heads + h)),
        pl.BlockSpec((2,) + table.shape[1:], lambda b, h, i: (h, 0, 0, 0))]
    args = list(params) + [qkv3, qkv3, qkv3, table]
    max_pairs = (nt + 1) // 2
    return pl.pallas_call(
        kern,
        grid=(B, n_heads, n_tiles),
        in_specs=in_specs,
        out_specs=pl.BlockSpec((1, tq, hw), lambda b, h, i: (b, i, h)),
        out_shape=jax.ShapeDtypeStruct((B, n_tiles * tq, DW), BF16),
        scratch_shapes=[pltpu.VMEM((2, max_pairs, live, 2 * KT), F32), pltpu.VMEM((2, live, KT), F32),
                        pltpu.VMEM((2, live, KT), F32), pltpu.VMEM((2, live, hw), F32),
                        pltpu.VMEM((nt * KT, hw), BF16), pltpu.VMEM((nt * KT, hw), BF16)],
        compiler_params=_cparams(("parallel", "parallel", "arbitrary")),
        name="diff_attention_%d" % tq,
    )(*args)


def _attn_tail_kernel(lq1_ref, lk1_ref, lq2_ref, lk2_ref, sub_ref, q_ref, k_ref, v_ref, w_ref,
                      o_ref, k_sc, v_sc, *, hd, lam_init, jt0, live):
    KT = ATTN_TILE
    scale = hd ** -0.5
    tq = q_ref.shape[1]
    Lk = k_ref.shape[1]
    nt = k_sc.shape[0] // KT
    n_slabs = w_ref.shape[1]
    hw = 2 * hd
    n_heads = q_ref.shape[2] // hw
    k_sc[0:Lk, :] = k_ref[0]
    v_sc[0:Lk, :] = v_ref[0]
    if k_sc.shape[0] > Lk:
        pad = jnp.zeros((k_sc.shape[0] - Lk, k_sc.shape[1]), k_sc.dtype)
        k_sc[Lk:, :] = pad
        v_sc[Lk:, :] = pad
    slabs = [min(max(j - jt0 + 2, 0), n_slabs - 1) for j in range(nt)]
    lam = (jnp.exp(jnp.sum(lq1_ref[...] * lk1_ref[...], axis=-1, keepdims=True))
           - jnp.exp(jnp.sum(lq2_ref[...] * lk2_ref[...], axis=-1, keepdims=True)) + lam_init)
    for h in range(n_heads):
        v_h = v_sc[:, h * hw:(h + 1) * hw]
        outs = []
        for s in range(2):
            cols = slice((2 * h + s) * hd, (2 * h + s + 1) * hd)
            sc = _dot_nt(q_ref[0, 0:live, cols], k_sc[:, cols]) * scale
            sc = sc + jnp.concatenate([w_ref[2 * h + s, i, 0:live, :] for i in slabs], axis=1)
            p = jnp.exp(sc - jnp.max(sc, axis=-1, keepdims=True))
            l = jnp.sum(p, axis=-1, keepdims=True)
            outs.append(jnp.dot(p.astype(BF16), v_h, preferred_element_type=F32) / l)
        o = outs[0] - lam * outs[1]
        y = o * lax.rsqrt(jnp.mean(o * o, axis=-1, keepdims=True) + DIFF_SUBLN_EPS)
        o_ref[0, 0:live, h * hw:(h + 1) * hw] = (y * sub_ref[...] * (1.0 - lam_init)).astype(o_ref.dtype)
    if live < tq:
        o_ref[0, live:tq, :] = jnp.zeros((tq - live, o_ref.shape[2]), o_ref.dtype)


def _attn_tail(qkv3, table, params, lam_init, tq, row0, live):
    B, Lp, W3 = qkv3.shape
    DW = W3 // 3
    hd = params[0].shape[1]
    KT = ATTN_TILE
    nt = -(-Lp // KT)
    assert row0 % KT == 0 and row0 // KT == nt - 1 and row0 % tq == 0
    par = lambda n: pl.BlockSpec((1, n), lambda b: (0, 0))
    return pl.pallas_call(
        functools.partial(_attn_tail_kernel, hd=hd, lam_init=lam_init, jt0=row0 // KT, live=live),
        grid=(B,),
        in_specs=[par(hd)] * 4 + [
            par(2 * hd),
            pl.BlockSpec((1, tq, DW), lambda b: (b, row0 // tq, 0)),
            pl.BlockSpec((1, Lp, DW), lambda b: (b, 0, 1)),
            pl.BlockSpec((1, Lp, DW), lambda b: (b, 0, 2)),
            pl.BlockSpec(table.shape, lambda b: (0, 0, 0, 0))],
        out_specs=pl.BlockSpec((1, tq, DW), lambda b: (b, 0, 0)),
        out_shape=jax.ShapeDtypeStruct((B, tq, DW), BF16),
        scratch_shapes=[pltpu.VMEM((nt * KT, DW), BF16), pltpu.VMEM((nt * KT, DW), BF16)],
        compiler_params=_cparams(("parallel",)),
        name="diff_attention_tail",
    )(*(list(params) + [qkv3, qkv3, qkv3, table]))


def _attn_segments(Lp):
    big = ATTN_Q_TILE
    n_big = Lp // big
    segs = []
    if n_big:
        segs.append((big, 0, n_big))
    row0 = n_big * big
    n_mid = (Lp - row0) // ATTN_TILE
    if n_mid:
        segs.append((ATTN_TILE, row0, n_mid))
        row0 += n_mid * ATTN_TILE
    if Lp > row0:
        segs.append((Lp - row0, row0, 1))
    return segs


def _attn_call(zqkv, tables, lq1, lk1, lq2, lk2, subln, B, L, Lp, lam_init):
    vec = lambda a: a.reshape(1, -1)
    params = [vec(lq1), vec(lk1), vec(lq2), vec(lk2), vec(subln)]
    qkv3 = zqkv.reshape(B, Lp, zqkv.shape[-1])
    outs = []
    for tq, row0, n_tiles in _attn_segments(Lp):
        live = min(tq, _round_up(L - row0, BF16_ROWS)) if n_tiles == 1 else tq
        if tq < ATTN_TILE:
            outs.append(_attn_tail(qkv3, tables[tq], params, lam_init, tq, row0, live))
        else:
            outs.append(_attn_segment(qkv3, tables[tq], params, lam_init, tq, row0, n_tiles, live))
    out = outs[0] if len(outs) == 1 else jnp.concatenate(outs, axis=1)
    return out.reshape(B * Lp, -1)


def _rwkv_prep_kernel(*refs, rw, has_vres):
    if has_vres:
        (u_ref, up_ref, w_ref, mu_ref, w0_ref, w2_ref, a0_ref, a2_ref, g2_ref, kk_ref, ka_ref, ones_ref,
         vf_ref, v0_ref, v1_ref, v2_ref,
         r_o, lw_o, k_o, v_o, a_o, b_o, g_o, zp_o) = refs
    else:
        (u_ref, up_ref, w_ref, mu_ref, w0_ref, w2_ref, a0_ref, a2_ref, g2_ref, kk_ref, ka_ref, ones_ref,
         r_o, lw_o, k_o, v_o, a_o, b_o, g_o, zp_o) = refs
    i = pl.program_id(1)
    halo = up_ref.shape[1]
    zz = jnp.dot(jnp.concatenate([up_ref[0], u_ref[0]], axis=0), w_ref[...], preferred_element_type=F32)
    zw = mu_ref.shape[1]
    zp_o[0] = zz[halo:, zw:].astype(zp_o.dtype)
    z = zz[halo:, :zw]
    last_prev = jnp.where(i > 0, zz[halo - 1:halo, :zw], 0.0)
    row = lax.broadcasted_iota(jnp.int32, z.shape, 0)
    prev = jnp.where(row >= 1, pltpu.roll(z, 1, 0), last_prev)
    zs = z + (prev - z) * mu_ref[...]
    r = zs[:, 0:rw]
    kx = zs[:, rw:2 * rw]
    vx = zs[:, 2 * rw:3 * rw]
    wd = zs[:, 3 * rw:3 * rw + LANES]
    ad = zs[:, 3 * rw + LANES:3 * rw + 2 * LANES]
    gd = zs[:, 3 * rw + 2 * LANES:3 * rw + 3 * LANES]
    wl = w0_ref[...] + _dot(jnp.tanh(wd), w2_ref[...])
    sp = jnp.maximum(-wl, 0.0) + jnp.log(1.0 + jnp.exp(-jnp.abs(wl)))
    lw = -jnp.exp(-sp - 0.5)
    if has_vres:
        gate_v = _sigmoid(v0_ref[...] + _dot(_dot(vx, v1_ref[...]), v2_ref[...]))
        vx = vx + (vf_ref[0] - vx) * gate_v
    a = _sigmoid(a0_ref[...] + _dot(ad, a2_ref[...]))
    g = _dot(_sigmoid(gd), g2_ref[...])
    kk = kx * kk_ref[...]
    ss = _dot_exact_rhs(kk * kk, ones_ref[...])
    kk = kk * lax.rsqrt(jnp.maximum(ss, 1e-24))
    kmod = kx * (1.0 + (a - 1.0) * ka_ref[...])
    r_o[0] = r
    lw_o[0] = lw
    k_o[0] = kmod
    v_o[0] = vx
    a_o[0] = -kk
    b_o[0] = kk * a
    g_o[0] = g


def _pad_rows(w, n):
    return jnp.pad(w, ((0, n - w.shape[0]), (0, 0)))


def _rwkv_prep_call(u, w_rwkv, B, Lp, rw, mu_p, w0, w2, a0, a2, g2, k_k, k_a, ones_bd, v_first, v_res):
    D = u.shape[-1]
    u3 = u.reshape(B, Lp, D)
    HALO = BF16_ROWS
    tl = _pick_tile(Lp, PREP_ROWS, HALO)
    nb = tl // HALO
    has_vres = v_res is not None
    vec = lambda a: a.reshape(1, -1)
    full = lambda a: pl.BlockSpec(a.shape, lambda b, i: (0,) * a.ndim)
    w2p = _pad_rows(w2, LANES).astype(BF16)
    a2p = _pad_rows(a2, LANES).astype(BF16)
    g2p = _pad_rows(g2, LANES).astype(BF16)
    args = [u3, u3, w_rwkv, vec(mu_p), vec(w0), w2p, vec(a0), a2p, g2p, vec(k_k), vec(k_a), ones_bd]
    in_specs = [pl.BlockSpec((1, tl, D), lambda b, i: (b, i, 0)),
                pl.BlockSpec((1, HALO, D), lambda b, i: (b, jnp.maximum(i * nb - 1, 0), 0)),
                pl.BlockSpec(w_rwkv.shape, lambda b, i: (0, 0), pipeline_mode=pl.Buffered(1))]
    in_specs += [full(a) for a in args[3:]]
    if has_vres:
        v0, v1, v2 = v_res
        v1p = jnp.pad(v1, ((0, 0), (0, LANES - v1.shape[1]))).astype(BF16)
        v2p = _pad_rows(v2, LANES).astype(BF16)
        extra = [v_first, vec(v0), v1p, v2p]
        args += extra
        in_specs += [pl.BlockSpec((1, tl, rw), lambda b, i: (b, i, 0))] + [full(a) for a in extra[1:]]
    o_spec = pl.BlockSpec((1, tl, rw), lambda b, i: (b, i, 0))
    o_shape = jax.ShapeDtypeStruct((B, Lp, rw), F32)
    pw = w_rwkv.shape[1] - mu_p.shape[0]
    return pl.pallas_call(
        functools.partial(_rwkv_prep_kernel, rw=rw, has_vres=has_vres),
        grid=(B, Lp // tl),
        in_specs=in_specs,
        out_specs=[o_spec] * 7 + [pl.BlockSpec((1, tl, pw), lambda b, i: (b, i, 0))],
        out_shape=[o_shape] * 7 + [jax.ShapeDtypeStruct((B, Lp, pw), BF16)],
        compiler_params=_cparams(("parallel", "arbitrary")),
        name="rwkv_prep",
    )(*args)


def _scan_kernel(r_ref, lw_ref, k_ref, v_ref, a_ref, b_ref, g_ref, rk_ref, lnw_ref, lnb_ref, ones_ref,
                 o_ref, h_sc, *, hd):
    nb = r_ref.shape[0]
    C = r_ref.shape[1]
    C2 = 2 * C
    WB = r_ref.shape[2]
    W = nb * WB
    c = pl.program_id(1)

    @pl.when(c == 0)
    def _():
        h_sc[...] = jnp.zeros_like(h_sc)

    side_by_side = lambda ref: jnp.concatenate([ref[i] for i in range(nb)], axis=1)
    r = side_by_side(r_ref)
    lw = side_by_side(lw_ref)
    k = side_by_side(k_ref)
    v = side_by_side(v_ref)
    a = side_by_side(a_ref)
    b = side_by_side(b_ref)

    cum = lw
    d = 1
    while d < C:
        cum = cum + _shift_down(cum, d)
        d *= 2
    tot = cum[C - 1:C, :]
    e_end = jnp.exp(tot - cum)
    e_neg = jnp.exp(-cum)
    at = a * jnp.exp(cum - lw)
    rt = r * jnp.exp(cum)
    bt = b * e_neg
    kt = k * e_neg
    bb = b * e_end
    kb = k * e_end
    p_c = jnp.exp(tot)

    lane = lax.broadcasted_iota(jnp.int32, (C, LANES), 1)
    first = lane < hd
    rr = lax.broadcasted_iota(jnp.int32, (C2, C2), 0)
    cc = lax.broadcasted_iota(jnp.int32, (C2, C2), 1)
    tr = jnp.where(rr >= C, rr - C, rr)
    tc = jnp.where(cc >= C, cc - C, cc)
    strict = tr > tc
    lower = tr >= tc
    eye = rr == cc

    def stack(x):
        return jnp.concatenate([jnp.where(first, x, 0.0), jnp.where(first, 0.0, x)], axis=0)

    pairs = range(W // LANES)
    stacked = lambda x: [stack(x[:, p * LANES:(p + 1) * LANES]) for p in pairs]
    s_a, s_r, s_b, s_k = stacked(at), stacked(rt), stacked(bt), stacked(kt)
    s_bb, s_kb, s_v = stacked(bb), stacked(kb), stacked(v)
    sc = [_dot_nt(jnp.concatenate([s_a[p], s_r[p]], axis=0), jnp.concatenate([s_b[p], s_k[p]], axis=0))
          for p in pairs]
    n = [jnp.where(strict, sc[p][:C2, :C2], 0.0) for p in pairs]
    a_ak = [jnp.where(strict, sc[p][:C2, C2:], 0.0) for p in pairs]
    m_rb = [jnp.where(lower, sc[p][C2:, :C2], 0.0) for p in pairs]
    m_rk = [jnp.where(lower, sc[p][C2:, C2:], 0.0) for p in pairs]
    xv = [_dot(jnp.concatenate([a_ak[p], m_rk[p], s_kb[p].T], axis=0), s_v[p]) for p in pairs]
    t_inv = [jnp.where(eye, 1.0, 0.0) + n[p] for p in pairs]
    pw = [_dot(n[p], n[p]) for p in pairs]
    step = 4
    while step < C:
        tp = [_dot(jnp.concatenate([t_inv[p], pw[p]], axis=0), pw[p]) for p in pairs]
        t_inv = [t_inv[p] + tp[p][:C2] for p in pairs]
        pw = [tp[p][C2:] for p in pairs]
        step *= 2
    t_inv = [t_inv[p] + _dot(t_inv[p], pw[p]) for p in pairs]
    au = [_dot(t_inv[p], jnp.concatenate([s_a[p], xv[p][:C2]], axis=1)) for p in pairs]
    mbu = [_dot(jnp.concatenate([m_rb[p], s_bb[p].T], axis=0), au[p]) for p in pairs]
    ys = []
    for p in pairs:
        r_hat = s_r[p] + mbu[p][:C2, :LANES]
        y0 = mbu[p][:C2, LANES:] + xv[p][C2:2 * C2]
        g_mat = jnp.where(eye, p_c[:, p * LANES:(p + 1) * LANES], 0.0) + mbu[p][C2:, :LANES]
        h_add = mbu[p][C2:, LANES:] + xv[p][2 * C2:]
        yh = _dot(jnp.concatenate([r_hat, g_mat], axis=0), h_sc[p])
        y = yh[:C2] + y0
        h_sc[p] = yh[C2:] + h_add
        ys.append(y[0:C] + y[C:C2])
    ones = ones_ref[...]
    inv_hd = 1.0 / hd
    ppb = WB // LANES
    n = nb * C
    y = jnp.concatenate([jnp.concatenate(ys[i * ppb:(i + 1) * ppb], axis=1) for i in range(nb)], axis=0)
    rk = jnp.concatenate([r_ref[i] * k_ref[i] * rk_ref[...] for i in range(nb)], axis=0)
    sums = _dot_exact_rhs(jnp.concatenate([y, rk], axis=0), ones)
    yc = y - sums[:n] * inv_hd
    var = _dot_exact_rhs(yc * yc, ones) * inv_hd
    yn = yc * lax.rsqrt(var + RWKV_LNX_EPS) * lnw_ref[...] + lnb_ref[...]
    for i in range(nb):
        rows = slice(i * C, (i + 1) * C)
        bonus = sums[n:][rows] * v_ref[i]
        o_ref[i] = ((yn[rows] + bonus) * g_ref[i]).astype(o_ref.dtype)


def _scan_call(streams, r_k, lnx_w, lnx_b, ones_bd, B, Lp, hd):
    r, lw, k, v, a, b, g = streams
    rw = r.shape[-1]
    C = CHUNK
    assert 2 * hd == LANES and rw % LANES == 0 and Lp % C == 0
    vec = lambda x: x.reshape(1, -1)
    nb = SCAN_BATCH if B % SCAN_BATCH == 0 else 1
    blk = pl.BlockSpec((nb, C, rw), lambda bi, c: (bi, c, 0))
    par = pl.BlockSpec((1, rw), lambda bi, c: (0, 0))
    out = pl.pallas_call(
        functools.partial(_scan_kernel, hd=hd),
        grid=(B // nb, Lp // C),
        in_specs=[blk] * 7 + [par] * 3 + [pl.BlockSpec((rw, rw), lambda bi, c: (0, 0))],
        out_specs=blk,
        out_shape=jax.ShapeDtypeStruct((B, Lp, rw), BF16),
        scratch_shapes=[pltpu.VMEM((nb * rw // LANES, LANES, LANES), F32)],
        compiler_params=_cparams(("parallel", "arbitrary")),
        name="rwkv_scan",
    )(r, lw, k, v, a, b, g, vec(r_k), vec(lnx_w), vec(lnx_b), ones_bd)
    return out.reshape(B * Lp, rw)


def _merge_kernel(yp_ref, yd_ref, yr_ref, g0_ref, g1_ref, g2_ref, pa_ref, pb_ref, pc_ref, o_ref):
    dot = lambda x, w: jnp.dot(x[...], w[...], preferred_element_type=F32)
    m = _sigmoid(g0_ref[...].astype(F32)) * dot(yp_ref, pa_ref)
    m = m + _sigmoid(g1_ref[...].astype(F32)) * dot(yd_ref, pb_ref)
    m = m + _sigmoid(g2_ref[...].astype(F32)) * dot(yr_ref, pc_ref)
    o_ref[...] = m.astype(o_ref.dtype)


def _merge_call(yp, yd, yr, zg, pa, pb, pc, layer, tm):
    M = yp.shape[0]
    D = pa.shape[2]
    rows = lambda a: pl.BlockSpec((tm, a.shape[1]), lambda i: (i, 0))
    full = lambda a: pl.BlockSpec((None,) + a.shape[1:], lambda i: (layer, 0, 0))
    gate = lambda n: pl.BlockSpec((tm, D), lambda i: (i, n))
    return pl.pallas_call(
        _merge_kernel,
        grid=(M // tm,),
        in_specs=[rows(yp), rows(yd), rows(yr), gate(0), gate(1), gate(2), full(pa), full(pb), full(pc)],
        out_specs=pl.BlockSpec((tm, D), lambda i: (i, 0)),
        out_shape=jax.ShapeDtypeStruct((M, D), BF16),
        compiler_params=_cparams(("parallel",)),
        name="gated_merge",
    )(yp, yd, yr, zg, zg, zg, pa, pb, pc)


def _ffn_up_kernel(u_ref, wv_ref, wg_ref, cwv_ref, cwg_ref, cbv_ref, cbg_ref, o_ref, *scratch, n_sub):
    xg_scs, xv_scs, a_scs = scratch[:n_sub], scratch[n_sub:2 * n_sub], scratch[2 * n_sub:]
    L = u_ref.shape[0]
    mb = L // n_sub
    st = mb // SUBLANES
    PAD = SUBLANES
    n_slab = a_scs[0].shape[0]
    k_in = math.sqrt(2.0 / math.pi)
    zeros = jnp.zeros((PAD, LANES), F32)
    for s in range(n_slab):
        xv_scs[0][s, 0:PAD, :] = zeros
        xg_scs[0][s, 0:PAD, :] = zeros

    def taps(cw_ref, cb_ref, s):
        sl = slice(s * LANES, (s + 1) * LANES)
        rows = [jnp.broadcast_to(cw_ref[j:j + 1, sl], (SUBLANES, LANES)) for j in range(FFN_CONV)]
        return rows, jnp.broadcast_to(cb_ref[:, sl], (SUBLANES, LANES))

    def tail(sb):
        lo = sb * mb
        xg_sc, xv_sc, a_sc = xg_scs[sb], xv_scs[sb], a_scs[sb]
        for s in range(n_slab):
            wg, bg = taps(cwg_ref, cbg_ref, s)
            wv, bv = taps(cwv_ref, cbv_ref, s)
            ld = lambda ref, j: ref[s, pl.ds(PAD + j, SUBLANES, stride=st), :]
            xg = [ld(xg_sc, j) for j in range(1 - FFN_CONV, 0)]
            xv = [ld(xv_sc, j) for j in range(1 - FFN_CONV, 0)]
            for j in range(st):
                xg.append(ld(xg_sc, j))
                xv.append(ld(xv_sc, j))
                gate = bg
                val = bv
                for t in range(FFN_CONV):
                    gate = gate + xg[j + t] * wg[t]
                    val = val + xv[j + t] * wv[t]
                inner = gate * (2.0 * k_in + (2.0 * k_in * 0.044715) * (gate * gate))
                act = gate / (1.0 + jnp.exp(-inner)) * val
                a_sc[s, pl.ds(j, SUBLANES, stride=st), :] = act
        for s in range(n_slab):
            o_ref[lo:lo + mb, s * LANES:(s + 1) * LANES] = a_sc[s].astype(o_ref.dtype)

    for sb in range(n_sub):
        lo = sb * mb
        u = u_ref[lo:lo + mb, :]
        for x_scs, w_ref in ((xg_scs, wg_ref), (xv_scs, wv_ref)):
            x = jnp.dot(u, w_ref[...], preferred_element_type=F32)
            for s in range(n_slab):
                xs = x[:, s * LANES:(s + 1) * LANES]
                x_scs[sb][s, PAD:PAD + mb, :] = xs
                if sb + 1 < n_sub:
                    x_scs[sb + 1][s, 0:PAD, :] = xs[mb - PAD:mb]
        if sb:
            tail(sb - 1)
    tail(n_sub - 1)


def _ffn_up_call(u, w_up, layer, conv_w, conv_b, B, Lp, tn):
    M, D = u.shape
    FF = w_up.shape[2] // 2
    nt = FF // tn
    n_sub = FFN_SUBBLOCKS if Lp % (FFN_SUBBLOCKS * BF16_ROWS) == 0 else 1
    mb = Lp // n_sub
    return pl.pallas_call(
        functools.partial(_ffn_up_kernel, n_sub=n_sub),
        grid=(B, nt),
        in_specs=[pl.BlockSpec((Lp, D), lambda b, j: (b, 0)),
                  pl.BlockSpec((None, D, tn), lambda b, j: (layer, 0, j)),
                  pl.BlockSpec((None, D, tn), lambda b, j: (layer, 0, nt + j)),
                  pl.BlockSpec((FFN_CONV, tn), lambda b, j: (0, j)),
                  pl.BlockSpec((FFN_CONV, tn), lambda b, j: (0, nt + j)),
                  pl.BlockSpec((1, tn), lambda b, j: (0, j)),
                  pl.BlockSpec((1, tn), lambda b, j: (0, nt + j))],
        out_specs=pl.BlockSpec((Lp, tn), lambda b, j: (b, j)),
        out_shape=jax.ShapeDtypeStruct((M, FF), BF16),
        scratch_shapes=([pltpu.VMEM((tn // LANES, SUBLANES + mb, LANES), F32)] * (2 * n_sub)
                        + [pltpu.VMEM((tn // LANES, mb, LANES), F32)] * n_sub),
        compiler_params=_cparams(("parallel", "arbitrary")),
        name="ffn_up_conv_geglu",
    )(u, w_up, w_up, conv_w, conv_w, conv_b.reshape(1, -1), conv_b.reshape(1, -1))


def kernel(x, meta, rel_bias, norm_mix_pre, norm_mix_post, norm_ffn_pre, norm_ffn_post, w_in, pool_w, pool_scale, diff_lq1, diff_lk1, diff_lq2, diff_lk2, diff_subln, rwkv_mu, rwkv_w0, rwkv_w2, rwkv_a0, rwkv_a2, rwkv_g2, rwkv_kk, rwkv_ka, rwkv_rk, rwkv_lnx_w, rwkv_lnx_b, rwkv_v0, rwkv_v1, rwkv_v2, p_a, p_b, p_c, w_o, ffn_up, ffn_conv_w, ffn_conv_b, ffn_down):
    B, S, D = x.shape
    n_meta = meta.shape[0]
    depth = w_in.shape[0]
    L = S + n_meta
    Lp = _round_up(L, CHUNK)
    M = B * Lp

    PW = pool_scale.shape[1]
    DW = p_b.shape[1]
    RW = rwkv_w0.shape[1]
    hd_r = rwkv_rk.shape[2]
    dl, al, gl = rwkv_w2.shape[1], rwkv_a2.shape[1], rwkv_g2.shape[1]
    off_q = PW
    off_r = PW + 3 * DW
    off_g = off_r + 3 * RW + dl + al + gl
    assert off_g + 3 * D == w_in.shape[2]
    assert Lp == L or S % CHUNK == 0
    assert max(dl, al, gl, rwkv_v1.shape[2]) <= LANES

    h = jnp.concatenate([jnp.broadcast_to(meta.astype(x.dtype)[None], (B, n_meta, D)), x,
                         jnp.zeros((B, Lp - L, D), x.dtype)], axis=1).reshape(M, D)

    hid = np.arange(RW) // hd_r
    ones_bd = jnp.asarray((hid[:, None] == hid[None, :]).astype(np.float32), dtype=BF16)

    def pad_lora(a, axis):
        segs = [lax.slice_in_dim(a, 0, 3 * RW, axis=axis)]
        o = 3 * RW
        for n in (dl, al, gl):
            seg = lax.slice_in_dim(a, o, o + n, axis=axis)
            pad = [(0, 0)] * a.ndim
            pad[axis] = (0, LANES - n)
            segs.append(jnp.pad(seg, pad))
            o += n
        return jnp.concatenate(segs, axis=axis)

    table = _bias_table_call(rel_bias, n_meta, ATTN_TILE)
    tables = {tq: table for tq, _, _ in _attn_segments(Lp)}

    tm_big = _pick_tile(M, MM_ROWS, BF16_ROWS)
    tm_res = _pick_tile(M, EW_ROWS, BF16_ROWS)
    u = _rms_call(h, norm_mix_pre[0], tm_res)
    v_first = None
    w_in_b, p_a_b, p_b_b, p_c_b, w_o_b, ffn_up_b, ffn_down_b = (
        a.astype(BF16) for a in (w_in, p_a, p_b, p_c, w_o, ffn_up, ffn_down))
    for l in range(depth):
        lam_init = 0.8 - 0.6 * math.exp(-0.3 * l)
        w_gate = w_in_b[l, :, off_g:]
        w_rwkv = pad_lora(w_in_b[l, :, off_r:off_g], 1)
        mu_p = pad_lora(rwkv_mu[l], 0)

        zg = _mm_call(u, w_gate, BF16, tm_big, _pick_tile(3 * D, MM_COLS, LANES), "in_proj_gate")
        zqkv = _mm_call(u, w_in_b[l, :, off_q:off_r], BF16, tm_big, _pick_tile(3 * DW, MM_COLS, LANES),
                        "in_proj_qkv")

        yd = _attn_call(zqkv, tables, diff_lq1[l], diff_lk1[l], diff_lq2[l], diff_lk2[l],
                        diff_subln[l], B, L, Lp, lam_init)
        v_res = None if l == 0 else (rwkv_v0[l - 1], rwkv_v1[l - 1], rwkv_v2[l - 1])
        w_rp = jnp.concatenate([w_rwkv, w_in_b[l, :, :PW]], axis=1)
        *streams, zp = _rwkv_prep_call(u, w_rp, B, Lp, RW, mu_p, rwkv_w0[l], rwkv_w2[l], rwkv_a0[l], rwkv_a2[l],
                                  rwkv_g2[l], rwkv_kk[l], rwkv_ka[l], ones_bd, v_first, v_res)
        if l == 0:
            v_first = streams[3]
        yp = _pool_call(zp.reshape(M, PW), pool_w[l], pool_scale[l], B, Lp)
        yr = _scan_call(streams, rwkv_rk[l], rwkv_lnx_w[l], rwkv_lnx_b[l], ones_bd, B, Lp, hd_r)

        merged = _merge_call(yp, yd, yr, zg, p_a_b, p_b_b, p_c_b, l, tm_res)
        h, u2 = _mm_res_call(merged, w_o_b, l, h, norm_mix_post[l], norm_ffn_pre[l], "out_proj_res")
        FF = ffn_down.shape[1]
        act = _ffn_up_call(u2, ffn_up_b, l, ffn_conv_w[l], ffn_conv_b[l], B, Lp,
                           _pick_tile(FF, FFN_COLS, LANES))
        g_next = norm_mix_pre[l + 1] if l + 1 < depth else norm_mix_pre[l]
        h, u = _mm_res_call(act, ffn_down_b, l, h, norm_ffn_post[l], g_next, "ffn_down_res")
    return h.reshape(B, Lp, D)[:, n_meta:n_meta + S]
```
